```python
import math
import jax, jax.numpy as jnp
from jax import lax
import numpy as np

D_MODEL = 2048
BATCH = 1
SEQ = 8192
DEPTH = 1

D_MIX = D_MODEL
D_GMLP = D_MIX // 2
GMLP_GROUPS = 8
GMLP_GROUP_DIM = D_GMLP // GMLP_GROUPS
CHUNK = 128
D_ATTN = D_MIX - D_GMLP
HEAD_DIM = 128
N_HEADS = D_ATTN // HEAD_DIM
DILATED_BRANCHES = ((128, 1), (512, 4), (2048, 16))
MAX_DILATION = 16
BLK = 128
D_IN_PROJ = 2 * D_GMLP + 3 * D_ATTN
N_EXPERTS = 32
TOP_K = 4
D_EXPERT = D_MODEL
SWIGLU_LIMIT = 7.0
SWIGLU_ALPHA = 1.702
EXPERT_BLOCK = 128
LN_EPS = 1e-5
DEEPNORM_ALPHA = (2.0 * DEPTH) ** 0.25
DEEPNORM_BETA = (8.0 * DEPTH) ** -0.25

kernel_name = "hymba_gmlp_dilated_moe_block"


def layer_norm(x, g, b):
    xf = x.astype(jnp.float32)
    mu = jnp.mean(xf, axis=-1, keepdims=True)
    xc = xf - mu
    var = jnp.mean(xc * xc, axis=-1, keepdims=True)
    return (xc * lax.rsqrt(var + LN_EPS) * g + b).astype(x.dtype)


def gmlp_mixer(ua, va, ln_g, ln_b, w_s, b_s):
    B, S, _ = ua.shape
    n = S // CHUNK
    u = jax.nn.gelu(ua, approximate=False)
    v = jax.nn.gelu(va, approximate=False).reshape(B, n, CHUNK, GMLP_GROUPS, GMLP_GROUP_DIM)
    v = layer_norm(v, ln_g, ln_b)
    w = jnp.tril(w_s).astype(v.dtype)
    s = jnp.einsum('gts,bnsgc->bntgc', w, v) + b_s.T[None, None, :, :, None].astype(v.dtype)
    return u * s.reshape(B, S, D_GMLP)


def dilated_branch(q, k, v, slopes, window, dilation):
    B, H, Sp, hd = q.shape
    steps = window // dilation
    assert steps <= BLK
    L = Sp // dilation
    nb = L // BLK

    def to_res(t):
        return t.reshape(B, H, L, dilation, hd).transpose(0, 1, 3, 2, 4).reshape(B, H, dilation, nb, BLK, hd)

    qr, kr, vr = to_res(q), to_res(k), to_res(v)

    def with_prev(t):
        prev = jnp.pad(t[:, :, :, :-1], [(0, 0), (0, 0), (0, 0), (1, 0), (0, 0), (0, 0)])
        return jnp.concatenate([prev, t], axis=4)

    kb, vb = with_prev(kr), with_prev(vr)
    s = jnp.einsum('bhrnqc,bhrnkc->bhrnqk', qr, kb).astype(jnp.float32) * (hd ** -0.5)
    qi = jnp.arange(BLK)[:, None] + BLK
    ki = jnp.arange(2 * BLK)[None, :]
    step = qi - ki
    blk = jnp.arange(nb)[:, None, None]
    valid = (step >= 0) & (step <= steps) & ((blk > 0) | (ki >= BLK))
    dist = (step * dilation).astype(jnp.float32)
    s = s - slopes[None, :, None, None, None, None] * dist
    s = jnp.where(valid, s, -1e30)
    m = jnp.max(s, axis=-1, keepdims=True)
    p = jnp.exp(s - m)
    den = jnp.sum(p, axis=-1)
    o = jnp.einsum('bhrnqk,bhrnkc->bhrnqc', p.astype(vb.dtype), vb).astype(jnp.float32) / den[..., None]
    lse = m[..., 0] + jnp.log(den)
    o = o.reshape(B, H, dilation, L, hd).transpose(0, 1, 3, 2, 4).reshape(B, H, Sp, hd)
    lse = lse.reshape(B, H, dilation, L).transpose(0, 1, 3, 2).reshape(B, H, Sp)
    return o, lse


def dilated_attention(q, k, v):
    B, S, H, hd = q.shape
    span = MAX_DILATION * BLK
    Sp = -(-S // span) * span
    pad = [(0, 0), (0, Sp - S), (0, 0), (0, 0)]
    q, k, v = [jnp.pad(t, pad).transpose(0, 2, 1, 3) for t in (q, k, v)]
    slopes = jnp.exp2(-8.0 * jnp.arange(1, H + 1, dtype=jnp.float32) / H)
    outs, lses = [], []
    for window, dilation in DILATED_BRANCHES:
        o, lse = dilated_branch(q, k, v, slopes, window, dilation)
        outs.append(o)
        lses.append(lse)
    w = jax.nn.softmax(jnp.stack(lses, axis=0), axis=0)
    o = jnp.sum(w[..., None] * jnp.stack(outs, axis=0), axis=0)
    return o.transpose(0, 2, 1, 3)[:, :S].reshape(B, S, H * hd).astype(q.dtype)


def moe_ffn(h, w_router, b_router, w1, b1, w2, b2):
    B, S, D = h.shape
    N = B * S
    T = EXPERT_BLOCK
    xt = h.reshape(N, D)
    logits = (xt @ w_router + b_router).astype(jnp.float32)
    top_val, top_idx = lax.top_k(logits, TOP_K)
    gates = jax.nn.softmax(top_val, axis=-1)
    NK = N * TOP_K
    e_flat = top_idx.reshape(NK).astype(jnp.int32)
    g_flat = gates.reshape(NK)
    tok_flat = jnp.arange(NK, dtype=jnp.int32) // TOP_K
    order = jnp.argsort(e_flat)
    e_sorted, tok_sorted, g_sorted = e_flat[order], tok_flat[order], g_flat[order]
    counts = jnp.zeros((N_EXPERTS,), jnp.int32).at[e_flat].add(1)
    padded = (counts + T - 1) // T * T
    start = jnp.cumsum(counts) - counts
    pend = jnp.cumsum(padded)
    pstart = pend - padded
    dest = pstart[e_sorted] + jnp.arange(NK, dtype=jnp.int32) - start[e_sorted]
    cap = -(-NK // T) * T + N_EXPERTS * T
    n_blocks = cap // T
    buf_tok = jnp.full((cap,), N, jnp.int32).at[dest].set(tok_sorted)
    buf_gate = jnp.zeros((cap,), jnp.float32).at[dest].set(g_sorted)
    block_e = jnp.minimum(jnp.searchsorted(pend, jnp.arange(n_blocks, dtype=jnp.int32) * T, side='right'),
                          N_EXPERTS - 1)
    x_pad = jnp.concatenate([xt, jnp.zeros((1, D), xt.dtype)], axis=0)
    x_buf = x_pad[buf_tok].reshape(n_blocks, T, D)

    def run_block(args):
        xb, e = args
        h1 = xb @ w1[e] + b1[e]
        glu, lin = h1[:, :D_EXPERT], h1[:, D_EXPERT:]
        glu = jnp.minimum(glu, SWIGLU_LIMIT)
        lin = jnp.clip(lin, -SWIGLU_LIMIT, SWIGLU_LIMIT)
        act = glu * jax.nn.sigmoid(SWIGLU_ALPHA * glu) * (lin + 1.0)
        return act @ w2[e] + b2[e]

    out = lax.map(run_block, (x_buf, block_e)).reshape(cap, D)
    y = jnp.zeros((N + 1, D), jnp.float32).at[buf_tok].add(out.astype(jnp.float32) * buf_gate[:, None])[:N]
    return y.astype(h.dtype).reshape(B, S, D)


def setup_inputs(seed: int = 0) -> dict:
    key = jax.random.key(seed)
    ks = jax.random.split(key, 22)
    L = DEPTH

    def nrm(k, shape, scale):
        return jax.random.normal(k, shape, jnp.float32) * scale

    return {
        "x": nrm(ks[0], (BATCH, SEQ, D_MODEL), 1.0),
        "c": nrm(ks[1], (BATCH, D_MODEL), 1.0),
        "w_ada": nrm(ks[2], (L, D_MODEL, 6 * D_MODEL), 0.5 * D_MODEL ** -0.5),
        "b_ada": nrm(ks[3], (L, 6 * D_MODEL), 0.02),
        "w_in": nrm(ks[4], (L, D_MODEL, D_IN_PROJ), D_MODEL ** -0.5),
        "sgu_ln_g": 1.0 + nrm(ks[5], (L, GMLP_GROUPS, GMLP_GROUP_DIM), 0.02),
        "sgu_ln_b": nrm(ks[6], (L, GMLP_GROUPS, GMLP_GROUP_DIM), 0.02),
        "w_spatial": nrm(ks[7], (L, GMLP_GROUPS, CHUNK, CHUNK), 0.5 * CHUNK ** -0.5),
        "b_spatial": 1.0 + nrm(ks[8], (L, GMLP_GROUPS, CHUNK), 0.02),
        "w_o": nrm(ks[9], (L, D_MIX, D_MODEL), DEEPNORM_BETA * D_MIX ** -0.5),
        "ln1_g": 1.0 + nrm(ks[10], (L, D_MODEL), 0.02),
        "ln1_b": nrm(ks[11], (L, D_MODEL), 0.02),
        "w_router": nrm(ks[12], (L, D_MODEL, N_EXPERTS), D_MODEL ** -0.5),
        "b_router": nrm(ks[13], (L, N_EXPERTS), 0.01),
        "w_exp1": nrm(ks[14], (L, N_EXPERTS, D_MODEL, 2 * D_EXPERT), D_MODEL ** -0.5),
        "b_exp1": nrm(ks[15], (L, N_EXPERTS, 2 * D_EXPERT), 0.02),
        "w_exp2": nrm(ks[16], (L, N_EXPERTS, D_EXPERT, D_MODEL), DEEPNORM_BETA * D_EXPERT ** -0.5),
        "b_exp2": nrm(ks[17], (L, N_EXPERTS, D_MODEL), 0.02),
        "ln2_g": 1.0 + nrm(ks[18], (L, D_MODEL), 0.02),
        "ln2_b": nrm(ks[19], (L, D_MODEL), 0.02),
    }


def reference(x, c, w_ada, b_ada, w_in, sgu_ln_g, sgu_ln_b, w_spatial, b_spatial, w_o, ln1_g, ln1_b,
              w_router, b_router, w_exp1, b_exp1, w_exp2, b_exp2, ln2_g, ln2_b):
    B, S, D = x.shape
    splits = [D_GMLP, 2 * D_GMLP, 2 * D_GMLP + D_ATTN, 2 * D_GMLP + 2 * D_ATTN]
    for l in range(DEPTH):
        ada = jax.nn.silu(c) @ w_ada[l] + b_ada[l]
        shift1, scale1, gate1, shift2, scale2, gate2 = jnp.split(ada[:, None, :], 6, axis=-1)

        h = x * (1.0 + scale1) + shift1
        proj = h @ w_in[l]
        ua, va, q, k, v = jnp.split(proj, splits, axis=-1)
        y_a = gmlp_mixer(ua, va, sgu_ln_g[l], sgu_ln_b[l], w_spatial[l], b_spatial[l])
        y_b = dilated_attention(q.reshape(B, S, N_HEADS, HEAD_DIM),
                                k.reshape(B, S, N_HEADS, HEAD_DIM),
                                v.reshape(B, S, N_HEADS, HEAD_DIM))
        mix = jnp.concatenate([y_a, y_b], axis=-1) @ w_o[l]
        x = layer_norm(DEEPNORM_ALPHA * x + gate1 * mix, ln1_g[l], ln1_b[l])

        h = x * (1.0 + scale2) + shift2
        y = moe_ffn(h, w_router[l], b_router[l], w_exp1[l], b_exp1[l], w_exp2[l], b_exp2[l])
        x = layer_norm(DEEPNORM_ALPHA * x + gate2 * y, ln2_g[l], ln2_b[l])
    return x
```

```python
import functools
import math

import jax
import jax.numpy as jnp
from jax import lax
from jax.experimental import pallas as pl
from jax.experimental.pallas import tpu as pltpu

D_MODEL = 2048
SEQ = 8192
D_GMLP = 1024
GMLP_GROUPS = 8
GROUP_DIM = 128
CHUNK = 128
D_ATTN = 1024
HEAD_DIM = 128
N_HEADS = 8
BRANCHES = ((128, 1), (512, 4), (2048, 16))
BLK = 128
SPAN = 16 * BLK
D_IN_PROJ = 2 * D_GMLP + 3 * D_ATTN
N_EXPERTS = 32
TOP_K = 4
D_EXPERT = 2048
SWIGLU_LIMIT = 7.0
SWIGLU_ALPHA = 1.702
LN_EPS = 1e-5
DEEPNORM_ALPHA = 2.0 ** 0.25
NEG = -1e30

LANES = 128
VMEM_LIMIT = 56 * 1024 * 1024

ADA_TN = 1536
PROJ_TM = 1024
PROJ_TN = 1024
OUT_TM = 512
MOE_TM = 512
MOE_TN = 512
MOE_CAP = SEQ * TOP_K + N_EXPERTS * MOE_TM
MOE_TILES = MOE_CAP // MOE_TM
GATHER_ROWS = 128
COMBINE_TM = 128


def _params(*sem):
    return pltpu.CompilerParams(dimension_semantics=sem, vmem_limit_bytes=VMEM_LIMIT)


def _layer_norm(x, g, b):
    mu = jnp.mean(x, axis=-1, keepdims=True)
    xc = x - mu
    var = jnp.mean(xc * xc, axis=-1, keepdims=True)
    return xc * lax.rsqrt(var + LN_EPS) * g + b


def _ada_kernel(c_ref, w_ref, b_ref, o_ref):
    c = c_ref[...]
    s = c * jax.nn.sigmoid(c)
    o_ref[...] = jnp.sum(s * w_ref[...], axis=0, keepdims=True) + b_ref[...]


def _ada(c, w_ada, b_ada):
    n = w_ada.shape[1]
    return pl.pallas_call(
        _ada_kernel,
        grid=(n // ADA_TN,),
        in_specs=[
            pl.BlockSpec((D_MODEL, 1), lambda j: (0, 0)),
            pl.BlockSpec((D_MODEL, ADA_TN), lambda j: (0, j)),
            pl.BlockSpec((1, ADA_TN), lambda j: (0, j)),
        ],
        out_specs=pl.BlockSpec((1, ADA_TN), lambda j: (0, j)),
        out_shape=jax.ShapeDtypeStruct((1, n), jnp.float32),
        compiler_params=_params("arbitrary"),
        name="ada",
    )(c.reshape(D_MODEL, 1), w_ada, b_ada.reshape(1, n))


def _in_proj_kernel(x_ref, sc_ref, sh_ref, w_ref, o_ref, h_ref):
    @pl.when(pl.program_id(1) == 0)
    def _():
        h_ref[...] = (x_ref[...] * (1.0 + sc_ref[...]) + sh_ref[...]).astype(jnp.bfloat16)

    o_ref[...] = jnp.dot(h_ref[...], w_ref[...], preferred_element_type=jnp.float32)


def _in_proj(x, scale1, shift1, w_in_bf16):
    return pl.pallas_call(
        _in_proj_kernel,
        grid=(SEQ // PROJ_TM, D_IN_PROJ // PROJ_TN),
        in_specs=[
            pl.BlockSpec((PROJ_TM, D_MODEL), lambda i, j: (i, 0)),
            pl.BlockSpec((1, D_MODEL), lambda i, j: (0, 0)),
            pl.BlockSpec((1, D_MODEL), lambda i, j: (0, 0)),
            pl.BlockSpec((D_MODEL, PROJ_TN), lambda i, j: (0, j)),
        ],
        out_specs=pl.BlockSpec((PROJ_TM, PROJ_TN), lambda i, j: (i, j)),
        out_shape=jax.ShapeDtypeStruct((SEQ, D_IN_PROJ), jnp.float32),
        scratch_shapes=[pltpu.VMEM((PROJ_TM, D_MODEL), jnp.bfloat16)],
        compiler_params=_params("arbitrary", "arbitrary"),
        name="in_proj",
    )(x, scale1, shift1, w_in_bf16)


def _gelu(x):
    return 0.5 * x * (1.0 + lax.erf(x * (1.0 / math.sqrt(2.0))))


def _gmlp_kernel(ua_ref, va_ref, g_ref, b_ref, w_ref, bs_ref, o_ref):
    row = lax.broadcasted_iota(jnp.int32, (CHUNK, CHUNK), 0)
    col = lax.broadcasted_iota(jnp.int32, (CHUNK, CHUNK), 1)
    causal = col <= row
    for g in range(GMLP_GROUPS):
        sl = slice(g * GROUP_DIM, (g + 1) * GROUP_DIM)
        v = _layer_norm(_gelu(va_ref[:, sl]), g_ref[g:g + 1, :], b_ref[g:g + 1, :])
        w = jnp.where(causal, w_ref[g], 0.0).astype(jnp.bfloat16)
        s = jnp.dot(w, v.astype(jnp.bfloat16), preferred_element_type=jnp.float32)
        s = s + bs_ref[:, g:g + 1]
        o_ref[:, sl] = (_gelu(ua_ref[:, sl]) * s).astype(o_ref.dtype)


def _gmlp(proj, ln_g, ln_b, w_spatial, b_spatial):
    return pl.pallas_call(
        _gmlp_kernel,
        grid=(SEQ // CHUNK,),
        in_specs=[
            pl.BlockSpec((CHUNK, D_GMLP), lambda n: (n, 0)),
            pl.BlockSpec((CHUNK, D_GMLP), lambda n: (n, 1)),
            pl.BlockSpec((GMLP_GROUPS, GROUP_DIM), lambda n: (0, 0)),
            pl.BlockSpec((GMLP_GROUPS, GROUP_DIM), lambda n: (0, 0)),
            pl.BlockSpec((GMLP_GROUPS, CHUNK, CHUNK), lambda n: (0, 0, 0)),
            pl.BlockSpec((CHUNK, GMLP_GROUPS), lambda n: (0, 0)),
        ],
        out_specs=pl.BlockSpec((CHUNK, D_GMLP), lambda n: (n, 0)),
        out_shape=jax.ShapeDtypeStruct((SEQ, D_GMLP), jnp.bfloat16),
        compiler_params=_params("arbitrary"),
        name="gmlp",
    )(proj, proj, ln_g, ln_b, w_spatial, b_spatial.T)


def _attn_kernel(q_ref, k_ref, v_ref, o_ref, acc_ref, m_ref, l_ref):
    head = pl.program_id(0)
    span = pl.program_id(1)
    head_no = (jnp.zeros((BLK, BLK), jnp.int32) + (head + 1)).astype(jnp.float32)
    slope = jnp.exp2(head_no * (-8.0 / N_HEADS))
    scale = HEAD_DIM ** -0.5
    qi = lax.broadcasted_iota(jnp.int32, (BLK, BLK), 0)
    ki = lax.broadcasted_iota(jnp.int32, (BLK, BLK), 1)
    contract_last = (((1,), (1,)), ((), ()))

    for b, (window, d) in enumerate(BRANCHES):
        assert window // d == BLK
        step_cur = (qi - ki).astype(jnp.float32)
        bias_cur = jnp.where(qi >= ki, -slope * d * step_cur, NEG)
        bias_prev = jnp.where(ki >= qi, -slope * d * (step_cur + BLK), NEG)

        def tile(t, carry, d=d, b=b, bias_cur=bias_cur, bias_prev=bias_prev):
            r = t % d
            n = t // d
            q0 = n * (BLK * d) + r
            k0 = span * SPAN + q0
            first = k0 < BLK * d
            kp = jnp.where(first, k0, k0 - BLK * d)
            q = (q_ref[pl.ds(q0, BLK, stride=d), :] * scale).astype(jnp.bfloat16)
            kc = k_ref[pl.ds(k0, BLK, stride=d), :].astype(jnp.bfloat16)
            kpv = k_ref[pl.ds(kp, BLK, stride=d), :].astype(jnp.bfloat16)
            vc = v_ref[pl.ds(k0, BLK, stride=d), :].astype(jnp.bfloat16)
            vp = v_ref[pl.ds(kp, BLK, stride=d), :].astype(jnp.bfloat16)
            s_cur = lax.dot_general(q, kc, contract_last, preferred_element_type=jnp.float32) + bias_cur
            s_prev = lax.dot_general(q, kpv, contract_last, preferred_element_type=jnp.float32)
            s_prev = s_prev + jnp.where(first, NEG, bias_prev)
            m = jnp.maximum(jnp.max(s_cur, axis=-1, keepdims=True), jnp.max(s_prev, axis=-1, keepdims=True))
            p_cur = jnp.exp(s_cur - m)
            p_prev = jnp.exp(s_prev - m)
            den = jnp.sum(p_cur, axis=-1, keepdims=True) + jnp.sum(p_prev, axis=-1, keepdims=True)
            acc = jnp.dot(p_cur.astype(jnp.bfloat16), vc, preferred_element_type=jnp.float32)
            acc = acc + jnp.dot(p_prev.astype(jnp.bfloat16), vp, preferred_element_type=jnp.float32)
            rows = pl.ds(q0, BLK, stride=d)
            acc_ref[b, rows, :] = acc
            m_ref[b, rows, :] = jnp.broadcast_to(m, (BLK, LANES))
            l_ref[b, rows, :] = jnp.broadcast_to(den, (BLK, LANES))
            return carry

        lax.fori_loop(0, SPAN // BLK, tile, 0)

    m_all = jnp.maximum(jnp.maximum(m_ref[0], m_ref[1]), m_ref[2])
    num = jnp.zeros((SPAN, HEAD_DIM), jnp.float32)
    den = jnp.zeros((SPAN, LANES), jnp.float32)
    for b in range(len(BRANCHES)):
        w = jnp.exp(m_ref[b] - m_all)
        num = num + w * acc_ref[b]
        den = den + w * l_ref[b]
    o_ref[...] = (num / den).astype(o_ref.dtype)


def _attention(proj):
    q_col = 2 * D_GMLP // HEAD_DIM
    k_col = q_col + N_HEADS
    v_col = k_col + N_HEADS
    nb = len(BRANCHES)
    return pl.pallas_call(
        _attn_kernel,
        grid=(N_HEADS, SEQ // SPAN),
        in_specs=[
            pl.BlockSpec((SPAN, HEAD_DIM), lambda h, s: (s, q_col + h)),
            pl.BlockSpec((SEQ, HEAD_DIM), lambda h, s: (0, k_col + h)),
            pl.BlockSpec((SEQ, HEAD_DIM), lambda h, s: (0, v_col + h)),
        ],
        out_specs=pl.BlockSpec((SPAN, HEAD_DIM), lambda h, s: (s, h)),
        out_shape=jax.ShapeDtypeStruct((SEQ, D_ATTN), jnp.bfloat16),
        scratch_shapes=[
            pltpu.VMEM((nb, SPAN, HEAD_DIM), jnp.float32),
            pltpu.VMEM((nb, SPAN, LANES), jnp.float32),
            pltpu.VMEM((nb, SPAN, LANES), jnp.float32),
        ],
        compiler_params=_params("arbitrary", "arbitrary"),
        name="attn",
    )(proj, proj, proj)


def _split_bf16(x):
    hi = x.astype(jnp.bfloat16)
    lo = (x - hi.astype(jnp.float32)).astype(jnp.bfloat16)
    return hi, lo


def _out_proj_kernel(ya_ref, yb_ref, wa_ref, wb_ref, x_ref, gate_ref, g_ref, b_ref, sc_ref, sh_ref,
                     wr_ref, br_ref, x1_ref, h_ref, lg_ref):
    mix = jnp.dot(ya_ref[...], wa_ref[...], preferred_element_type=jnp.float32)
    mix = mix + jnp.dot(yb_ref[...], wb_ref[...], preferred_element_type=jnp.float32)
    x1 = _layer_norm(DEEPNORM_ALPHA * x_ref[...] + gate_ref[...] * mix, g_ref[...], b_ref[...])
    x1_ref[...] = x1
    h = x1 * (1.0 + sc_ref[...]) + sh_ref[...]
    h_ref[...] = h
    h_hi, h_lo = _split_bf16(h)
    w_hi, w_lo = _split_bf16(wr_ref[...])
    lg = jnp.dot(h_hi, w_hi, preferred_element_type=jnp.float32)
    lg = lg + jnp.dot(h_hi, w_lo, preferred_element_type=jnp.float32)
    lg = lg + jnp.dot(h_lo, w_hi, preferred_element_type=jnp.float32)
    lg_ref[...] = lg + br_ref[...]


def _out_proj(y_a, y_b, w_o_bf16, x, gate1, ln_g, ln_b, scale2, shift2, w_router, b_router):
    row = lambda i: (i, 0)
    fixed = lambda i: (0, 0)
    vec = pl.BlockSpec((1, D_MODEL), fixed)
    return pl.pallas_call(
        _out_proj_kernel,
        grid=(SEQ // OUT_TM,),
        in_specs=[
            pl.BlockSpec((OUT_TM, D_GMLP), row),
            pl.BlockSpec((OUT_TM, D_ATTN), row),
            pl.BlockSpec((D_GMLP, D_MODEL), lambda i: (0, 0)),
            pl.BlockSpec((D_ATTN, D_MODEL), lambda i: (1, 0)),
            pl.BlockSpec((OUT_TM, D_MODEL), row),
            vec, vec, vec, vec, vec,
            pl.BlockSpec((D_MODEL, N_EXPERTS), fixed),
            pl.BlockSpec((1, N_EXPERTS), fixed),
        ],
        out_specs=[
            pl.BlockSpec((OUT_TM, D_MODEL), row),
            pl.BlockSpec((OUT_TM, D_MODEL), row),
            pl.BlockSpec((OUT_TM, N_EXPERTS), row),
        ],
        out_shape=[
            jax.ShapeDtypeStruct((SEQ, D_MODEL), jnp.float32),
            jax.ShapeDtypeStruct((SEQ, D_MODEL), jnp.float32),
            jax.ShapeDtypeStruct((SEQ, N_EXPERTS), jnp.float32),
        ],
        compiler_params=_params("arbitrary"),
        name="out_proj",
    )(y_a, y_b, w_o_bf16, w_o_bf16, x, gate1, ln_g, ln_b, scale2, shift2, w_router, b_router)


def _gather_kernel(n_used_ref, tok_ref, tok_next_ref, h_hbm, o_ref, buf_ref, sem_ref):
    i = pl.program_id(0)
    n_used = n_used_ref[0]
    slot = i % 2

    def start(idx_ref, s):
        def issue(r, carry):
            pltpu.make_async_copy(h_hbm.at[pl.ds(idx_ref[0, 0, r], 1), :],
                                  buf_ref.at[s, pl.ds(r, 1), :], sem_ref.at[s]).start()
            return carry
        lax.fori_loop(0, GATHER_ROWS, issue, 0)

    @pl.when((i == 0) & (n_used > 0))
    def _():
        start(tok_ref, slot)

    @pl.when(i + 1 < n_used)
    def _():
        start(tok_next_ref, 1 - slot)

    @pl.when(i < n_used)
    def _():
        pltpu.make_async_copy(buf_ref.at[slot], buf_ref.at[slot], sem_ref.at[slot]).wait()
        o_ref[...] = buf_ref[slot].astype(o_ref.dtype)

    @pl.when(i >= n_used)
    def _():
        o_ref[...] = jnp.zeros_like(o_ref)


def _gather_rows(h, buf_tok, n_used_blocks):
    n_blocks = MOE_CAP // GATHER_ROWS
    tok = buf_tok.reshape(n_blocks, 1, GATHER_ROWS)
    smem = functools.partial(pl.BlockSpec, (1, 1, GATHER_ROWS), memory_space=pltpu.SMEM)
    return pl.pallas_call(
        _gather_kernel,
        grid_spec=pltpu.PrefetchScalarGridSpec(
            num_scalar_prefetch=1,
            grid=(n_blocks,),
            in_specs=[
                smem(lambda i, n: (i, 0, 0)),
                smem(lambda i, n: (jnp.minimum(i + 1, n_blocks - 1), 0, 0)),
                pl.BlockSpec(memory_space=pl.ANY),
            ],
            out_specs=pl.BlockSpec((GATHER_ROWS, D_MODEL), lambda i, n: (i, 0)),
            scratch_shapes=[
                pltpu.VMEM((2, GATHER_ROWS, D_MODEL), jnp.float32),
                pltpu.SemaphoreType.DMA((2,)),
            ],
        ),
        out_shape=jax.ShapeDtypeStruct((MOE_CAP, D_MODEL), jnp.bfloat16),
        compiler_params=_params("arbitrary"),
        name="gather",
    )(n_used_blocks, tok, tok, h)


def _experts_kernel(meta_ref, x_ref, wg_ref, wl_ref, bg_ref, bl_ref, w2_ref, b2_ref, o_ref):
    i = pl.program_id(0)
    j = pl.program_id(1)
    used = i < meta_ref[MOE_TILES]

    @pl.when(used & (j == 0))
    def _():
        o_ref[...] = jnp.broadcast_to(b2_ref[...], o_ref.shape)

    @pl.when(jnp.logical_not(used) & (j == 0))
    def _():
        o_ref[...] = jnp.zeros_like(o_ref)

    @pl.when(used)
    def _():
        x = x_ref[...]
        glu = jnp.dot(x, wg_ref[...].astype(jnp.bfloat16), preferred_element_type=jnp.float32) + bg_ref[...]
        lin = jnp.dot(x, wl_ref[...].astype(jnp.bfloat16), preferred_element_type=jnp.float32) + bl_ref[...]
        glu = jnp.minimum(glu, SWIGLU_LIMIT)
        lin = jnp.clip(lin, -SWIGLU_LIMIT, SWIGLU_LIMIT)
        act = glu * jax.nn.sigmoid(SWIGLU_ALPHA * glu) * (lin + 1.0)
        o_ref[...] += jnp.dot(act.astype(jnp.bfloat16), w2_ref[...].astype(jnp.bfloat16),
                              preferred_element_type=jnp.float32)


def _experts(x_buf, meta, w1, b1, w2, b2):
    n_j = D_EXPERT // MOE_TN

    def tile(i, m):
        return jnp.minimum(i, jnp.maximum(m[MOE_TILES] - 1, 0))

    def col(i, j, m):
        return jnp.where(i < m[MOE_TILES], j, n_j - 1)

    def expert(i, m):
        return m[tile(i, m)]

    return pl.pallas_call(
        _experts_kernel,
        grid_spec=pltpu.PrefetchScalarGridSpec(
            num_scalar_prefetch=1,
            grid=(MOE_TILES, n_j),
            in_specs=[
                pl.BlockSpec((MOE_TM, D_MODEL), lambda i, j, m: (tile(i, m), 0)),
                pl.BlockSpec((None, D_MODEL, MOE_TN), lambda i, j, m: (expert(i, m), 0, col(i, j, m))),
                pl.BlockSpec((None, D_MODEL, MOE_TN), lambda i, j, m: (expert(i, m), 0, n_j + col(i, j, m))),
                pl.BlockSpec((None, 1, MOE_TN), lambda i, j, m: (expert(i, m), 0, col(i, j, m))),
                pl.BlockSpec((None, 1, MOE_TN), lambda i, j, m: (expert(i, m), 0, n_j + col(i, j, m))),
                pl.BlockSpec((None, MOE_TN, D_MODEL), lambda i, j, m: (expert(i, m), col(i, j, m), 0)),
                pl.BlockSpec((None, 1, D_MODEL), lambda i, j, m: (expert(i, m), 0, 0)),
            ],
            out_specs=pl.BlockSpec((MOE_TM, D_MODEL), lambda i, j, m: (i, 0)),
        ),
        out_shape=jax.ShapeDtypeStruct((MOE_CAP, D_MODEL), jnp.float32),
        compiler_params=_params("arbitrary", "arbitrary"),
        name="experts",
    )(meta, x_buf, w1, w1, b1.reshape(N_EXPERTS, 1, 2 * D_EXPERT), b1.reshape(N_EXPERTS, 1, 2 * D_EXPERT),
      w2, b2.reshape(N_EXPERTS, 1, D_MODEL))


def _combine_kernel(pos_ref, pos_next_ref, y_hbm, gates_ref, x1_ref, gate2_ref, g_ref, b_ref, o_ref,
                    buf_ref, sem_ref):
    i = pl.program_id(0)
    n = pl.num_programs(0)
    slot = i % 2
    rows = COMBINE_TM * TOP_K

    def start(idx_ref, s):
        def issue(r, carry):
            pltpu.make_async_copy(y_hbm.at[pl.ds(idx_ref[0, 0, r], 1), :],
                                  buf_ref.at[s, pl.ds(r, 1), :], sem_ref.at[s]).start()
            return carry
        lax.fori_loop(0, rows, issue, 0)

    @pl.when(i == 0)
    def _():
        start(pos_ref, slot)

    @pl.when(i + 1 < n)
    def _():
        start(pos_next_ref, 1 - slot)

    pltpu.make_async_copy(buf_ref.at[slot], buf_ref.at[slot], sem_ref.at[slot]).wait()
    y = jnp.zeros((COMBINE_TM, D_MODEL), jnp.float32)
    for k in range(TOP_K):
        y = y + gates_ref[:, k:k + 1] * buf_ref[slot, k * COMBINE_TM:(k + 1) * COMBINE_TM, :]
    o_ref[...] = _layer_norm(DEEPNORM_ALPHA * x1_ref[...] + gate2_ref[...] * y, g_ref[...], b_ref[...])


def _combine(y_sorted, pos, gates, x1, gate2, ln_g, ln_b):
    n_tiles = SEQ // COMBINE_TM
    rows = COMBINE_TM * TOP_K
    pos_t = pos.reshape(n_tiles, COMBINE_TM, TOP_K).transpose(0, 2, 1).reshape(n_tiles, 1, rows)
    smem = functools.partial(pl.BlockSpec, (1, 1, rows), memory_space=pltpu.SMEM)
    row = lambda i: (i, 0)
    vec = pl.BlockSpec((1, D_MODEL), lambda i: (0, 0))
    return pl.pallas_call(
        _combine_kernel,
        grid=(n_tiles,),
        in_specs=[
            smem(lambda i: (i, 0, 0)),
            smem(lambda i: (jnp.minimum(i + 1, n_tiles - 1), 0, 0)),
            pl.BlockSpec(memory_space=pl.ANY),
            pl.BlockSpec((COMBINE_TM, TOP_K), row),
            pl.BlockSpec((COMBINE_TM, D_MODEL), row),
            vec, vec, vec,
        ],
        out_specs=pl.BlockSpec((COMBINE_TM, D_MODEL), row),
        out_shape=jax.ShapeDtypeStruct((SEQ, D_MODEL), jnp.float32),
        scratch_shapes=[
            pltpu.VMEM((2, rows, D_MODEL), jnp.float32),
            pltpu.SemaphoreType.DMA((2,)),
        ],
        compiler_params=_params("arbitrary"),
        name="combine",
    )(pos_t, pos_t, y_sorted, gates, x1, gate2, ln_g, ln_b)


def _route(logits):
    top_val, top_idx = lax.top_k(logits, TOP_K)
    gates = jax.nn.softmax(top_val, axis=-1)
    e_flat = top_idx.reshape(-1).astype(jnp.int32)
    nk = e_flat.shape[0]
    one_hot = (e_flat[:, None] == jnp.arange(N_EXPERTS, dtype=jnp.int32)[None, :]).astype(jnp.int32)
    running = jnp.cumsum(one_hot, axis=0)
    rank = jnp.take_along_axis(running, e_flat[:, None], axis=1)[:, 0] - 1
    counts = running[-1]
    padded = (counts + MOE_TM - 1) // MOE_TM * MOE_TM
    pend = jnp.cumsum(padded)
    pstart = pend - padded
    dest = pstart[e_flat] + rank
    tok_flat = jnp.arange(nk, dtype=jnp.int32) // TOP_K
    buf_tok = jnp.zeros((MOE_CAP,), jnp.int32).at[dest].set(tok_flat)
    tile_e = jnp.minimum(
        jnp.searchsorted(pend, jnp.arange(MOE_TILES, dtype=jnp.int32) * MOE_TM, side='right'),
        N_EXPERTS - 1).astype(jnp.int32)
    n_used = (pend[-1] // MOE_TM).astype(jnp.int32)
    meta = jnp.concatenate([tile_e, n_used[None]])
    return gates, dest.reshape(-1, TOP_K), buf_tok, meta, n_used


def kernel(x, c, w_ada, b_ada, w_in, sgu_ln_g, sgu_ln_b, w_spatial, b_spatial, w_o, ln1_g, ln1_b,
           w_router, b_router, w_exp1, b_exp1, w_exp2, b_exp2, ln2_g, ln2_b):
    depth = w_ada.shape[0]
    xs = x.reshape(SEQ, D_MODEL)
    for l in range(depth):
        ada = _ada(c, w_ada[l], b_ada[l])
        shift1, scale1, gate1, shift2, scale2, gate2 = jnp.split(ada, 6, axis=-1)

        proj = _in_proj(xs, scale1, shift1, w_in[l].astype(jnp.bfloat16))
        y_a = _gmlp(proj, sgu_ln_g[l], sgu_ln_b[l], w_spatial[l], b_spatial[l])
        y_b = _attention(proj)
        x1, h, logits = _out_proj(y_a, y_b, w_o[l].astype(jnp.bfloat16), xs, gate1,
                                  ln1_g[l].reshape(1, -1), ln1_b[l].reshape(1, -1), scale2, shift2,
                                  w_router[l], b_router[l].reshape(1, -1))

        gates, pos, buf_tok, meta, n_used = _route(logits)
        x_buf = _gather_rows(h, buf_tok, (n_used * (MOE_TM // GATHER_ROWS))[None])
        y_sorted = _experts(x_buf, meta, w_exp1[l], b_exp1[l], w_exp2[l], b_exp2[l])
        xs = _combine(y_sorted, pos, gates, x1, gate2, ln2_g[l].reshape(1, -1), ln2_b[l].reshape(1, -1))
    return xs.reshape(x.shape)
```

```python
import math

import jax
import jax.numpy as jnp
from jax import lax
from jax.experimental import pallas as pl
from jax.experimental.pallas import tpu as pltpu

D_MODEL = 2048
SEQ = 8192
D_GMLP = 1024
GMLP_GROUPS = 8
GROUP_DIM = 128
CHUNK = 128
D_ATTN = 1024
HEAD_DIM = 128
N_HEADS = 8
BRANCHES = ((128, 1), (512, 4), (2048, 16))
BLK = 128
SPAN = 16 * BLK
D_IN_PROJ = 2 * D_GMLP + 3 * D_ATTN
N_EXPERTS = 32
TOP_K = 4
D_EXPERT = 2048
SWIGLU_LIMIT = 7.0
SWIGLU_ALPHA = 1.702
LN_EPS = 1e-5
DEEPNORM_ALPHA = 2.0 ** 0.25
NEG = -1e30

LANES = 128
VMEM_LIMIT = 56 * 1024 * 1024

ADA_TN = 1536
PROJ_TM = 1024
PROJ_TN = 1024
OUT_TM = 512
ROW_SUB = 8
D_PACK = D_MODEL // 2
MOE_G = 128
MOE_RMAX = 1280
MOE_ITEMS = 64
MOE_TN = 256
MOE_CHUNKS = D_EXPERT // MOE_TN
MOE_SUBTILES = (512, 256, 128)
Y_ROWS = SEQ * TOP_K + 2 * MOE_RMAX
COMBINE_TM = 256


def _params(*sem):
    return pltpu.CompilerParams(dimension_semantics=sem, vmem_limit_bytes=VMEM_LIMIT)


def _layer_norm(x, g, b):
    mu = jnp.mean(x, axis=-1, keepdims=True)
    xc = x - mu
    var = jnp.mean(xc * xc, axis=-1, keepdims=True)
    return xc * lax.rsqrt(var + LN_EPS) * g + b


def _pack_rows(x):
    r = x.astype(jnp.bfloat16).astype(jnp.float32)
    bits = lax.bitcast_convert_type(r, jnp.uint32)
    return (bits[:, D_PACK:] & jnp.uint32(0xFFFF0000)) | (bits[:, :D_PACK] >> 16)


def _unpack_words(u):
    lo = lax.bitcast_convert_type(u << 16, jnp.float32)
    hi = lax.bitcast_convert_type(u & jnp.uint32(0xFFFF0000), jnp.float32)
    return lo, hi


def _store_packed(ref, lead, row0, packed):
    m = packed.shape[0]
    for c in range(ROW_SUB):
        rows = pl.ds(row0 * ROW_SUB + c, m, stride=ROW_SUB)
        ref[lead + (rows, slice(None))] = packed[:, c * LANES:(c + 1) * LANES]


def _load_packed(ref, lead, row0, m):
    cols = [ref[lead + (pl.ds(row0 * ROW_SUB + c, m, stride=ROW_SUB), slice(None))] for c in range(ROW_SUB)]
    return jnp.concatenate(cols, axis=1)


def _ada_kernel(c_ref, w_ref, b_ref, o_ref):
    c = c_ref[...]
    s = c * jax.nn.sigmoid(c)
    o_ref[...] = jnp.sum(s * w_ref[...], axis=0, keepdims=True) + b_ref[...]


def _ada(c, w_ada, b_ada):
    n = w_ada.shape[1]
    return pl.pallas_call(
        _ada_kernel,
        grid=(n // ADA_TN,),
        in_specs=[
            pl.BlockSpec((D_MODEL, 1), lambda j: (0, 0)),
            pl.BlockSpec((D_MODEL, ADA_TN), lambda j: (0, j)),
            pl.BlockSpec((1, ADA_TN), lambda j: (0, j)),
        ],
        out_specs=pl.BlockSpec((1, ADA_TN), lambda j: (0, j)),
        out_shape=jax.ShapeDtypeStruct((1, n), jnp.float32),
        compiler_params=_params("arbitrary"),
        name="ada",
    )(c.reshape(D_MODEL, 1), w_ada, b_ada.reshape(1, n))


def _in_proj_kernel(x_ref, sc_ref, sh_ref, w_ref, o_ref, h_ref):
    @pl.when(pl.program_id(1) == 0)
    def _():
        h_ref[...] = (x_ref[...] * (1.0 + sc_ref[...]) + sh_ref[...]).astype(jnp.bfloat16)

    o_ref[...] = jnp.dot(h_ref[...], w_ref[...], preferred_element_type=jnp.float32)


def _in_proj(x, scale1, shift1, w_in_bf16):
    return pl.pallas_call(
        _in_proj_kernel,
        grid=(SEQ // PROJ_TM, D_IN_PROJ // PROJ_TN),
        in_specs=[
            pl.BlockSpec((PROJ_TM, D_MODEL), lambda i, j: (i, 0)),
            pl.BlockSpec((1, D_MODEL), lambda i, j: (0, 0)),
            pl.BlockSpec((1, D_MODEL), lambda i, j: (0, 0)),
            pl.BlockSpec((D_MODEL, PROJ_TN), lambda i, j: (0, j)),
        ],
        out_specs=pl.BlockSpec((PROJ_TM, PROJ_TN), lambda i, j: (i, j)),
        out_shape=jax.ShapeDtypeStruct((SEQ, D_IN_PROJ), jnp.float32),
        scratch_shapes=[pltpu.VMEM((PROJ_TM, D_MODEL), jnp.bfloat16)],
        compiler_params=_params("arbitrary", "arbitrary"),
        name="in_proj",
    )(x, scale1, shift1, w_in_bf16)


def _gelu(x):
    return 0.5 * x * (1.0 + lax.erf(x * (1.0 / math.sqrt(2.0))))


def _gmlp_kernel(ua_ref, va_ref, g_ref, b_ref, w_ref, bs_ref, o_ref):
    row = lax.broadcasted_iota(jnp.int32, (CHUNK, CHUNK), 0)
    col = lax.broadcasted_iota(jnp.int32, (CHUNK, CHUNK), 1)
    causal = col <= row
    for g in range(GMLP_GROUPS):
        sl = slice(g * GROUP_DIM, (g + 1) * GROUP_DIM)
        v = _layer_norm(_gelu(va_ref[:, sl]), g_ref[g:g + 1, :], b_ref[g:g + 1, :])
        w = jnp.where(causal, w_ref[g], 0.0).astype(jnp.bfloat16)
        s = jnp.dot(w, v.astype(jnp.bfloat16), preferred_element_type=jnp.float32)
        s = s + bs_ref[:, g:g + 1]
        o_ref[:, sl] = (_gelu(ua_ref[:, sl]) * s).astype(o_ref.dtype)


def _gmlp(proj, ln_g, ln_b, w_spatial, b_spatial):
    return pl.pallas_call(
        _gmlp_kernel,
        grid=(SEQ // CHUNK,),
        in_specs=[
            pl.BlockSpec((CHUNK, D_GMLP), lambda n: (n, 0)),
            pl.BlockSpec((CHUNK, D_GMLP), lambda n: (n, 1)),
            pl.BlockSpec((GMLP_GROUPS, GROUP_DIM), lambda n: (0, 0)),
            pl.BlockSpec((GMLP_GROUPS, GROUP_DIM), lambda n: (0, 0)),
            pl.BlockSpec((GMLP_GROUPS, CHUNK, CHUNK), lambda n: (0, 0, 0)),
            pl.BlockSpec((CHUNK, GMLP_GROUPS), lambda n: (0, 0)),
        ],
        out_specs=pl.BlockSpec((CHUNK, D_GMLP), lambda n: (n, 0)),
        out_shape=jax.ShapeDtypeStruct((SEQ, D_GMLP), jnp.bfloat16),
        compiler_params=_params("arbitrary"),
        name="gmlp",
    )(proj, proj, ln_g, ln_b, w_spatial, b_spatial.T)


def _attn_kernel(q_ref, k_ref, v_ref, o_ref, acc_ref, m_ref, l_ref):
    head = pl.program_id(0)
    span = pl.program_id(1)
    head_no = (jnp.zeros((BLK, BLK), jnp.int32) + (head + 1)).astype(jnp.float32)
    slope = jnp.exp2(head_no * (-8.0 / N_HEADS))
    scale = HEAD_DIM ** -0.5
    qi = lax.broadcasted_iota(jnp.int32, (BLK, BLK), 0)
    ki = lax.broadcasted_iota(jnp.int32, (BLK, BLK), 1)
    contract_last = (((1,), (1,)), ((), ()))

    for b, (window, d) in enumerate(BRANCHES):
        assert window // d == BLK
        step_cur = (qi - ki).astype(jnp.float32)
        bias_cur = jnp.where(qi >= ki, -slope * d * step_cur, NEG)
        bias_prev = jnp.where(ki >= qi, -slope * d * (step_cur + BLK), NEG)

        def tile(t, carry, d=d, b=b, bias_cur=bias_cur, bias_prev=bias_prev):
            r = t % d
            n = t // d
            q0 = n * (BLK * d) + r
            k0 = span * SPAN + q0
            first = k0 < BLK * d
            kp = jnp.where(first, k0, k0 - BLK * d)
            q = (q_ref[pl.ds(q0, BLK, stride=d), :] * scale).astype(jnp.bfloat16)
            kc = k_ref[pl.ds(k0, BLK, stride=d), :].astype(jnp.bfloat16)
            kpv = k_ref[pl.ds(kp, BLK, stride=d), :].astype(jnp.bfloat16)
            vc = v_ref[pl.ds(k0, BLK, stride=d), :].astype(jnp.bfloat16)
            vp = v_ref[pl.ds(kp, BLK, stride=d), :].astype(jnp.bfloat16)
            s_cur = lax.dot_general(q, kc, contract_last, preferred_element_type=jnp.float32) + bias_cur
            s_prev = lax.dot_general(q, kpv, contract_last, preferred_element_type=jnp.float32)
            s_prev = s_prev + jnp.where(first, NEG, bias_prev)
            m = jnp.maximum(jnp.max(s_cur, axis=-1, keepdims=True), jnp.max(s_prev, axis=-1, keepdims=True))
            p_cur = jnp.exp(s_cur - m)
            p_prev = jnp.exp(s_prev - m)
            den = jnp.sum(p_cur, axis=-1, keepdims=True) + jnp.sum(p_prev, axis=-1, keepdims=True)
            acc = jnp.dot(p_cur.astype(jnp.bfloat16), vc, preferred_element_type=jnp.float32)
            acc = acc + jnp.dot(p_prev.astype(jnp.bfloat16), vp, preferred_element_type=jnp.float32)
            rows = pl.ds(q0, BLK, stride=d)
            acc_ref[b, rows, :] = acc
            m_ref[b, rows, :] = jnp.broadcast_to(m, (BLK, LANES))
            l_ref[b, rows, :] = jnp.broadcast_to(den, (BLK, LANES))
            return carry

        lax.fori_loop(0, SPAN // BLK, tile, 0)

    m_all = jnp.maximum(jnp.maximum(m_ref[0], m_ref[1]), m_ref[2])
    num = jnp.zeros((SPAN, HEAD_DIM), jnp.float32)
    den = jnp.zeros((SPAN, LANES), jnp.float32)
    for b in range(len(BRANCHES)):
        w = jnp.exp(m_ref[b] - m_all)
        num = num + w * acc_ref[b]
        den = den + w * l_ref[b]
    o_ref[...] = (num / den).astype(o_ref.dtype)


def _attention(proj):
    q_col = 2 * D_GMLP // HEAD_DIM
    k_col = q_col + N_HEADS
    v_col = k_col + N_HEADS
    nb = len(BRANCHES)
    return pl.pallas_call(
        _attn_kernel,
        grid=(N_HEADS, SEQ // SPAN),
        in_specs=[
            pl.BlockSpec((SPAN, HEAD_DIM), lambda h, s: (s, q_col + h)),
            pl.BlockSpec((SEQ, HEAD_DIM), lambda h, s: (0, k_col + h)),
            pl.BlockSpec((SEQ, HEAD_DIM), lambda h, s: (0, v_col + h)),
        ],
        out_specs=pl.BlockSpec((SPAN, HEAD_DIM), lambda h, s: (s, h)),
        out_shape=jax.ShapeDtypeStruct((SEQ, D_ATTN), jnp.bfloat16),
        scratch_shapes=[
            pltpu.VMEM((nb, SPAN, HEAD_DIM), jnp.float32),
            pltpu.VMEM((nb, SPAN, LANES), jnp.float32),
            pltpu.VMEM((nb, SPAN, LANES), jnp.float32),
        ],
        compiler_params=_params("arbitrary", "arbitrary"),
        name="attn",
    )(proj, proj, proj)


def _split_bf16(x):
    hi = x.astype(jnp.bfloat16)
    lo = (x - hi.astype(jnp.float32)).astype(jnp.bfloat16)
    return hi, lo


def _out_proj_kernel(ya_ref, yb_ref, wa_ref, wb_ref, x_ref, gate_ref, g_ref, b_ref, sc_ref, sh_ref,
                     wr_ref, br_ref, x1_ref, hp_ref, lg_ref):
    mix = jnp.dot(ya_ref[...], wa_ref[...], preferred_element_type=jnp.float32)
    mix = mix + jnp.dot(yb_ref[...], wb_ref[...], preferred_element_type=jnp.float32)
    x1 = _layer_norm(DEEPNORM_ALPHA * x_ref[...] + gate_ref[...] * mix, g_ref[...], b_ref[...])
    x1_ref[...] = x1
    h = x1 * (1.0 + sc_ref[...]) + sh_ref[...]
    _store_packed(hp_ref, (), 0, _pack_rows(h))
    h_hi, h_lo = _split_bf16(h)
    w_hi, w_lo = _split_bf16(wr_ref[...])
    lg = jnp.dot(h_hi, w_hi, preferred_element_type=jnp.float32)
    lg = lg + jnp.dot(h_hi, w_lo, preferred_element_type=jnp.float32)
    lg = lg + jnp.dot(h_lo, w_hi, preferred_element_type=jnp.float32)
    lg_ref[...] = lg + br_ref[...]


def _out_proj(y_a, y_b, w_o_bf16, x, gate1, ln_g, ln_b, scale2, shift2, w_router, b_router):
    row = lambda i: (i, 0)
    fixed = lambda i: (0, 0)
    vec = pl.BlockSpec((1, D_MODEL), fixed)
    return pl.pallas_call(
        _out_proj_kernel,
        grid=(SEQ // OUT_TM,),
        in_specs=[
            pl.BlockSpec((OUT_TM, D_GMLP), row),
            pl.BlockSpec((OUT_TM, D_ATTN), row),
            pl.BlockSpec((D_GMLP, D_MODEL), lambda i: (0, 0)),
            pl.BlockSpec((D_ATTN, D_MODEL), lambda i: (1, 0)),
            pl.BlockSpec((OUT_TM, D_MODEL), row),
            vec, vec, vec, vec, vec,
            pl.BlockSpec((D_MODEL, N_EXPERTS), fixed),
            pl.BlockSpec((1, N_EXPERTS), fixed),
        ],
        out_specs=[
            pl.BlockSpec((OUT_TM, D_MODEL), row),
            pl.BlockSpec((OUT_TM * ROW_SUB, LANES), row),
            pl.BlockSpec((OUT_TM, N_EXPERTS), row),
        ],
        out_shape=[
            jax.ShapeDtypeStruct((SEQ, D_MODEL), jnp.float32),
            jax.ShapeDtypeStruct((SEQ * ROW_SUB, LANES), jnp.uint32),
            jax.ShapeDtypeStruct((SEQ, N_EXPERTS), jnp.float32),
        ],
        compiler_params=_params("arbitrary"),
        name="out_proj",
    )(y_a, y_b, w_o_bf16, w_o_bf16, x, gate1, ln_g, ln_b, scale2, shift2, w_router, b_router)


def _experts_kernel(meta_ref, tok0_ref, tokn_ref, dstp_ref, h_hbm, wg_ref, wl_ref, bg_ref, bl_ref, w2_ref,
                    b2_ref, y_hbm, xs_ref, os_ref, acc_ref, wg_bf, wl_bf, w2_bf, gsem, ssem):
    i = pl.program_id(0)
    j = pl.program_id(1)
    n_items = pl.num_programs(0)
    last_j = pl.num_programs(1) - 1

    def item_rows(k):
        inside = (k >= 0) & (k < n_items)
        return jnp.where(inside, meta_ref[MOE_ITEMS + jnp.clip(k, 0, n_items - 1)], 0)

    rows = item_rows(i)
    rows_prev = item_rows(i - 1)
    rows_prev2 = item_rows(i - 2)
    rows_next = item_rows(i + 1)
    p = i % 2
    q = 1 - p

    def gather(tab_ref, slot, row):
        src = pl.multiple_of(tab_ref[0, 0, row], ROW_SUB)
        dst = pl.multiple_of(row * ROW_SUB, ROW_SUB)
        pltpu.make_async_copy(h_hbm.at[pl.ds(src, ROW_SUB), :], xs_ref.at[slot, pl.ds(dst, ROW_SUB), :],
                              gsem.at[slot]).start()

    def scatter(slot, row):
        src = pl.multiple_of(row * ROW_SUB, ROW_SUB)
        dst = pl.multiple_of(dstp_ref[0, 0, row], ROW_SUB)
        pltpu.make_async_copy(os_ref.at[slot, pl.ds(src, ROW_SUB), :], y_hbm.at[pl.ds(dst, ROW_SUB), :],
                              ssem.at[slot]).start()

    def wait_rows(buf_ref, sem, slot, n):
        group = buf_ref.at[slot, pl.ds(0, MOE_G * ROW_SUB), :]

        def wait_group(t, c):
            pltpu.make_async_copy(group, group, sem.at[slot]).wait()
            return c
        lax.fori_loop(0, n // MOE_G, wait_group, 0)

    def each_row(lo, hi, fn):
        lax.fori_loop(lo, jnp.maximum(lo, hi), lambda r, c: (fn(r), c)[1], 0)

    @pl.when((i == 0) & (j == 0))
    def _():
        os_ref[...] = jnp.zeros_like(os_ref)
        for half in range(2):
            fill = pltpu.make_async_copy(
                os_ref.at[half],
                y_hbm.at[pl.ds((SEQ * TOP_K + half * MOE_RMAX) * ROW_SUB, MOE_RMAX * ROW_SUB), :],
                ssem.at[half])
            fill.start()
            fill.wait()
        each_row(0, rows, lambda r: gather(tok0_ref, 0, r))

    @pl.when(j == 0)
    def _():
        wait_rows(xs_ref, gsem, p, jnp.where(i == 0, rows, jnp.maximum(rows_prev, rows)))

    def sub_tile(a, m):
        a = pl.multiple_of(a, MOE_G)
        base = (j * rows + a) // ROW_SUB
        for u in range(m // ROW_SUB):
            gather(tokn_ref, q, base + u)
            scatter(q, base + u)
        x_lo, x_hi = _unpack_words(_load_packed(xs_ref, (p,), a, m))
        x_lo = x_lo.astype(jnp.bfloat16)
        x_hi = x_hi.astype(jnp.bfloat16)

        def up(w_bf, b_ref):
            y = jnp.dot(x_lo, w_bf[:D_PACK, :], preferred_element_type=jnp.float32)
            y = y + jnp.dot(x_hi, w_bf[D_PACK:, :], preferred_element_type=jnp.float32)
            return y + b_ref[...]

        glu = jnp.minimum(up(wg_bf, bg_ref), SWIGLU_LIMIT)
        lin = jnp.clip(up(wl_bf, bl_ref), -SWIGLU_LIMIT, SWIGLU_LIMIT)
        act = glu * jax.nn.sigmoid(SWIGLU_ALPHA * glu) * (lin + 1.0)
        acc_ref[pl.ds(a, m), :] += jnp.dot(act.astype(jnp.bfloat16), w2_bf[...],
                                           preferred_element_type=jnp.float32)

    @pl.when(rows > 0)
    def _():
        @pl.when(j == 0)
        def _():
            bias = jnp.broadcast_to(b2_ref[...], (MOE_G, D_MODEL))

            def init(t, c):
                acc_ref[pl.ds(pl.multiple_of(t * MOE_G, MOE_G), MOE_G), :] = bias
                return c
            lax.fori_loop(0, rows // MOE_G, init, 0)

        wg_bf[...] = wg_ref[...].astype(jnp.bfloat16)
        wl_bf[...] = wl_ref[...].astype(jnp.bfloat16)
        w2_bf[...] = w2_ref[...].astype(jnp.bfloat16)

        main = MOE_SUBTILES[0]
        n_main = rows // main
        lax.fori_loop(0, n_main, lambda t, c: (sub_tile(t * main, main), c)[1], 0)
        done = n_main * main
        for m in MOE_SUBTILES[1:]:
            has = ((rows - done) // m) % 2 == 1
            pl.when(has)(lambda done=done, m=m: sub_tile(done, m))
            done = done + jnp.where(has, m, 0)

    @pl.when(j == last_j)
    def _():
        wait_rows(os_ref, ssem, p, jnp.maximum(rows_prev, rows_prev2))

        def pack(t, c):
            a = pl.multiple_of(t * MOE_G, MOE_G)
            _store_packed(os_ref, (p,), a, _pack_rows(acc_ref[pl.ds(a, MOE_G), :]))
            return c
        lax.fori_loop(0, rows // MOE_G, pack, 0)
        each_row(rows, rows_next, lambda r: gather(tokn_ref, q, r))
        each_row(rows, rows_prev, lambda r: scatter(q, r))

    @pl.when((i == n_items - 1) & (j == last_j))
    def _():
        wait_rows(os_ref, ssem, q, rows_prev)


def _experts(h_packed, meta, tok_tab, dst_tab, w1, b1, w2, b2):
    n_j = MOE_CHUNKS

    def col(i, j, m):
        return jnp.where(m[MOE_ITEMS + i] > 0, j, n_j - 1)

    def expert(i, m):
        return m[i]

    tab = lambda f: pl.BlockSpec((1, 1, MOE_RMAX), f, memory_space=pltpu.SMEM)
    return pl.pallas_call(
        _experts_kernel,
        grid_spec=pltpu.PrefetchScalarGridSpec(
            num_scalar_prefetch=1,
            grid=(MOE_ITEMS, n_j),
            in_specs=[
                tab(lambda i, j, m: (0, 0, 0)),
                tab(lambda i, j, m: (jnp.minimum(i + 1, MOE_ITEMS - 1), 0, 0)),
                tab(lambda i, j, m: (i, 0, 0)),
                pl.BlockSpec(memory_space=pl.ANY),
                pl.BlockSpec((None, D_MODEL, MOE_TN), lambda i, j, m: (expert(i, m), 0, col(i, j, m))),
                pl.BlockSpec((None, D_MODEL, MOE_TN), lambda i, j, m: (expert(i, m), 0, n_j + col(i, j, m))),
                pl.BlockSpec((None, 1, MOE_TN), lambda i, j, m: (expert(i, m), 0, col(i, j, m))),
                pl.BlockSpec((None, 1, MOE_TN), lambda i, j, m: (expert(i, m), 0, n_j + col(i, j, m))),
                pl.BlockSpec((None, MOE_TN, D_MODEL), lambda i, j, m: (expert(i, m), col(i, j, m), 0)),
                pl.BlockSpec((None, 1, D_MODEL), lambda i, j, m: (expert(i, m), 0, 0)),
            ],
            out_specs=pl.BlockSpec(memory_space=pl.ANY),
            scratch_shapes=[
                pltpu.VMEM((2, MOE_RMAX * ROW_SUB, LANES), jnp.uint32),
                pltpu.VMEM((2, MOE_RMAX * ROW_SUB, LANES), jnp.uint32),
                pltpu.VMEM((MOE_RMAX, D_MODEL), jnp.float32),
                pltpu.VMEM((D_MODEL, MOE_TN), jnp.bfloat16),
                pltpu.VMEM((D_MODEL, MOE_TN), jnp.bfloat16),
                pltpu.VMEM((MOE_TN, D_MODEL), jnp.bfloat16),
                pltpu.SemaphoreType.DMA((2,)),
                pltpu.SemaphoreType.DMA((2,)),
            ],
        ),
        out_shape=jax.ShapeDtypeStruct((Y_ROWS * ROW_SUB, LANES), jnp.uint32),
        compiler_params=_params("arbitrary", "arbitrary"),
        name="experts",
    )(meta, tok_tab, tok_tab, dst_tab, h_packed, w1, w1,
      b1.reshape(N_EXPERTS, 1, 2 * D_EXPERT), b1.reshape(N_EXPERTS, 1, 2 * D_EXPERT),
      w2, b2.reshape(N_EXPERTS, 1, D_MODEL))


def _combine_kernel(y0_ref, y1_ref, y2_ref, y3_ref, gates_ref, x1_ref, gate2_ref, g_ref, b_ref, o_ref):
    y_lo = jnp.zeros((COMBINE_TM, D_PACK), jnp.float32)
    y_hi = jnp.zeros((COMBINE_TM, D_PACK), jnp.float32)
    for k, y_ref in enumerate((y0_ref, y1_ref, y2_ref, y3_ref)):
        lo, hi = _unpack_words(_load_packed(y_ref, (), 0, COMBINE_TM))
        gate = gates_ref[:, k:k + 1]
        y_lo = y_lo + gate * lo
        y_hi = y_hi + gate * hi
    y = jnp.concatenate([y_lo, y_hi], axis=1)
    o_ref[...] = _layer_norm(DEEPNORM_ALPHA * x1_ref[...] + gate2_ref[...] * y, g_ref[...], b_ref[...])


def _combine(y_packed, gates, x1, gate2, ln_g, ln_b):
    n_tiles = SEQ // COMBINE_TM
    row = lambda i: (i, 0)
    vec = pl.BlockSpec((1, D_MODEL), lambda i: (0, 0))
    slot = lambda k: pl.BlockSpec((COMBINE_TM * ROW_SUB, LANES), lambda i: (k * n_tiles + i, 0))
    return pl.pallas_call(
        _combine_kernel,
        grid=(n_tiles,),
        in_specs=[
            slot(0), slot(1), slot(2), slot(3),
            pl.BlockSpec((COMBINE_TM, TOP_K), row),
            pl.BlockSpec((COMBINE_TM, D_MODEL), row),
            vec, vec, vec,
        ],
        out_specs=pl.BlockSpec((COMBINE_TM, D_MODEL), row),
        out_shape=jax.ShapeDtypeStruct((SEQ, D_MODEL), jnp.float32),
        compiler_params=_params("arbitrary"),
        name="combine",
    )(y_packed, y_packed, y_packed, y_packed, gates, x1, gate2, ln_g, ln_b)


def _route(logits):
    top_val, top_idx = lax.top_k(logits, TOP_K)
    gates = jax.nn.softmax(top_val, axis=-1)
    e_flat = top_idx.reshape(-1).astype(jnp.int32)
    nk = e_flat.shape[0]
    order = jnp.argsort(e_flat, stable=True).astype(jnp.int32)
    experts = jnp.arange(N_EXPERTS, dtype=jnp.int32)
    counts = jnp.sum((e_flat[:, None] == experts[None, :]).astype(jnp.int32), axis=0)
    start = jnp.cumsum(counts) - counts
    groups = (counts + MOE_G - 1) // MOE_G
    per_item = MOE_RMAX // MOE_G
    n_items_e = (groups + per_item - 1) // per_item
    item_end = jnp.cumsum(n_items_e)
    item_start = item_end - n_items_e
    n_items = item_end[-1]

    item = jnp.arange(MOE_ITEMS, dtype=jnp.int32)
    used = item < n_items
    e_item = jnp.minimum(jnp.searchsorted(item_end, item, side='right'), N_EXPERTS - 1).astype(jnp.int32)
    e_last = e_item[jnp.maximum(n_items - 1, 0)]
    part = item - item_start[e_item]
    rows = jnp.where(used, jnp.clip(groups[e_item] - part * per_item, 0, per_item) * MOE_G, 0)
    meta = jnp.concatenate([jnp.where(used, e_item, e_last), rows]).astype(jnp.int32)

    r = jnp.arange(MOE_RMAX, dtype=jnp.int32)
    rank = part[:, None] * MOE_RMAX + r[None, :]
    valid = used[:, None] & (rank < counts[e_item][:, None])
    flat = order[jnp.clip(start[e_item][:, None] + rank, 0, nk - 1)]
    tok = flat // TOP_K
    slot = flat % TOP_K
    tok_tab = jnp.where(valid, tok, 0) * ROW_SUB
    spill_half = (jnp.arange(MOE_ITEMS + 1, dtype=jnp.int32)[:, None] + 1) % 2
    spill = SEQ * TOP_K + spill_half * MOE_RMAX + r[None, :]
    dst = jnp.where(valid, slot * SEQ + tok, spill[1:])
    dst_tab = jnp.concatenate([spill[:1], dst], axis=0) * ROW_SUB
    shape = (-1, 1, MOE_RMAX)
    return gates, meta, tok_tab.astype(jnp.int32).reshape(shape), dst_tab.astype(jnp.int32).reshape(shape)


def kernel(x, c, w_ada, b_ada, w_in, sgu_ln_g, sgu_ln_b, w_spatial, b_spatial, w_o, ln1_g, ln1_b,
           w_router, b_router, w_exp1, b_exp1, w_exp2, b_exp2, ln2_g, ln2_b):
    depth = w_ada.shape[0]
    assert x.shape == (1, SEQ, D_MODEL)
    xs = x.reshape(SEQ, D_MODEL)
    for l in range(depth):
        ada = _ada(c, w_ada[l], b_ada[l])
        shift1, scale1, gate1, shift2, scale2, gate2 = jnp.split(ada, 6, axis=-1)

        proj = _in_proj(xs, scale1, shift1, w_in[l].astype(jnp.bfloat16))
        y_a = _gmlp(proj, sgu_ln_g[l], sgu_ln_b[l], w_spatial[l], b_spatial[l])
        y_b = _attention(proj)
        x1, h_packed, logits = _out_proj(y_a, y_b, w_o[l].astype(jnp.bfloat16), xs, gate1,
                                         ln1_g[l].reshape(1, -1), ln1_b[l].reshape(1, -1), scale2, shift2,
                                         w_router[l], b_router[l].reshape(1, -1))

        gates, meta, tok_tab, dst_tab = _route(logits)
        y_packed = _experts(h_packed, meta, tok_tab, dst_tab, w_exp1[l], b_exp1[l], w_exp2[l], b_exp2[l])
        xs = _combine(y_packed, gates, x1, gate2, ln2_g[l].reshape(1, -1), ln2_b[l].reshape(1, -1))
    return xs.reshape(x.shape)
```

```python
import math

import jax
import jax.numpy as jnp
from jax import lax
from jax.experimental import pallas as pl
from jax.experimental.pallas import tpu as pltpu

D_MODEL = 2048
SEQ = 8192
D_GMLP = 1024
GMLP_GROUPS = 8
GROUP_DIM = 128
CHUNK = 128
D_ATTN = 1024
HEAD_DIM = 128
N_HEADS = 8
BRANCHES = ((128, 1), (512, 4), (2048, 16))
BLK = 128
SPAN = 16 * BLK
D_IN_PROJ = 2 * D_GMLP + 3 * D_ATTN
N_EXPERTS = 32
TOP_K = 4
D_EXPERT = 2048
SWIGLU_LIMIT = 7.0
SWIGLU_ALPHA = 1.702
LN_EPS = 1e-5
DEEPNORM_ALPHA = 2.0 ** 0.25
NEG = -1e30

LANES = 128
VMEM_LIMIT = 56 * 1024 * 1024

ADA_TN = 1536
PROJ_TM = 1024
PROJ_TN = 1024
OUT_TM = 512
ROW_SUB = 8
D_PACK = D_MODEL // 2
MOE_G = 128
MOE_RMAX = 1280
MOE_ITEMS = 64
MOE_TN = 256
MOE_CHUNKS = D_EXPERT // MOE_TN
MOE_SUBTILES = (512, 256, 128)
Y_ROWS = SEQ * TOP_K + 2 * MOE_RMAX
COMBINE_TM = 256


def _params(*sem):
    return pltpu.CompilerParams(dimension_semantics=sem, vmem_limit_bytes=VMEM_LIMIT)


def _layer_norm(x, g, b):
    mu = jnp.mean(x, axis=-1, keepdims=True)
    xc = x - mu
    var = jnp.mean(xc * xc, axis=-1, keepdims=True)
    return xc * lax.rsqrt(var + LN_EPS) * g + b


def _pack_rows(x):
    r = x.astype(jnp.bfloat16).astype(jnp.float32)
    bits = lax.bitcast_convert_type(r, jnp.uint32)
    return (bits[:, D_PACK:] & jnp.uint32(0xFFFF0000)) | (bits[:, :D_PACK] >> 16)


def _unpack_words(u):
    lo = lax.bitcast_convert_type(u << 16, jnp.float32)
    hi = lax.bitcast_convert_type(u & jnp.uint32(0xFFFF0000), jnp.float32)
    return lo, hi


def _store_packed(ref, lead, row0, packed):
    m = packed.shape[0]
    for c in range(ROW_SUB):
        rows = pl.ds(row0 * ROW_SUB + c, m, stride=ROW_SUB)
        ref[lead + (rows, slice(None))] = packed[:, c * LANES:(c + 1) * LANES]


def _load_packed(ref, lead, row0, m):
    cols = [ref[lead + (pl.ds(row0 * ROW_SUB + c, m, stride=ROW_SUB), slice(None))] for c in range(ROW_SUB)]
    return jnp.concatenate(cols, axis=1)


def _ada_kernel(c_ref, w_ref, b_ref, o_ref):
    c = c_ref[...]
    s = c * jax.nn.sigmoid(c)
    o_ref[...] = jnp.sum(s * w_ref[...], axis=0, keepdims=True) + b_ref[...]


def _ada(c, w_ada, b_ada):
    n = w_ada.shape[1]
    return pl.pallas_call(
        _ada_kernel,
        grid=(n // ADA_TN,),
        in_specs=[
            pl.BlockSpec((D_MODEL, 1), lambda j: (0, 0)),
            pl.BlockSpec((D_MODEL, ADA_TN), lambda j: (0, j)),
            pl.BlockSpec((1, ADA_TN), lambda j: (0, j)),
        ],
        out_specs=pl.BlockSpec((1, ADA_TN), lambda j: (0, j)),
        out_shape=jax.ShapeDtypeStruct((1, n), jnp.float32),
        compiler_params=_params("arbitrary"),
        name="ada",
    )(c.reshape(D_MODEL, 1), w_ada, b_ada.reshape(1, n))


def _in_proj_kernel(x_ref, sc_ref, sh_ref, w_ref, o_ref, h_ref):
    @pl.when(pl.program_id(1) == 0)
    def _():
        h_ref[...] = (x_ref[...] * (1.0 + sc_ref[...]) + sh_ref[...]).astype(jnp.bfloat16)

    o_ref[...] = jnp.dot(h_ref[...], w_ref[...], preferred_element_type=jnp.float32)


def _in_proj(x, scale1, shift1, w_in_bf16):
    return pl.pallas_call(
        _in_proj_kernel,
        grid=(SEQ // PROJ_TM, D_IN_PROJ // PROJ_TN),
        in_specs=[
            pl.BlockSpec((PROJ_TM, D_MODEL), lambda i, j: (i, 0)),
            pl.BlockSpec((1, D_MODEL), lambda i, j: (0, 0)),
            pl.BlockSpec((1, D_MODEL), lambda i, j: (0, 0)),
            pl.BlockSpec((D_MODEL, PROJ_TN), lambda i, j: (0, j)),
        ],
        out_specs=pl.BlockSpec((PROJ_TM, PROJ_TN), lambda i, j: (i, j)),
        out_shape=jax.ShapeDtypeStruct((SEQ, D_IN_PROJ), jnp.float32),
        scratch_shapes=[pltpu.VMEM((PROJ_TM, D_MODEL), jnp.bfloat16)],
        compiler_params=_params("arbitrary", "arbitrary"),
        name="in_proj",
    )(x, scale1, shift1, w_in_bf16)


def _gelu(x):
    return 0.5 * x * (1.0 + lax.erf(x * (1.0 / math.sqrt(2.0))))


def _gmlp_kernel(ua_ref, va_ref, g_ref, b_ref, w_ref, bs_ref, o_ref):
    row = lax.broadcasted_iota(jnp.int32, (CHUNK, CHUNK), 0)
    col = lax.broadcasted_iota(jnp.int32, (CHUNK, CHUNK), 1)
    causal = col <= row
    for g in range(GMLP_GROUPS):
        sl = slice(g * GROUP_DIM, (g + 1) * GROUP_DIM)
        v = _layer_norm(_gelu(va_ref[:, sl]), g_ref[g:g + 1, :], b_ref[g:g + 1, :])
        w = jnp.where(causal, w_ref[g], 0.0).astype(jnp.bfloat16)
        s = jnp.dot(w, v.astype(jnp.bfloat16), preferred_element_type=jnp.float32)
        s = s + bs_ref[:, g:g + 1]
        o_ref[:, sl] = (_gelu(ua_ref[:, sl]) * s).astype(o_ref.dtype)


def _gmlp(proj, ln_g, ln_b, w_spatial, b_spatial):
    return pl.pallas_call(
        _gmlp_kernel,
        grid=(SEQ // CHUNK,),
        in_specs=[
            pl.BlockSpec((CHUNK, D_GMLP), lambda n: (n, 0)),
            pl.BlockSpec((CHUNK, D_GMLP), lambda n: (n, 1)),
            pl.BlockSpec((GMLP_GROUPS, GROUP_DIM), lambda n: (0, 0)),
            pl.BlockSpec((GMLP_GROUPS, GROUP_DIM), lambda n: (0, 0)),
            pl.BlockSpec((GMLP_GROUPS, CHUNK, CHUNK), lambda n: (0, 0, 0)),
            pl.BlockSpec((CHUNK, GMLP_GROUPS), lambda n: (0, 0)),
        ],
        out_specs=pl.BlockSpec((CHUNK, D_GMLP), lambda n: (n, 0)),
        out_shape=jax.ShapeDtypeStruct((SEQ, D_GMLP), jnp.bfloat16),
        compiler_params=_params("arbitrary"),
        name="gmlp",
    )(proj, proj, ln_g, ln_b, w_spatial, b_spatial.T)


def _attn_kernel(q_ref, k_ref, v_ref, o_ref, acc_ref, m_ref, l_ref):
    head = pl.program_id(0)
    span = pl.program_id(1)
    head_no = (jnp.zeros((BLK, BLK), jnp.int32) + (head + 1)).astype(jnp.float32)
    slope = jnp.exp2(head_no * (-8.0 / N_HEADS))
    scale = HEAD_DIM ** -0.5
    qi = lax.broadcasted_iota(jnp.int32, (BLK, BLK), 0)
    ki = lax.broadcasted_iota(jnp.int32, (BLK, BLK), 1)
    contract_last = (((1,), (1,)), ((), ()))

    for b, (window, d) in enumerate(BRANCHES):
        assert window // d == BLK
        step_cur = (qi - ki).astype(jnp.float32)
        bias_cur = jnp.where(qi >= ki, -slope * d * step_cur, NEG)
        bias_prev = jnp.where(ki >= qi, -slope * d * (step_cur + BLK), NEG)

        def tile(t, carry, d=d, b=b, bias_cur=bias_cur, bias_prev=bias_prev):
            r = t % d
            n = t // d
            q0 = n * (BLK * d) + r
            k0 = span * SPAN + q0
            first = k0 < BLK * d
            kp = jnp.where(first, k0, k0 - BLK * d)
            q = (q_ref[pl.ds(q0, BLK, stride=d), :] * scale).astype(jnp.bfloat16)
            kc = k_ref[pl.ds(k0, BLK, stride=d), :].astype(jnp.bfloat16)
            kpv = k_ref[pl.ds(kp, BLK, stride=d), :].astype(jnp.bfloat16)
            vc = v_ref[pl.ds(k0, BLK, stride=d), :].astype(jnp.bfloat16)
            vp = v_ref[pl.ds(kp, BLK, stride=d), :].astype(jnp.bfloat16)
            s_cur = lax.dot_general(q, kc, contract_last, preferred_element_type=jnp.float32) + bias_cur
            s_prev = lax.dot_general(q, kpv, contract_last, preferred_element_type=jnp.float32)
            s_prev = s_prev + jnp.where(first, NEG, bias_prev)
            m = jnp.maximum(jnp.max(s_cur, axis=-1, keepdims=True), jnp.max(s_prev, axis=-1, keepdims=True))
            p_cur = jnp.exp(s_cur - m)
            p_prev = jnp.exp(s_prev - m)
            den = jnp.sum(p_cur, axis=-1, keepdims=True) + jnp.sum(p_prev, axis=-1, keepdims=True)
            acc = jnp.dot(p_cur.astype(jnp.bfloat16), vc, preferred_element_type=jnp.float32)
            acc = acc + jnp.dot(p_prev.astype(jnp.bfloat16), vp, preferred_element_type=jnp.float32)
            rows = pl.ds(q0, BLK, stride=d)
            acc_ref[b, rows, :] = acc
            m_ref[b, rows, :] = jnp.broadcast_to(m, (BLK, LANES))
            l_ref[b, rows, :] = jnp.broadcast_to(den, (BLK, LANES))
            return carry

        lax.fori_loop(0, SPAN // BLK, tile, 0)

    m_all = jnp.maximum(jnp.maximum(m_ref[0], m_ref[1]), m_ref[2])
    num = jnp.zeros((SPAN, HEAD_DIM), jnp.float32)
    den = jnp.zeros((SPAN, LANES), jnp.float32)
    for b in range(len(BRANCHES)):
        w = jnp.exp(m_ref[b] - m_all)
        num = num + w * acc_ref[b]
        den = den + w * l_ref[b]
    o_ref[...] = (num / den).astype(o_ref.dtype)


def _attention(proj):
    q_col = 2 * D_GMLP // HEAD_DIM
    k_col = q_col + N_HEADS
    v_col = k_col + N_HEADS
    nb = len(BRANCHES)
    return pl.pallas_call(
        _attn_kernel,
        grid=(N_HEADS, SEQ // SPAN),
        in_specs=[
            pl.BlockSpec((SPAN, HEAD_DIM), lambda h, s: (s, q_col + h)),
            pl.BlockSpec((SEQ, HEAD_DIM), lambda h, s: (0, k_col + h)),
            pl.BlockSpec((SEQ, HEAD_DIM), lambda h, s: (0, v_col + h)),
        ],
        out_specs=pl.BlockSpec((SPAN, HEAD_DIM), lambda h, s: (s, h)),
        out_shape=jax.ShapeDtypeStruct((SEQ, D_ATTN), jnp.bfloat16),
        scratch_shapes=[
            pltpu.VMEM((nb, SPAN, HEAD_DIM), jnp.float32),
            pltpu.VMEM((nb, SPAN, LANES), jnp.float32),
            pltpu.VMEM((nb, SPAN, LANES), jnp.float32),
        ],
        compiler_params=_params("arbitrary", "arbitrary"),
        name="attn",
    )(proj, proj, proj)


def _split_bf16(x):
    hi = x.astype(jnp.bfloat16)
    lo = (x - hi.astype(jnp.float32)).astype(jnp.bfloat16)
    return hi, lo


def _out_proj_kernel(ya_ref, yb_ref, wa_ref, wb_ref, x_ref, gate_ref, g_ref, b_ref, sc_ref, sh_ref,
                     wr_ref, br_ref, x1_ref, hp_ref, lg_ref):
    mix = jnp.dot(ya_ref[...], wa_ref[...], preferred_element_type=jnp.float32)
    mix = mix + jnp.dot(yb_ref[...], wb_ref[...], preferred_element_type=jnp.float32)
    x1 = _layer_norm(DEEPNORM_ALPHA * x_ref[...] + gate_ref[...] * mix, g_ref[...], b_ref[...])
    x1_ref[...] = x1
    h = x1 * (1.0 + sc_ref[...]) + sh_ref[...]
    _store_packed(hp_ref, (), 0, _pack_rows(h))
    h_hi, h_lo = _split_bf16(h)
    w_hi, w_lo = _split_bf16(wr_ref[...])
    lg = jnp.dot(h_hi, w_hi, preferred_element_type=jnp.float32)
    lg = lg + jnp.dot(h_hi, w_lo, preferred_element_type=jnp.float32)
    lg = lg + jnp.dot(h_lo, w_hi, preferred_element_type=jnp.float32)
    lg_ref[...] = lg + br_ref[...]


def _out_proj(y_a, y_b, w_o_bf16, x, gate1, ln_g, ln_b, scale2, shift2, w_router, b_router):
    row = lambda i: (i, 0)
    fixed = lambda i: (0, 0)
    vec = pl.BlockSpec((1, D_MODEL), fixed)
    return pl.pallas_call(
        _out_proj_kernel,
        grid=(SEQ // OUT_TM,),
        in_specs=[
            pl.BlockSpec((OUT_TM, D_GMLP), row),
            pl.BlockSpec((OUT_TM, D_ATTN), row),
            pl.BlockSpec((D_GMLP, D_MODEL), lambda i: (0, 0)),
            pl.BlockSpec((D_ATTN, D_MODEL), lambda i: (1, 0)),
            pl.BlockSpec((OUT_TM, D_MODEL), row),
            vec, vec, vec, vec, vec,
            pl.BlockSpec((D_MODEL, N_EXPERTS), fixed),
            pl.BlockSpec((1, N_EXPERTS), fixed),
        ],
        out_specs=[
            pl.BlockSpec((OUT_TM, D_MODEL), row),
            pl.BlockSpec((OUT_TM * ROW_SUB, LANES), row),
            pl.BlockSpec((OUT_TM, N_EXPERTS), row),
        ],
        out_shape=[
            jax.ShapeDtypeStruct((SEQ, D_MODEL), jnp.float32),
            jax.ShapeDtypeStruct((SEQ * ROW_SUB, LANES), jnp.uint32),
            jax.ShapeDtypeStruct((SEQ, N_EXPERTS), jnp.float32),
        ],
        compiler_params=_params("arbitrary"),
        name="out_proj",
    )(y_a, y_b, w_o_bf16, w_o_bf16, x, gate1, ln_g, ln_b, scale2, shift2, w_router, b_router)


def _experts_kernel(meta_ref, tok0_ref, tokn_ref, dstp_ref, h_hbm, wg_ref, wl_ref, bg_ref, bl_ref, w2_ref,
                    b2_ref, y_hbm, xs_ref, os_ref, acc_ref, gsem, ssem):
    i = pl.program_id(0)
    j = pl.program_id(1)
    n_items = pl.num_programs(0)
    last_j = pl.num_programs(1) - 1

    def item_rows(k):
        inside = (k >= 0) & (k < n_items)
        return jnp.where(inside, meta_ref[MOE_ITEMS + jnp.clip(k, 0, n_items - 1)], 0)

    rows = item_rows(i)
    rows_prev = item_rows(i - 1)
    rows_prev2 = item_rows(i - 2)
    rows_next = item_rows(i + 1)
    p = i % 2
    q = 1 - p

    def gather(tab_ref, slot, row):
        entry = tab_ref[0, 0, row]
        src = pl.multiple_of(entry, ROW_SUB)
        dst = pl.multiple_of(row * ROW_SUB, ROW_SUB)
        pltpu.make_async_copy(h_hbm.at[pl.ds(src, ROW_SUB), :], xs_ref.at[slot, pl.ds(dst, ROW_SUB), :],
                              gsem.at[slot]).start(priority=0)
        return entry

    def scatter(slot, row):
        entry = dstp_ref[0, 0, row]
        src = pl.multiple_of(row * ROW_SUB, ROW_SUB)
        dst = pl.multiple_of(entry, ROW_SUB)
        pltpu.make_async_copy(os_ref.at[slot, pl.ds(src, ROW_SUB), :], y_hbm.at[pl.ds(dst, ROW_SUB), :],
                              ssem.at[slot]).start(priority=1)
        return entry

    def wait_rows(buf_ref, sem, slot, n):
        group = buf_ref.at[slot, pl.ds(0, MOE_G * ROW_SUB), :]

        def wait_group(t, c):
            pltpu.make_async_copy(group, group, sem.at[slot]).wait()
            return c
        lax.fori_loop(0, n // MOE_G, wait_group, 0)

    def each_row(lo, hi, fn):
        lax.fori_loop(lo, jnp.maximum(lo, hi), lambda r, c: (fn(r), c)[1], 0)

    @pl.when((i == 0) & (j == 0))
    def _():
        os_ref[...] = jnp.zeros_like(os_ref)
        for half in range(2):
            fill = pltpu.make_async_copy(
                os_ref.at[half],
                y_hbm.at[pl.ds((SEQ * TOP_K + half * MOE_RMAX) * ROW_SUB, MOE_RMAX * ROW_SUB), :],
                ssem.at[half])
            fill.start()
            fill.wait()
        each_row(0, rows, lambda r: gather(tok0_ref, 0, r))

    @pl.when(j == 0)
    def _():
        wait_rows(xs_ref, gsem, p, jnp.where(i == 0, rows, jnp.maximum(rows_prev, rows)))

    def sub_tile(a, m):
        a = pl.multiple_of(a, MOE_G)
        base = (j * rows + a) // ROW_SUB
        lag = 0
        for u in range(m // ROW_SUB):
            lag = gather(tokn_ref, q, base + u + lag) >> 31
            lag = scatter(q, base + u + lag) >> 31
        x_lo, x_hi = _unpack_words(_load_packed(xs_ref, (p,), a, m))
        x_lo = x_lo.astype(jnp.bfloat16)
        x_hi = x_hi.astype(jnp.bfloat16)

        def up(w_ref, b_ref):
            y = jnp.dot(x_lo, w_ref[:D_PACK, :].astype(jnp.bfloat16), preferred_element_type=jnp.float32)
            y = y + jnp.dot(x_hi, w_ref[D_PACK:, :].astype(jnp.bfloat16), preferred_element_type=jnp.float32)
            return y + b_ref[...]

        glu = jnp.minimum(up(wg_ref, bg_ref), SWIGLU_LIMIT)
        lin = jnp.clip(up(wl_ref, bl_ref), -SWIGLU_LIMIT, SWIGLU_LIMIT)
        act = glu * jax.nn.sigmoid(SWIGLU_ALPHA * glu) * (lin + 1.0)
        acc_ref[pl.ds(a, m), :] += jnp.dot(act.astype(jnp.bfloat16), w2_ref[...].astype(jnp.bfloat16),
                                           preferred_element_type=jnp.float32)

    @pl.when(rows > 0)
    def _():
        @pl.when(j == 0)
        def _():
            bias = jnp.broadcast_to(b2_ref[...], (MOE_G, D_MODEL))

            def init(t, c):
                acc_ref[pl.ds(pl.multiple_of(t * MOE_G, MOE_G), MOE_G), :] = bias
                return c
            lax.fori_loop(0, rows // MOE_G, init, 0)

        main = MOE_SUBTILES[0]
        n_main = rows // main
        lax.fori_loop(0, n_main, lambda t, c: (sub_tile(t * main, main), c)[1], 0)
        done = n_main * main
        for m in MOE_SUBTILES[1:]:
            has = ((rows - done) // m) % 2 == 1
            pl.when(has)(lambda done=done, m=m: sub_tile(done, m))
            done = done + jnp.where(has, m, 0)

    @pl.when(j == last_j)
    def _():
        wait_rows(os_ref, ssem, p, jnp.maximum(rows_prev, rows_prev2))

        def pack(t, c):
            a = pl.multiple_of(t * MOE_G, MOE_G)
            _store_packed(os_ref, (p,), a, _pack_rows(acc_ref[pl.ds(a, MOE_G), :]))
            return c
        lax.fori_loop(0, rows // MOE_G, pack, 0)
        each_row(rows, rows_next, lambda r: gather(tokn_ref, q, r))
        each_row(rows, rows_prev, lambda r: scatter(q, r))

    @pl.when((i == n_items - 1) & (j == last_j))
    def _():
        wait_rows(os_ref, ssem, q, rows_prev)


def _experts(h_packed, meta, tok_tab, dst_tab, w1, b1, w2, b2):
    n_j = MOE_CHUNKS

    def col(i, j, m):
        return jnp.where(m[MOE_ITEMS + i] > 0, j, n_j - 1)

    def expert(i, m):
        return m[i]

    tab = lambda f: pl.BlockSpec((1, 1, MOE_RMAX), f, memory_space=pltpu.SMEM)
    return pl.pallas_call(
        _experts_kernel,
        grid_spec=pltpu.PrefetchScalarGridSpec(
            num_scalar_prefetch=1,
            grid=(MOE_ITEMS, n_j),
            in_specs=[
                tab(lambda i, j, m: (1, 0, 0)),
                tab(lambda i, j, m: (jnp.minimum(i + 2, MOE_ITEMS), 0, 0)),
                tab(lambda i, j, m: (i, 0, 0)),
                pl.BlockSpec(memory_space=pl.ANY),
                pl.BlockSpec((None, D_MODEL, MOE_TN), lambda i, j, m: (expert(i, m), 0, col(i, j, m))),
                pl.BlockSpec((None, D_MODEL, MOE_TN), lambda i, j, m: (expert(i, m), 0, n_j + col(i, j, m))),
                pl.BlockSpec((None, 1, MOE_TN), lambda i, j, m: (expert(i, m), 0, col(i, j, m))),
                pl.BlockSpec((None, 1, MOE_TN), lambda i, j, m: (expert(i, m), 0, n_j + col(i, j, m))),
                pl.BlockSpec((None, MOE_TN, D_MODEL), lambda i, j, m: (expert(i, m), col(i, j, m), 0)),
                pl.BlockSpec((None, 1, D_MODEL), lambda i, j, m: (expert(i, m), 0, 0)),
            ],
            out_specs=pl.BlockSpec(memory_space=pl.ANY),
            scratch_shapes=[
                pltpu.VMEM((2, MOE_RMAX * ROW_SUB, LANES), jnp.uint32),
                pltpu.VMEM((2, MOE_RMAX * ROW_SUB, LANES), jnp.uint32),
                pltpu.VMEM((MOE_RMAX, D_MODEL), jnp.float32),
                pltpu.SemaphoreType.DMA((2,)),
                pltpu.SemaphoreType.DMA((2,)),
            ],
        ),
        out_shape=jax.ShapeDtypeStruct((Y_ROWS * ROW_SUB, LANES), jnp.uint32),
        compiler_params=_params("arbitrary", "arbitrary"),
        name="experts",
    )(meta, tok_tab, tok_tab, dst_tab, h_packed, w1, w1,
      b1.reshape(N_EXPERTS, 1, 2 * D_EXPERT), b1.reshape(N_EXPERTS, 1, 2 * D_EXPERT),
      w2, b2.reshape(N_EXPERTS, 1, D_MODEL))


def _combine_kernel(y0_ref, y1_ref, y2_ref, y3_ref, gates_ref, x1_ref, gate2_ref, g_ref, b_ref, o_ref):
    y_lo = jnp.zeros((COMBINE_TM, D_PACK), jnp.float32)
    y_hi = jnp.zeros((COMBINE_TM, D_PACK), jnp.float32)
    for k, y_ref in enumerate((y0_ref, y1_ref, y2_ref, y3_ref)):
        lo, hi = _unpack_words(_load_packed(y_ref, (), 0, COMBINE_TM))
        gate = gates_ref[:, k:k + 1]
        y_lo = y_lo + gate * lo
        y_hi = y_hi + gate * hi
    y = jnp.concatenate([y_lo, y_hi], axis=1)
    o_ref[...] = _layer_norm(DEEPNORM_ALPHA * x1_ref[...] + gate2_ref[...] * y, g_ref[...], b_ref[...])


def _combine(y_packed, gates, x1, gate2, ln_g, ln_b):
    n_tiles = SEQ // COMBINE_TM
    row = lambda i: (i, 0)
    vec = pl.BlockSpec((1, D_MODEL), lambda i: (0, 0))
    slot = lambda k: pl.BlockSpec((COMBINE_TM * ROW_SUB, LANES), lambda i: (k * n_tiles + i, 0))
    return pl.pallas_call(
        _combine_kernel,
        grid=(n_tiles,),
        in_specs=[
            slot(0), slot(1), slot(2), slot(3),
            pl.BlockSpec((COMBINE_TM, TOP_K), row),
            pl.BlockSpec((COMBINE_TM, D_MODEL), row),
            vec, vec, vec,
        ],
        out_specs=pl.BlockSpec((COMBINE_TM, D_MODEL), row),
        out_shape=jax.ShapeDtypeStruct((SEQ, D_MODEL), jnp.float32),
        compiler_params=_params("arbitrary"),
        name="combine",
    )(y_packed, y_packed, y_packed, y_packed, gates, x1, gate2, ln_g, ln_b)


TAB_ROWS = MOE_RMAX // LANES
WINDOW_ROWS = 2 * ROW_SUB
assert WINDOW_ROWS > TAB_ROWS and TOP_K == 4


def _row_tables_kernel(win_ref, order_ref, tok_ref, dst_ref):
    b = pl.program_id(0)
    w0 = win_ref[b]
    n_real = win_ref[MOE_ITEMS + 1 + b]
    off = w0 % LANES
    x = order_ref[pl.ds(w0 // LANES, WINDOW_ROWS), :]
    x = pltpu.roll(x, (LANES - off) % LANES, axis=1)
    lane = lax.broadcasted_iota(jnp.int32, (TAB_ROWS, LANES), 1)
    sub = lax.broadcasted_iota(jnp.int32, (TAB_ROWS, LANES), 0)
    flat = jnp.where(lane < LANES - off, x[:TAB_ROWS], x[1:TAB_ROWS + 1])
    r = sub * LANES + lane
    valid = r < n_real
    tok = flat >> 2
    slot = flat & 3
    tok_ref[...] = jnp.where(valid, tok, 0) * ROW_SUB
    spill = SEQ * TOP_K + ((b + 1) % 2) * MOE_RMAX + r
    dst_ref[...] = jnp.where(valid, slot * SEQ + tok, spill) * ROW_SUB


def _row_tables(win, order):
    n_rows = SEQ * TOP_K // LANES
    order2d = jnp.concatenate([order, jnp.zeros((WINDOW_ROWS * LANES,), jnp.int32)]).reshape(-1, LANES)
    out = jax.ShapeDtypeStruct((MOE_ITEMS + 1, TAB_ROWS, LANES), jnp.int32)
    tab = pl.BlockSpec((None, TAB_ROWS, LANES), lambda b, w: (b, 0, 0))
    return pl.pallas_call(
        _row_tables_kernel,
        grid_spec=pltpu.PrefetchScalarGridSpec(
            num_scalar_prefetch=1,
            grid=(MOE_ITEMS + 1,),
            in_specs=[pl.BlockSpec((n_rows + WINDOW_ROWS, LANES), lambda b, w: (0, 0))],
            out_specs=[tab, tab],
        ),
        out_shape=[out, out],
        compiler_params=_params("arbitrary"),
        name="row_tables",
    )(win, order2d)


def _route(logits):
    top_val, top_idx = lax.top_k(logits, TOP_K)
    gates = jax.nn.softmax(top_val, axis=-1)
    e_flat = top_idx.reshape(-1).astype(jnp.int32)
    nk = e_flat.shape[0]
    experts = jnp.arange(N_EXPERTS, dtype=jnp.int32)
    counts = jnp.sum((e_flat[:, None] == experts[None, :]).astype(jnp.int32), axis=0)
    groups = (counts + MOE_G - 1) // MOE_G
    per_item = MOE_RMAX // MOE_G
    n_items_e = (groups + per_item - 1) // per_item
    item_end = jnp.cumsum(n_items_e)
    item_start = item_end - n_items_e
    n_items = item_end[-1]

    item = jnp.arange(MOE_ITEMS, dtype=jnp.int32)
    used = item < n_items
    e_item = jnp.minimum(jnp.searchsorted(item_end, item, side='right'), N_EXPERTS - 1).astype(jnp.int32)
    e_last = e_item[jnp.maximum(n_items - 1, 0)]
    part = item - item_start[e_item]
    rows = jnp.where(used, jnp.clip(groups[e_item] - part * per_item, 0, per_item) * MOE_G, 0)
    meta = jnp.concatenate([jnp.where(used, e_item, e_last), rows]).astype(jnp.int32)

    order = jnp.argsort(e_flat, stable=True).astype(jnp.int32)
    start = jnp.cumsum(counts) - counts
    window0 = jnp.where(used, start[e_item] + part * MOE_RMAX, 0)
    n_real = jnp.where(used, jnp.clip(counts[e_item] - part * MOE_RMAX, 0, MOE_RMAX), 0)
    zero = jnp.zeros((1,), jnp.int32)
    win = jnp.concatenate([zero, window0, zero, n_real]).astype(jnp.int32)
    tok_tab, dst_tab = _row_tables(win, order)
    shape = (MOE_ITEMS + 1, 1, MOE_RMAX)
    return gates, meta, tok_tab.reshape(shape), dst_tab.reshape(shape)


def kernel(x, c, w_ada, b_ada, w_in, sgu_ln_g, sgu_ln_b, w_spatial, b_spatial, w_o, ln1_g, ln1_b,
           w_router, b_router, w_exp1, b_exp1, w_exp2, b_exp2, ln2_g, ln2_b):
    depth = w_ada.shape[0]
    assert x.shape == (1, SEQ, D_MODEL)
    xs = x.reshape(SEQ, D_MODEL)
    for l in range(depth):
        ada = _ada(c, w_ada[l], b_ada[l])
        shift1, scale1, gate1, shift2, scale2, gate2 = jnp.split(ada, 6, axis=-1)

        proj = _in_proj(xs, scale1, shift1, w_in[l].astype(jnp.bfloat16))
        y_a = _gmlp(proj, sgu_ln_g[l], sgu_ln_b[l], w_spatial[l], b_spatial[l])
        y_b = _attention(proj)
        x1, h_packed, logits = _out_proj(y_a, y_b, w_o[l].astype(jnp.bfloat16), xs, gate1,
                                         ln1_g[l].reshape(1, -1), ln1_b[l].reshape(1, -1), scale2, shift2,
                                         w_router[l], b_router[l].reshape(1, -1))

        gates, meta, tok_tab, dst_tab = _route(logits)
        y_packed = _experts(h_packed, meta, tok_tab, dst_tab, w_exp1[l], b_exp1[l], w_exp2[l], b_exp2[l])
        xs = _combine(y_packed, gates, x1, gate2, ln2_g[l].reshape(1, -1), ln2_b[l].reshape(1, -1))
    return xs.reshape(x.shape)
```

```python
import math

import jax
import jax.numpy as jnp
from jax import lax
from jax.experimental import pallas as pl
from jax.experimental.pallas import tpu as pltpu

D_MODEL = 2048
SEQ = 8192
D_GMLP = 1024
GMLP_GROUPS = 8
GROUP_DIM = 128
CHUNK = 128
D_ATTN = 1024
HEAD_DIM = 128
N_HEADS = 8
BRANCHES = ((128, 1), (512, 4), (2048, 16))
BLK = 128
SPAN = 16 * BLK
D_IN_PROJ = 2 * D_GMLP + 3 * D_ATTN
N_EXPERTS = 32
TOP_K = 4
D_EXPERT = 2048
SWIGLU_LIMIT = 7.0
SWIGLU_ALPHA = 1.702
LN_EPS = 1e-5
DEEPNORM_ALPHA = 2.0 ** 0.25
NEG = -1e30

LANES = 128
VMEM_LIMIT = 56 * 1024 * 1024

ADA_TN = 1536
PROJ_TM = 1024
PROJ_TN = 1024
OUT_TM = 512
ATTN_UNROLL = 8
ROW_SUB = 8
D_PACK = D_MODEL // 2
MOE_G = 128
MOE_RMAX = 1280
MOE_ITEMS = 64
MOE_TN = 256
MOE_CHUNKS = D_EXPERT // MOE_TN
MOE_SUBTILES = (512, 256, 128)
Y_ROWS = SEQ * TOP_K + 2 * MOE_RMAX
COMBINE_TM = 256


def _params(*sem):
    return pltpu.CompilerParams(dimension_semantics=sem, vmem_limit_bytes=VMEM_LIMIT)


def _layer_norm(x, g, b):
    mu = jnp.mean(x, axis=-1, keepdims=True)
    xc = x - mu
    var = jnp.mean(xc * xc, axis=-1, keepdims=True)
    return xc * lax.rsqrt(var + LN_EPS) * g + b


def _pack_rows(x):
    r = x.astype(jnp.bfloat16).astype(jnp.float32)
    bits = lax.bitcast_convert_type(r, jnp.uint32)
    return (bits[:, D_PACK:] & jnp.uint32(0xFFFF0000)) | (bits[:, :D_PACK] >> 16)


def _unpack_words(u):
    lo = lax.bitcast_convert_type(u << 16, jnp.float32)
    hi = lax.bitcast_convert_type(u & jnp.uint32(0xFFFF0000), jnp.float32)
    return lo, hi


def _store_packed(ref, lead, row0, packed):
    m = packed.shape[0]
    for c in range(ROW_SUB):
        rows = pl.ds(row0 * ROW_SUB + c, m, stride=ROW_SUB)
        ref[lead + (rows, slice(None))] = packed[:, c * LANES:(c + 1) * LANES]


def _load_packed(ref, lead, row0, m):
    cols = [ref[lead + (pl.ds(row0 * ROW_SUB + c, m, stride=ROW_SUB), slice(None))] for c in range(ROW_SUB)]
    return jnp.concatenate(cols, axis=1)


def _ada_kernel(c_ref, w_ref, b_ref, o_ref):
    c = c_ref[...]
    s = c * jax.nn.sigmoid(c)
    o_ref[...] = jnp.sum(s * w_ref[...], axis=0, keepdims=True) + b_ref[...]


def _ada(c, w_ada, b_ada):
    n = w_ada.shape[1]
    return pl.pallas_call(
        _ada_kernel,
        grid=(n // ADA_TN,),
        in_specs=[
            pl.BlockSpec((D_MODEL, 1), lambda j: (0, 0)),
            pl.BlockSpec((D_MODEL, ADA_TN), lambda j: (0, j)),
            pl.BlockSpec((1, ADA_TN), lambda j: (0, j)),
        ],
        out_specs=pl.BlockSpec((1, ADA_TN), lambda j: (0, j)),
        out_shape=jax.ShapeDtypeStruct((1, n), jnp.float32),
        compiler_params=_params("arbitrary"),
        name="ada",
    )(c.reshape(D_MODEL, 1), w_ada, b_ada.reshape(1, n))


def _in_proj_kernel(x_ref, sc_ref, sh_ref, w_ref, o_ref, h_ref):
    @pl.when(pl.program_id(1) == 0)
    def _():
        h_ref[...] = (x_ref[...] * (1.0 + sc_ref[...]) + sh_ref[...]).astype(jnp.bfloat16)

    o_ref[...] = jnp.dot(h_ref[...], w_ref[...], preferred_element_type=jnp.float32)


def _in_proj(x, scale1, shift1, w_in_bf16):
    return pl.pallas_call(
        _in_proj_kernel,
        grid=(SEQ // PROJ_TM, D_IN_PROJ // PROJ_TN),
        in_specs=[
            pl.BlockSpec((PROJ_TM, D_MODEL), lambda i, j: (i, 0)),
            pl.BlockSpec((1, D_MODEL), lambda i, j: (0, 0)),
            pl.BlockSpec((1, D_MODEL), lambda i, j: (0, 0)),
            pl.BlockSpec((D_MODEL, PROJ_TN), lambda i, j: (0, j)),
        ],
        out_specs=pl.BlockSpec((PROJ_TM, PROJ_TN), lambda i, j: (i, j)),
        out_shape=jax.ShapeDtypeStruct((SEQ, D_IN_PROJ), jnp.float32),
        scratch_shapes=[pltpu.VMEM((PROJ_TM, D_MODEL), jnp.bfloat16)],
        compiler_params=_params("arbitrary", "arbitrary"),
        name="in_proj",
    )(x, scale1, shift1, w_in_bf16)


def _gelu(x):
    return 0.5 * x * (1.0 + lax.erf(x * (1.0 / math.sqrt(2.0))))


def _gmlp_kernel(ua_ref, va_ref, g_ref, b_ref, w_ref, bs_ref, o_ref):
    row = lax.broadcasted_iota(jnp.int32, (CHUNK, CHUNK), 0)
    col = lax.broadcasted_iota(jnp.int32, (CHUNK, CHUNK), 1)
    causal = col <= row
    for g in range(GMLP_GROUPS):
        sl = slice(g * GROUP_DIM, (g + 1) * GROUP_DIM)
        v = _layer_norm(_gelu(va_ref[:, sl]), g_ref[g:g + 1, :], b_ref[g:g + 1, :])
        w = jnp.where(causal, w_ref[g], 0.0).astype(jnp.bfloat16)
        s = jnp.dot(w, v.astype(jnp.bfloat16), preferred_element_type=jnp.float32)
        s = s + bs_ref[:, g:g + 1]
        o_ref[:, sl] = (_gelu(ua_ref[:, sl]) * s).astype(o_ref.dtype)


def _gmlp(proj, ln_g, ln_b, w_spatial, b_spatial):
    return pl.pallas_call(
        _gmlp_kernel,
        grid=(SEQ // CHUNK,),
        in_specs=[
            pl.BlockSpec((CHUNK, D_GMLP), lambda n: (n, 0)),
            pl.BlockSpec((CHUNK, D_GMLP), lambda n: (n, 1)),
            pl.BlockSpec((GMLP_GROUPS, GROUP_DIM), lambda n: (0, 0)),
            pl.BlockSpec((GMLP_GROUPS, GROUP_DIM), lambda n: (0, 0)),
            pl.BlockSpec((GMLP_GROUPS, CHUNK, CHUNK), lambda n: (0, 0, 0)),
            pl.BlockSpec((CHUNK, GMLP_GROUPS), lambda n: (0, 0)),
        ],
        out_specs=pl.BlockSpec((CHUNK, D_GMLP), lambda n: (n, 0)),
        out_shape=jax.ShapeDtypeStruct((SEQ, D_GMLP), jnp.bfloat16),
        compiler_params=_params("arbitrary"),
        name="gmlp",
    )(proj, proj, ln_g, ln_b, w_spatial, b_spatial.T)


def _attn_kernel(q_ref, k_ref, v_ref, o_ref, acc_ref, m_ref, l_ref):
    head = pl.program_id(0)
    span = pl.program_id(1)
    log2e = 1.0 / math.log(2.0)
    head_no = (jnp.zeros((BLK, 2 * BLK), jnp.int32) + (head + 1)).astype(jnp.float32)
    slope = jnp.exp2(head_no * (-8.0 / N_HEADS)) * log2e
    scale = HEAD_DIM ** -0.5 * log2e
    qi = lax.broadcasted_iota(jnp.int32, (BLK, 2 * BLK), 0)
    ki = lax.broadcasted_iota(jnp.int32, (BLK, 2 * BLK), 1)
    step = qi + BLK - ki
    contract_last = (((1,), (1,)), ((), ()))
    ones = jnp.ones((2 * BLK, LANES), jnp.bfloat16)

    for b, (window, d) in enumerate(BRANCHES):
        assert window // d == BLK
        valid = (step >= 0) & (step <= BLK)
        bias = jnp.where(valid, -slope * d * step.astype(jnp.float32), NEG)
        bias_first = jnp.where(ki >= BLK, bias, NEG)

        def tile(t, carry, d=d, b=b, bias=bias, bias_first=bias_first):
            r = t % d
            n = t // d
            q0 = n * (BLK * d) + r
            k0 = span * SPAN + q0
            first = k0 < BLK * d
            kp = jnp.where(first, k0, k0 - BLK * d)
            q = (q_ref[pl.ds(q0, BLK, stride=d), :] * scale).astype(jnp.bfloat16)
            k = jnp.concatenate([k_ref[pl.ds(kp, BLK, stride=d), :], k_ref[pl.ds(k0, BLK, stride=d), :]], axis=0)
            v = jnp.concatenate([v_ref[pl.ds(kp, BLK, stride=d), :], v_ref[pl.ds(k0, BLK, stride=d), :]], axis=0)
            s = lax.dot_general(q, k.astype(jnp.bfloat16), contract_last, preferred_element_type=jnp.float32)
            s = s + jnp.where(first, bias_first, bias)
            m = jnp.max(jnp.maximum(s[:, :BLK], s[:, BLK:]), axis=-1, keepdims=True)
            p = jnp.exp2(s - m).astype(jnp.bfloat16)
            v_one = jnp.concatenate([v.astype(jnp.bfloat16), ones], axis=1)
            pv = jnp.dot(p, v_one, preferred_element_type=jnp.float32)
            rows = pl.ds(q0, BLK, stride=d)
            acc_ref[b, rows, :] = pv[:, :HEAD_DIM]
            l_ref[b, rows, :] = pv[:, HEAD_DIM:]
            m_ref[b, rows, :] = jnp.broadcast_to(m, (BLK, LANES))
            return carry

        lax.fori_loop(0, SPAN // BLK, tile, 0, unroll=ATTN_UNROLL)

    m_all = jnp.maximum(jnp.maximum(m_ref[0], m_ref[1]), m_ref[2])
    num = jnp.zeros((SPAN, HEAD_DIM), jnp.float32)
    den = jnp.zeros((SPAN, LANES), jnp.float32)
    for b in range(len(BRANCHES)):
        w = jnp.exp2(m_ref[b] - m_all)
        num = num + w * acc_ref[b]
        den = den + w * l_ref[b]
    o_ref[...] = (num / den).astype(o_ref.dtype)


def _attention(proj):
    q_col = 2 * D_GMLP // HEAD_DIM
    k_col = q_col + N_HEADS
    v_col = k_col + N_HEADS
    nb = len(BRANCHES)
    return pl.pallas_call(
        _attn_kernel,
        grid=(N_HEADS, SEQ // SPAN),
        in_specs=[
            pl.BlockSpec((SPAN, HEAD_DIM), lambda h, s: (s, q_col + h)),
            pl.BlockSpec((SEQ, HEAD_DIM), lambda h, s: (0, k_col + h)),
            pl.BlockSpec((SEQ, HEAD_DIM), lambda h, s: (0, v_col + h)),
        ],
        out_specs=pl.BlockSpec((SPAN, HEAD_DIM), lambda h, s: (s, h)),
        out_shape=jax.ShapeDtypeStruct((SEQ, D_ATTN), jnp.bfloat16),
        scratch_shapes=[
            pltpu.VMEM((nb, SPAN, HEAD_DIM), jnp.float32),
            pltpu.VMEM((nb, SPAN, LANES), jnp.float32),
            pltpu.VMEM((nb, SPAN, LANES), jnp.float32),
        ],
        compiler_params=_params("arbitrary", "arbitrary"),
        name="attn",
    )(proj, proj, proj)


def _split_bf16(x):
    hi = x.astype(jnp.bfloat16)
    lo = (x - hi.astype(jnp.float32)).astype(jnp.bfloat16)
    return hi, lo


def _out_proj_kernel(ya_ref, yb_ref, wa_ref, wb_ref, x_ref, gate_ref, g_ref, b_ref, sc_ref, sh_ref,
                     wr_ref, br_ref, x1_ref, hp_ref, lg_ref):
    mix = jnp.dot(ya_ref[...], wa_ref[...], preferred_element_type=jnp.float32)
    mix = mix + jnp.dot(yb_ref[...], wb_ref[...], preferred_element_type=jnp.float32)
    x1 = _layer_norm(DEEPNORM_ALPHA * x_ref[...] + gate_ref[...] * mix, g_ref[...], b_ref[...])
    x1_ref[...] = x1
    h = x1 * (1.0 + sc_ref[...]) + sh_ref[...]
    _store_packed(hp_ref, (), 0, _pack_rows(h))
    h_hi, h_lo = _split_bf16(h)
    w_hi, w_lo = _split_bf16(wr_ref[...])
    lg = jnp.dot(h_hi, w_hi, preferred_element_type=jnp.float32)
    lg = lg + jnp.dot(h_hi, w_lo, preferred_element_type=jnp.float32)
    lg = lg + jnp.dot(h_lo, w_hi, preferred_element_type=jnp.float32)
    lg_ref[...] = lg + br_ref[...]


def _out_proj(y_a, y_b, w_o_bf16, x, gate1, ln_g, ln_b, scale2, shift2, w_router, b_router):
    row = lambda i: (i, 0)
    fixed = lambda i: (0, 0)
    vec = pl.BlockSpec((1, D_MODEL), fixed)
    return pl.pallas_call(
        _out_proj_kernel,
        grid=(SEQ // OUT_TM,),
        in_specs=[
            pl.BlockSpec((OUT_TM, D_GMLP), row),
            pl.BlockSpec((OUT_TM, D_ATTN), row),
            pl.BlockSpec((D_GMLP, D_MODEL), lambda i: (0, 0)),
            pl.BlockSpec((D_ATTN, D_MODEL), lambda i: (1, 0)),
            pl.BlockSpec((OUT_TM, D_MODEL), row),
            vec, vec, vec, vec, vec,
            pl.BlockSpec((D_MODEL, N_EXPERTS), fixed),
            pl.BlockSpec((1, N_EXPERTS), fixed),
        ],
        out_specs=[
            pl.BlockSpec((OUT_TM, D_MODEL), row),
            pl.BlockSpec((OUT_TM * ROW_SUB, LANES), row),
            pl.BlockSpec((OUT_TM, N_EXPERTS), row),
        ],
        out_shape=[
            jax.ShapeDtypeStruct((SEQ, D_MODEL), jnp.float32),
            jax.ShapeDtypeStruct((SEQ * ROW_SUB, LANES), jnp.uint32),
            jax.ShapeDtypeStruct((SEQ, N_EXPERTS), jnp.float32),
        ],
        compiler_params=_params("arbitrary"),
        name="out_proj",
    )(y_a, y_b, w_o_bf16, w_o_bf16, x, gate1, ln_g, ln_b, scale2, shift2, w_router, b_router)


def _experts_kernel(meta_ref, tok0_ref, tokn_ref, dstp_ref, h_hbm, wg_ref, wl_ref, bg_ref, bl_ref, w2_ref,
                    b2_ref, y_hbm, xs_ref, os_ref, acc_ref, gsem, ssem):
    i = pl.program_id(0)
    j = pl.program_id(1)
    n_items = pl.num_programs(0)
    last_j = pl.num_programs(1) - 1

    def item_rows(k):
        inside = (k >= 0) & (k < n_items)
        return jnp.where(inside, meta_ref[MOE_ITEMS + jnp.clip(k, 0, n_items - 1)], 0)

    rows = item_rows(i)
    rows_prev = item_rows(i - 1)
    rows_prev2 = item_rows(i - 2)
    rows_next = item_rows(i + 1)
    p = i % 2
    q = 1 - p

    def gather(tab_ref, slot, row, queue=0):
        entry = tab_ref[0, 0, row]
        src = pl.multiple_of(entry, ROW_SUB)
        dst = pl.multiple_of(row * ROW_SUB, ROW_SUB)
        pltpu.make_async_copy(h_hbm.at[pl.ds(src, ROW_SUB), :], xs_ref.at[slot, pl.ds(dst, ROW_SUB), :],
                              gsem.at[slot]).start(priority=queue)
        return entry

    def scatter(slot, row, queue=1):
        entry = dstp_ref[0, 0, row]
        src = pl.multiple_of(row * ROW_SUB, ROW_SUB)
        dst = pl.multiple_of(entry, ROW_SUB)
        pltpu.make_async_copy(os_ref.at[slot, pl.ds(src, ROW_SUB), :], y_hbm.at[pl.ds(dst, ROW_SUB), :],
                              ssem.at[slot]).start(priority=queue)
        return entry

    def wait_rows(buf_ref, sem, slot, n):
        group = buf_ref.at[slot, pl.ds(0, MOE_G * ROW_SUB), :]

        def wait_group(t, c):
            pltpu.make_async_copy(group, group, sem.at[slot]).wait()
            return c
        lax.fori_loop(0, n // MOE_G, wait_group, 0)

    def each_row(lo, hi, fn):
        lax.fori_loop(lo, jnp.maximum(lo, hi), lambda r, c: (fn(r), c)[1], 0)

    @pl.when((i == 0) & (j == 0))
    def _():
        os_ref[...] = jnp.zeros_like(os_ref)
        for half in range(2):
            fill = pltpu.make_async_copy(
                os_ref.at[half],
                y_hbm.at[pl.ds((SEQ * TOP_K + half * MOE_RMAX) * ROW_SUB, MOE_RMAX * ROW_SUB), :],
                ssem.at[half])
            fill.start()
            fill.wait()
        each_row(0, rows, lambda r: gather(tok0_ref, 0, r))

    @pl.when(j == 0)
    def _():
        wait_rows(xs_ref, gsem, p, jnp.where(i == 0, rows, jnp.maximum(rows_prev, rows)))

    def sub_tile(a, m):
        a = pl.multiple_of(a, MOE_G)
        base = (j * rows + a) // ROW_SUB
        lag = 0
        for u in range(m // ROW_SUB):
            lag = scatter(q, base + u + lag, queue=u % 2) >> 31
        x_lo, x_hi = _unpack_words(_load_packed(xs_ref, (p,), a, m))
        x_lo = x_lo.astype(jnp.bfloat16)
        x_hi = x_hi.astype(jnp.bfloat16)

        def up(w_ref, b_ref):
            y = jnp.dot(x_lo, w_ref[:D_PACK, :].astype(jnp.bfloat16), preferred_element_type=jnp.float32)
            y = y + jnp.dot(x_hi, w_ref[D_PACK:, :].astype(jnp.bfloat16), preferred_element_type=jnp.float32)
            return y + b_ref[...]

        glu = jnp.minimum(up(wg_ref, bg_ref), SWIGLU_LIMIT)
        lin = jnp.clip(up(wl_ref, bl_ref), -SWIGLU_LIMIT, SWIGLU_LIMIT)
        act = glu * jax.nn.sigmoid(SWIGLU_ALPHA * glu) * (lin + 1.0)
        acc_ref[pl.ds(a, m), :] += jnp.dot(act.astype(jnp.bfloat16), w2_ref[...].astype(jnp.bfloat16),
                                           preferred_element_type=jnp.float32)

    @pl.when(rows > 0)
    def _():
        @pl.when(j == 0)
        def _():
            bias = jnp.broadcast_to(b2_ref[...], (MOE_G, D_MODEL))

            def init(t, c):
                acc_ref[pl.ds(pl.multiple_of(t * MOE_G, MOE_G), MOE_G), :] = bias
                return c
            lax.fori_loop(0, rows // MOE_G, init, 0)

        per_trip = MOE_G // ROW_SUB

        def gather_group(t, c):
            first = (j * rows) // ROW_SUB + t * per_trip
            for u in range(per_trip):
                gather(tokn_ref, q, first + u, queue=u % 2)
            return c
        lax.fori_loop(0, rows // MOE_G, gather_group, 0)

        main = MOE_SUBTILES[0]
        n_main = rows // main
        lax.fori_loop(0, n_main, lambda t, c: (sub_tile(t * main, main), c)[1], 0)
        done = n_main * main
        for m in MOE_SUBTILES[1:]:
            has = ((rows - done) // m) % 2 == 1
            pl.when(has)(lambda done=done, m=m: sub_tile(done, m))
            done = done + jnp.where(has, m, 0)

    @pl.when(j == last_j)
    def _():
        wait_rows(os_ref, ssem, p, jnp.maximum(rows_prev, rows_prev2))

        def pack(t, c):
            a = pl.multiple_of(t * MOE_G, MOE_G)
            _store_packed(os_ref, (p,), a, _pack_rows(acc_ref[pl.ds(a, MOE_G), :]))
            return c
        lax.fori_loop(0, rows // MOE_G, pack, 0)
        each_row(rows, rows_next, lambda r: gather(tokn_ref, q, r))
        each_row(rows, rows_prev, lambda r: scatter(q, r))

    @pl.when((i == n_items - 1) & (j == last_j))
    def _():
        wait_rows(os_ref, ssem, q, rows_prev)


def _experts(h_packed, meta, tok_tab, dst_tab, w1, b1, w2, b2):
    n_j = MOE_CHUNKS

    def col(i, j, m):
        return jnp.where(m[MOE_ITEMS + i] > 0, j, n_j - 1)

    def expert(i, m):
        return m[i]

    tab = lambda f: pl.BlockSpec((1, 1, MOE_RMAX), f, memory_space=pltpu.SMEM)
    return pl.pallas_call(
        _experts_kernel,
        grid_spec=pltpu.PrefetchScalarGridSpec(
            num_scalar_prefetch=1,
            grid=(MOE_ITEMS, n_j),
            in_specs=[
                tab(lambda i, j, m: (1, 0, 0)),
                tab(lambda i, j, m: (jnp.minimum(i + 2, MOE_ITEMS), 0, 0)),
                tab(lambda i, j, m: (i, 0, 0)),
                pl.BlockSpec(memory_space=pl.ANY),
                pl.BlockSpec((None, D_MODEL, MOE_TN), lambda i, j, m: (expert(i, m), 0, col(i, j, m))),
                pl.BlockSpec((None, D_MODEL, MOE_TN), lambda i, j, m: (expert(i, m), 0, n_j + col(i, j, m))),
                pl.BlockSpec((None, 1, MOE_TN), lambda i, j, m: (expert(i, m), 0, col(i, j, m))),
                pl.BlockSpec((None, 1, MOE_TN), lambda i, j, m: (expert(i, m), 0, n_j + col(i, j, m))),
                pl.BlockSpec((None, MOE_TN, D_MODEL), lambda i, j, m: (expert(i, m), col(i, j, m), 0)),
                pl.BlockSpec((None, 1, D_MODEL), lambda i, j, m: (expert(i, m), 0, 0)),
            ],
            out_specs=pl.BlockSpec(memory_space=pl.ANY),
            scratch_shapes=[
                pltpu.VMEM((2, MOE_RMAX * ROW_SUB, LANES), jnp.uint32),
                pltpu.VMEM((2, MOE_RMAX * ROW_SUB, LANES), jnp.uint32),
                pltpu.VMEM((MOE_RMAX, D_MODEL), jnp.float32),
                pltpu.SemaphoreType.DMA((2,)),
                pltpu.SemaphoreType.DMA((2,)),
            ],
        ),
        out_shape=jax.ShapeDtypeStruct((Y_ROWS * ROW_SUB, LANES), jnp.uint32),
        compiler_params=_params("arbitrary", "arbitrary"),
        name="experts",
    )(meta, tok_tab, tok_tab, dst_tab, h_packed, w1, w1,
      b1.reshape(N_EXPERTS, 1, 2 * D_EXPERT), b1.reshape(N_EXPERTS, 1, 2 * D_EXPERT),
      w2, b2.reshape(N_EXPERTS, 1, D_MODEL))


def _combine_kernel(y0_ref, y1_ref, y2_ref, y3_ref, gates_ref, x1_ref, gate2_ref, g_ref, b_ref, o_ref):
    y_lo = jnp.zeros((COMBINE_TM, D_PACK), jnp.float32)
    y_hi = jnp.zeros((COMBINE_TM, D_PACK), jnp.float32)
    for k, y_ref in enumerate((y0_ref, y1_ref, y2_ref, y3_ref)):
        lo, hi = _unpack_words(_load_packed(y_ref, (), 0, COMBINE_TM))
        gate = gates_ref[:, k:k + 1]
        y_lo = y_lo + gate * lo
        y_hi = y_hi + gate * hi
    y = jnp.concatenate([y_lo, y_hi], axis=1)
    o_ref[...] = _layer_norm(DEEPNORM_ALPHA * x1_ref[...] + gate2_ref[...] * y, g_ref[...], b_ref[...])


def _combine(y_packed, gates, x1, gate2, ln_g, ln_b):
    n_tiles = SEQ // COMBINE_TM
    row = lambda i: (i, 0)
    vec = pl.BlockSpec((1, D_MODEL), lambda i: (0, 0))
    slot = lambda k: pl.BlockSpec((COMBINE_TM * ROW_SUB, LANES), lambda i: (k * n_tiles + i, 0))
    return pl.pallas_call(
        _combine_kernel,
        grid=(n_tiles,),
        in_specs=[
            slot(0), slot(1), slot(2), slot(3),
            pl.BlockSpec((COMBINE_TM, TOP_K), row),
            pl.BlockSpec((COMBINE_TM, D_MODEL), row),
            vec, vec, vec,
        ],
        out_specs=pl.BlockSpec((COMBINE_TM, D_MODEL), row),
        out_shape=jax.ShapeDtypeStruct((SEQ, D_MODEL), jnp.float32),
        compiler_params=_params("arbitrary"),
        name="combine",
    )(y_packed, y_packed, y_packed, y_packed, gates, x1, gate2, ln_g, ln_b)


TAB_ROWS = MOE_RMAX // LANES
WINDOW_ROWS = 2 * ROW_SUB
assert WINDOW_ROWS > TAB_ROWS and TOP_K == 4


def _row_tables_kernel(win_ref, order_ref, tok_ref, dst_ref):
    b = pl.program_id(0)
    w0 = win_ref[b]
    n_real = win_ref[MOE_ITEMS + 1 + b]
    off = w0 % LANES
    x = order_ref[pl.ds(w0 // LANES, WINDOW_ROWS), :]
    x = pltpu.roll(x, (LANES - off) % LANES, axis=1)
    lane = lax.broadcasted_iota(jnp.int32, (TAB_ROWS, LANES), 1)
    sub = lax.broadcasted_iota(jnp.int32, (TAB_ROWS, LANES), 0)
    flat = jnp.where(lane < LANES - off, x[:TAB_ROWS], x[1:TAB_ROWS + 1])
    r = sub * LANES + lane
    valid = r < n_real
    tok = flat >> 2
    slot = flat & 3
    tok_ref[...] = jnp.where(valid, tok, 0) * ROW_SUB
    spill = SEQ * TOP_K + ((b + 1) % 2) * MOE_RMAX + r
    dst_ref[...] = jnp.where(valid, slot * SEQ + tok, spill) * ROW_SUB


def _row_tables(win, order):
    n_rows = SEQ * TOP_K // LANES
    order2d = jnp.concatenate([order, jnp.zeros((WINDOW_ROWS * LANES,), jnp.int32)]).reshape(-1, LANES)
    out = jax.ShapeDtypeStruct((MOE_ITEMS + 1, TAB_ROWS, LANES), jnp.int32)
    tab = pl.BlockSpec((None, TAB_ROWS, LANES), lambda b, w: (b, 0, 0))
    return pl.pallas_call(
        _row_tables_kernel,
        grid_spec=pltpu.PrefetchScalarGridSpec(
            num_scalar_prefetch=1,
            grid=(MOE_ITEMS + 1,),
            in_specs=[pl.BlockSpec((n_rows + WINDOW_ROWS, LANES), lambda b, w: (0, 0))],
            out_specs=[tab, tab],
        ),
        out_shape=[out, out],
        compiler_params=_params("arbitrary"),
        name="row_tables",
    )(win, order2d)


def _route(logits):
    top_val, top_idx = lax.top_k(logits, TOP_K)
    gates = jax.nn.softmax(top_val, axis=-1)
    e_flat = top_idx.reshape(-1).astype(jnp.int32)
    nk = e_flat.shape[0]
    experts = jnp.arange(N_EXPERTS, dtype=jnp.int32)
    counts = jnp.sum((e_flat[:, None] == experts[None, :]).astype(jnp.int32), axis=0)
    groups = (counts + MOE_G - 1) // MOE_G
    per_item = MOE_RMAX // MOE_G
    n_items_e = (groups + per_item - 1) // per_item
    item_end = jnp.cumsum(n_items_e)
    item_start = item_end - n_items_e
    n_items = item_end[-1]

    item = jnp.arange(MOE_ITEMS, dtype=jnp.int32)
    used = item < n_items
    e_item = jnp.minimum(jnp.searchsorted(item_end, item, side='right'), N_EXPERTS - 1).astype(jnp.int32)
    e_last = e_item[jnp.maximum(n_items - 1, 0)]
    part = item - item_start[e_item]
    rows = jnp.where(used, jnp.clip(groups[e_item] - part * per_item, 0, per_item) * MOE_G, 0)
    meta = jnp.concatenate([jnp.where(used, e_item, e_last), rows]).astype(jnp.int32)

    order = jnp.argsort(e_flat, stable=True).astype(jnp.int32)
    start = jnp.cumsum(counts) - counts
    window0 = jnp.where(used, start[e_item] + part * MOE_RMAX, 0)
    n_real = jnp.where(used, jnp.clip(counts[e_item] - part * MOE_RMAX, 0, MOE_RMAX), 0)
    zero = jnp.zeros((1,), jnp.int32)
    win = jnp.concatenate([zero, window0, zero, n_real]).astype(jnp.int32)
    tok_tab, dst_tab = _row_tables(win, order)
    shape = (MOE_ITEMS + 1, 1, MOE_RMAX)
    return gates, meta, tok_tab.reshape(shape), dst_tab.reshape(shape)


def kernel(x, c, w_ada, b_ada, w_in, sgu_ln_g, sgu_ln_b, w_spatial, b_spatial, w_o, ln1_g, ln1_b,
           w_router, b_router, w_exp1, b_exp1, w_exp2, b_exp2, ln2_g, ln2_b):
    depth = w_ada.shape[0]
    assert x.shape == (1, SEQ, D_MODEL)
    xs = x.reshape(SEQ, D_MODEL)
    for l in range(depth):
        ada = _ada(c, w_ada[l], b_ada[l])
        shift1, scale1, gate1, shift2, scale2, gate2 = jnp.split(ada, 6, axis=-1)

        proj = _in_proj(xs, scale1, shift1, w_in[l].astype(jnp.bfloat16))
        y_a = _gmlp(proj, sgu_ln_g[l], sgu_ln_b[l], w_spatial[l], b_spatial[l])
        y_b = _attention(proj)
        x1, h_packed, logits = _out_proj(y_a, y_b, w_o[l].astype(jnp.bfloat16), xs, gate1,
                                         ln1_g[l].reshape(1, -1), ln1_b[l].reshape(1, -1), scale2, shift2,
                                         w_router[l], b_router[l].reshape(1, -1))

        gates, meta, tok_tab, dst_tab = _route(logits)
        y_packed = _experts(h_packed, meta, tok_tab, dst_tab, w_exp1[l], b_exp1[l], w_exp2[l], b_exp2[l])
        xs = _combine(y_packed, gates, x1, gate2, ln2_g[l].reshape(1, -1), ln2_b[l].reshape(1, -1))
    return xs.reshape(x.shape)
```

```python
import math

import jax
import jax.numpy as jnp
from jax import lax
from jax.experimental import pallas as pl
from jax.experimental.pallas import tpu as pltpu

D_MODEL = 2048
SEQ = 8192
D_GMLP = 1024
GMLP_GROUPS = 8
GROUP_DIM = 128
CHUNK = 128
D_ATTN = 1024
HEAD_DIM = 128
N_HEADS = 8
BRANCHES = ((128, 1), (512, 4), (2048, 16))
BLK = 128
SPAN = 16 * BLK
D_IN_PROJ = 2 * D_GMLP + 3 * D_ATTN
N_EXPERTS = 32
TOP_K = 4
D_EXPERT = 2048
SWIGLU_LIMIT = 7.0
SWIGLU_ALPHA = 1.702
LN_EPS = 1e-5
DEEPNORM_ALPHA = 2.0 ** 0.25
NEG = -1e30

LANES = 128
VMEM_LIMIT = 56 * 1024 * 1024

ADA_TN = 1536
PROJ_TM = 1024
PROJ_TN = 1024
OUT_TM = 512
ATTN_UNROLL = 8
ROW_SUB = 8
D_PACK = D_MODEL // 2
MOE_G = 128
MOE_RMAX = 1280
MOE_ITEMS = 64
MOE_TN = 256
MOE_CHUNKS = D_EXPERT // MOE_TN
MOE_SUBTILES = (512, 256, 128)
X_ROWS = SEQ * TOP_K + N_EXPERTS * MOE_G
Y_ROWS = SEQ * TOP_K + 2 * MOE_RMAX
DISPATCH_TM = 256
COMBINE_TM = 256


def _params(*sem):
    return pltpu.CompilerParams(dimension_semantics=sem, vmem_limit_bytes=VMEM_LIMIT)


def _layer_norm(x, g, b):
    mu = jnp.mean(x, axis=-1, keepdims=True)
    xc = x - mu
    var = jnp.mean(xc * xc, axis=-1, keepdims=True)
    return xc * lax.rsqrt(var + LN_EPS) * g + b


def _pack_rows(x):
    r = x.astype(jnp.bfloat16).astype(jnp.float32)
    bits = lax.bitcast_convert_type(r, jnp.uint32)
    return (bits[:, D_PACK:] & jnp.uint32(0xFFFF0000)) | (bits[:, :D_PACK] >> 16)


def _unpack_words(u):
    lo = lax.bitcast_convert_type(u << 16, jnp.float32)
    hi = lax.bitcast_convert_type(u & jnp.uint32(0xFFFF0000), jnp.float32)
    return lo, hi


def _store_packed(ref, lead, row0, packed):
    m = packed.shape[0]
    for c in range(ROW_SUB):
        rows = pl.ds(row0 * ROW_SUB + c, m, stride=ROW_SUB)
        ref[lead + (rows, slice(None))] = packed[:, c * LANES:(c + 1) * LANES]


def _load_packed(ref, lead, row0, m):
    cols = [ref[lead + (pl.ds(row0 * ROW_SUB + c, m, stride=ROW_SUB), slice(None))] for c in range(ROW_SUB)]
    return jnp.concatenate(cols, axis=1)


def _ada_kernel(c_ref, w_ref, b_ref, o_ref):
    c = c_ref[...]
    s = c * jax.nn.sigmoid(c)
    o_ref[...] = jnp.sum(s * w_ref[...], axis=0, keepdims=True) + b_ref[...]


def _ada(c, w_ada, b_ada):
    n = w_ada.shape[1]
    return pl.pallas_call(
        _ada_kernel,
        grid=(n // ADA_TN,),
        in_specs=[
            pl.BlockSpec((D_MODEL, 1), lambda j: (0, 0)),
            pl.BlockSpec((D_MODEL, ADA_TN), lambda j: (0, j)),
            pl.BlockSpec((1, ADA_TN), lambda j: (0, j)),
        ],
        out_specs=pl.BlockSpec((1, ADA_TN), lambda j: (0, j)),
        out_shape=jax.ShapeDtypeStruct((1, n), jnp.float32),
        compiler_params=_params("arbitrary"),
        name="ada",
    )(c.reshape(D_MODEL, 1), w_ada, b_ada.reshape(1, n))


def _in_proj_kernel(x_ref, sc_ref, sh_ref, w_ref, o_ref, h_ref):
    @pl.when(pl.program_id(1) == 0)
    def _():
        h_ref[...] = (x_ref[...] * (1.0 + sc_ref[...]) + sh_ref[...]).astype(jnp.bfloat16)

    o_ref[...] = jnp.dot(h_ref[...], w_ref[...], preferred_element_type=jnp.float32)


def _in_proj(x, scale1, shift1, w_in_bf16):
    return pl.pallas_call(
        _in_proj_kernel,
        grid=(SEQ // PROJ_TM, D_IN_PROJ // PROJ_TN),
        in_specs=[
            pl.BlockSpec((PROJ_TM, D_MODEL), lambda i, j: (i, 0)),
            pl.BlockSpec((1, D_MODEL), lambda i, j: (0, 0)),
            pl.BlockSpec((1, D_MODEL), lambda i, j: (0, 0)),
            pl.BlockSpec((D_MODEL, PROJ_TN), lambda i, j: (0, j)),
        ],
        out_specs=pl.BlockSpec((PROJ_TM, PROJ_TN), lambda i, j: (i, j)),
        out_shape=jax.ShapeDtypeStruct((SEQ, D_IN_PROJ), jnp.float32),
        scratch_shapes=[pltpu.VMEM((PROJ_TM, D_MODEL), jnp.bfloat16)],
        compiler_params=_params("arbitrary", "arbitrary"),
        name="in_proj",
    )(x, scale1, shift1, w_in_bf16)


def _gelu(x):
    return 0.5 * x * (1.0 + lax.erf(x * (1.0 / math.sqrt(2.0))))


def _gmlp_kernel(ua_ref, va_ref, g_ref, b_ref, w_ref, bs_ref, o_ref):
    row = lax.broadcasted_iota(jnp.int32, (CHUNK, CHUNK), 0)
    col = lax.broadcasted_iota(jnp.int32, (CHUNK, CHUNK), 1)
    causal = col <= row
    for g in range(GMLP_GROUPS):
        sl = slice(g * GROUP_DIM, (g + 1) * GROUP_DIM)
        v = _layer_norm(_gelu(va_ref[:, sl]), g_ref[g:g + 1, :], b_ref[g:g + 1, :])
        w = jnp.where(causal, w_ref[g], 0.0).astype(jnp.bfloat16)
        s = jnp.dot(w, v.astype(jnp.bfloat16), preferred_element_type=jnp.float32)
        s = s + bs_ref[:, g:g + 1]
        o_ref[:, sl] = (_gelu(ua_ref[:, sl]) * s).astype(o_ref.dtype)


def _gmlp(proj, ln_g, ln_b, w_spatial, b_spatial):
    return pl.pallas_call(
        _gmlp_kernel,
        grid=(SEQ // CHUNK,),
        in_specs=[
            pl.BlockSpec((CHUNK, D_GMLP), lambda n: (n, 0)),
            pl.BlockSpec((CHUNK, D_GMLP), lambda n: (n, 1)),
            pl.BlockSpec((GMLP_GROUPS, GROUP_DIM), lambda n: (0, 0)),
            pl.BlockSpec((GMLP_GROUPS, GROUP_DIM), lambda n: (0, 0)),
            pl.BlockSpec((GMLP_GROUPS, CHUNK, CHUNK), lambda n: (0, 0, 0)),
            pl.BlockSpec((CHUNK, GMLP_GROUPS), lambda n: (0, 0)),
        ],
        out_specs=pl.BlockSpec((CHUNK, D_GMLP), lambda n: (n, 0)),
        out_shape=jax.ShapeDtypeStruct((SEQ, D_GMLP), jnp.bfloat16),
        compiler_params=_params("arbitrary"),
        name="gmlp",
    )(proj, proj, ln_g, ln_b, w_spatial, b_spatial.T)


def _attn_kernel(q_ref, k_ref, v_ref, o_ref, acc_ref, m_ref, l_ref):
    head = pl.program_id(0)
    span = pl.program_id(1)
    log2e = 1.0 / math.log(2.0)
    head_no = (jnp.zeros((BLK, 2 * BLK), jnp.int32) + (head + 1)).astype(jnp.float32)
    slope = jnp.exp2(head_no * (-8.0 / N_HEADS)) * log2e
    scale = HEAD_DIM ** -0.5 * log2e
    qi = lax.broadcasted_iota(jnp.int32, (BLK, 2 * BLK), 0)
    ki = lax.broadcasted_iota(jnp.int32, (BLK, 2 * BLK), 1)
    step = qi + BLK - ki
    contract_last = (((1,), (1,)), ((), ()))
    ones = jnp.ones((2 * BLK, LANES), jnp.bfloat16)

    for b, (window, d) in enumerate(BRANCHES):
        assert window // d == BLK
        valid = (step >= 0) & (step <= BLK)
        bias = jnp.where(valid, -slope * d * step.astype(jnp.float32), NEG)
        bias_first = jnp.where(ki >= BLK, bias, NEG)

        def tile(t, carry, d=d, b=b, bias=bias, bias_first=bias_first):
            r = t % d
            n = t // d
            q0 = n * (BLK * d) + r
            k0 = span * SPAN + q0
            first = k0 < BLK * d
            kp = jnp.where(first, k0, k0 - BLK * d)
            q = (q_ref[pl.ds(q0, BLK, stride=d), :] * scale).astype(jnp.bfloat16)
            k = jnp.concatenate([k_ref[pl.ds(kp, BLK, stride=d), :], k_ref[pl.ds(k0, BLK, stride=d), :]], axis=0)
            v = jnp.concatenate([v_ref[pl.ds(kp, BLK, stride=d), :], v_ref[pl.ds(k0, BLK, stride=d), :]], axis=0)
            s = lax.dot_general(q, k.astype(jnp.bfloat16), contract_last, preferred_element_type=jnp.float32)
            s = s + jnp.where(first, bias_first, bias)
            m = jnp.max(jnp.maximum(s[:, :BLK], s[:, BLK:]), axis=-1, keepdims=True)
            p = jnp.exp2(s - m).astype(jnp.bfloat16)
            v_one = jnp.concatenate([v.astype(jnp.bfloat16), ones], axis=1)
            pv = jnp.dot(p, v_one, preferred_element_type=jnp.float32)
            rows = pl.ds(q0, BLK, stride=d)
            acc_ref[b, rows, :] = pv[:, :HEAD_DIM]
            l_ref[b, rows, :] = pv[:, HEAD_DIM:]
            m_ref[b, rows, :] = jnp.broadcast_to(m, (BLK, LANES))
            return carry

        lax.fori_loop(0, SPAN // BLK, tile, 0, unroll=ATTN_UNROLL)

    m_all = jnp.maximum(jnp.maximum(m_ref[0], m_ref[1]), m_ref[2])
    num = jnp.zeros((SPAN, HEAD_DIM), jnp.float32)
    den = jnp.zeros((SPAN, LANES), jnp.float32)
    for b in range(len(BRANCHES)):
        w = jnp.exp2(m_ref[b] - m_all)
        num = num + w * acc_ref[b]
        den = den + w * l_ref[b]
    o_ref[...] = (num / den).astype(o_ref.dtype)


def _attention(proj):
    q_col = 2 * D_GMLP // HEAD_DIM
    k_col = q_col + N_HEADS
    v_col = k_col + N_HEADS
    nb = len(BRANCHES)
    return pl.pallas_call(
        _attn_kernel,
        grid=(N_HEADS, SEQ // SPAN),
        in_specs=[
            pl.BlockSpec((SPAN, HEAD_DIM), lambda h, s: (s, q_col + h)),
            pl.BlockSpec((SEQ, HEAD_DIM), lambda h, s: (0, k_col + h)),
            pl.BlockSpec((SEQ, HEAD_DIM), lambda h, s: (0, v_col + h)),
        ],
        out_specs=pl.BlockSpec((SPAN, HEAD_DIM), lambda h, s: (s, h)),
        out_shape=jax.ShapeDtypeStruct((SEQ, D_ATTN), jnp.bfloat16),
        scratch_shapes=[
            pltpu.VMEM((nb, SPAN, HEAD_DIM), jnp.float32),
            pltpu.VMEM((nb, SPAN, LANES), jnp.float32),
            pltpu.VMEM((nb, SPAN, LANES), jnp.float32),
        ],
        compiler_params=_params("arbitrary", "arbitrary"),
        name="attn",
    )(proj, proj, proj)


def _split_bf16(x):
    hi = x.astype(jnp.bfloat16)
    lo = (x - hi.astype(jnp.float32)).astype(jnp.bfloat16)
    return hi, lo


def _out_proj_kernel(ya_ref, yb_ref, wa_ref, wb_ref, x_ref, gate_ref, g_ref, b_ref, sc_ref, sh_ref,
                     wr_ref, br_ref, x1_ref, hp_ref, lg_ref):
    mix = jnp.dot(ya_ref[...], wa_ref[...], preferred_element_type=jnp.float32)
    mix = mix + jnp.dot(yb_ref[...], wb_ref[...], preferred_element_type=jnp.float32)
    x1 = _layer_norm(DEEPNORM_ALPHA * x_ref[...] + gate_ref[...] * mix, g_ref[...], b_ref[...])
    x1_ref[...] = x1
    h = x1 * (1.0 + sc_ref[...]) + sh_ref[...]
    _store_packed(hp_ref, (), 0, _pack_rows(h))
    h_hi, h_lo = _split_bf16(h)
    w_hi, w_lo = _split_bf16(wr_ref[...])
    lg = jnp.dot(h_hi, w_hi, preferred_element_type=jnp.float32)
    lg = lg + jnp.dot(h_hi, w_lo, preferred_element_type=jnp.float32)
    lg = lg + jnp.dot(h_lo, w_hi, preferred_element_type=jnp.float32)
    lg_ref[...] = lg + br_ref[...]


def _out_proj(y_a, y_b, w_o_bf16, x, gate1, ln_g, ln_b, scale2, shift2, w_router, b_router):
    row = lambda i: (i, 0)
    fixed = lambda i: (0, 0)
    vec = pl.BlockSpec((1, D_MODEL), fixed)
    return pl.pallas_call(
        _out_proj_kernel,
        grid=(SEQ // OUT_TM,),
        in_specs=[
            pl.BlockSpec((OUT_TM, D_GMLP), row),
            pl.BlockSpec((OUT_TM, D_ATTN), row),
            pl.BlockSpec((D_GMLP, D_MODEL), lambda i: (0, 0)),
            pl.BlockSpec((D_ATTN, D_MODEL), lambda i: (1, 0)),
            pl.BlockSpec((OUT_TM, D_MODEL), row),
            vec, vec, vec, vec, vec,
            pl.BlockSpec((D_MODEL, N_EXPERTS), fixed),
            pl.BlockSpec((1, N_EXPERTS), fixed),
        ],
        out_specs=[
            pl.BlockSpec((OUT_TM, D_MODEL), row),
            pl.BlockSpec((OUT_TM * ROW_SUB, LANES), row),
            pl.BlockSpec((OUT_TM, N_EXPERTS), row),
        ],
        out_shape=[
            jax.ShapeDtypeStruct((SEQ, D_MODEL), jnp.float32),
            jax.ShapeDtypeStruct((SEQ * ROW_SUB, LANES), jnp.uint32),
            jax.ShapeDtypeStruct((SEQ, N_EXPERTS), jnp.float32),
        ],
        compiler_params=_params("arbitrary"),
        name="out_proj",
    )(y_a, y_b, w_o_bf16, w_o_bf16, x, gate1, ln_g, ln_b, scale2, shift2, w_router, b_router)


def _experts_kernel(meta_ref, dstp_ref, x_hbm, wg_ref, wl_ref, bg_ref, bl_ref, w2_ref,
                    b2_ref, y_hbm, xs_ref, os_ref, acc_ref, gsem, ssem):
    i = pl.program_id(0)
    j = pl.program_id(1)
    n_items = pl.num_programs(0)
    last_j = pl.num_programs(1) - 1

    def item_rows(k):
        inside = (k >= 0) & (k < n_items)
        return jnp.where(inside, meta_ref[MOE_ITEMS + jnp.clip(k, 0, n_items - 1)], 0)

    rows = item_rows(i)
    rows_prev = item_rows(i - 1)
    rows_prev2 = item_rows(i - 2)
    p = i % 2
    q = 1 - p
    group_sub = MOE_G * ROW_SUB

    def load_rows(k, slot):
        first = meta_ref[2 * MOE_ITEMS + jnp.clip(k, 0, n_items - 1)] * ROW_SUB

        def load_group(t, c):
            src = pl.multiple_of(first + t * group_sub, group_sub)
            dst = pl.multiple_of(t * group_sub, group_sub)
            pltpu.make_async_copy(x_hbm.at[pl.ds(src, group_sub), :], xs_ref.at[slot, pl.ds(dst, group_sub), :],
                                  gsem.at[slot]).start()
            return c
        lax.fori_loop(0, item_rows(k) // MOE_G, load_group, 0)

    def scatter(slot, row, queue=1):
        entry = dstp_ref[0, 0, row]
        src = pl.multiple_of(row * ROW_SUB, ROW_SUB)
        dst = pl.multiple_of(entry, ROW_SUB)
        pltpu.make_async_copy(os_ref.at[slot, pl.ds(src, ROW_SUB), :], y_hbm.at[pl.ds(dst, ROW_SUB), :],
                              ssem.at[slot]).start(priority=queue)
        return entry

    def wait_rows(buf_ref, sem, slot, n):
        group = buf_ref.at[slot, pl.ds(0, group_sub), :]

        def wait_group(t, c):
            pltpu.make_async_copy(group, group, sem.at[slot]).wait()
            return c
        lax.fori_loop(0, n // MOE_G, wait_group, 0)

    def each_row(lo, hi, fn):
        lax.fori_loop(lo, jnp.maximum(lo, hi), lambda r, c: (fn(r), c)[1], 0)

    @pl.when((i == 0) & (j == 0))
    def _():
        os_ref[...] = jnp.zeros_like(os_ref)
        for half in range(2):
            fill = pltpu.make_async_copy(
                os_ref.at[half],
                y_hbm.at[pl.ds((SEQ * TOP_K + half * MOE_RMAX) * ROW_SUB, MOE_RMAX * ROW_SUB), :],
                ssem.at[half])
            fill.start()
            fill.wait()
        load_rows(0, 0)

    @pl.when(j == 0)
    def _():
        wait_rows(xs_ref, gsem, p, rows)
        load_rows(i + 1, q)

    def sub_tile(a, m):
        a = pl.multiple_of(a, MOE_G)
        base = (j * rows + a) // ROW_SUB
        lag = 0
        for u in range(m // ROW_SUB):
            lag = scatter(q, base + u + lag, queue=u % 2) >> 31
        x_lo, x_hi = _unpack_words(_load_packed(xs_ref, (p,), a, m))
        x_lo = x_lo.astype(jnp.bfloat16)
        x_hi = x_hi.astype(jnp.bfloat16)

        def up(w_ref, b_ref):
            y = jnp.dot(x_lo, w_ref[:D_PACK, :].astype(jnp.bfloat16), preferred_element_type=jnp.float32)
            y = y + jnp.dot(x_hi, w_ref[D_PACK:, :].astype(jnp.bfloat16), preferred_element_type=jnp.float32)
            return y + b_ref[...]

        glu = jnp.minimum(up(wg_ref, bg_ref), SWIGLU_LIMIT)
        lin = jnp.clip(up(wl_ref, bl_ref), -SWIGLU_LIMIT, SWIGLU_LIMIT)
        act = glu * jax.nn.sigmoid(SWIGLU_ALPHA * glu) * (lin + 1.0)
        acc_ref[pl.ds(a, m), :] += jnp.dot(act.astype(jnp.bfloat16), w2_ref[...].astype(jnp.bfloat16),
                                           preferred_element_type=jnp.float32)

    @pl.when(rows > 0)
    def _():
        @pl.when(j == 0)
        def _():
            bias = jnp.broadcast_to(b2_ref[...], (MOE_G, D_MODEL))

            def init(t, c):
                acc_ref[pl.ds(pl.multiple_of(t * MOE_G, MOE_G), MOE_G), :] = bias
                return c
            lax.fori_loop(0, rows // MOE_G, init, 0)

        main = MOE_SUBTILES[0]
        n_main = rows // main
        lax.fori_loop(0, n_main, lambda t, c: (sub_tile(t * main, main), c)[1], 0)
        done = n_main * main
        for m in MOE_SUBTILES[1:]:
            has = ((rows - done) // m) % 2 == 1
            pl.when(has)(lambda done=done, m=m: sub_tile(done, m))
            done = done + jnp.where(has, m, 0)

    @pl.when(j == last_j)
    def _():
        wait_rows(os_ref, ssem, p, jnp.maximum(rows_prev, rows_prev2))

        def pack(t, c):
            a = pl.multiple_of(t * MOE_G, MOE_G)
            _store_packed(os_ref, (p,), a, _pack_rows(acc_ref[pl.ds(a, MOE_G), :]))
            return c
        lax.fori_loop(0, rows // MOE_G, pack, 0)
        each_row(rows, rows_prev, lambda r: scatter(q, r))

    @pl.when((i == n_items - 1) & (j == last_j))
    def _():
        wait_rows(os_ref, ssem, q, rows_prev)


def _experts(x_sorted, meta, dst_tab, w1, b1, w2, b2):
    n_j = MOE_CHUNKS

    def col(i, j, m):
        return jnp.where(m[MOE_ITEMS + i] > 0, j, n_j - 1)

    def expert(i, m):
        return m[i]

    tab = lambda f: pl.BlockSpec((1, 1, MOE_RMAX), f, memory_space=pltpu.SMEM)
    return pl.pallas_call(
        _experts_kernel,
        grid_spec=pltpu.PrefetchScalarGridSpec(
            num_scalar_prefetch=1,
            grid=(MOE_ITEMS, n_j),
            in_specs=[
                tab(lambda i, j, m: (i, 0, 0)),
                pl.BlockSpec(memory_space=pl.ANY),
                pl.BlockSpec((None, D_MODEL, MOE_TN), lambda i, j, m: (expert(i, m), 0, col(i, j, m))),
                pl.BlockSpec((None, D_MODEL, MOE_TN), lambda i, j, m: (expert(i, m), 0, n_j + col(i, j, m))),
                pl.BlockSpec((None, 1, MOE_TN), lambda i, j, m: (expert(i, m), 0, col(i, j, m))),
                pl.BlockSpec((None, 1, MOE_TN), lambda i, j, m: (expert(i, m), 0, n_j + col(i, j, m))),
                pl.BlockSpec((None, MOE_TN, D_MODEL), lambda i, j, m: (expert(i, m), col(i, j, m), 0)),
                pl.BlockSpec((None, 1, D_MODEL), lambda i, j, m: (expert(i, m), 0, 0)),
            ],
            out_specs=pl.BlockSpec(memory_space=pl.ANY),
            scratch_shapes=[
                pltpu.VMEM((2, MOE_RMAX * ROW_SUB, LANES), jnp.uint32),
                pltpu.VMEM((2, MOE_RMAX * ROW_SUB, LANES), jnp.uint32),
                pltpu.VMEM((MOE_RMAX, D_MODEL), jnp.float32),
                pltpu.SemaphoreType.DMA((2,)),
                pltpu.SemaphoreType.DMA((2,)),
            ],
        ),
        out_shape=jax.ShapeDtypeStruct((Y_ROWS * ROW_SUB, LANES), jnp.uint32),
        compiler_params=_params("arbitrary", "arbitrary"),
        name="experts",
    )(meta, dst_tab, x_sorted, w1, w1,
      b1.reshape(N_EXPERTS, 1, 2 * D_EXPERT), b1.reshape(N_EXPERTS, 1, 2 * D_EXPERT),
      w2, b2.reshape(N_EXPERTS, 1, D_MODEL))


def _combine_kernel(y0_ref, y1_ref, y2_ref, y3_ref, gates_ref, x1_ref, gate2_ref, g_ref, b_ref, o_ref):
    y_lo = jnp.zeros((COMBINE_TM, D_PACK), jnp.float32)
    y_hi = jnp.zeros((COMBINE_TM, D_PACK), jnp.float32)
    for k, y_ref in enumerate((y0_ref, y1_ref, y2_ref, y3_ref)):
        lo, hi = _unpack_words(_load_packed(y_ref, (), 0, COMBINE_TM))
        gate = gates_ref[:, k:k + 1]
        y_lo = y_lo + gate * lo
        y_hi = y_hi + gate * hi
    y = jnp.concatenate([y_lo, y_hi], axis=1)
    o_ref[...] = _layer_norm(DEEPNORM_ALPHA * x1_ref[...] + gate2_ref[...] * y, g_ref[...], b_ref[...])


def _combine(y_packed, gates, x1, gate2, ln_g, ln_b):
    n_tiles = SEQ // COMBINE_TM
    row = lambda i: (i, 0)
    vec = pl.BlockSpec((1, D_MODEL), lambda i: (0, 0))
    slot = lambda k: pl.BlockSpec((COMBINE_TM * ROW_SUB, LANES), lambda i: (k * n_tiles + i, 0))
    return pl.pallas_call(
        _combine_kernel,
        grid=(n_tiles,),
        in_specs=[
            slot(0), slot(1), slot(2), slot(3),
            pl.BlockSpec((COMBINE_TM, TOP_K), row),
            pl.BlockSpec((COMBINE_TM, D_MODEL), row),
            vec, vec, vec,
        ],
        out_specs=pl.BlockSpec((COMBINE_TM, D_MODEL), row),
        out_shape=jax.ShapeDtypeStruct((SEQ, D_MODEL), jnp.float32),
        compiler_params=_params("arbitrary"),
        name="combine",
    )(y_packed, y_packed, y_packed, y_packed, gates, x1, gate2, ln_g, ln_b)


TAB_ROWS = MOE_RMAX // LANES
WINDOW_ROWS = 2 * ROW_SUB
assert WINDOW_ROWS > TAB_ROWS and TOP_K == 4


def _row_tables_kernel(win_ref, order_ref, dst_ref):
    b = pl.program_id(0)
    w0 = win_ref[b]
    n_real = win_ref[MOE_ITEMS + 1 + b]
    off = w0 % LANES
    x = order_ref[pl.ds(w0 // LANES, WINDOW_ROWS), :]
    x = pltpu.roll(x, (LANES - off) % LANES, axis=1)
    lane = lax.broadcasted_iota(jnp.int32, (TAB_ROWS, LANES), 1)
    sub = lax.broadcasted_iota(jnp.int32, (TAB_ROWS, LANES), 0)
    flat = jnp.where(lane < LANES - off, x[:TAB_ROWS], x[1:TAB_ROWS + 1])
    r = sub * LANES + lane
    tok = flat >> 2
    slot = flat & 3
    spill = SEQ * TOP_K + ((b + 1) % 2) * MOE_RMAX + r
    dst_ref[...] = jnp.where(r < n_real, slot * SEQ + tok, spill) * ROW_SUB


def _row_tables(win, order):
    n_rows = SEQ * TOP_K // LANES
    order2d = jnp.concatenate([order, jnp.zeros((WINDOW_ROWS * LANES,), jnp.int32)]).reshape(-1, LANES)
    return pl.pallas_call(
        _row_tables_kernel,
        grid_spec=pltpu.PrefetchScalarGridSpec(
            num_scalar_prefetch=1,
            grid=(MOE_ITEMS + 1,),
            in_specs=[pl.BlockSpec((n_rows + WINDOW_ROWS, LANES), lambda b, w: (0, 0))],
            out_specs=pl.BlockSpec((None, TAB_ROWS, LANES), lambda b, w: (b, 0, 0)),
        ),
        out_shape=jax.ShapeDtypeStruct((MOE_ITEMS + 1, TAB_ROWS, LANES), jnp.int32),
        compiler_params=_params("arbitrary"),
        name="row_tables",
    )(win, order2d)


def _dispatch_kernel(fill_ref, pos_ref, h_ref, x_hbm, zero_ref, sem, zsem):
    i = pl.program_id(0)
    group_sub = MOE_G * ROW_SUB

    @pl.when(i == 0)
    def _():
        zero_ref[...] = jnp.zeros_like(zero_ref)

        def fill(g):
            dst = pl.multiple_of(g * group_sub, group_sub)
            pltpu.make_async_copy(zero_ref, x_hbm.at[pl.ds(dst, group_sub), :], zsem).start()

        for e in range(N_EXPERTS):
            end = fill_ref[e]
            begin = fill_ref[e - 1] if e else 0
            pl.when(end > begin)(lambda end=end: fill(end - 1))
        lax.fori_loop(fill_ref[N_EXPERTS - 1], X_ROWS // MOE_G, lambda g, c: (fill(g), c)[1], 0)

        def wait_fill(t, c):
            pltpu.make_async_copy(zero_ref, zero_ref, zsem).wait()
            return c
        lax.fori_loop(0, fill_ref[N_EXPERTS], wait_fill, 0)

    def token(t, c):
        src = pl.multiple_of(t * ROW_SUB, ROW_SUB)
        for k in range(TOP_K):
            dst = pl.multiple_of(pos_ref[0, 0, t * TOP_K + k], ROW_SUB)
            pltpu.make_async_copy(h_ref.at[pl.ds(src, ROW_SUB), :], x_hbm.at[pl.ds(dst, ROW_SUB), :],
                                  sem).start(priority=k % 2)
        return c
    lax.fori_loop(0, DISPATCH_TM, token, 0, unroll=4)
    for k in range(TOP_K):
        pltpu.make_async_copy(h_ref, h_ref, sem).wait()


def _dispatch(h_packed, pos, fill):
    n_tiles = SEQ // DISPATCH_TM
    return pl.pallas_call(
        _dispatch_kernel,
        grid_spec=pltpu.PrefetchScalarGridSpec(
            num_scalar_prefetch=1,
            grid=(n_tiles,),
            in_specs=[
                pl.BlockSpec((1, 1, DISPATCH_TM * TOP_K), lambda i, f: (i, 0, 0), memory_space=pltpu.SMEM),
                pl.BlockSpec((DISPATCH_TM * ROW_SUB, LANES), lambda i, f: (i, 0)),
            ],
            out_specs=pl.BlockSpec(memory_space=pl.ANY),
            scratch_shapes=[
                pltpu.VMEM((MOE_G * ROW_SUB, LANES), jnp.uint32),
                pltpu.SemaphoreType.DMA(()),
                pltpu.SemaphoreType.DMA(()),
            ],
        ),
        out_shape=jax.ShapeDtypeStruct((X_ROWS * ROW_SUB, LANES), jnp.uint32),
        compiler_params=_params("arbitrary"),
        name="dispatch",
    )(fill, pos.reshape(n_tiles, 1, DISPATCH_TM * TOP_K), h_packed)


def _route(logits):
    top_val, top_idx = lax.top_k(logits, TOP_K)
    gates = jax.nn.softmax(top_val, axis=-1)
    e_flat = top_idx.reshape(-1).astype(jnp.int32)
    experts = jnp.arange(N_EXPERTS, dtype=jnp.int32)
    counts = jnp.sum((e_flat[:, None] == experts[None, :]).astype(jnp.int32), axis=0)
    groups = (counts + MOE_G - 1) // MOE_G
    group_end = jnp.cumsum(groups)
    row0 = (group_end - groups) * MOE_G
    per_item = MOE_RMAX // MOE_G
    n_items_e = (groups + per_item - 1) // per_item
    item_end = jnp.cumsum(n_items_e)
    item_start = item_end - n_items_e
    n_items = item_end[-1]

    item = jnp.arange(MOE_ITEMS, dtype=jnp.int32)
    used = item < n_items
    e_item = jnp.minimum(jnp.searchsorted(item_end, item, side='right'), N_EXPERTS - 1).astype(jnp.int32)
    e_last = e_item[jnp.maximum(n_items - 1, 0)]
    part = item - item_start[e_item]
    rows = jnp.where(used, jnp.clip(groups[e_item] - part * per_item, 0, per_item) * MOE_G, 0)
    first_row = jnp.where(used, row0[e_item] + part * MOE_RMAX, 0)
    meta = jnp.concatenate([jnp.where(used, e_item, e_last), rows, first_row]).astype(jnp.int32)

    order = jnp.argsort(e_flat, stable=True).astype(jnp.int32)
    rank = jnp.argsort(order).astype(jnp.int32)
    start = jnp.cumsum(counts) - counts
    pos = (row0[e_flat] + rank - start[e_flat]) * ROW_SUB
    n_fill = jnp.sum((groups > 0).astype(jnp.int32)) + X_ROWS // MOE_G - group_end[-1]
    fill = jnp.concatenate([group_end, n_fill[None]]).astype(jnp.int32)

    window0 = jnp.where(used, start[e_item] + part * MOE_RMAX, 0)
    n_real = jnp.where(used, jnp.clip(counts[e_item] - part * MOE_RMAX, 0, MOE_RMAX), 0)
    zero = jnp.zeros((1,), jnp.int32)
    win = jnp.concatenate([zero, window0, zero, n_real]).astype(jnp.int32)
    dst_tab = _row_tables(win, order).reshape(MOE_ITEMS + 1, 1, MOE_RMAX)
    return gates, meta, pos.astype(jnp.int32), fill, dst_tab


def kernel(x, c, w_ada, b_ada, w_in, sgu_ln_g, sgu_ln_b, w_spatial, b_spatial, w_o, ln1_g, ln1_b,
           w_router, b_router, w_exp1, b_exp1, w_exp2, b_exp2, ln2_g, ln2_b):
    depth = w_ada.shape[0]
    assert x.shape == (1, SEQ, D_MODEL)
    xs = x.reshape(SEQ, D_MODEL)
    for l in range(depth):
        ada = _ada(c, w_ada[l], b_ada[l])
        shift1, scale1, gate1, shift2, scale2, gate2 = jnp.split(ada, 6, axis=-1)

        proj = _in_proj(xs, scale1, shift1, w_in[l].astype(jnp.bfloat16))
        y_a = _gmlp(proj, sgu_ln_g[l], sgu_ln_b[l], w_spatial[l], b_spatial[l])
        y_b = _attention(proj)
        x1, h_packed, logits = _out_proj(y_a, y_b, w_o[l].astype(jnp.bfloat16), xs, gate1,
                                         ln1_g[l].reshape(1, -1), ln1_b[l].reshape(1, -1), scale2, shift2,
                                         w_router[l], b_router[l].reshape(1, -1))

        gates, meta, pos, fill, dst_tab = _route(logits)
        x_sorted = _dispatch(h_packed, pos, fill)
        y_packed = _experts(x_sorted, meta, dst_tab, w_exp1[l], b_exp1[l], w_exp2[l], b_exp2[l])
        xs = _combine(y_packed, gates, x1, gate2, ln2_g[l].reshape(1, -1), ln2_b[l].reshape(1, -1))
    return xs.reshape(x.shape)
```

```python
import math

import jax
import jax.numpy as jnp
from jax import lax
from jax.experimental import pallas as pl
from jax.experimental.pallas import tpu as pltpu

D_MODEL = 2048
SEQ = 8192
D_GMLP = 1024
GMLP_GROUPS = 8
GROUP_DIM = 128
CHUNK = 128
D_ATTN = 1024
HEAD_DIM = 128
N_HEADS = 8
BRANCHES = ((128, 1), (512, 4), (2048, 16))
BLK = 128
SPAN = 16 * BLK
D_IN_PROJ = 2 * D_GMLP + 3 * D_ATTN
N_EXPERTS = 32
TOP_K = 4
D_EXPERT = 2048
SWIGLU_LIMIT = 7.0
SWIGLU_ALPHA = 1.702
LN_EPS = 1e-5
DEEPNORM_ALPHA = 2.0 ** 0.25
NEG = -1e30

LANES = 128
VMEM_LIMIT = 56 * 1024 * 1024

ADA_TN = 1536
PROJ_TM = 1024
PROJ_TN = 1024
OUT_TM = 512
ATTN_UNROLL = 8
ROW_SUB = 8
D_PACK = D_MODEL // 2
MOE_G = 128
MOE_RMAX = 1152
MOE_ITEMS = 64
MOE_TN = 512
MOE_CHUNKS = D_EXPERT // MOE_TN
MOE_SUBTILES = (512, 256, 128)
X_ROWS = SEQ * TOP_K + N_EXPERTS * MOE_G
Y_ROWS = SEQ * TOP_K + 2 * MOE_RMAX
DISPATCH_TM = 256
COMBINE_TM = 256


def _params(*sem):
    return pltpu.CompilerParams(dimension_semantics=sem, vmem_limit_bytes=VMEM_LIMIT)


def _layer_norm(x, g, b):
    mu = jnp.mean(x, axis=-1, keepdims=True)
    xc = x - mu
    var = jnp.mean(xc * xc, axis=-1, keepdims=True)
    return xc * lax.rsqrt(var + LN_EPS) * g + b


def _pack_rows(x):
    r = x.astype(jnp.bfloat16).astype(jnp.float32)
    bits = lax.bitcast_convert_type(r, jnp.uint32)
    return (bits[:, D_PACK:] & jnp.uint32(0xFFFF0000)) | (bits[:, :D_PACK] >> 16)


def _unpack_words(u):
    lo = lax.bitcast_convert_type(u << 16, jnp.float32)
    hi = lax.bitcast_convert_type(u & jnp.uint32(0xFFFF0000), jnp.float32)
    return lo, hi


def _store_packed(ref, lead, row0, packed):
    m = packed.shape[0]
    for c in range(ROW_SUB):
        rows = pl.ds(row0 * ROW_SUB + c, m, stride=ROW_SUB)
        ref[lead + (rows, slice(None))] = packed[:, c * LANES:(c + 1) * LANES]


def _load_packed(ref, lead, row0, m):
    cols = [ref[lead + (pl.ds(row0 * ROW_SUB + c, m, stride=ROW_SUB), slice(None))] for c in range(ROW_SUB)]
    return jnp.concatenate(cols, axis=1)


def _ada_kernel(c_ref, w_ref, b_ref, o_ref):
    c = c_ref[...]
    s = c * jax.nn.sigmoid(c)
    o_ref[...] = jnp.sum(s * w_ref[...], axis=0, keepdims=True) + b_ref[...]


def _ada(c, w_ada, b_ada):
    n = w_ada.shape[1]
    return pl.pallas_call(
        _ada_kernel,
        grid=(n // ADA_TN,),
        in_specs=[
            pl.BlockSpec((D_MODEL, 1), lambda j: (0, 0)),
            pl.BlockSpec((D_MODEL, ADA_TN), lambda j: (0, j)),
            pl.BlockSpec((1, ADA_TN), lambda j: (0, j)),
        ],
        out_specs=pl.BlockSpec((1, ADA_TN), lambda j: (0, j)),
        out_shape=jax.ShapeDtypeStruct((1, n), jnp.float32),
        compiler_params=_params("arbitrary"),
        name="ada",
    )(c.reshape(D_MODEL, 1), w_ada, b_ada.reshape(1, n))


def _in_proj_kernel(x_ref, sc_ref, sh_ref, w_ref, o_ref, h_ref):
    @pl.when(pl.program_id(1) == 0)
    def _():
        h_ref[...] = (x_ref[...] * (1.0 + sc_ref[...]) + sh_ref[...]).astype(jnp.bfloat16)

    o_ref[...] = jnp.dot(h_ref[...], w_ref[...], preferred_element_type=jnp.float32)


def _in_proj(x, scale1, shift1, w_in_bf16):
    return pl.pallas_call(
        _in_proj_kernel,
        grid=(SEQ // PROJ_TM, D_IN_PROJ // PROJ_TN),
        in_specs=[
            pl.BlockSpec((PROJ_TM, D_MODEL), lambda i, j: (i, 0)),
            pl.BlockSpec((1, D_MODEL), lambda i, j: (0, 0)),
            pl.BlockSpec((1, D_MODEL), lambda i, j: (0, 0)),
            pl.BlockSpec((D_MODEL, PROJ_TN), lambda i, j: (0, j)),
        ],
        out_specs=pl.BlockSpec((PROJ_TM, PROJ_TN), lambda i, j: (i, j)),
        out_shape=jax.ShapeDtypeStruct((SEQ, D_IN_PROJ), jnp.float32),
        scratch_shapes=[pltpu.VMEM((PROJ_TM, D_MODEL), jnp.bfloat16)],
        compiler_params=_params("arbitrary", "arbitrary"),
        name="in_proj",
    )(x, scale1, shift1, w_in_bf16)


def _gelu(x):
    return 0.5 * x * (1.0 + lax.erf(x * (1.0 / math.sqrt(2.0))))


def _gmlp_kernel(ua_ref, va_ref, g_ref, b_ref, w_ref, bs_ref, o_ref):
    row = lax.broadcasted_iota(jnp.int32, (CHUNK, CHUNK), 0)
    col = lax.broadcasted_iota(jnp.int32, (CHUNK, CHUNK), 1)
    causal = col <= row
    for g in range(GMLP_GROUPS):
        sl = slice(g * GROUP_DIM, (g + 1) * GROUP_DIM)
        v = _layer_norm(_gelu(va_ref[:, sl]), g_ref[g:g + 1, :], b_ref[g:g + 1, :])
        w = jnp.where(causal, w_ref[g], 0.0).astype(jnp.bfloat16)
        s = jnp.dot(w, v.astype(jnp.bfloat16), preferred_element_type=jnp.float32)
        s = s + bs_ref[:, g:g + 1]
        o_ref[:, sl] = (_gelu(ua_ref[:, sl]) * s).astype(o_ref.dtype)


def _gmlp(proj, ln_g, ln_b, w_spatial, b_spatial):
    return pl.pallas_call(
        _gmlp_kernel,
        grid=(SEQ // CHUNK,),
        in_specs=[
            pl.BlockSpec((CHUNK, D_GMLP), lambda n: (n, 0)),
            pl.BlockSpec((CHUNK, D_GMLP), lambda n: (n, 1)),
            pl.BlockSpec((GMLP_GROUPS, GROUP_DIM), lambda n: (0, 0)),
            pl.BlockSpec((GMLP_GROUPS, GROUP_DIM), lambda n: (0, 0)),
            pl.BlockSpec((GMLP_GROUPS, CHUNK, CHUNK), lambda n: (0, 0, 0)),
            pl.BlockSpec((CHUNK, GMLP_GROUPS), lambda n: (0, 0)),
        ],
        out_specs=pl.BlockSpec((CHUNK, D_GMLP), lambda n: (n, 0)),
        out_shape=jax.ShapeDtypeStruct((SEQ, D_GMLP), jnp.bfloat16),
        compiler_params=_params("arbitrary"),
        name="gmlp",
    )(proj, proj, ln_g, ln_b, w_spatial, b_spatial.T)


def _attn_kernel(q_ref, k_ref, v_ref, o_ref, acc_ref, m_ref, l_ref):
    head = pl.program_id(0)
    span = pl.program_id(1)
    log2e = 1.0 / math.log(2.0)
    head_no = (jnp.zeros((BLK, 2 * BLK), jnp.int32) + (head + 1)).astype(jnp.float32)
    slope = jnp.exp2(head_no * (-8.0 / N_HEADS)) * log2e
    scale = HEAD_DIM ** -0.5 * log2e
    qi = lax.broadcasted_iota(jnp.int32, (BLK, 2 * BLK), 0)
    ki = lax.broadcasted_iota(jnp.int32, (BLK, 2 * BLK), 1)
    step = qi + BLK - ki
    contract_last = (((1,), (1,)), ((), ()))
    ones = jnp.ones((2 * BLK, LANES), jnp.bfloat16)

    for b, (window, d) in enumerate(BRANCHES):
        assert window // d == BLK
        valid = (step >= 0) & (step <= BLK)
        bias = jnp.where(valid, -slope * d * step.astype(jnp.float32), NEG)
        bias_first = jnp.where(ki >= BLK, bias, NEG)

        def tile(t, carry, d=d, b=b, bias=bias, bias_first=bias_first):
            r = t % d
            n = t // d
            q0 = n * (BLK * d) + r
            k0 = span * SPAN + q0
            first = k0 < BLK * d
            kp = jnp.where(first, k0, k0 - BLK * d)
            q = (q_ref[pl.ds(q0, BLK, stride=d), :] * scale).astype(jnp.bfloat16)
            k = jnp.concatenate([k_ref[pl.ds(kp, BLK, stride=d), :], k_ref[pl.ds(k0, BLK, stride=d), :]], axis=0)
            v = jnp.concatenate([v_ref[pl.ds(kp, BLK, stride=d), :], v_ref[pl.ds(k0, BLK, stride=d), :]], axis=0)
            s = lax.dot_general(q, k.astype(jnp.bfloat16), contract_last, preferred_element_type=jnp.float32)
            s = s + jnp.where(first, bias_first, bias)
            m = jnp.max(jnp.maximum(s[:, :BLK], s[:, BLK:]), axis=-1, keepdims=True)
            p = jnp.exp2(s - m).astype(jnp.bfloat16)
            v_one = jnp.concatenate([v.astype(jnp.bfloat16), ones], axis=1)
            pv = jnp.dot(p, v_one, preferred_element_type=jnp.float32)
            rows = pl.ds(q0, BLK, stride=d)
            acc_ref[b, rows, :] = pv[:, :HEAD_DIM]
            l_ref[b, rows, :] = pv[:, HEAD_DIM:]
            m_ref[b, rows, :] = jnp.broadcast_to(m, (BLK, LANES))
            return carry

        lax.fori_loop(0, SPAN // BLK, tile, 0, unroll=ATTN_UNROLL)

    m_all = jnp.maximum(jnp.maximum(m_ref[0], m_ref[1]), m_ref[2])
    num = jnp.zeros((SPAN, HEAD_DIM), jnp.float32)
    den = jnp.zeros((SPAN, LANES), jnp.float32)
    for b in range(len(BRANCHES)):
        w = jnp.exp2(m_ref[b] - m_all)
        num = num + w * acc_ref[b]
        den = den + w * l_ref[b]
    o_ref[...] = (num / den).astype(o_ref.dtype)


def _attention(proj):
    q_col = 2 * D_GMLP // HEAD_DIM
    k_col = q_col + N_HEADS
    v_col = k_col + N_HEADS
    nb = len(BRANCHES)
    return pl.pallas_call(
        _attn_kernel,
        grid=(N_HEADS, SEQ // SPAN),
        in_specs=[
            pl.BlockSpec((SPAN, HEAD_DIM), lambda h, s: (s, q_col + h)),
            pl.BlockSpec((SEQ, HEAD_DIM), lambda h, s: (0, k_col + h)),
            pl.BlockSpec((SEQ, HEAD_DIM), lambda h, s: (0, v_col + h)),
        ],
        out_specs=pl.BlockSpec((SPAN, HEAD_DIM), lambda h, s: (s, h)),
        out_shape=jax.ShapeDtypeStruct((SEQ, D_ATTN), jnp.bfloat16),
        scratch_shapes=[
            pltpu.VMEM((nb, SPAN, HEAD_DIM), jnp.float32),
            pltpu.VMEM((nb, SPAN, LANES), jnp.float32),
            pltpu.VMEM((nb, SPAN, LANES), jnp.float32),
        ],
        compiler_params=_params("arbitrary", "arbitrary"),
        name="attn",
    )(proj, proj, proj)


def _split_bf16(x):
    hi = x.astype(jnp.bfloat16)
    lo = (x - hi.astype(jnp.float32)).astype(jnp.bfloat16)
    return hi, lo


def _out_proj_kernel(ya_ref, yb_ref, wa_ref, wb_ref, x_ref, gate_ref, g_ref, b_ref, sc_ref, sh_ref,
                     wr_ref, br_ref, x1_ref, hp_ref, lg_ref):
    mix = jnp.dot(ya_ref[...], wa_ref[...], preferred_element_type=jnp.float32)
    mix = mix + jnp.dot(yb_ref[...], wb_ref[...], preferred_element_type=jnp.float32)
    x1 = _layer_norm(DEEPNORM_ALPHA * x_ref[...] + gate_ref[...] * mix, g_ref[...], b_ref[...])
    x1_ref[...] = x1
    h = x1 * (1.0 + sc_ref[...]) + sh_ref[...]
    _store_packed(hp_ref, (), 0, _pack_rows(h))
    h_hi, h_lo = _split_bf16(h)
    w_hi, w_lo = _split_bf16(wr_ref[...])
    lg = jnp.dot(h_hi, w_hi, preferred_element_type=jnp.float32)
    lg = lg + jnp.dot(h_hi, w_lo, preferred_element_type=jnp.float32)
    lg = lg + jnp.dot(h_lo, w_hi, preferred_element_type=jnp.float32)
    lg_ref[...] = lg + br_ref[...]


def _out_proj(y_a, y_b, w_o_bf16, x, gate1, ln_g, ln_b, scale2, shift2, w_router, b_router):
    row = lambda i: (i, 0)
    fixed = lambda i: (0, 0)
    vec = pl.BlockSpec((1, D_MODEL), fixed)
    return pl.pallas_call(
        _out_proj_kernel,
        grid=(SEQ // OUT_TM,),
        in_specs=[
            pl.BlockSpec((OUT_TM, D_GMLP), row),
            pl.BlockSpec((OUT_TM, D_ATTN), row),
            pl.BlockSpec((D_GMLP, D_MODEL), lambda i: (0, 0)),
            pl.BlockSpec((D_ATTN, D_MODEL), lambda i: (1, 0)),
            pl.BlockSpec((OUT_TM, D_MODEL), row),
            vec, vec, vec, vec, vec,
            pl.BlockSpec((D_MODEL, N_EXPERTS), fixed),
            pl.BlockSpec((1, N_EXPERTS), fixed),
        ],
        out_specs=[
            pl.BlockSpec((OUT_TM, D_MODEL), row),
            pl.BlockSpec((OUT_TM * ROW_SUB, LANES), row),
            pl.BlockSpec((OUT_TM, N_EXPERTS), row),
        ],
        out_shape=[
            jax.ShapeDtypeStruct((SEQ, D_MODEL), jnp.float32),
            jax.ShapeDtypeStruct((SEQ * ROW_SUB, LANES), jnp.uint32),
            jax.ShapeDtypeStruct((SEQ, N_EXPERTS), jnp.float32),
        ],
        compiler_params=_params("arbitrary"),
        name="out_proj",
    )(y_a, y_b, w_o_bf16, w_o_bf16, x, gate1, ln_g, ln_b, scale2, shift2, w_router, b_router)


def _experts_kernel(meta_ref, dstp_ref, x_hbm, wg_ref, wl_ref, bg_ref, bl_ref, w2_ref,
                    b2_ref, y_hbm, xs_ref, os_ref, acc_ref, gsem, ssem):
    i = pl.program_id(0)
    j = pl.program_id(1)
    n_items = pl.num_programs(0)
    last_j = pl.num_programs(1) - 1

    def item_rows(k):
        inside = (k >= 0) & (k < n_items)
        return jnp.where(inside, meta_ref[MOE_ITEMS + jnp.clip(k, 0, n_items - 1)], 0)

    rows = item_rows(i)
    rows_prev = item_rows(i - 1)
    p = i % 2
    q = 1 - p
    group_sub = MOE_G * ROW_SUB

    def load_rows(k, slot):
        first = meta_ref[2 * MOE_ITEMS + jnp.clip(k, 0, n_items - 1)] * ROW_SUB

        def load_group(t, c):
            src = pl.multiple_of(first + t * group_sub, group_sub)
            dst = pl.multiple_of(t * group_sub, group_sub)
            pltpu.make_async_copy(x_hbm.at[pl.ds(src, group_sub), :], xs_ref.at[slot, pl.ds(dst, group_sub), :],
                                  gsem.at[slot]).start()
            return c
        lax.fori_loop(0, item_rows(k) // MOE_G, load_group, 0)

    def scatter(slot, row, queue=1):
        entry = dstp_ref[0, 0, row]
        src = pl.multiple_of(row * ROW_SUB, ROW_SUB)
        dst = pl.multiple_of(entry, ROW_SUB)
        pltpu.make_async_copy(os_ref.at[slot, pl.ds(src, ROW_SUB), :], y_hbm.at[pl.ds(dst, ROW_SUB), :],
                              ssem.at[slot]).start(priority=queue)
        return entry

    def wait_rows(buf_ref, sem, slot, n):
        group = buf_ref.at[slot, pl.ds(0, group_sub), :]

        def wait_group(t, c):
            pltpu.make_async_copy(group, group, sem.at[slot]).wait()
            return c
        lax.fori_loop(0, n // MOE_G, wait_group, 0)

    def each_row(lo, hi, fn):
        lax.fori_loop(lo, jnp.maximum(lo, hi), lambda r, c: (fn(r), c)[1], 0)

    @pl.when((i == 0) & (j == 0))
    def _():
        os_ref[...] = jnp.zeros_like(os_ref)
        for half in range(2):
            fill = pltpu.make_async_copy(
                os_ref.at[0],
                y_hbm.at[pl.ds((SEQ * TOP_K + half * MOE_RMAX) * ROW_SUB, MOE_RMAX * ROW_SUB), :],
                ssem.at[0])
            fill.start()
            fill.wait()
        load_rows(0, 0)

    @pl.when(j == 0)
    def _():
        wait_rows(xs_ref, gsem, p, rows)
        load_rows(i + 1, q)

    def sub_tile(a, m):
        a = pl.multiple_of(a, MOE_G)
        base = (j * rows + a) // MOE_CHUNKS
        lag = 0
        for u in range(m // MOE_CHUNKS):
            lag = scatter(0, base + u + lag, queue=u % 2) >> 31
        x_lo, x_hi = _unpack_words(_load_packed(xs_ref, (p,), a, m))
        x = jnp.concatenate([x_lo.astype(jnp.bfloat16), x_hi.astype(jnp.bfloat16)], axis=1)

        def up(w_ref, b_ref):
            return jnp.dot(x, w_ref[...].astype(jnp.bfloat16), preferred_element_type=jnp.float32) + b_ref[...]

        glu = jnp.minimum(up(wg_ref, bg_ref), SWIGLU_LIMIT)
        lin = jnp.clip(up(wl_ref, bl_ref), -SWIGLU_LIMIT, SWIGLU_LIMIT)
        act = glu * jax.nn.sigmoid(SWIGLU_ALPHA * glu) * (lin + 1.0)
        acc_ref[pl.ds(a, m), :] += jnp.dot(act.astype(jnp.bfloat16), w2_ref[...].astype(jnp.bfloat16),
                                           preferred_element_type=jnp.float32)

    @pl.when(rows > 0)
    def _():
        @pl.when(j == 0)
        def _():
            bias = jnp.broadcast_to(b2_ref[...], (MOE_G, D_MODEL))

            def init(t, c):
                acc_ref[pl.ds(pl.multiple_of(t * MOE_G, MOE_G), MOE_G), :] = bias
                return c
            lax.fori_loop(0, rows // MOE_G, init, 0)

        main = MOE_SUBTILES[0]
        n_main = rows // main
        lax.fori_loop(0, n_main, lambda t, c: (sub_tile(t * main, main), c)[1], 0)
        done = n_main * main
        for m in MOE_SUBTILES[1:]:
            has = ((rows - done) // m) % 2 == 1
            pl.when(has)(lambda done=done, m=m: sub_tile(done, m))
            done = done + jnp.where(has, m, 0)

    @pl.when(j == last_j)
    def _():
        each_row(rows, rows_prev, lambda r: scatter(0, r))
        wait_rows(os_ref, ssem, 0, jnp.maximum(rows, rows_prev))

        def pack(t, c):
            a = pl.multiple_of(t * MOE_G, MOE_G)
            _store_packed(os_ref, (0,), a, _pack_rows(acc_ref[pl.ds(a, MOE_G), :]))
            return c
        lax.fori_loop(0, rows // MOE_G, pack, 0)


def _experts(x_sorted, meta, dst_tab, w1, b1, w2, b2):
    n_j = MOE_CHUNKS

    def col(i, j, m):
        return jnp.where(m[MOE_ITEMS + i] > 0, j, n_j - 1)

    def expert(i, m):
        return m[i]

    tab = lambda f: pl.BlockSpec((1, 1, MOE_RMAX), f, memory_space=pltpu.SMEM)
    return pl.pallas_call(
        _experts_kernel,
        grid_spec=pltpu.PrefetchScalarGridSpec(
            num_scalar_prefetch=1,
            grid=(MOE_ITEMS, n_j),
            in_specs=[
                tab(lambda i, j, m: (i, 0, 0)),
                pl.BlockSpec(memory_space=pl.ANY),
                pl.BlockSpec((None, D_MODEL, MOE_TN), lambda i, j, m: (expert(i, m), 0, col(i, j, m))),
                pl.BlockSpec((None, D_MODEL, MOE_TN), lambda i, j, m: (expert(i, m), 0, n_j + col(i, j, m))),
                pl.BlockSpec((None, 1, MOE_TN), lambda i, j, m: (expert(i, m), 0, col(i, j, m))),
                pl.BlockSpec((None, 1, MOE_TN), lambda i, j, m: (expert(i, m), 0, n_j + col(i, j, m))),
                pl.BlockSpec((None, MOE_TN, D_MODEL), lambda i, j, m: (expert(i, m), col(i, j, m), 0)),
                pl.BlockSpec((None, 1, D_MODEL), lambda i, j, m: (expert(i, m), 0, 0)),
            ],
            out_specs=pl.BlockSpec(memory_space=pl.ANY),
            scratch_shapes=[
                pltpu.VMEM((2, MOE_RMAX * ROW_SUB, LANES), jnp.uint32),
                pltpu.VMEM((1, MOE_RMAX * ROW_SUB, LANES), jnp.uint32),
                pltpu.VMEM((MOE_RMAX, D_MODEL), jnp.float32),
                pltpu.SemaphoreType.DMA((2,)),
                pltpu.SemaphoreType.DMA((1,)),
            ],
        ),
        out_shape=jax.ShapeDtypeStruct((Y_ROWS * ROW_SUB, LANES), jnp.uint32),
        compiler_params=_params("arbitrary", "arbitrary"),
        name="experts",
    )(meta, dst_tab, x_sorted, w1, w1,
      b1.reshape(N_EXPERTS, 1, 2 * D_EXPERT), b1.reshape(N_EXPERTS, 1, 2 * D_EXPERT),
      w2, b2.reshape(N_EXPERTS, 1, D_MODEL))


def _combine_kernel(y0_ref, y1_ref, y2_ref, y3_ref, gates_ref, x1_ref, gate2_ref, g_ref, b_ref, o_ref):
    y_lo = jnp.zeros((COMBINE_TM, D_PACK), jnp.float32)
    y_hi = jnp.zeros((COMBINE_TM, D_PACK), jnp.float32)
    for k, y_ref in enumerate((y0_ref, y1_ref, y2_ref, y3_ref)):
        lo, hi = _unpack_words(_load_packed(y_ref, (), 0, COMBINE_TM))
        gate = gates_ref[:, k:k + 1]
        y_lo = y_lo + gate * lo
        y_hi = y_hi + gate * hi
    y = jnp.concatenate([y_lo, y_hi], axis=1)
    o_ref[...] = _layer_norm(DEEPNORM_ALPHA * x1_ref[...] + gate2_ref[...] * y, g_ref[...], b_ref[...])


def _combine(y_packed, gates, x1, gate2, ln_g, ln_b):
    n_tiles = SEQ // COMBINE_TM
    row = lambda i: (i, 0)
    vec = pl.BlockSpec((1, D_MODEL), lambda i: (0, 0))
    slot = lambda k: pl.BlockSpec((COMBINE_TM * ROW_SUB, LANES), lambda i: (k * n_tiles + i, 0))
    return pl.pallas_call(
        _combine_kernel,
        grid=(n_tiles,),
        in_specs=[
            slot(0), slot(1), slot(2), slot(3),
            pl.BlockSpec((COMBINE_TM, TOP_K), row),
            pl.BlockSpec((COMBINE_TM, D_MODEL), row),
            vec, vec, vec,
        ],
        out_specs=pl.BlockSpec((COMBINE_TM, D_MODEL), row),
        out_shape=jax.ShapeDtypeStruct((SEQ, D_MODEL), jnp.float32),
        compiler_params=_params("arbitrary"),
        name="combine",
    )(y_packed, y_packed, y_packed, y_packed, gates, x1, gate2, ln_g, ln_b)


TAB_ROWS = MOE_RMAX // LANES
WINDOW_ROWS = 2 * ROW_SUB
assert WINDOW_ROWS > TAB_ROWS and TOP_K == 4


def _row_tables_kernel(win_ref, order_ref, dst_ref):
    b = pl.program_id(0)
    w0 = win_ref[b]
    n_real = win_ref[MOE_ITEMS + 1 + b]
    off = w0 % LANES
    x = order_ref[pl.ds(w0 // LANES, WINDOW_ROWS), :]
    x = pltpu.roll(x, (LANES - off) % LANES, axis=1)
    lane = lax.broadcasted_iota(jnp.int32, (TAB_ROWS, LANES), 1)
    sub = lax.broadcasted_iota(jnp.int32, (TAB_ROWS, LANES), 0)
    flat = jnp.where(lane < LANES - off, x[:TAB_ROWS], x[1:TAB_ROWS + 1])
    r = sub * LANES + lane
    tok = flat >> 2
    slot = flat & 3
    spill = SEQ * TOP_K + ((b + 1) % 2) * MOE_RMAX + r
    dst_ref[...] = jnp.where(r < n_real, slot * SEQ + tok, spill) * ROW_SUB


def _row_tables(win, order):
    n_rows = SEQ * TOP_K // LANES
    order2d = jnp.concatenate([order, jnp.zeros((WINDOW_ROWS * LANES,), jnp.int32)]).reshape(-1, LANES)
    return pl.pallas_call(
        _row_tables_kernel,
        grid_spec=pltpu.PrefetchScalarGridSpec(
            num_scalar_prefetch=1,
            grid=(MOE_ITEMS + 1,),
            in_specs=[pl.BlockSpec((n_rows + WINDOW_ROWS, LANES), lambda b, w: (0, 0))],
            out_specs=pl.BlockSpec((None, TAB_ROWS, LANES), lambda b, w: (b, 0, 0)),
        ),
        out_shape=jax.ShapeDtypeStruct((MOE_ITEMS + 1, TAB_ROWS, LANES), jnp.int32),
        compiler_params=_params("arbitrary"),
        name="row_tables",
    )(win, order2d)


def _dispatch_kernel(fill_ref, pos_ref, h_ref, x_hbm, zero_ref, sem, zsem):
    i = pl.program_id(0)
    group_sub = MOE_G * ROW_SUB

    @pl.when(i == 0)
    def _():
        zero_ref[...] = jnp.zeros_like(zero_ref)

        def fill(g):
            dst = pl.multiple_of(g * group_sub, group_sub)
            pltpu.make_async_copy(zero_ref, x_hbm.at[pl.ds(dst, group_sub), :], zsem).start()

        for e in range(N_EXPERTS):
            end = fill_ref[e]
            begin = fill_ref[e - 1] if e else 0
            pl.when(end > begin)(lambda end=end: fill(end - 1))
        lax.fori_loop(fill_ref[N_EXPERTS - 1], X_ROWS // MOE_G, lambda g, c: (fill(g), c)[1], 0)

        def wait_fill(t, c):
            pltpu.make_async_copy(zero_ref, zero_ref, zsem).wait()
            return c
        lax.fori_loop(0, fill_ref[N_EXPERTS], wait_fill, 0)

    def token(t, c):
        src = pl.multiple_of(t * ROW_SUB, ROW_SUB)
        for k in range(TOP_K):
            dst = pl.multiple_of(pos_ref[0, 0, t * TOP_K + k], ROW_SUB)
            pltpu.make_async_copy(h_ref.at[pl.ds(src, ROW_SUB), :], x_hbm.at[pl.ds(dst, ROW_SUB), :],
                                  sem).start(priority=k % 2)
        return c
    lax.fori_loop(0, DISPATCH_TM, token, 0, unroll=4)
    for k in range(TOP_K):
        pltpu.make_async_copy(h_ref, h_ref, sem).wait()


def _dispatch(h_packed, pos, fill):
    n_tiles = SEQ // DISPATCH_TM
    return pl.pallas_call(
        _dispatch_kernel,
        grid_spec=pltpu.PrefetchScalarGridSpec(
            num_scalar_prefetch=1,
            grid=(n_tiles,),
            in_specs=[
                pl.BlockSpec((1, 1, DISPATCH_TM * TOP_K), lambda i, f: (i, 0, 0), memory_space=pltpu.SMEM),
                pl.BlockSpec((DISPATCH_TM * ROW_SUB, LANES), lambda i, f: (i, 0)),
            ],
            out_specs=pl.BlockSpec(memory_space=pl.ANY),
            scratch_shapes=[
                pltpu.VMEM((MOE_G * ROW_SUB, LANES), jnp.uint32),
                pltpu.SemaphoreType.DMA(()),
                pltpu.SemaphoreType.DMA(()),
            ],
        ),
        out_shape=jax.ShapeDtypeStruct((X_ROWS * ROW_SUB, LANES), jnp.uint32),
        compiler_params=_params("arbitrary"),
        name="dispatch",
    )(fill, pos.reshape(n_tiles, 1, DISPATCH_TM * TOP_K), h_packed)


def _route(logits):
    top_val, top_idx = lax.top_k(logits, TOP_K)
    gates = jax.nn.softmax(top_val, axis=-1)
    e_flat = top_idx.reshape(-1).astype(jnp.int32)
    experts = jnp.arange(N_EXPERTS, dtype=jnp.int32)
    counts = jnp.sum((e_flat[:, None] == experts[None, :]).astype(jnp.int32), axis=0)
    groups = (counts + MOE_G - 1) // MOE_G
    group_end = jnp.cumsum(groups)
    row0 = (group_end - groups) * MOE_G
    per_item = MOE_RMAX // MOE_G
    n_items_e = (groups + per_item - 1) // per_item
    item_end = jnp.cumsum(n_items_e)
    item_start = item_end - n_items_e
    n_items = item_end[-1]

    item = jnp.arange(MOE_ITEMS, dtype=jnp.int32)
    used = item < n_items
    e_item = jnp.minimum(jnp.searchsorted(item_end, item, side='right'), N_EXPERTS - 1).astype(jnp.int32)
    e_last = e_item[jnp.maximum(n_items - 1, 0)]
    part = item - item_start[e_item]
    rows = jnp.where(used, jnp.clip(groups[e_item] - part * per_item, 0, per_item) * MOE_G, 0)
    first_row = jnp.where(used, row0[e_item] + part * MOE_RMAX, 0)
    meta = jnp.concatenate([jnp.where(used, e_item, e_last), rows, first_row]).astype(jnp.int32)

    order = jnp.argsort(e_flat, stable=True).astype(jnp.int32)
    rank = jnp.argsort(order).astype(jnp.int32)
    start = jnp.cumsum(counts) - counts
    pos = (row0[e_flat] + rank - start[e_flat]) * ROW_SUB
    n_fill = jnp.sum((groups > 0).astype(jnp.int32)) + X_ROWS // MOE_G - group_end[-1]
    fill = jnp.concatenate([group_end, n_fill[None]]).astype(jnp.int32)

    window0 = jnp.where(used, start[e_item] + part * MOE_RMAX, 0)
    n_real = jnp.where(used, jnp.clip(counts[e_item] - part * MOE_RMAX, 0, MOE_RMAX), 0)
    zero = jnp.zeros((1,), jnp.int32)
    win = jnp.concatenate([zero, window0, zero, n_real]).astype(jnp.int32)
    dst_tab = _row_tables(win, order).reshape(MOE_ITEMS + 1, 1, MOE_RMAX)
    return gates, meta, pos.astype(jnp.int32), fill, dst_tab


def kernel(x, c, w_ada, b_ada, w_in, sgu_ln_g, sgu_ln_b, w_spatial, b_spatial, w_o, ln1_g, ln1_b,
           w_router, b_router, w_exp1, b_exp1, w_exp2, b_exp2, ln2_g, ln2_b):
    depth = w_ada.shape[0]
    assert x.shape == (1, SEQ, D_MODEL)
    xs = x.reshape(SEQ, D_MODEL)
    for l in range(depth):
        ada = _ada(c, w_ada[l], b_ada[l])
        shift1, scale1, gate1, shift2, scale2, gate2 = jnp.split(ada, 6, axis=-1)

        proj = _in_proj(xs, scale1, shift1, w_in[l].astype(jnp.bfloat16))
        y_a = _gmlp(proj, sgu_ln_g[l], sgu_ln_b[l], w_spatial[l], b_spatial[l])
        y_b = _attention(proj)
        x1, h_packed, logits = _out_proj(y_a, y_b, w_o[l].astype(jnp.bfloat16), xs, gate1,
                                         ln1_g[l].reshape(1, -1), ln1_b[l].reshape(1, -1), scale2, shift2,
                                         w_router[l], b_router[l].reshape(1, -1))

        gates, meta, pos, fill, dst_tab = _route(logits)
        x_sorted = _dispatch(h_packed, pos, fill)
        y_packed = _experts(x_sorted, meta, dst_tab, w_exp1[l], b_exp1[l], w_exp2[l], b_exp2[l])
        xs = _combine(y_packed, gates, x1, gate2, ln2_g[l].reshape(1, -1), ln2_b[l].reshape(1, -1))
    return xs.reshape(x.shape)
```

```python
import math

import jax
import jax.numpy as jnp
from jax import lax
from jax.experimental import pallas as pl
from jax.experimental.pallas import tpu as pltpu

D_MODEL = 2048
SEQ = 8192
D_GMLP = 1024
GMLP_GROUPS = 8
GROUP_DIM = 128
CHUNK = 128
D_ATTN = 1024
HEAD_DIM = 128
N_HEADS = 8
BRANCHES = ((128, 1), (512, 4), (2048, 16))
BLK = 128
SPAN = 16 * BLK
D_IN_PROJ = 2 * D_GMLP + 3 * D_ATTN
N_EXPERTS = 32
TOP_K = 4
D_EXPERT = 2048
SWIGLU_LIMIT = 7.0
SWIGLU_ALPHA = 1.702
LN_EPS = 1e-5
DEEPNORM_ALPHA = 2.0 ** 0.25
NEG = -1e30

LANES = 128
VMEM_LIMIT = 56 * 1024 * 1024

ADA_TN = 1536
PROJ_TM = 1024
PROJ_TN = 1024
OUT_TM = 512
ATTN_UNROLL = 8
ROW_SUB = 8
D_PACK = D_MODEL // 2
MOE_G = 128
MOE_RMAX = 1152
MOE_ITEMS = 64
MOE_TN = 512
MOE_CHUNKS = D_EXPERT // MOE_TN
MOE_SUBTILES = (512, 256, 128)
X_ROWS = SEQ * TOP_K + N_EXPERTS * MOE_G
Y_ROWS = SEQ * TOP_K + 2 * MOE_RMAX
DISPATCH_TM = 256
COMBINE_TM = 256


def _params(*sem):
    return pltpu.CompilerParams(dimension_semantics=sem, vmem_limit_bytes=VMEM_LIMIT)


def _layer_norm(x, g, b):
    mu = jnp.mean(x, axis=-1, keepdims=True)
    xc = x - mu
    var = jnp.mean(xc * xc, axis=-1, keepdims=True)
    return xc * lax.rsqrt(var + LN_EPS) * g + b


def _pack_rows(x):
    r = x.astype(jnp.bfloat16).astype(jnp.float32)
    bits = lax.bitcast_convert_type(r, jnp.uint32)
    return (bits[:, D_PACK:] & jnp.uint32(0xFFFF0000)) | (bits[:, :D_PACK] >> 16)


def _unpack_words(u):
    lo = lax.bitcast_convert_type(u << 16, jnp.float32)
    hi = lax.bitcast_convert_type(u & jnp.uint32(0xFFFF0000), jnp.float32)
    return lo, hi


def _store_packed(ref, lead, row0, packed):
    m = packed.shape[0]
    for c in range(ROW_SUB):
        rows = pl.ds(row0 * ROW_SUB + c, m, stride=ROW_SUB)
        ref[lead + (rows, slice(None))] = packed[:, c * LANES:(c + 1) * LANES]


def _load_packed(ref, lead, row0, m):
    cols = [ref[lead + (pl.ds(row0 * ROW_SUB + c, m, stride=ROW_SUB), slice(None))] for c in range(ROW_SUB)]
    return jnp.concatenate(cols, axis=1)


def _ada_kernel(c_ref, w_ref, b_ref, o_ref):
    c = c_ref[...]
    s = c * jax.nn.sigmoid(c)
    o_ref[...] = jnp.sum(s * w_ref[...], axis=0, keepdims=True) + b_ref[...]


def _ada(c, w_ada, b_ada):
    n = w_ada.shape[1]
    return pl.pallas_call(
        _ada_kernel,
        grid=(n // ADA_TN,),
        in_specs=[
            pl.BlockSpec((D_MODEL, 1), lambda j: (0, 0)),
            pl.BlockSpec((D_MODEL, ADA_TN), lambda j: (0, j)),
            pl.BlockSpec((1, ADA_TN), lambda j: (0, j)),
        ],
        out_specs=pl.BlockSpec((1, ADA_TN), lambda j: (0, j)),
        out_shape=jax.ShapeDtypeStruct((1, n), jnp.float32),
        compiler_params=_params("arbitrary"),
        name="ada",
    )(c.reshape(D_MODEL, 1), w_ada, b_ada.reshape(1, n))


def _in_proj_kernel(x_ref, sc_ref, sh_ref, w_ref, o_ref, h_ref):
    @pl.when(pl.program_id(1) == 0)
    def _():
        h_ref[...] = (x_ref[...] * (1.0 + sc_ref[...]) + sh_ref[...]).astype(jnp.bfloat16)

    o_ref[...] = jnp.dot(h_ref[...], w_ref[...], preferred_element_type=jnp.float32)


def _in_proj(x, scale1, shift1, w_in_bf16):
    return pl.pallas_call(
        _in_proj_kernel,
        grid=(SEQ // PROJ_TM, D_IN_PROJ // PROJ_TN),
        in_specs=[
            pl.BlockSpec((PROJ_TM, D_MODEL), lambda i, j: (i, 0)),
            pl.BlockSpec((1, D_MODEL), lambda i, j: (0, 0)),
            pl.BlockSpec((1, D_MODEL), lambda i, j: (0, 0)),
            pl.BlockSpec((D_MODEL, PROJ_TN), lambda i, j: (0, j)),
        ],
        out_specs=pl.BlockSpec((PROJ_TM, PROJ_TN), lambda i, j: (i, j)),
        out_shape=jax.ShapeDtypeStruct((SEQ, D_IN_PROJ), jnp.float32),
        scratch_shapes=[pltpu.VMEM((PROJ_TM, D_MODEL), jnp.bfloat16)],
        compiler_params=_params("arbitrary", "arbitrary"),
        name="in_proj",
    )(x, scale1, shift1, w_in_bf16)


def _gelu(x):
    return 0.5 * x * (1.0 + lax.erf(x * (1.0 / math.sqrt(2.0))))


def _gmlp_kernel(ua_ref, va_ref, g_ref, b_ref, w_ref, bs_ref, o_ref):
    row = lax.broadcasted_iota(jnp.int32, (CHUNK, CHUNK), 0)
    col = lax.broadcasted_iota(jnp.int32, (CHUNK, CHUNK), 1)
    causal = col <= row
    for g in range(GMLP_GROUPS):
        sl = slice(g * GROUP_DIM, (g + 1) * GROUP_DIM)
        v = _layer_norm(_gelu(va_ref[:, sl]), g_ref[g:g + 1, :], b_ref[g:g + 1, :])
        w = jnp.where(causal, w_ref[g], 0.0).astype(jnp.bfloat16)
        s = jnp.dot(w, v.astype(jnp.bfloat16), preferred_element_type=jnp.float32)
        s = s + bs_ref[:, g:g + 1]
        o_ref[:, sl] = (_gelu(ua_ref[:, sl]) * s).astype(o_ref.dtype)


def _gmlp(proj, ln_g, ln_b, w_spatial, b_spatial):
    return pl.pallas_call(
        _gmlp_kernel,
        grid=(SEQ // CHUNK,),
        in_specs=[
            pl.BlockSpec((CHUNK, D_GMLP), lambda n: (n, 0)),
            pl.BlockSpec((CHUNK, D_GMLP), lambda n: (n, 1)),
            pl.BlockSpec((GMLP_GROUPS, GROUP_DIM), lambda n: (0, 0)),
            pl.BlockSpec((GMLP_GROUPS, GROUP_DIM), lambda n: (0, 0)),
            pl.BlockSpec((GMLP_GROUPS, CHUNK, CHUNK), lambda n: (0, 0, 0)),
            pl.BlockSpec((CHUNK, GMLP_GROUPS), lambda n: (0, 0)),
        ],
        out_specs=pl.BlockSpec((CHUNK, D_GMLP), lambda n: (n, 0)),
        out_shape=jax.ShapeDtypeStruct((SEQ, D_GMLP), jnp.bfloat16),
        compiler_params=_params("arbitrary"),
        name="gmlp",
    )(proj, proj, ln_g, ln_b, w_spatial, b_spatial.T)


def _attn_kernel(q_ref, k_ref, v_ref, o_ref, acc_ref, m_ref, l_ref):
    head = pl.program_id(0)
    span = pl.program_id(1)
    log2e = 1.0 / math.log(2.0)
    head_no = (jnp.zeros((BLK, 2 * BLK), jnp.int32) + (head + 1)).astype(jnp.float32)
    slope = jnp.exp2(head_no * (-8.0 / N_HEADS)) * log2e
    scale = HEAD_DIM ** -0.5 * log2e
    qi = lax.broadcasted_iota(jnp.int32, (BLK, 2 * BLK), 0)
    ki = lax.broadcasted_iota(jnp.int32, (BLK, 2 * BLK), 1)
    step = qi + BLK - ki
    contract_last = (((1,), (1,)), ((), ()))
    ones = jnp.ones((2 * BLK, LANES), jnp.bfloat16)

    for b, (window, d) in enumerate(BRANCHES):
        assert window // d == BLK
        valid = (step >= 0) & (step <= BLK)
        bias = jnp.where(valid, -slope * d * step.astype(jnp.float32), NEG)
        bias_first = jnp.where(ki >= BLK, bias, NEG)

        def tile(t, carry, d=d, b=b, bias=bias, bias_first=bias_first):
            r = t % d
            n = t // d
            q0 = n * (BLK * d) + r
            k0 = span * SPAN + q0
            first = k0 < BLK * d
            kp = jnp.where(first, k0, k0 - BLK * d)
            q = (q_ref[pl.ds(q0, BLK, stride=d), :] * scale).astype(jnp.bfloat16)
            k = jnp.concatenate([k_ref[pl.ds(kp, BLK, stride=d), :], k_ref[pl.ds(k0, BLK, stride=d), :]], axis=0)
            v = jnp.concatenate([v_ref[pl.ds(kp, BLK, stride=d), :], v_ref[pl.ds(k0, BLK, stride=d), :]], axis=0)
            s = lax.dot_general(q, k.astype(jnp.bfloat16), contract_last, preferred_element_type=jnp.float32)
            s = s + jnp.where(first, bias_first, bias)
            m = jnp.max(jnp.maximum(s[:, :BLK], s[:, BLK:]), axis=-1, keepdims=True)
            p = jnp.exp2(s - m).astype(jnp.bfloat16)
            v_one = jnp.concatenate([v.astype(jnp.bfloat16), ones], axis=1)
            pv = jnp.dot(p, v_one, preferred_element_type=jnp.float32)
            rows = pl.ds(q0, BLK, stride=d)
            acc_ref[b, rows, :] = pv[:, :HEAD_DIM]
            l_ref[b, rows, :] = pv[:, HEAD_DIM:]
            m_ref[b, rows, :] = jnp.broadcast_to(m, (BLK, LANES))
            return carry

        lax.fori_loop(0, SPAN // BLK, tile, 0, unroll=ATTN_UNROLL)

    m_all = jnp.maximum(jnp.maximum(m_ref[0], m_ref[1]), m_ref[2])
    num = jnp.zeros((SPAN, HEAD_DIM), jnp.float32)
    den = jnp.zeros((SPAN, LANES), jnp.float32)
    for b in range(len(BRANCHES)):
        w = jnp.exp2(m_ref[b] - m_all)
        num = num + w * acc_ref[b]
        den = den + w * l_ref[b]
    o_ref[...] = (num / den).astype(o_ref.dtype)


def _attention(proj):
    q_col = 2 * D_GMLP // HEAD_DIM
    k_col = q_col + N_HEADS
    v_col = k_col + N_HEADS
    nb = len(BRANCHES)
    return pl.pallas_call(
        _attn_kernel,
        grid=(N_HEADS, SEQ // SPAN),
        in_specs=[
            pl.BlockSpec((SPAN, HEAD_DIM), lambda h, s: (s, q_col + h)),
            pl.BlockSpec((SEQ, HEAD_DIM), lambda h, s: (0, k_col + h)),
            pl.BlockSpec((SEQ, HEAD_DIM), lambda h, s: (0, v_col + h)),
        ],
        out_specs=pl.BlockSpec((SPAN, HEAD_DIM), lambda h, s: (s, h)),
        out_shape=jax.ShapeDtypeStruct((SEQ, D_ATTN), jnp.bfloat16),
        scratch_shapes=[
            pltpu.VMEM((nb, SPAN, HEAD_DIM), jnp.float32),
            pltpu.VMEM((nb, SPAN, LANES), jnp.float32),
            pltpu.VMEM((nb, SPAN, LANES), jnp.float32),
        ],
        compiler_params=_params("arbitrary", "arbitrary"),
        name="attn",
    )(proj, proj, proj)


def _split_bf16(x):
    hi = x.astype(jnp.bfloat16)
    lo = (x - hi.astype(jnp.float32)).astype(jnp.bfloat16)
    return hi, lo


def _out_proj_kernel(ya_ref, yb_ref, wa_ref, wb_ref, x_ref, gate_ref, g_ref, b_ref, sc_ref, sh_ref,
                     wr_ref, br_ref, x1_ref, hp_ref, lg_ref):
    mix = jnp.dot(ya_ref[...], wa_ref[...], preferred_element_type=jnp.float32)
    mix = mix + jnp.dot(yb_ref[...], wb_ref[...], preferred_element_type=jnp.float32)
    x1 = _layer_norm(DEEPNORM_ALPHA * x_ref[...] + gate_ref[...] * mix, g_ref[...], b_ref[...])
    x1_ref[...] = x1
    h = x1 * (1.0 + sc_ref[...]) + sh_ref[...]
    _store_packed(hp_ref, (), 0, _pack_rows(h))
    h_hi, h_lo = _split_bf16(h)
    w_hi, w_lo = _split_bf16(wr_ref[...])
    both = jnp.dot(h_hi, jnp.concatenate([w_hi, w_lo], axis=1), preferred_element_type=jnp.float32)
    lg = both[:, :N_EXPERTS] + both[:, N_EXPERTS:] + jnp.dot(h_lo, w_hi, preferred_element_type=jnp.float32)
    lg_ref[...] = lg + br_ref[...]


def _out_proj(y_a, y_b, w_o_bf16, x, gate1, ln_g, ln_b, scale2, shift2, w_router, b_router):
    row = lambda i: (i, 0)
    fixed = lambda i: (0, 0)
    vec = pl.BlockSpec((1, D_MODEL), fixed)
    return pl.pallas_call(
        _out_proj_kernel,
        grid=(SEQ // OUT_TM,),
        in_specs=[
            pl.BlockSpec((OUT_TM, D_GMLP), row),
            pl.BlockSpec((OUT_TM, D_ATTN), row),
            pl.BlockSpec((D_GMLP, D_MODEL), lambda i: (0, 0)),
            pl.BlockSpec((D_ATTN, D_MODEL), lambda i: (1, 0)),
            pl.BlockSpec((OUT_TM, D_MODEL), row),
            vec, vec, vec, vec, vec,
            pl.BlockSpec((D_MODEL, N_EXPERTS), fixed),
            pl.BlockSpec((1, N_EXPERTS), fixed),
        ],
        out_specs=[
            pl.BlockSpec((OUT_TM, D_MODEL), row),
            pl.BlockSpec((OUT_TM * ROW_SUB, LANES), row),
            pl.BlockSpec((OUT_TM, N_EXPERTS), row),
        ],
        out_shape=[
            jax.ShapeDtypeStruct((SEQ, D_MODEL), jnp.float32),
            jax.ShapeDtypeStruct((SEQ * ROW_SUB, LANES), jnp.uint32),
            jax.ShapeDtypeStruct((SEQ, N_EXPERTS), jnp.float32),
        ],
        compiler_params=_params("arbitrary"),
        name="out_proj",
    )(y_a, y_b, w_o_bf16, w_o_bf16, x, gate1, ln_g, ln_b, scale2, shift2, w_router, b_router)


def _experts_kernel(meta_ref, dstp_ref, x_hbm, wg_ref, wl_ref, bg_ref, bl_ref, w2_ref,
                    b2_ref, y_hbm, xs_ref, os_ref, acc_ref, gsem, ssem):
    i = pl.program_id(0)
    j = pl.program_id(1)
    n_items = pl.num_programs(0)
    last_j = pl.num_programs(1) - 1

    def item_rows(k):
        inside = (k >= 0) & (k < n_items)
        return jnp.where(inside, meta_ref[MOE_ITEMS + jnp.clip(k, 0, n_items - 1)], 0)

    rows = item_rows(i)
    rows_prev = item_rows(i - 1)
    p = i % 2
    q = 1 - p
    group_sub = MOE_G * ROW_SUB

    def load_rows(k, slot):
        first = meta_ref[2 * MOE_ITEMS + jnp.clip(k, 0, n_items - 1)] * ROW_SUB

        def load_group(t, c):
            src = pl.multiple_of(first + t * group_sub, group_sub)
            dst = pl.multiple_of(t * group_sub, group_sub)
            pltpu.make_async_copy(x_hbm.at[pl.ds(src, group_sub), :], xs_ref.at[slot, pl.ds(dst, group_sub), :],
                                  gsem.at[slot]).start()
            return c
        lax.fori_loop(0, item_rows(k) // MOE_G, load_group, 0)

    def scatter(slot, row, queue=1):
        entry = dstp_ref[0, 0, row]
        src = pl.multiple_of(row * ROW_SUB, ROW_SUB)
        dst = pl.multiple_of(entry, ROW_SUB)
        pltpu.make_async_copy(os_ref.at[slot, pl.ds(src, ROW_SUB), :], y_hbm.at[pl.ds(dst, ROW_SUB), :],
                              ssem.at[slot]).start(priority=queue)
        return entry

    def wait_rows(buf_ref, sem, slot, n):
        group = buf_ref.at[slot, pl.ds(0, group_sub), :]

        def wait_group(t, c):
            pltpu.make_async_copy(group, group, sem.at[slot]).wait()
            return c
        lax.fori_loop(0, n // MOE_G, wait_group, 0)

    def each_row(lo, hi, fn):
        lax.fori_loop(lo, jnp.maximum(lo, hi), lambda r, c: (fn(r), c)[1], 0)

    @pl.when((i == 0) & (j == 0))
    def _():
        os_ref[...] = jnp.zeros_like(os_ref)
        for half in range(2):
            fill = pltpu.make_async_copy(
                os_ref.at[0],
                y_hbm.at[pl.ds((SEQ * TOP_K + half * MOE_RMAX) * ROW_SUB, MOE_RMAX * ROW_SUB), :],
                ssem.at[0])
            fill.start()
            fill.wait()
        load_rows(0, 0)

    @pl.when(j == 0)
    def _():
        wait_rows(xs_ref, gsem, p, rows)
        load_rows(i + 1, q)

    def sub_tile(a, m):
        a = pl.multiple_of(a, MOE_G)
        base = (j * rows + a) // MOE_CHUNKS
        lag = 0
        for u in range(m // MOE_CHUNKS):
            lag = scatter(0, base + u + lag, queue=u % 2) >> 31
        x_lo, x_hi = _unpack_words(_load_packed(xs_ref, (p,), a, m))
        x = jnp.concatenate([x_lo.astype(jnp.bfloat16), x_hi.astype(jnp.bfloat16)], axis=1)

        def up(w_ref, b_ref):
            return jnp.dot(x, w_ref[...].astype(jnp.bfloat16), preferred_element_type=jnp.float32) + b_ref[...]

        glu = jnp.minimum(up(wg_ref, bg_ref), SWIGLU_LIMIT)
        lin = jnp.clip(up(wl_ref, bl_ref), -SWIGLU_LIMIT, SWIGLU_LIMIT)
        act = glu * jax.nn.sigmoid(SWIGLU_ALPHA * glu) * (lin + 1.0)
        acc_ref[pl.ds(a, m), :] += jnp.dot(act.astype(jnp.bfloat16), w2_ref[...].astype(jnp.bfloat16),
                                           preferred_element_type=jnp.float32)

    @pl.when(rows > 0)
    def _():
        @pl.when(j == 0)
        def _():
            bias = jnp.broadcast_to(b2_ref[...], (MOE_G, D_MODEL))

            def init(t, c):
                acc_ref[pl.ds(pl.multiple_of(t * MOE_G, MOE_G), MOE_G), :] = bias
                return c
            lax.fori_loop(0, rows // MOE_G, init, 0)

        main = MOE_SUBTILES[0]
        n_main = rows // main
        lax.fori_loop(0, n_main, lambda t, c: (sub_tile(t * main, main), c)[1], 0)
        done = n_main * main
        for m in MOE_SUBTILES[1:]:
            has = ((rows - done) // m) % 2 == 1
            pl.when(has)(lambda done=done, m=m: sub_tile(done, m))
            done = done + jnp.where(has, m, 0)

    @pl.when(j == last_j)
    def _():
        each_row(rows, rows_prev, lambda r: scatter(0, r))
        wait_rows(os_ref, ssem, 0, jnp.maximum(rows, rows_prev))

        def pack(t, c):
            a = pl.multiple_of(t * MOE_G, MOE_G)
            _store_packed(os_ref, (0,), a, _pack_rows(acc_ref[pl.ds(a, MOE_G), :]))
            return c
        lax.fori_loop(0, rows // MOE_G, pack, 0)


def _experts(x_sorted, meta, dst_tab, w1, b1, w2, b2):
    n_j = MOE_CHUNKS

    def col(i, j, m):
        return jnp.where(m[MOE_ITEMS + i] > 0, j, n_j - 1)

    def expert(i, m):
        return m[i]

    tab = lambda f: pl.BlockSpec((1, 1, MOE_RMAX), f, memory_space=pltpu.SMEM)
    return pl.pallas_call(
        _experts_kernel,
        grid_spec=pltpu.PrefetchScalarGridSpec(
            num_scalar_prefetch=1,
            grid=(MOE_ITEMS, n_j),
            in_specs=[
                tab(lambda i, j, m: (i, 0, 0)),
                pl.BlockSpec(memory_space=pl.ANY),
                pl.BlockSpec((None, D_MODEL, MOE_TN), lambda i, j, m: (expert(i, m), 0, col(i, j, m))),
                pl.BlockSpec((None, D_MODEL, MOE_TN), lambda i, j, m: (expert(i, m), 0, n_j + col(i, j, m))),
                pl.BlockSpec((None, 1, MOE_TN), lambda i, j, m: (expert(i, m), 0, col(i, j, m))),
                pl.BlockSpec((None, 1, MOE_TN), lambda i, j, m: (expert(i, m), 0, n_j + col(i, j, m))),
                pl.BlockSpec((None, MOE_TN, D_MODEL), lambda i, j, m: (expert(i, m), col(i, j, m), 0)),
                pl.BlockSpec((None, 1, D_MODEL), lambda i, j, m: (expert(i, m), 0, 0)),
            ],
            out_specs=pl.BlockSpec(memory_space=pl.ANY),
            scratch_shapes=[
                pltpu.VMEM((2, MOE_RMAX * ROW_SUB, LANES), jnp.uint32),
                pltpu.VMEM((1, MOE_RMAX * ROW_SUB, LANES), jnp.uint32),
                pltpu.VMEM((MOE_RMAX, D_MODEL), jnp.float32),
                pltpu.SemaphoreType.DMA((2,)),
                pltpu.SemaphoreType.DMA((1,)),
            ],
        ),
        out_shape=jax.ShapeDtypeStruct((Y_ROWS * ROW_SUB, LANES), jnp.uint32),
        compiler_params=_params("arbitrary", "arbitrary"),
        name="experts",
    )(meta, dst_tab, x_sorted, w1, w1,
      b1.reshape(N_EXPERTS, 1, 2 * D_EXPERT), b1.reshape(N_EXPERTS, 1, 2 * D_EXPERT),
      w2, b2.reshape(N_EXPERTS, 1, D_MODEL))


def _combine_kernel(y0_ref, y1_ref, y2_ref, y3_ref, gates_ref, x1_ref, gate2_ref, g_ref, b_ref, o_ref):
    y_lo = jnp.zeros((COMBINE_TM, D_PACK), jnp.float32)
    y_hi = jnp.zeros((COMBINE_TM, D_PACK), jnp.float32)
    for k, y_ref in enumerate((y0_ref, y1_ref, y2_ref, y3_ref)):
        lo, hi = _unpack_words(_load_packed(y_ref, (), 0, COMBINE_TM))
        gate = gates_ref[:, k:k + 1]
        y_lo = y_lo + gate * lo
        y_hi = y_hi + gate * hi
    y = jnp.concatenate([y_lo, y_hi], axis=1)
    o_ref[...] = _layer_norm(DEEPNORM_ALPHA * x1_ref[...] + gate2_ref[...] * y, g_ref[...], b_ref[...])


def _combine(y_packed, gates, x1, gate2, ln_g, ln_b):
    n_tiles = SEQ // COMBINE_TM
    row = lambda i: (i, 0)
    vec = pl.BlockSpec((1, D_MODEL), lambda i: (0, 0))
    slot = lambda k: pl.BlockSpec((COMBINE_TM * ROW_SUB, LANES), lambda i: (k * n_tiles + i, 0))
    return pl.pallas_call(
        _combine_kernel,
        grid=(n_tiles,),
        in_specs=[
            slot(0), slot(1), slot(2), slot(3),
            pl.BlockSpec((COMBINE_TM, TOP_K), row),
            pl.BlockSpec((COMBINE_TM, D_MODEL), row),
            vec, vec, vec,
        ],
        out_specs=pl.BlockSpec((COMBINE_TM, D_MODEL), row),
        out_shape=jax.ShapeDtypeStruct((SEQ, D_MODEL), jnp.float32),
        compiler_params=_params("arbitrary"),
        name="combine",
    )(y_packed, y_packed, y_packed, y_packed, gates, x1, gate2, ln_g, ln_b)


TAB_ROWS = MOE_RMAX // LANES
WINDOW_ROWS = 2 * ROW_SUB
assert WINDOW_ROWS > TAB_ROWS and TOP_K == 4


def _row_tables_kernel(win_ref, order_ref, dst_ref):
    b = pl.program_id(0)
    w0 = win_ref[b]
    n_real = win_ref[MOE_ITEMS + 1 + b]
    off = w0 % LANES
    x = order_ref[pl.ds(w0 // LANES, WINDOW_ROWS), :]
    x = pltpu.roll(x, (LANES - off) % LANES, axis=1)
    lane = lax.broadcasted_iota(jnp.int32, (TAB_ROWS, LANES), 1)
    sub = lax.broadcasted_iota(jnp.int32, (TAB_ROWS, LANES), 0)
    flat = jnp.where(lane < LANES - off, x[:TAB_ROWS], x[1:TAB_ROWS + 1])
    r = sub * LANES + lane
    tok = flat >> 2
    slot = flat & 3
    spill = SEQ * TOP_K + ((b + 1) % 2) * MOE_RMAX + r
    dst_ref[...] = jnp.where(r < n_real, slot * SEQ + tok, spill) * ROW_SUB


def _row_tables(win, order):
    n_rows = SEQ * TOP_K // LANES
    order2d = jnp.concatenate([order, jnp.zeros((WINDOW_ROWS * LANES,), jnp.int32)]).reshape(-1, LANES)
    return pl.pallas_call(
        _row_tables_kernel,
        grid_spec=pltpu.PrefetchScalarGridSpec(
            num_scalar_prefetch=1,
            grid=(MOE_ITEMS + 1,),
            in_specs=[pl.BlockSpec((n_rows + WINDOW_ROWS, LANES), lambda b, w: (0, 0))],
            out_specs=pl.BlockSpec((None, TAB_ROWS, LANES), lambda b, w: (b, 0, 0)),
        ),
        out_shape=jax.ShapeDtypeStruct((MOE_ITEMS + 1, TAB_ROWS, LANES), jnp.int32),
        compiler_params=_params("arbitrary"),
        name="row_tables",
    )(win, order2d)


def _dispatch_kernel(fill_ref, pos_ref, h_ref, x_hbm, zero_ref, sem, zsem):
    i = pl.program_id(0)
    group_sub = MOE_G * ROW_SUB

    @pl.when(i == 0)
    def _():
        zero_ref[...] = jnp.zeros_like(zero_ref)

        def fill(g):
            dst = pl.multiple_of(g * group_sub, group_sub)
            pltpu.make_async_copy(zero_ref, x_hbm.at[pl.ds(dst, group_sub), :], zsem).start()

        for e in range(N_EXPERTS):
            end = fill_ref[e]
            begin = fill_ref[e - 1] if e else 0
            pl.when(end > begin)(lambda end=end: fill(end - 1))
        lax.fori_loop(fill_ref[N_EXPERTS - 1], X_ROWS // MOE_G, lambda g, c: (fill(g), c)[1], 0)

        def wait_fill(t, c):
            pltpu.make_async_copy(zero_ref, zero_ref, zsem).wait()
            return c
        lax.fori_loop(0, fill_ref[N_EXPERTS], wait_fill, 0)

    def token(t, c):
        src = pl.multiple_of(t * ROW_SUB, ROW_SUB)
        for k in range(TOP_K):
            dst = pl.multiple_of(pos_ref[0, 0, t * TOP_K + k], ROW_SUB)
            pltpu.make_async_copy(h_ref.at[pl.ds(src, ROW_SUB), :], x_hbm.at[pl.ds(dst, ROW_SUB), :],
                                  sem).start(priority=k % 2)
        return c
    lax.fori_loop(0, DISPATCH_TM, token, 0, unroll=4)
    for k in range(TOP_K):
        pltpu.make_async_copy(h_ref, h_ref, sem).wait()


def _dispatch(h_packed, pos, fill):
    n_tiles = SEQ // DISPATCH_TM
    return pl.pallas_call(
        _dispatch_kernel,
        grid_spec=pltpu.PrefetchScalarGridSpec(
            num_scalar_prefetch=1,
            grid=(n_tiles,),
            in_specs=[
                pl.BlockSpec((1, 1, DISPATCH_TM * TOP_K), lambda i, f: (i, 0, 0), memory_space=pltpu.SMEM),
                pl.BlockSpec((DISPATCH_TM * ROW_SUB, LANES), lambda i, f: (i, 0)),
            ],
            out_specs=pl.BlockSpec(memory_space=pl.ANY),
            scratch_shapes=[
                pltpu.VMEM((MOE_G * ROW_SUB, LANES), jnp.uint32),
                pltpu.SemaphoreType.DMA(()),
                pltpu.SemaphoreType.DMA(()),
            ],
        ),
        out_shape=jax.ShapeDtypeStruct((X_ROWS * ROW_SUB, LANES), jnp.uint32),
        compiler_params=_params("arbitrary"),
        name="dispatch",
    )(fill, pos.reshape(n_tiles, 1, DISPATCH_TM * TOP_K), h_packed)


def _route(logits):
    top_val, top_idx = lax.top_k(logits, TOP_K)
    gates = jax.nn.softmax(top_val, axis=-1)
    e_flat = top_idx.reshape(-1).astype(jnp.int32)
    experts = jnp.arange(N_EXPERTS, dtype=jnp.int32)
    counts = jnp.sum((e_flat[:, None] == experts[None, :]).astype(jnp.int32), axis=0)
    groups = (counts + MOE_G - 1) // MOE_G
    group_end = jnp.cumsum(groups)
    row0 = (group_end - groups) * MOE_G
    per_item = MOE_RMAX // MOE_G
    n_items_e = (groups + per_item - 1) // per_item
    item_end = jnp.cumsum(n_items_e)
    item_start = item_end - n_items_e
    n_items = item_end[-1]

    item = jnp.arange(MOE_ITEMS, dtype=jnp.int32)
    used = item < n_items
    e_item = jnp.minimum(jnp.searchsorted(item_end, item, side='right'), N_EXPERTS - 1).astype(jnp.int32)
    e_last = e_item[jnp.maximum(n_items - 1, 0)]
    part = item - item_start[e_item]
    rows = jnp.where(used, jnp.clip(groups[e_item] - part * per_item, 0, per_item) * MOE_G, 0)
    first_row = jnp.where(used, row0[e_item] + part * MOE_RMAX, 0)
    meta = jnp.concatenate([jnp.where(used, e_item, e_last), rows, first_row]).astype(jnp.int32)

    order = jnp.argsort(e_flat, stable=True).astype(jnp.int32)
    rank = jnp.argsort(order).astype(jnp.int32)
    start = jnp.cumsum(counts) - counts
    shift = jnp.sum(jnp.where(e_flat[:, None] == experts[None, :], (row0 - start)[None, :], 0), axis=1)
    pos = (rank + shift) * ROW_SUB
    n_fill = jnp.sum((groups > 0).astype(jnp.int32)) + X_ROWS // MOE_G - group_end[-1]
    fill = jnp.concatenate([group_end, n_fill[None]]).astype(jnp.int32)

    window0 = jnp.where(used, start[e_item] + part * MOE_RMAX, 0)
    n_real = jnp.where(used, jnp.clip(counts[e_item] - part * MOE_RMAX, 0, MOE_RMAX), 0)
    zero = jnp.zeros((1,), jnp.int32)
    win = jnp.concatenate([zero, window0, zero, n_real]).astype(jnp.int32)
    dst_tab = _row_tables(win, order).reshape(MOE_ITEMS + 1, 1, MOE_RMAX)
    return gates, meta, pos.astype(jnp.int32), fill, dst_tab


def kernel(x, c, w_ada, b_ada, w_in, sgu_ln_g, sgu_ln_b, w_spatial, b_spatial, w_o, ln1_g, ln1_b,
           w_router, b_router, w_exp1, b_exp1, w_exp2, b_exp2, ln2_g, ln2_b):
    depth = w_ada.shape[0]
    assert x.shape == (1, SEQ, D_MODEL)
    xs = x.reshape(SEQ, D_MODEL)
    for l in range(depth):
        ada = _ada(c, w_ada[l], b_ada[l])
        shift1, scale1, gate1, shift2, scale2, gate2 = jnp.split(ada, 6, axis=-1)

        proj = _in_proj(xs, scale1, shift1, w_in[l].astype(jnp.bfloat16))
        y_a = _gmlp(proj, sgu_ln_g[l], sgu_ln_b[l], w_spatial[l], b_spatial[l])
        y_b = _attention(proj)
        x1, h_packed, logits = _out_proj(y_a, y_b, w_o[l].astype(jnp.bfloat16), xs, gate1,
                                         ln1_g[l].reshape(1, -1), ln1_b[l].reshape(1, -1), scale2, shift2,
                                         w_router[l], b_router[l].reshape(1, -1))

        gates, meta, pos, fill, dst_tab = _route(logits)
        x_sorted = _dispatch(h_packed, pos, fill)
        y_packed = _experts(x_sorted, meta, dst_tab, w_exp1[l], b_exp1[l], w_exp2[l], b_exp2[l])
        xs = _combine(y_packed, gates, x1, gate2, ln2_g[l].reshape(1, -1), ln2_b[l].reshape(1, -1))
    return xs.reshape(x.shape)
```

```python
import math

import jax
import jax.numpy as jnp
from jax import lax
from jax.experimental import pallas as pl
from jax.experimental.pallas import tpu as pltpu

D_MODEL = 2048
SEQ = 8192
D_GMLP = 1024
GMLP_GROUPS = 8
GROUP_DIM = 128
CHUNK = 128
D_ATTN = 1024
HEAD_DIM = 128
N_HEADS = 8
BRANCHES = ((128, 1), (512, 4), (2048, 16))
BLK = 128
SPAN = 16 * BLK
D_IN_PROJ = 2 * D_GMLP + 3 * D_ATTN
N_EXPERTS = 32
TOP_K = 4
D_EXPERT = 2048
SWIGLU_LIMIT = 7.0
SWIGLU_ALPHA = 1.702
LN_EPS = 1e-5
DEEPNORM_ALPHA = 2.0 ** 0.25
NEG = -1e30

LANES = 128
VMEM_LIMIT = 56 * 1024 * 1024

ADA_TN = 1536
PROJ_TM = 1024
PROJ_TN = 1024
OUT_TM = 512
ATTN_UNROLL = 8
ROW_SUB = 8
D_PACK = D_MODEL // 2
MOE_G = 128
MOE_RMAX = 1152
MOE_ITEMS = 64
MOE_TN = 512
MOE_CHUNKS = D_EXPERT // MOE_TN
MOE_SUBTILES = (512, 256, 128)
X_ROWS = SEQ * TOP_K + N_EXPERTS * MOE_G
Y_ROWS = SEQ * TOP_K + 2 * MOE_RMAX
DISPATCH_TM = 256
COMBINE_TM = 256


def _params(*sem):
    return pltpu.CompilerParams(dimension_semantics=sem, vmem_limit_bytes=VMEM_LIMIT)


def _layer_norm(x, g, b):
    mu = jnp.mean(x, axis=-1, keepdims=True)
    xc = x - mu
    var = jnp.mean(xc * xc, axis=-1, keepdims=True)
    return xc * lax.rsqrt(var + LN_EPS) * g + b


def _pack_rows(x):
    r = x.astype(jnp.bfloat16).astype(jnp.float32)
    bits = lax.bitcast_convert_type(r, jnp.uint32)
    return (bits[:, D_PACK:] & jnp.uint32(0xFFFF0000)) | (bits[:, :D_PACK] >> 16)


def _unpack_words(u):
    lo = lax.bitcast_convert_type(u << 16, jnp.float32)
    hi = lax.bitcast_convert_type(u & jnp.uint32(0xFFFF0000), jnp.float32)
    return lo, hi


def _store_packed(ref, lead, row0, packed):
    m = packed.shape[0]
    for c in range(ROW_SUB):
        rows = pl.ds(row0 * ROW_SUB + c, m, stride=ROW_SUB)
        ref[lead + (rows, slice(None))] = packed[:, c * LANES:(c + 1) * LANES]


def _load_packed(ref, lead, row0, m):
    cols = [ref[lead + (pl.ds(row0 * ROW_SUB + c, m, stride=ROW_SUB), slice(None))] for c in range(ROW_SUB)]
    return jnp.concatenate(cols, axis=1)


def _ada_kernel(c_ref, w_ref, b_ref, o_ref):
    c = c_ref[...]
    s = c * jax.nn.sigmoid(c)
    o_ref[...] = jnp.sum(s * w_ref[...], axis=0, keepdims=True) + b_ref[...]


def _ada(c, w_ada, b_ada):
    n = w_ada.shape[1]
    return pl.pallas_call(
        _ada_kernel,
        grid=(n // ADA_TN,),
        in_specs=[
            pl.BlockSpec((D_MODEL, 1), lambda j: (0, 0)),
            pl.BlockSpec((D_MODEL, ADA_TN), lambda j: (0, j)),
            pl.BlockSpec((1, ADA_TN), lambda j: (0, j)),
        ],
        out_specs=pl.BlockSpec((1, ADA_TN), lambda j: (0, j)),
        out_shape=jax.ShapeDtypeStruct((1, n), jnp.float32),
        compiler_params=_params("arbitrary"),
        name="ada",
    )(c.reshape(D_MODEL, 1), w_ada, b_ada.reshape(1, n))


def _in_proj_kernel(x_ref, sc_ref, sh_ref, w_ref, o_ref, h_ref):
    @pl.when(pl.program_id(1) == 0)
    def _():
        h_ref[...] = (x_ref[...] * (1.0 + sc_ref[...]) + sh_ref[...]).astype(jnp.bfloat16)

    o_ref[...] = jnp.dot(h_ref[...], w_ref[...], preferred_element_type=jnp.float32)


def _in_proj(x, scale1, shift1, w_in_bf16):
    return pl.pallas_call(
        _in_proj_kernel,
        grid=(SEQ // PROJ_TM, D_IN_PROJ // PROJ_TN),
        in_specs=[
            pl.BlockSpec((PROJ_TM, D_MODEL), lambda i, j: (i, 0)),
            pl.BlockSpec((1, D_MODEL), lambda i, j: (0, 0)),
            pl.BlockSpec((1, D_MODEL), lambda i, j: (0, 0)),
            pl.BlockSpec((D_MODEL, PROJ_TN), lambda i, j: (0, j)),
        ],
        out_specs=pl.BlockSpec((PROJ_TM, PROJ_TN), lambda i, j: (i, j)),
        out_shape=jax.ShapeDtypeStruct((SEQ, D_IN_PROJ), jnp.float32),
        scratch_shapes=[pltpu.VMEM((PROJ_TM, D_MODEL), jnp.bfloat16)],
        compiler_params=_params("arbitrary", "arbitrary"),
        name="in_proj",
    )(x, scale1, shift1, w_in_bf16)


def _gelu(x):
    return 0.5 * x * (1.0 + lax.erf(x * (1.0 / math.sqrt(2.0))))


def _gmlp_kernel(ua_ref, va_ref, g_ref, b_ref, w_ref, bs_ref, o_ref):
    row = lax.broadcasted_iota(jnp.int32, (CHUNK, CHUNK), 0)
    col = lax.broadcasted_iota(jnp.int32, (CHUNK, CHUNK), 1)
    causal = col <= row
    for g in range(GMLP_GROUPS):
        sl = slice(g * GROUP_DIM, (g + 1) * GROUP_DIM)
        v = _layer_norm(_gelu(va_ref[:, sl]), g_ref[g:g + 1, :], b_ref[g:g + 1, :])
        w = jnp.where(causal, w_ref[g], 0.0).astype(jnp.bfloat16)
        s = jnp.dot(w, v.astype(jnp.bfloat16), preferred_element_type=jnp.float32)
        s = s + bs_ref[:, g:g + 1]
        o_ref[:, sl] = (_gelu(ua_ref[:, sl]) * s).astype(o_ref.dtype)


def _gmlp(proj, ln_g, ln_b, w_spatial, b_spatial):
    return pl.pallas_call(
        _gmlp_kernel,
        grid=(SEQ // CHUNK,),
        in_specs=[
            pl.BlockSpec((CHUNK, D_GMLP), lambda n: (n, 0)),
            pl.BlockSpec((CHUNK, D_GMLP), lambda n: (n, 1)),
            pl.BlockSpec((GMLP_GROUPS, GROUP_DIM), lambda n: (0, 0)),
            pl.BlockSpec((GMLP_GROUPS, GROUP_DIM), lambda n: (0, 0)),
            pl.BlockSpec((GMLP_GROUPS, CHUNK, CHUNK), lambda n: (0, 0, 0)),
            pl.BlockSpec((CHUNK, GMLP_GROUPS), lambda n: (0, 0)),
        ],
        out_specs=pl.BlockSpec((CHUNK, D_GMLP), lambda n: (n, 0)),
        out_shape=jax.ShapeDtypeStruct((SEQ, D_GMLP), jnp.bfloat16),
        compiler_params=_params("arbitrary"),
        name="gmlp",
    )(proj, proj, ln_g, ln_b, w_spatial, b_spatial.T)


def _attn_kernel(q_ref, k_ref, v_ref, o_ref, acc_ref, m_ref, l_ref, kv_ref):
    head = pl.program_id(0)
    span = pl.program_id(1)
    log2e = 1.0 / math.log(2.0)
    head_no = (jnp.zeros((BLK, 2 * BLK), jnp.int32) + (head + 1)).astype(jnp.float32)
    slope = jnp.exp2(head_no * (-8.0 / N_HEADS)) * log2e
    scale = HEAD_DIM ** -0.5 * log2e
    qi = lax.broadcasted_iota(jnp.int32, (BLK, 2 * BLK), 0)
    ki = lax.broadcasted_iota(jnp.int32, (BLK, 2 * BLK), 1)
    step = qi + BLK - ki
    contract_last = (((1,), (1,)), ((), ()))
    ones = jnp.ones((2 * BLK, LANES), jnp.bfloat16)

    for b, (window, d) in enumerate(BRANCHES):
        assert window // d == BLK
        valid = (step >= 0) & (step <= BLK)
        bias = jnp.where(valid, -slope * d * step.astype(jnp.float32), NEG)
        bias_first = jnp.where(ki >= BLK, bias, NEG)

        per_class = d * BLK == SPAN
        cur = span % 2
        if per_class:
            def stash(r, carry, d=d):
                rows = pl.ds(pl.multiple_of(r * BLK, BLK), BLK)
                kv_ref[0, cur, rows, :] = k_ref[pl.ds(span * SPAN + r, BLK, stride=d), :].astype(jnp.bfloat16)
                kv_ref[1, cur, rows, :] = v_ref[pl.ds(span * SPAN + r, BLK, stride=d), :].astype(jnp.bfloat16)
                return carry
            lax.fori_loop(0, d, stash, 0, unroll=4)

        def tile(t, carry, d=d, b=b, bias=bias, bias_first=bias_first, per_class=per_class):
            r = t % d
            n = t // d
            q0 = n * (BLK * d) + r
            k0 = span * SPAN + q0
            first = k0 < BLK * d
            q = (q_ref[pl.ds(q0, BLK, stride=d), :] * scale).astype(jnp.bfloat16)
            if per_class:
                rows = pl.ds(pl.multiple_of(r * BLK, BLK), BLK)
                prev = jnp.where(first, cur, 1 - cur)
                k = jnp.concatenate([kv_ref[0, prev, rows, :], kv_ref[0, cur, rows, :]], axis=0)
                v = jnp.concatenate([kv_ref[1, prev, rows, :], kv_ref[1, cur, rows, :]], axis=0)
            else:
                kp = jnp.where(first, k0, k0 - BLK * d)
                k = jnp.concatenate([k_ref[pl.ds(kp, BLK, stride=d), :], k_ref[pl.ds(k0, BLK, stride=d), :]],
                                    axis=0).astype(jnp.bfloat16)
                v = jnp.concatenate([v_ref[pl.ds(kp, BLK, stride=d), :], v_ref[pl.ds(k0, BLK, stride=d), :]],
                                    axis=0).astype(jnp.bfloat16)
            s = lax.dot_general(q, k, contract_last, preferred_element_type=jnp.float32)
            s = s + jnp.where(first, bias_first, bias)
            m = jnp.max(jnp.maximum(s[:, :BLK], s[:, BLK:]), axis=-1, keepdims=True)
            p = jnp.exp2(s - m).astype(jnp.bfloat16)
            v_one = jnp.concatenate([v, ones], axis=1)
            pv = jnp.dot(p, v_one, preferred_element_type=jnp.float32)
            rows = pl.ds(q0, BLK, stride=d)
            acc_ref[b, rows, :] = pv[:, :HEAD_DIM]
            l_ref[b, rows, :] = pv[:, HEAD_DIM:]
            m_ref[b, rows, :] = jnp.broadcast_to(m, (BLK, LANES))
            return carry

        lax.fori_loop(0, SPAN // BLK, tile, 0, unroll=ATTN_UNROLL)

    m_all = jnp.maximum(jnp.maximum(m_ref[0], m_ref[1]), m_ref[2])
    num = jnp.zeros((SPAN, HEAD_DIM), jnp.float32)
    den = jnp.zeros((SPAN, LANES), jnp.float32)
    for b in range(len(BRANCHES)):
        w = jnp.exp2(m_ref[b] - m_all)
        num = num + w * acc_ref[b]
        den = den + w * l_ref[b]
    o_ref[...] = (num / den).astype(o_ref.dtype)


def _attention(proj):
    q_col = 2 * D_GMLP // HEAD_DIM
    k_col = q_col + N_HEADS
    v_col = k_col + N_HEADS
    nb = len(BRANCHES)
    return pl.pallas_call(
        _attn_kernel,
        grid=(N_HEADS, SEQ // SPAN),
        in_specs=[
            pl.BlockSpec((SPAN, HEAD_DIM), lambda h, s: (s, q_col + h)),
            pl.BlockSpec((SEQ, HEAD_DIM), lambda h, s: (0, k_col + h)),
            pl.BlockSpec((SEQ, HEAD_DIM), lambda h, s: (0, v_col + h)),
        ],
        out_specs=pl.BlockSpec((SPAN, HEAD_DIM), lambda h, s: (s, h)),
        out_shape=jax.ShapeDtypeStruct((SEQ, D_ATTN), jnp.bfloat16),
        scratch_shapes=[
            pltpu.VMEM((nb, SPAN, HEAD_DIM), jnp.float32),
            pltpu.VMEM((nb, SPAN, LANES), jnp.float32),
            pltpu.VMEM((nb, SPAN, LANES), jnp.float32),
            pltpu.VMEM((2, 2, SPAN, HEAD_DIM), jnp.bfloat16),
        ],
        compiler_params=_params("arbitrary", "arbitrary"),
        name="attn",
    )(proj, proj, proj)


def _split_bf16(x):
    hi = x.astype(jnp.bfloat16)
    lo = (x - hi.astype(jnp.float32)).astype(jnp.bfloat16)
    return hi, lo


def _out_proj_kernel(ya_ref, yb_ref, wa_ref, wb_ref, x_ref, gate_ref, g_ref, b_ref, sc_ref, sh_ref,
                     wr_ref, br_ref, x1_ref, hp_ref, lg_ref):
    mix = jnp.dot(ya_ref[...], wa_ref[...], preferred_element_type=jnp.float32)
    mix = mix + jnp.dot(yb_ref[...], wb_ref[...], preferred_element_type=jnp.float32)
    x1 = _layer_norm(DEEPNORM_ALPHA * x_ref[...] + gate_ref[...] * mix, g_ref[...], b_ref[...])
    x1_ref[...] = x1
    h = x1 * (1.0 + sc_ref[...]) + sh_ref[...]
    _store_packed(hp_ref, (), 0, _pack_rows(h))
    h_hi, h_lo = _split_bf16(h)
    w_hi, w_lo = _split_bf16(wr_ref[...])
    both = jnp.dot(h_hi, jnp.concatenate([w_hi, w_lo], axis=1), preferred_element_type=jnp.float32)
    lg = both[:, :N_EXPERTS] + both[:, N_EXPERTS:] + jnp.dot(h_lo, w_hi, preferred_element_type=jnp.float32)
    lg_ref[...] = lg + br_ref[...]


def _out_proj(y_a, y_b, w_o_bf16, x, gate1, ln_g, ln_b, scale2, shift2, w_router, b_router):
    row = lambda i: (i, 0)
    fixed = lambda i: (0, 0)
    vec = pl.BlockSpec((1, D_MODEL), fixed)
    return pl.pallas_call(
        _out_proj_kernel,
        grid=(SEQ // OUT_TM,),
        in_specs=[
            pl.BlockSpec((OUT_TM, D_GMLP), row),
            pl.BlockSpec((OUT_TM, D_ATTN), row),
            pl.BlockSpec((D_GMLP, D_MODEL), lambda i: (0, 0)),
            pl.BlockSpec((D_ATTN, D_MODEL), lambda i: (1, 0)),
            pl.BlockSpec((OUT_TM, D_MODEL), row),
            vec, vec, vec, vec, vec,
            pl.BlockSpec((D_MODEL, N_EXPERTS), fixed),
            pl.BlockSpec((1, N_EXPERTS), fixed),
        ],
        out_specs=[
            pl.BlockSpec((OUT_TM, D_MODEL), row),
            pl.BlockSpec((OUT_TM * ROW_SUB, LANES), row),
            pl.BlockSpec((OUT_TM, N_EXPERTS), row),
        ],
        out_shape=[
            jax.ShapeDtypeStruct((SEQ, D_MODEL), jnp.float32),
            jax.ShapeDtypeStruct((SEQ * ROW_SUB, LANES), jnp.uint32),
            jax.ShapeDtypeStruct((SEQ, N_EXPERTS), jnp.float32),
        ],
        compiler_params=_params("arbitrary"),
        name="out_proj",
    )(y_a, y_b, w_o_bf16, w_o_bf16, x, gate1, ln_g, ln_b, scale2, shift2, w_router, b_router)


def _experts_kernel(meta_ref, dstp_ref, x_hbm, wg_ref, wl_ref, bg_ref, bl_ref, w2_ref,
                    b2_ref, y_hbm, xs_ref, os_ref, acc_ref, gsem, ssem):
    i = pl.program_id(0)
    j = pl.program_id(1)
    n_items = pl.num_programs(0)
    last_j = pl.num_programs(1) - 1

    def item_rows(k):
        inside = (k >= 0) & (k < n_items)
        return jnp.where(inside, meta_ref[MOE_ITEMS + jnp.clip(k, 0, n_items - 1)], 0)

    rows = item_rows(i)
    rows_prev = item_rows(i - 1)
    p = i % 2
    q = 1 - p
    group_sub = MOE_G * ROW_SUB

    def load_rows(k, slot):
        first = meta_ref[2 * MOE_ITEMS + jnp.clip(k, 0, n_items - 1)] * ROW_SUB

        def load_group(t, c):
            src = pl.multiple_of(first + t * group_sub, group_sub)
            dst = pl.multiple_of(t * group_sub, group_sub)
            pltpu.make_async_copy(x_hbm.at[pl.ds(src, group_sub), :], xs_ref.at[slot, pl.ds(dst, group_sub), :],
                                  gsem.at[slot]).start()
            return c
        lax.fori_loop(0, item_rows(k) // MOE_G, load_group, 0)

    def scatter(slot, row, queue=1):
        entry = dstp_ref[0, 0, row]
        src = pl.multiple_of(row * ROW_SUB, ROW_SUB)
        dst = pl.multiple_of(entry, ROW_SUB)
        pltpu.make_async_copy(os_ref.at[slot, pl.ds(src, ROW_SUB), :], y_hbm.at[pl.ds(dst, ROW_SUB), :],
                              ssem.at[slot]).start(priority=queue)
        return entry

    def wait_rows(buf_ref, sem, slot, n):
        group = buf_ref.at[slot, pl.ds(0, group_sub), :]

        def wait_group(t, c):
            pltpu.make_async_copy(group, group, sem.at[slot]).wait()
            return c
        lax.fori_loop(0, n // MOE_G, wait_group, 0)

    def each_row(lo, hi, fn):
        lax.fori_loop(lo, jnp.maximum(lo, hi), lambda r, c: (fn(r), c)[1], 0)

    @pl.when((i == 0) & (j == 0))
    def _():
        os_ref[...] = jnp.zeros_like(os_ref)
        for half in range(2):
            fill = pltpu.make_async_copy(
                os_ref.at[0],
                y_hbm.at[pl.ds((SEQ * TOP_K + half * MOE_RMAX) * ROW_SUB, MOE_RMAX * ROW_SUB), :],
                ssem.at[0])
            fill.start()
            fill.wait()
        load_rows(0, 0)

    @pl.when(j == 0)
    def _():
        wait_rows(xs_ref, gsem, p, rows)
        load_rows(i + 1, q)

    def sub_tile(a, m):
        a = pl.multiple_of(a, MOE_G)
        base = (j * rows + a) // MOE_CHUNKS
        lag = 0
        for u in range(m // MOE_CHUNKS):
            lag = scatter(0, base + u + lag, queue=u % 2) >> 31
        x_lo, x_hi = _unpack_words(_load_packed(xs_ref, (p,), a, m))
        x = jnp.concatenate([x_lo.astype(jnp.bfloat16), x_hi.astype(jnp.bfloat16)], axis=1)

        def up(w_ref, b_ref):
            return jnp.dot(x, w_ref[...].astype(jnp.bfloat16), preferred_element_type=jnp.float32) + b_ref[...]

        glu = jnp.minimum(up(wg_ref, bg_ref), SWIGLU_LIMIT)
        lin = jnp.clip(up(wl_ref, bl_ref), -SWIGLU_LIMIT, SWIGLU_LIMIT)
        act = glu * jax.nn.sigmoid(SWIGLU_ALPHA * glu) * (lin + 1.0)
        acc_ref[pl.ds(a, m), :] += jnp.dot(act.astype(jnp.bfloat16), w2_ref[...].astype(jnp.bfloat16),
                                           preferred_element_type=jnp.float32)

    @pl.when(rows > 0)
    def _():
        @pl.when(j == 0)
        def _():
            bias = jnp.broadcast_to(b2_ref[...], (MOE_G, D_MODEL))

            def init(t, c):
                acc_ref[pl.ds(pl.multiple_of(t * MOE_G, MOE_G), MOE_G), :] = bias
                return c
            lax.fori_loop(0, rows // MOE_G, init, 0)

        main = MOE_SUBTILES[0]
        n_main = rows // main
        lax.fori_loop(0, n_main, lambda t, c: (sub_tile(t * main, main), c)[1], 0)
        done = n_main * main
        for m in MOE_SUBTILES[1:]:
            has = ((rows - done) // m) % 2 == 1
            pl.when(has)(lambda done=done, m=m: sub_tile(done, m))
            done = done + jnp.where(has, m, 0)

    @pl.when(j == last_j)
    def _():
        each_row(rows, rows_prev, lambda r: scatter(0, r))
        wait_rows(os_ref, ssem, 0, jnp.maximum(rows, rows_prev))

        def pack(t, c):
            a = pl.multiple_of(t * MOE_G, MOE_G)
            _store_packed(os_ref, (0,), a, _pack_rows(acc_ref[pl.ds(a, MOE_G), :]))
            return c
        lax.fori_loop(0, rows // MOE_G, pack, 0)


def _experts(x_sorted, meta, dst_tab, w1, b1, w2, b2):
    n_j = MOE_CHUNKS

    def col(i, j, m):
        return jnp.where(m[MOE_ITEMS + i] > 0, j, n_j - 1)

    def expert(i, m):
        return m[i]

    tab = lambda f: pl.BlockSpec((1, 1, MOE_RMAX), f, memory_space=pltpu.SMEM)
    return pl.pallas_call(
        _experts_kernel,
        grid_spec=pltpu.PrefetchScalarGridSpec(
            num_scalar_prefetch=1,
            grid=(MOE_ITEMS, n_j),
            in_specs=[
                tab(lambda i, j, m: (i, 0, 0)),
                pl.BlockSpec(memory_space=pl.ANY),
                pl.BlockSpec((None, D_MODEL, MOE_TN), lambda i, j, m: (expert(i, m), 0, col(i, j, m))),
                pl.BlockSpec((None, D_MODEL, MOE_TN), lambda i, j, m: (expert(i, m), 0, n_j + col(i, j, m))),
                pl.BlockSpec((None, 1, MOE_TN), lambda i, j, m: (expert(i, m), 0, col(i, j, m))),
                pl.BlockSpec((None, 1, MOE_TN), lambda i, j, m: (expert(i, m), 0, n_j + col(i, j, m))),
                pl.BlockSpec((None, MOE_TN, D_MODEL), lambda i, j, m: (expert(i, m), col(i, j, m), 0)),
                pl.BlockSpec((None, 1, D_MODEL), lambda i, j, m: (expert(i, m), 0, 0)),
            ],
            out_specs=pl.BlockSpec(memory_space=pl.ANY),
            scratch_shapes=[
                pltpu.VMEM((2, MOE_RMAX * ROW_SUB, LANES), jnp.uint32),
                pltpu.VMEM((1, MOE_RMAX * ROW_SUB, LANES), jnp.uint32),
                pltpu.VMEM((MOE_RMAX, D_MODEL), jnp.float32),
                pltpu.SemaphoreType.DMA((2,)),
                pltpu.SemaphoreType.DMA((1,)),
            ],
        ),
        out_shape=jax.ShapeDtypeStruct((Y_ROWS * ROW_SUB, LANES), jnp.uint32),
        compiler_params=_params("arbitrary", "arbitrary"),
        name="experts",
    )(meta, dst_tab, x_sorted, w1, w1,
      b1.reshape(N_EXPERTS, 1, 2 * D_EXPERT), b1.reshape(N_EXPERTS, 1, 2 * D_EXPERT),
      w2, b2.reshape(N_EXPERTS, 1, D_MODEL))


def _combine_kernel(y0_ref, y1_ref, y2_ref, y3_ref, gates_ref, x1_ref, gate2_ref, g_ref, b_ref, o_ref):
    y_lo = jnp.zeros((COMBINE_TM, D_PACK), jnp.float32)
    y_hi = jnp.zeros((COMBINE_TM, D_PACK), jnp.float32)
    for k, y_ref in enumerate((y0_ref, y1_ref, y2_ref, y3_ref)):
        lo, hi = _unpack_words(_load_packed(y_ref, (), 0, COMBINE_TM))
        gate = gates_ref[:, k:k + 1]
        y_lo = y_lo + gate * lo
        y_hi = y_hi + gate * hi
    y = jnp.concatenate([y_lo, y_hi], axis=1)
    o_ref[...] = _layer_norm(DEEPNORM_ALPHA * x1_ref[...] + gate2_ref[...] * y, g_ref[...], b_ref[...])


def _combine(y_packed, gates, x1, gate2, ln_g, ln_b):
    n_tiles = SEQ // COMBINE_TM
    row = lambda i: (i, 0)
    vec = pl.BlockSpec((1, D_MODEL), lambda i: (0, 0))
    slot = lambda k: pl.BlockSpec((COMBINE_TM * ROW_SUB, LANES), lambda i: (k * n_tiles + i, 0))
    return pl.pallas_call(
        _combine_kernel,
        grid=(n_tiles,),
        in_specs=[
            slot(0), slot(1), slot(2), slot(3),
            pl.BlockSpec((COMBINE_TM, TOP_K), row),
            pl.BlockSpec((COMBINE_TM, D_MODEL), row),
            vec, vec, vec,
        ],
        out_specs=pl.BlockSpec((COMBINE_TM, D_MODEL), row),
        out_shape=jax.ShapeDtypeStruct((SEQ, D_MODEL), jnp.float32),
        compiler_params=_params("arbitrary"),
        name="combine",
    )(y_packed, y_packed, y_packed, y_packed, gates, x1, gate2, ln_g, ln_b)


TAB_ROWS = MOE_RMAX // LANES
WINDOW_ROWS = 2 * ROW_SUB
assert WINDOW_ROWS > TAB_ROWS and TOP_K == 4


def _row_tables_kernel(win_ref, order_ref, dst_ref):
    b = pl.program_id(0)
    w0 = win_ref[b]
    n_real = win_ref[MOE_ITEMS + 1 + b]
    off = w0 % LANES
    x = order_ref[pl.ds(w0 // LANES, WINDOW_ROWS), :]
    x = pltpu.roll(x, (LANES - off) % LANES, axis=1)
    lane = lax.broadcasted_iota(jnp.int32, (TAB_ROWS, LANES), 1)
    sub = lax.broadcasted_iota(jnp.int32, (TAB_ROWS, LANES), 0)
    flat = jnp.where(lane < LANES - off, x[:TAB_ROWS], x[1:TAB_ROWS + 1])
    r = sub * LANES + lane
    tok = flat >> 2
    slot = flat & 3
    spill = SEQ * TOP_K + ((b + 1) % 2) * MOE_RMAX + r
    dst_ref[...] = jnp.where(r < n_real, slot * SEQ + tok, spill) * ROW_SUB


def _row_tables(win, order):
    n_rows = SEQ * TOP_K // LANES
    order2d = jnp.concatenate([order, jnp.zeros((WINDOW_ROWS * LANES,), jnp.int32)]).reshape(-1, LANES)
    return pl.pallas_call(
        _row_tables_kernel,
        grid_spec=pltpu.PrefetchScalarGridSpec(
            num_scalar_prefetch=1,
            grid=(MOE_ITEMS + 1,),
            in_specs=[pl.BlockSpec((n_rows + WINDOW_ROWS, LANES), lambda b, w: (0, 0))],
            out_specs=pl.BlockSpec((None, TAB_ROWS, LANES), lambda b, w: (b, 0, 0)),
        ),
        out_shape=jax.ShapeDtypeStruct((MOE_ITEMS + 1, TAB_ROWS, LANES), jnp.int32),
        compiler_params=_params("arbitrary"),
        name="row_tables",
    )(win, order2d)


def _dispatch_kernel(fill_ref, pos_ref, h_ref, x_hbm, zero_ref, buf_ref, sem, zsem):
    i = pl.program_id(0)
    group_sub = MOE_G * ROW_SUB

    @pl.when(i == 0)
    def _():
        zero_ref[...] = jnp.zeros_like(zero_ref)

        def fill(g):
            dst = pl.multiple_of(g * group_sub, group_sub)
            pltpu.make_async_copy(zero_ref, x_hbm.at[pl.ds(dst, group_sub), :], zsem).start()

        for e in range(N_EXPERTS):
            end = fill_ref[e]
            begin = fill_ref[e - 1] if e else 0
            pl.when(end > begin)(lambda end=end: fill(end - 1))
        lax.fori_loop(fill_ref[N_EXPERTS - 1], X_ROWS // MOE_G, lambda g, c: (fill(g), c)[1], 0)

        def wait_fill(t, c):
            pltpu.make_async_copy(zero_ref, zero_ref, zsem).wait()
            return c
        lax.fori_loop(0, fill_ref[N_EXPERTS], wait_fill, 0)

    slot = i % 2
    buf_ref[slot] = h_ref[...]

    def token(t, c):
        src = pl.multiple_of(t * ROW_SUB, ROW_SUB)
        for k in range(TOP_K):
            dst = pl.multiple_of(pos_ref[0, 0, t * TOP_K + k], ROW_SUB)
            pltpu.make_async_copy(buf_ref.at[slot, pl.ds(src, ROW_SUB), :], x_hbm.at[pl.ds(dst, ROW_SUB), :],
                                  sem.at[slot]).start(priority=k % 2)
        return c
    lax.fori_loop(0, DISPATCH_TM, token, 0, unroll=4)

    def wait_tile(s):
        for k in range(TOP_K):
            pltpu.make_async_copy(buf_ref.at[s], buf_ref.at[s], sem.at[s]).wait()

    pl.when(i > 0)(lambda: wait_tile(1 - slot))
    pl.when(i == pl.num_programs(0) - 1)(lambda: wait_tile(slot))


def _dispatch(h_packed, pos, fill):
    n_tiles = SEQ // DISPATCH_TM
    return pl.pallas_call(
        _dispatch_kernel,
        grid_spec=pltpu.PrefetchScalarGridSpec(
            num_scalar_prefetch=1,
            grid=(n_tiles,),
            in_specs=[
                pl.BlockSpec((1, 1, DISPATCH_TM * TOP_K), lambda i, f: (i, 0, 0), memory_space=pltpu.SMEM),
                pl.BlockSpec((DISPATCH_TM * ROW_SUB, LANES), lambda i, f: (i, 0)),
            ],
            out_specs=pl.BlockSpec(memory_space=pl.ANY),
            scratch_shapes=[
                pltpu.VMEM((MOE_G * ROW_SUB, LANES), jnp.uint32),
                pltpu.VMEM((2, DISPATCH_TM * ROW_SUB, LANES), jnp.uint32),
                pltpu.SemaphoreType.DMA((2,)),
                pltpu.SemaphoreType.DMA(()),
            ],
        ),
        out_shape=jax.ShapeDtypeStruct((X_ROWS * ROW_SUB, LANES), jnp.uint32),
        compiler_params=_params("arbitrary"),
        name="dispatch",
    )(fill, pos.reshape(n_tiles, 1, DISPATCH_TM * TOP_K), h_packed)


def _route(logits):
    top_val, top_idx = lax.top_k(logits, TOP_K)
    gates = jax.nn.softmax(top_val, axis=-1)
    e_flat = top_idx.reshape(-1).astype(jnp.int32)
    experts = jnp.arange(N_EXPERTS, dtype=jnp.int32)
    counts = jnp.sum((e_flat[:, None] == experts[None, :]).astype(jnp.int32), axis=0)
    groups = (counts + MOE_G - 1) // MOE_G
    group_end = jnp.cumsum(groups)
    row0 = (group_end - groups) * MOE_G
    per_item = MOE_RMAX // MOE_G
    n_items_e = (groups + per_item - 1) // per_item
    item_end = jnp.cumsum(n_items_e)
    item_start = item_end - n_items_e
    n_items = item_end[-1]

    item = jnp.arange(MOE_ITEMS, dtype=jnp.int32)
    used = item < n_items
    e_item = jnp.minimum(jnp.searchsorted(item_end, item, side='right'), N_EXPERTS - 1).astype(jnp.int32)
    e_last = e_item[jnp.maximum(n_items - 1, 0)]
    part = item - item_start[e_item]
    rows = jnp.where(used, jnp.clip(groups[e_item] - part * per_item, 0, per_item) * MOE_G, 0)
    first_row = jnp.where(used, row0[e_item] + part * MOE_RMAX, 0)
    meta = jnp.concatenate([jnp.where(used, e_item, e_last), rows, first_row]).astype(jnp.int32)

    order = jnp.argsort(e_flat, stable=True).astype(jnp.int32)
    rank = jnp.argsort(order).astype(jnp.int32)
    start = jnp.cumsum(counts) - counts
    shift = jnp.sum(jnp.where(e_flat[:, None] == experts[None, :], (row0 - start)[None, :], 0), axis=1)
    pos = (rank + shift) * ROW_SUB
    n_fill = jnp.sum((groups > 0).astype(jnp.int32)) + X_ROWS // MOE_G - group_end[-1]
    fill = jnp.concatenate([group_end, n_fill[None]]).astype(jnp.int32)

    window0 = jnp.where(used, start[e_item] + part * MOE_RMAX, 0)
    n_real = jnp.where(used, jnp.clip(counts[e_item] - part * MOE_RMAX, 0, MOE_RMAX), 0)
    zero = jnp.zeros((1,), jnp.int32)
    win = jnp.concatenate([zero, window0, zero, n_real]).astype(jnp.int32)
    dst_tab = _row_tables(win, order).reshape(MOE_ITEMS + 1, 1, MOE_RMAX)
    return gates, meta, pos.astype(jnp.int32), fill, dst_tab


def kernel(x, c, w_ada, b_ada, w_in, sgu_ln_g, sgu_ln_b, w_spatial, b_spatial, w_o, ln1_g, ln1_b,
           w_router, b_router, w_exp1, b_exp1, w_exp2, b_exp2, ln2_g, ln2_b):
    depth = w_ada.shape[0]
    assert x.shape == (1, SEQ, D_MODEL)
    xs = x.reshape(SEQ, D_MODEL)
    for l in range(depth):
        ada = _ada(c, w_ada[l], b_ada[l])
        shift1, scale1, gate1, shift2, scale2, gate2 = jnp.split(ada, 6, axis=-1)

        proj = _in_proj(xs, scale1, shift1, w_in[l].astype(jnp.bfloat16))
        y_a = _gmlp(proj, sgu_ln_g[l], sgu_ln_b[l], w_spatial[l], b_spatial[l])
        y_b = _attention(proj)
        x1, h_packed, logits = _out_proj(y_a, y_b, w_o[l].astype(jnp.bfloat16), xs, gate1,
                                         ln1_g[l].reshape(1, -1), ln1_b[l].reshape(1, -1), scale2, shift2,
                                         w_router[l], b_router[l].reshape(1, -1))

        gates, meta, pos, fill, dst_tab = _route(logits)
        x_sorted = _dispatch(h_packed, pos, fill)
        y_packed = _experts(x_sorted, meta, dst_tab, w_exp1[l], b_exp1[l], w_exp2[l], b_exp2[l])
        xs = _combine(y_packed, gates, x1, gate2, ln2_g[l].reshape(1, -1), ln2_b[l].reshape(1, -1))
    return xs.reshape(x.shape)
```

```python
import math

import jax
import jax.numpy as jnp
from jax import lax
from jax.experimental import pallas as pl
from jax.experimental.pallas import tpu as pltpu

D_MODEL = 2048
SEQ = 8192
D_GMLP = 1024
GMLP_GROUPS = 8
GROUP_DIM = 128
CHUNK = 128
D_ATTN = 1024
HEAD_DIM = 128
N_HEADS = 8
BRANCHES = ((128, 1), (512, 4), (2048, 16))
BLK = 128
SPAN = 16 * BLK
D_IN_PROJ = 2 * D_GMLP + 3 * D_ATTN
N_EXPERTS = 32
TOP_K = 4
D_EXPERT = 2048
SWIGLU_LIMIT = 7.0
SWIGLU_ALPHA = 1.702
LN_EPS = 1e-5
DEEPNORM_ALPHA = 2.0 ** 0.25
NEG = -1e30

LANES = 128
VMEM_LIMIT = 56 * 1024 * 1024

ADA_TN = 1536
PROJ_TM = 1024
PROJ_TN = 1024
OUT_TM = 512
ATTN_UNROLL = 8
ROW_SUB = 8
D_PACK = D_MODEL // 2
MOE_G = 128
MOE_RMAX = 1152
MOE_ITEMS = 64
MOE_TN = 512
MOE_CHUNKS = D_EXPERT // MOE_TN
MOE_SUBTILES = (512, 256, 128)
X_ROWS = SEQ * TOP_K + N_EXPERTS * MOE_G
Y_ROWS = SEQ * TOP_K + 2 * MOE_RMAX
DISPATCH_TM = 256
COMBINE_TM = 256


def _params(*sem):
    return pltpu.CompilerParams(dimension_semantics=sem, vmem_limit_bytes=VMEM_LIMIT)


def _layer_norm(x, g, b):
    mu = jnp.mean(x, axis=-1, keepdims=True)
    xc = x - mu
    var = jnp.mean(xc * xc, axis=-1, keepdims=True)
    return xc * lax.rsqrt(var + LN_EPS) * g + b


def _pack_rows(x):
    r = x.astype(jnp.bfloat16).astype(jnp.float32)
    bits = lax.bitcast_convert_type(r, jnp.uint32)
    return (bits[:, D_PACK:] & jnp.uint32(0xFFFF0000)) | (bits[:, :D_PACK] >> 16)


def _unpack_words(u):
    lo = lax.bitcast_convert_type(u << 16, jnp.float32)
    hi = lax.bitcast_convert_type(u & jnp.uint32(0xFFFF0000), jnp.float32)
    return lo, hi


def _store_packed(ref, lead, row0, packed):
    m = packed.shape[0]
    for c in range(ROW_SUB):
        rows = pl.ds(row0 * ROW_SUB + c, m, stride=ROW_SUB)
        ref[lead + (rows, slice(None))] = packed[:, c * LANES:(c + 1) * LANES]


def _load_packed(ref, lead, row0, m):
    cols = [ref[lead + (pl.ds(row0 * ROW_SUB + c, m, stride=ROW_SUB), slice(None))] for c in range(ROW_SUB)]
    return jnp.concatenate(cols, axis=1)


def _ada_kernel(c_ref, w_ref, b_ref, o_ref):
    c = c_ref[...]
    s = c * jax.nn.sigmoid(c)
    o_ref[...] = jnp.sum(s * w_ref[...], axis=0, keepdims=True) + b_ref[...]


def _ada(c, w_ada, b_ada):
    n = w_ada.shape[1]
    return pl.pallas_call(
        _ada_kernel,
        grid=(n // ADA_TN,),
        in_specs=[
            pl.BlockSpec((D_MODEL, 1), lambda j: (0, 0)),
            pl.BlockSpec((D_MODEL, ADA_TN), lambda j: (0, j)),
            pl.BlockSpec((1, ADA_TN), lambda j: (0, j)),
        ],
        out_specs=pl.BlockSpec((1, ADA_TN), lambda j: (0, j)),
        out_shape=jax.ShapeDtypeStruct((1, n), jnp.float32),
        compiler_params=_params("arbitrary"),
        name="ada",
    )(c.reshape(D_MODEL, 1), w_ada, b_ada.reshape(1, n))


def _in_proj_kernel(x_ref, sc_ref, sh_ref, w_ref, o_ref, h_ref):
    @pl.when(pl.program_id(1) == 0)
    def _():
        h_ref[...] = (x_ref[...] * (1.0 + sc_ref[...]) + sh_ref[...]).astype(jnp.bfloat16)

    o_ref[...] = jnp.dot(h_ref[...], w_ref[...], preferred_element_type=jnp.float32)


def _in_proj(x, scale1, shift1, w_in_bf16):
    return pl.pallas_call(
        _in_proj_kernel,
        grid=(SEQ // PROJ_TM, D_IN_PROJ // PROJ_TN),
        in_specs=[
            pl.BlockSpec((PROJ_TM, D_MODEL), lambda i, j: (i, 0)),
            pl.BlockSpec((1, D_MODEL), lambda i, j: (0, 0)),
            pl.BlockSpec((1, D_MODEL), lambda i, j: (0, 0)),
            pl.BlockSpec((D_MODEL, PROJ_TN), lambda i, j: (0, j)),
        ],
        out_specs=pl.BlockSpec((PROJ_TM, PROJ_TN), lambda i, j: (i, j)),
        out_shape=jax.ShapeDtypeStruct((SEQ, D_IN_PROJ), jnp.float32),
        scratch_shapes=[pltpu.VMEM((PROJ_TM, D_MODEL), jnp.bfloat16)],
        compiler_params=_params("arbitrary", "arbitrary"),
        name="in_proj",
    )(x, scale1, shift1, w_in_bf16)


def _gelu(x):
    return 0.5 * x * (1.0 + lax.erf(x * (1.0 / math.sqrt(2.0))))


def _gmlp_kernel(ua_ref, va_ref, g_ref, b_ref, w_ref, bs_ref, o_ref):
    row = lax.broadcasted_iota(jnp.int32, (CHUNK, CHUNK), 0)
    col = lax.broadcasted_iota(jnp.int32, (CHUNK, CHUNK), 1)
    causal = col <= row
    for g in range(GMLP_GROUPS):
        sl = slice(g * GROUP_DIM, (g + 1) * GROUP_DIM)
        v = _layer_norm(_gelu(va_ref[:, sl]), g_ref[g:g + 1, :], b_ref[g:g + 1, :])
        w = jnp.where(causal, w_ref[g], 0.0).astype(jnp.bfloat16)
        s = jnp.dot(w, v.astype(jnp.bfloat16), preferred_element_type=jnp.float32)
        s = s + bs_ref[:, g:g + 1]
        o_ref[:, sl] = (_gelu(ua_ref[:, sl]) * s).astype(o_ref.dtype)


def _gmlp(proj, ln_g, ln_b, w_spatial, b_spatial):
    return pl.pallas_call(
        _gmlp_kernel,
        grid=(SEQ // CHUNK,),
        in_specs=[
            pl.BlockSpec((CHUNK, D_GMLP), lambda n: (n, 0)),
            pl.BlockSpec((CHUNK, D_GMLP), lambda n: (n, 1)),
            pl.BlockSpec((GMLP_GROUPS, GROUP_DIM), lambda n: (0, 0)),
            pl.BlockSpec((GMLP_GROUPS, GROUP_DIM), lambda n: (0, 0)),
            pl.BlockSpec((GMLP_GROUPS, CHUNK, CHUNK), lambda n: (0, 0, 0)),
            pl.BlockSpec((CHUNK, GMLP_GROUPS), lambda n: (0, 0)),
        ],
        out_specs=pl.BlockSpec((CHUNK, D_GMLP), lambda n: (n, 0)),
        out_shape=jax.ShapeDtypeStruct((SEQ, D_GMLP), jnp.bfloat16),
        compiler_params=_params("arbitrary"),
        name="gmlp",
    )(proj, proj, ln_g, ln_b, w_spatial, b_spatial.T)


def _attn_kernel(q_ref, k_ref, v_ref, o_ref, out_ref, lse_ref, kv_ref):
    head = pl.program_id(0)
    span = pl.program_id(1)
    log2e = 1.0 / math.log(2.0)
    head_no = (jnp.zeros((BLK, 2 * BLK), jnp.int32) + (head + 1)).astype(jnp.float32)
    slope = jnp.exp2(head_no * (-8.0 / N_HEADS)) * log2e
    scale = HEAD_DIM ** -0.5 * log2e
    qi = lax.broadcasted_iota(jnp.int32, (BLK, 2 * BLK), 0)
    ki = lax.broadcasted_iota(jnp.int32, (BLK, 2 * BLK), 1)
    step = qi + BLK - ki
    contract_last = (((1,), (1,)), ((), ()))
    ones = jnp.ones((2 * BLK, LANES), jnp.bfloat16)

    for b, (window, d) in enumerate(BRANCHES):
        assert window // d == BLK
        valid = (step >= 0) & (step <= BLK)
        bias = jnp.where(valid, -slope * d * step.astype(jnp.float32), NEG)
        bias_first = jnp.where(ki >= BLK, bias, NEG)

        per_class = d * BLK == SPAN
        cur = span % 2
        if per_class:
            def stash(r, carry, d=d):
                rows = pl.ds(pl.multiple_of(r * BLK, BLK), BLK)
                kv_ref[0, cur, rows, :] = k_ref[pl.ds(span * SPAN + r, BLK, stride=d), :].astype(jnp.bfloat16)
                kv_ref[1, cur, rows, :] = v_ref[pl.ds(span * SPAN + r, BLK, stride=d), :].astype(jnp.bfloat16)
                return carry
            lax.fori_loop(0, d, stash, 0, unroll=4)

        def tile(t, carry, d=d, b=b, bias=bias, bias_first=bias_first, per_class=per_class):
            r = t % d
            n = t // d
            q0 = n * (BLK * d) + r
            k0 = span * SPAN + q0
            first = k0 < BLK * d
            q = (q_ref[pl.ds(q0, BLK, stride=d), :] * scale).astype(jnp.bfloat16)
            if per_class:
                rows = pl.ds(pl.multiple_of(r * BLK, BLK), BLK)
                prev = jnp.where(first, cur, 1 - cur)
                k = jnp.concatenate([kv_ref[0, prev, rows, :], kv_ref[0, cur, rows, :]], axis=0)
                v = jnp.concatenate([kv_ref[1, prev, rows, :], kv_ref[1, cur, rows, :]], axis=0)
            else:
                kp = jnp.where(first, k0, k0 - BLK * d)
                k = jnp.concatenate([k_ref[pl.ds(kp, BLK, stride=d), :], k_ref[pl.ds(k0, BLK, stride=d), :]],
                                    axis=0).astype(jnp.bfloat16)
                v = jnp.concatenate([v_ref[pl.ds(kp, BLK, stride=d), :], v_ref[pl.ds(k0, BLK, stride=d), :]],
                                    axis=0).astype(jnp.bfloat16)
            s = lax.dot_general(q, k, contract_last, preferred_element_type=jnp.float32)
            s = s + jnp.where(first, bias_first, bias)
            m = jnp.max(jnp.maximum(s[:, :BLK], s[:, BLK:]), axis=-1, keepdims=True)
            p = jnp.exp2(s - m).astype(jnp.bfloat16)
            v_one = jnp.concatenate([v, ones], axis=1)
            pv = jnp.dot(p, v_one, preferred_element_type=jnp.float32)
            den = pv[:, HEAD_DIM:]
            rows = pl.ds(q0, BLK, stride=d)
            out_ref[b, rows, :] = pv[:, :HEAD_DIM] / den
            lse_ref[b, rows, :] = m + jnp.log2(den)
            return carry

        lax.fori_loop(0, SPAN // BLK, tile, 0, unroll=ATTN_UNROLL)

    lse_all = jnp.maximum(jnp.maximum(lse_ref[0], lse_ref[1]), lse_ref[2])
    num = jnp.zeros((SPAN, HEAD_DIM), jnp.float32)
    den = jnp.zeros((SPAN, LANES), jnp.float32)
    for b in range(len(BRANCHES)):
        w = jnp.exp2(lse_ref[b] - lse_all)
        num = num + w * out_ref[b]
        den = den + w
    o_ref[...] = (num / den).astype(o_ref.dtype)


def _attention(proj):
    q_col = 2 * D_GMLP // HEAD_DIM
    k_col = q_col + N_HEADS
    v_col = k_col + N_HEADS
    nb = len(BRANCHES)
    return pl.pallas_call(
        _attn_kernel,
        grid=(N_HEADS, SEQ // SPAN),
        in_specs=[
            pl.BlockSpec((SPAN, HEAD_DIM), lambda h, s: (s, q_col + h)),
            pl.BlockSpec((SEQ, HEAD_DIM), lambda h, s: (0, k_col + h)),
            pl.BlockSpec((SEQ, HEAD_DIM), lambda h, s: (0, v_col + h)),
        ],
        out_specs=pl.BlockSpec((SPAN, HEAD_DIM), lambda h, s: (s, h)),
        out_shape=jax.ShapeDtypeStruct((SEQ, D_ATTN), jnp.bfloat16),
        scratch_shapes=[
            pltpu.VMEM((nb, SPAN, HEAD_DIM), jnp.float32),
            pltpu.VMEM((nb, SPAN, LANES), jnp.float32),
            pltpu.VMEM((2, 2, SPAN, HEAD_DIM), jnp.bfloat16),
        ],
        compiler_params=_params("arbitrary", "arbitrary"),
        name="attn",
    )(proj, proj, proj)


def _split_bf16(x):
    hi = x.astype(jnp.bfloat16)
    lo = (x - hi.astype(jnp.float32)).astype(jnp.bfloat16)
    return hi, lo


def _out_proj_kernel(ya_ref, yb_ref, wa_ref, wb_ref, x_ref, gate_ref, g_ref, b_ref, sc_ref, sh_ref,
                     wr_ref, br_ref, x1_ref, hp_ref, lg_ref):
    mix = jnp.dot(ya_ref[...], wa_ref[...], preferred_element_type=jnp.float32)
    mix = mix + jnp.dot(yb_ref[...], wb_ref[...], preferred_element_type=jnp.float32)
    x1 = _layer_norm(DEEPNORM_ALPHA * x_ref[...] + gate_ref[...] * mix, g_ref[...], b_ref[...])
    x1_ref[...] = x1
    h = x1 * (1.0 + sc_ref[...]) + sh_ref[...]
    _store_packed(hp_ref, (), 0, _pack_rows(h))
    h_hi, h_lo = _split_bf16(h)
    w_hi, w_lo = _split_bf16(wr_ref[...])
    both = jnp.dot(h_hi, jnp.concatenate([w_hi, w_lo], axis=1), preferred_element_type=jnp.float32)
    lg = both[:, :N_EXPERTS] + both[:, N_EXPERTS:] + jnp.dot(h_lo, w_hi, preferred_element_type=jnp.float32)
    lg_ref[...] = lg + br_ref[...]


def _out_proj(y_a, y_b, w_o_bf16, x, gate1, ln_g, ln_b, scale2, shift2, w_router, b_router):
    row = lambda i: (i, 0)
    fixed = lambda i: (0, 0)
    vec = pl.BlockSpec((1, D_MODEL), fixed)
    return pl.pallas_call(
        _out_proj_kernel,
        grid=(SEQ // OUT_TM,),
        in_specs=[
            pl.BlockSpec((OUT_TM, D_GMLP), row),
            pl.BlockSpec((OUT_TM, D_ATTN), row),
            pl.BlockSpec((D_GMLP, D_MODEL), lambda i: (0, 0)),
            pl.BlockSpec((D_ATTN, D_MODEL), lambda i: (1, 0)),
            pl.BlockSpec((OUT_TM, D_MODEL), row),
            vec, vec, vec, vec, vec,
            pl.BlockSpec((D_MODEL, N_EXPERTS), fixed),
            pl.BlockSpec((1, N_EXPERTS), fixed),
        ],
        out_specs=[
            pl.BlockSpec((OUT_TM, D_MODEL), row),
            pl.BlockSpec((OUT_TM * ROW_SUB, LANES), row),
            pl.BlockSpec((OUT_TM, N_EXPERTS), row),
        ],
        out_shape=[
            jax.ShapeDtypeStruct((SEQ, D_MODEL), jnp.float32),
            jax.ShapeDtypeStruct((SEQ * ROW_SUB, LANES), jnp.uint32),
            jax.ShapeDtypeStruct((SEQ, N_EXPERTS), jnp.float32),
        ],
        compiler_params=_params("arbitrary"),
        name="out_proj",
    )(y_a, y_b, w_o_bf16, w_o_bf16, x, gate1, ln_g, ln_b, scale2, shift2, w_router, b_router)


def _experts_kernel(meta_ref, dstp_ref, x_hbm, wg_ref, wl_ref, bg_ref, bl_ref, w2_ref,
                    b2_ref, y_hbm, xs_ref, os_ref, acc_ref, gsem, ssem):
    i = pl.program_id(0)
    j = pl.program_id(1)
    n_items = pl.num_programs(0)
    last_j = pl.num_programs(1) - 1

    def item_rows(k):
        inside = (k >= 0) & (k < n_items)
        return jnp.where(inside, meta_ref[MOE_ITEMS + jnp.clip(k, 0, n_items - 1)], 0)

    rows = item_rows(i)
    rows_prev = item_rows(i - 1)
    p = i % 2
    q = 1 - p
    group_sub = MOE_G * ROW_SUB

    def load_rows(k, slot):
        first = meta_ref[2 * MOE_ITEMS + jnp.clip(k, 0, n_items - 1)] * ROW_SUB

        def load_group(t, c):
            src = pl.multiple_of(first + t * group_sub, group_sub)
            dst = pl.multiple_of(t * group_sub, group_sub)
            pltpu.make_async_copy(x_hbm.at[pl.ds(src, group_sub), :], xs_ref.at[slot, pl.ds(dst, group_sub), :],
                                  gsem.at[slot]).start()
            return c
        lax.fori_loop(0, item_rows(k) // MOE_G, load_group, 0)

    def scatter(slot, row, queue=1):
        entry = dstp_ref[0, 0, row]
        src = pl.multiple_of(row * ROW_SUB, ROW_SUB)
        dst = pl.multiple_of(entry, ROW_SUB)
        pltpu.make_async_copy(os_ref.at[slot, pl.ds(src, ROW_SUB), :], y_hbm.at[pl.ds(dst, ROW_SUB), :],
                              ssem.at[slot]).start(priority=queue)
        return entry

    def wait_rows(buf_ref, sem, slot, n):
        group = buf_ref.at[slot, pl.ds(0, group_sub), :]

        def wait_group(t, c):
            pltpu.make_async_copy(group, group, sem.at[slot]).wait()
            return c
        lax.fori_loop(0, n // MOE_G, wait_group, 0)

    def each_row(lo, hi, fn):
        lax.fori_loop(lo, jnp.maximum(lo, hi), lambda r, c: (fn(r), c)[1], 0)

    @pl.when((i == 0) & (j == 0))
    def _():
        acc_ref[...] = jnp.zeros_like(acc_ref)
        os_ref[...] = jnp.zeros_like(os_ref)
        for half in range(2):
            fill = pltpu.make_async_copy(
                os_ref.at[0],
                y_hbm.at[pl.ds((SEQ * TOP_K + half * MOE_RMAX) * ROW_SUB, MOE_RMAX * ROW_SUB), :],
                ssem.at[0])
            fill.start()
            fill.wait()
        load_rows(0, 0)

    @pl.when(j == 0)
    def _():
        wait_rows(xs_ref, gsem, p, rows)
        load_rows(i + 1, q)

    def sub_tile(a, m):
        a = pl.multiple_of(a, MOE_G)
        base = (j * rows + a) // MOE_CHUNKS
        lag = 0
        for u in range(m // MOE_CHUNKS):
            lag = scatter(0, base + u + lag, queue=u % 2) >> 31
        x_lo, x_hi = _unpack_words(_load_packed(xs_ref, (p,), a, m))
        x = jnp.concatenate([x_lo.astype(jnp.bfloat16), x_hi.astype(jnp.bfloat16)], axis=1)

        def up(w_ref, b_ref):
            return jnp.dot(x, w_ref[...].astype(jnp.bfloat16), preferred_element_type=jnp.float32) + b_ref[...]

        glu = jnp.minimum(up(wg_ref, bg_ref), SWIGLU_LIMIT)
        lin = jnp.clip(up(wl_ref, bl_ref), -SWIGLU_LIMIT, SWIGLU_LIMIT)
        act = glu * jax.nn.sigmoid(SWIGLU_ALPHA * glu) * (lin + 1.0)
        down = jnp.dot(act.astype(jnp.bfloat16), w2_ref[...].astype(jnp.bfloat16),
                       preferred_element_type=jnp.float32)
        start = jnp.where(j == 0, jnp.broadcast_to(b2_ref[...], (m, D_MODEL)), acc_ref[pl.ds(a, m), :])
        acc_ref[pl.ds(a, m), :] = start + down

    @pl.when(rows > 0)
    def _():
        main = MOE_SUBTILES[0]
        n_main = rows // main
        lax.fori_loop(0, n_main, lambda t, c: (sub_tile(t * main, main), c)[1], 0)
        done = n_main * main
        for m in MOE_SUBTILES[1:]:
            has = ((rows - done) // m) % 2 == 1
            pl.when(has)(lambda done=done, m=m: sub_tile(done, m))
            done = done + jnp.where(has, m, 0)

    @pl.when(j == last_j)
    def _():
        each_row(rows, rows_prev, lambda r: scatter(0, r))
        wait_rows(os_ref, ssem, 0, jnp.maximum(rows, rows_prev))

        def pack(t, c):
            a = pl.multiple_of(t * MOE_G, MOE_G)
            _store_packed(os_ref, (0,), a, _pack_rows(acc_ref[pl.ds(a, MOE_G), :]))
            return c
        lax.fori_loop(0, rows // MOE_G, pack, 0)


def _experts(x_sorted, meta, dst_tab, w1, b1, w2, b2):
    n_j = MOE_CHUNKS

    def col(i, j, m):
        return jnp.where(m[MOE_ITEMS + i] > 0, j, n_j - 1)

    def expert(i, m):
        return m[i]

    tab = lambda f: pl.BlockSpec((1, 1, MOE_RMAX), f, memory_space=pltpu.SMEM)
    return pl.pallas_call(
        _experts_kernel,
        grid_spec=pltpu.PrefetchScalarGridSpec(
            num_scalar_prefetch=1,
            grid=(MOE_ITEMS, n_j),
            in_specs=[
                tab(lambda i, j, m: (i, 0, 0)),
                pl.BlockSpec(memory_space=pl.ANY),
                pl.BlockSpec((None, D_MODEL, MOE_TN), lambda i, j, m: (expert(i, m), 0, col(i, j, m))),
                pl.BlockSpec((None, D_MODEL, MOE_TN), lambda i, j, m: (expert(i, m), 0, n_j + col(i, j, m))),
                pl.BlockSpec((None, 1, MOE_TN), lambda i, j, m: (expert(i, m), 0, col(i, j, m))),
                pl.BlockSpec((None, 1, MOE_TN), lambda i, j, m: (expert(i, m), 0, n_j + col(i, j, m))),
                pl.BlockSpec((None, MOE_TN, D_MODEL), lambda i, j, m: (expert(i, m), col(i, j, m), 0)),
                pl.BlockSpec((None, 1, D_MODEL), lambda i, j, m: (expert(i, m), 0, 0)),
            ],
            out_specs=pl.BlockSpec(memory_space=pl.ANY),
            scratch_shapes=[
                pltpu.VMEM((2, MOE_RMAX * ROW_SUB, LANES), jnp.uint32),
                pltpu.VMEM((1, MOE_RMAX * ROW_SUB, LANES), jnp.uint32),
                pltpu.VMEM((MOE_RMAX, D_MODEL), jnp.float32),
                pltpu.SemaphoreType.DMA((2,)),
                pltpu.SemaphoreType.DMA((1,)),
            ],
        ),
        out_shape=jax.ShapeDtypeStruct((Y_ROWS * ROW_SUB, LANES), jnp.uint32),
        compiler_params=_params("arbitrary", "arbitrary"),
        name="experts",
    )(meta, dst_tab, x_sorted, w1, w1,
      b1.reshape(N_EXPERTS, 1, 2 * D_EXPERT), b1.reshape(N_EXPERTS, 1, 2 * D_EXPERT),
      w2, b2.reshape(N_EXPERTS, 1, D_MODEL))


def _combine_kernel(y0_ref, y1_ref, y2_ref, y3_ref, gates_ref, x1_ref, gate2_ref, g_ref, b_ref, o_ref):
    y_lo = jnp.zeros((COMBINE_TM, D_PACK), jnp.float32)
    y_hi = jnp.zeros((COMBINE_TM, D_PACK), jnp.float32)
    for k, y_ref in enumerate((y0_ref, y1_ref, y2_ref, y3_ref)):
        lo, hi = _unpack_words(_load_packed(y_ref, (), 0, COMBINE_TM))
        gate = gates_ref[:, k:k + 1]
        y_lo = y_lo + gate * lo
        y_hi = y_hi + gate * hi
    y = jnp.concatenate([y_lo, y_hi], axis=1)
    o_ref[...] = _layer_norm(DEEPNORM_ALPHA * x1_ref[...] + gate2_ref[...] * y, g_ref[...], b_ref[...])


def _combine(y_packed, gates, x1, gate2, ln_g, ln_b):
    n_tiles = SEQ // COMBINE_TM
    row = lambda i: (i, 0)
    vec = pl.BlockSpec((1, D_MODEL), lambda i: (0, 0))
    slot = lambda k: pl.BlockSpec((COMBINE_TM * ROW_SUB, LANES), lambda i: (k * n_tiles + i, 0))
    return pl.pallas_call(
        _combine_kernel,
        grid=(n_tiles,),
        in_specs=[
            slot(0), slot(1), slot(2), slot(3),
            pl.BlockSpec((COMBINE_TM, TOP_K), row),
            pl.BlockSpec((COMBINE_TM, D_MODEL), row),
            vec, vec, vec,
        ],
        out_specs=pl.BlockSpec((COMBINE_TM, D_MODEL), row),
        out_shape=jax.ShapeDtypeStruct((SEQ, D_MODEL), jnp.float32),
        compiler_params=_params("arbitrary"),
        name="combine",
    )(y_packed, y_packed, y_packed, y_packed, gates, x1, gate2, ln_g, ln_b)


TAB_ROWS = MOE_RMAX // LANES
WINDOW_ROWS = 2 * ROW_SUB
assert WINDOW_ROWS > TAB_ROWS and TOP_K == 4


def _row_tables_kernel(win_ref, order_ref, dst_ref):
    b = pl.program_id(0)
    w0 = win_ref[b]
    n_real = win_ref[MOE_ITEMS + 1 + b]
    off = w0 % LANES
    x = order_ref[pl.ds(w0 // LANES, WINDOW_ROWS), :]
    x = pltpu.roll(x, (LANES - off) % LANES, axis=1)
    lane = lax.broadcasted_iota(jnp.int32, (TAB_ROWS, LANES), 1)
    sub = lax.broadcasted_iota(jnp.int32, (TAB_ROWS, LANES), 0)
    flat = jnp.where(lane < LANES - off, x[:TAB_ROWS], x[1:TAB_ROWS + 1])
    r = sub * LANES + lane
    tok = flat >> 2
    slot = flat & 3
    spill = SEQ * TOP_K + ((b + 1) % 2) * MOE_RMAX + r
    dst_ref[...] = jnp.where(r < n_real, slot * SEQ + tok, spill) * ROW_SUB


def _row_tables(win, order):
    n_rows = SEQ * TOP_K // LANES
    order2d = jnp.concatenate([order, jnp.zeros((WINDOW_ROWS * LANES,), jnp.int32)]).reshape(-1, LANES)
    return pl.pallas_call(
        _row_tables_kernel,
        grid_spec=pltpu.PrefetchScalarGridSpec(
            num_scalar_prefetch=1,
            grid=(MOE_ITEMS + 1,),
            in_specs=[pl.BlockSpec((n_rows + WINDOW_ROWS, LANES), lambda b, w: (0, 0))],
            out_specs=pl.BlockSpec((None, TAB_ROWS, LANES), lambda b, w: (b, 0, 0)),
        ),
        out_shape=jax.ShapeDtypeStruct((MOE_ITEMS + 1, TAB_ROWS, LANES), jnp.int32),
        compiler_params=_params("arbitrary"),
        name="row_tables",
    )(win, order2d)


def _dispatch_kernel(fill_ref, pos_ref, h_ref, x_hbm, zero_ref, buf_ref, sem, zsem):
    i = pl.program_id(0)
    group_sub = MOE_G * ROW_SUB

    @pl.when(i == 0)
    def _():
        zero_ref[...] = jnp.zeros_like(zero_ref)

        def fill(g):
            dst = pl.multiple_of(g * group_sub, group_sub)
            pltpu.make_async_copy(zero_ref, x_hbm.at[pl.ds(dst, group_sub), :], zsem).start()

        for e in range(N_EXPERTS):
            end = fill_ref[e]
            begin = fill_ref[e - 1] if e else 0
            pl.when(end > begin)(lambda end=end: fill(end - 1))
        lax.fori_loop(fill_ref[N_EXPERTS - 1], X_ROWS // MOE_G, lambda g, c: (fill(g), c)[1], 0)

        def wait_fill(t, c):
            pltpu.make_async_copy(zero_ref, zero_ref, zsem).wait()
            return c
        lax.fori_loop(0, fill_ref[N_EXPERTS], wait_fill, 0)

    slot = i % 2
    buf_ref[slot] = h_ref[...]

    def token(t, c):
        src = pl.multiple_of(t * ROW_SUB, ROW_SUB)
        for k in range(TOP_K):
            dst = pl.multiple_of(pos_ref[0, 0, t * TOP_K + k], ROW_SUB)
            pltpu.make_async_copy(buf_ref.at[slot, pl.ds(src, ROW_SUB), :], x_hbm.at[pl.ds(dst, ROW_SUB), :],
                                  sem.at[slot]).start(priority=k % 2)
        return c
    lax.fori_loop(0, DISPATCH_TM, token, 0, unroll=4)

    def wait_tile(s):
        for k in range(TOP_K):
            pltpu.make_async_copy(buf_ref.at[s], buf_ref.at[s], sem.at[s]).wait()

    pl.when(i > 0)(lambda: wait_tile(1 - slot))
    pl.when(i == pl.num_programs(0) - 1)(lambda: wait_tile(slot))


def _dispatch(h_packed, pos, fill):
    n_tiles = SEQ // DISPATCH_TM
    return pl.pallas_call(
        _dispatch_kernel,
        grid_spec=pltpu.PrefetchScalarGridSpec(
            num_scalar_prefetch=1,
            grid=(n_tiles,),
            in_specs=[
                pl.BlockSpec((1, 1, DISPATCH_TM * TOP_K), lambda i, f: (i, 0, 0), memory_space=pltpu.SMEM),
                pl.BlockSpec((DISPATCH_TM * ROW_SUB, LANES), lambda i, f: (i, 0)),
            ],
            out_specs=pl.BlockSpec(memory_space=pl.ANY),
            scratch_shapes=[
                pltpu.VMEM((MOE_G * ROW_SUB, LANES), jnp.uint32),
                pltpu.VMEM((2, DISPATCH_TM * ROW_SUB, LANES), jnp.uint32),
                pltpu.SemaphoreType.DMA((2,)),
                pltpu.SemaphoreType.DMA(()),
            ],
        ),
        out_shape=jax.ShapeDtypeStruct((X_ROWS * ROW_SUB, LANES), jnp.uint32),
        compiler_params=_params("arbitrary"),
        name="dispatch",
    )(fill, pos.reshape(n_tiles, 1, DISPATCH_TM * TOP_K), h_packed)


def _route(logits):
    top_val, top_idx = lax.top_k(logits, TOP_K)
    gates = jax.nn.softmax(top_val, axis=-1)
    e_flat = top_idx.reshape(-1).astype(jnp.int32)
    experts = jnp.arange(N_EXPERTS, dtype=jnp.int32)
    counts = jnp.sum((e_flat[:, None] == experts[None, :]).astype(jnp.int32), axis=0)
    groups = (counts + MOE_G - 1) // MOE_G
    group_end = jnp.cumsum(groups)
    row0 = (group_end - groups) * MOE_G
    per_item = MOE_RMAX // MOE_G
    n_items_e = (groups + per_item - 1) // per_item
    item_end = jnp.cumsum(n_items_e)
    item_start = item_end - n_items_e
    n_items = item_end[-1]

    item = jnp.arange(MOE_ITEMS, dtype=jnp.int32)
    used = item < n_items
    e_item = jnp.minimum(jnp.searchsorted(item_end, item, side='right'), N_EXPERTS - 1).astype(jnp.int32)
    e_last = e_item[jnp.maximum(n_items - 1, 0)]
    part = item - item_start[e_item]
    rows = jnp.where(used, jnp.clip(groups[e_item] - part * per_item, 0, per_item) * MOE_G, 0)
    first_row = jnp.where(used, row0[e_item] + part * MOE_RMAX, 0)
    meta = jnp.concatenate([jnp.where(used, e_item, e_last), rows, first_row]).astype(jnp.int32)

    order = jnp.argsort(e_flat, stable=True).astype(jnp.int32)
    rank = jnp.argsort(order).astype(jnp.int32)
    start = jnp.cumsum(counts) - counts
    shift = jnp.sum(jnp.where(e_flat[:, None] == experts[None, :], (row0 - start)[None, :], 0), axis=1)
    pos = (rank + shift) * ROW_SUB
    n_fill = jnp.sum((groups > 0).astype(jnp.int32)) + X_ROWS // MOE_G - group_end[-1]
    fill = jnp.concatenate([group_end, n_fill[None]]).astype(jnp.int32)

    window0 = jnp.where(used, start[e_item] + part * MOE_RMAX, 0)
    n_real = jnp.where(used, jnp.clip(counts[e_item] - part * MOE_RMAX, 0, MOE_RMAX), 0)
    zero = jnp.zeros((1,), jnp.int32)
    win = jnp.concatenate([zero, window0, zero, n_real]).astype(jnp.int32)
    dst_tab = _row_tables(win, order).reshape(MOE_ITEMS + 1, 1, MOE_RMAX)
    return gates, meta, pos.astype(jnp.int32), fill, dst_tab


def kernel(x, c, w_ada, b_ada, w_in, sgu_ln_g, sgu_ln_b, w_spatial, b_spatial, w_o, ln1_g, ln1_b,
           w_router, b_router, w_exp1, b_exp1, w_exp2, b_exp2, ln2_g, ln2_b):
    depth = w_ada.shape[0]
    assert x.shape == (1, SEQ, D_MODEL)
    xs = x.reshape(SEQ, D_MODEL)
    for l in range(depth):
        ada = _ada(c, w_ada[l], b_ada[l])
        shift1, scale1, gate1, shift2, scale2, gate2 = jnp.split(ada, 6, axis=-1)

        proj = _in_proj(xs, scale1, shift1, w_in[l].astype(jnp.bfloat16))
        y_a = _gmlp(proj, sgu_ln_g[l], sgu_ln_b[l], w_spatial[l], b_spatial[l])
        y_b = _attention(proj)
        x1, h_packed, logits = _out_proj(y_a, y_b, w_o[l].astype(jnp.bfloat16), xs, gate1,
                                         ln1_g[l].reshape(1, -1), ln1_b[l].reshape(1, -1), scale2, shift2,
                                         w_router[l], b_router[l].reshape(1, -1))

        gates, meta, pos, fill, dst_tab = _route(logits)
        x_sorted = _dispatch(h_packed, pos, fill)
        y_packed = _experts(x_sorted, meta, dst_tab, w_exp1[l], b_exp1[l], w_exp2[l], b_exp2[l])
        xs = _combine(y_packed, gates, x1, gate2, ln2_g[l].reshape(1, -1), ln2_b[l].reshape(1, -1))
    return xs.reshape(x.shape)
```

```python
import math

import jax
import jax.numpy as jnp
from jax import lax
from jax.experimental import pallas as pl
from jax.experimental.pallas import tpu as pltpu

D_MODEL = 2048
SEQ = 8192
D_GMLP = 1024
GMLP_GROUPS = 8
GROUP_DIM = 128
CHUNK = 128
D_ATTN = 1024
HEAD_DIM = 128
N_HEADS = 8
BRANCHES = ((128, 1), (512, 4), (2048, 16))
BLK = 128
SPAN = 16 * BLK
D_IN_PROJ = 2 * D_GMLP + 3 * D_ATTN
N_EXPERTS = 32
TOP_K = 4
D_EXPERT = 2048
SWIGLU_LIMIT = 7.0
SWIGLU_ALPHA = 1.702
LN_EPS = 1e-5
DEEPNORM_ALPHA = 2.0 ** 0.25
NEG = -1e30

LANES = 128
VMEM_LIMIT = 56 * 1024 * 1024

ADA_TN = 1536
PROJ_TM = 1024
PROJ_TN = 1024
OUT_TM = 512
GMLP_TM = 512
ATTN_UNROLL = 8
ROW_SUB = 8
D_PACK = D_MODEL // 2
MOE_G = 128
MOE_RMAX = 1152
MOE_ITEMS = 64
MOE_TN = 512
MOE_CHUNKS = D_EXPERT // MOE_TN
MOE_SUBTILES = (512, 256, 128)
X_ROWS = SEQ * TOP_K + N_EXPERTS * MOE_G
Y_ROWS = SEQ * TOP_K + 2 * MOE_RMAX
DISPATCH_TM = 256
COMBINE_TM = 256


def _params(*sem):
    return pltpu.CompilerParams(dimension_semantics=sem, vmem_limit_bytes=VMEM_LIMIT)


def _layer_norm(x, g, b):
    mu = jnp.mean(x, axis=-1, keepdims=True)
    xc = x - mu
    var = jnp.mean(xc * xc, axis=-1, keepdims=True)
    return xc * lax.rsqrt(var + LN_EPS) * g + b


def _pack_rows(x):
    r = x.astype(jnp.bfloat16).astype(jnp.float32)
    bits = lax.bitcast_convert_type(r, jnp.uint32)
    return (bits[:, D_PACK:] & jnp.uint32(0xFFFF0000)) | (bits[:, :D_PACK] >> 16)


def _unpack_words(u):
    lo = lax.bitcast_convert_type(u << 16, jnp.float32)
    hi = lax.bitcast_convert_type(u & jnp.uint32(0xFFFF0000), jnp.float32)
    return lo, hi


def _store_packed(ref, lead, row0, packed):
    m = packed.shape[0]
    for c in range(ROW_SUB):
        rows = pl.ds(row0 * ROW_SUB + c, m, stride=ROW_SUB)
        ref[lead + (rows, slice(None))] = packed[:, c * LANES:(c + 1) * LANES]


def _load_packed(ref, lead, row0, m):
    cols = [ref[lead + (pl.ds(row0 * ROW_SUB + c, m, stride=ROW_SUB), slice(None))] for c in range(ROW_SUB)]
    return jnp.concatenate(cols, axis=1)


def _ada_kernel(c_ref, w_ref, b_ref, o_ref):
    c = c_ref[...]
    s = c * jax.nn.sigmoid(c)
    o_ref[...] = jnp.sum(s * w_ref[...], axis=0, keepdims=True) + b_ref[...]


def _ada(c, w_ada, b_ada):
    n = w_ada.shape[1]
    return pl.pallas_call(
        _ada_kernel,
        grid=(n // ADA_TN,),
        in_specs=[
            pl.BlockSpec((D_MODEL, 1), lambda j: (0, 0)),
            pl.BlockSpec((D_MODEL, ADA_TN), lambda j: (0, j)),
            pl.BlockSpec((1, ADA_TN), lambda j: (0, j)),
        ],
        out_specs=pl.BlockSpec((1, ADA_TN), lambda j: (0, j)),
        out_shape=jax.ShapeDtypeStruct((1, n), jnp.float32),
        compiler_params=_params("arbitrary"),
        name="ada",
    )(c.reshape(D_MODEL, 1), w_ada, b_ada.reshape(1, n))


def _in_proj_kernel(x_ref, sc_ref, sh_ref, w_ref, o_ref, h_ref):
    @pl.when(pl.program_id(1) == 0)
    def _():
        h_ref[...] = (x_ref[...] * (1.0 + sc_ref[...]) + sh_ref[...]).astype(jnp.bfloat16)

    o_ref[...] = jnp.dot(h_ref[...], w_ref[...], preferred_element_type=jnp.float32)


def _in_proj(x, scale1, shift1, w_in_bf16):
    return pl.pallas_call(
        _in_proj_kernel,
        grid=(SEQ // PROJ_TM, D_IN_PROJ // PROJ_TN),
        in_specs=[
            pl.BlockSpec((PROJ_TM, D_MODEL), lambda i, j: (i, 0)),
            pl.BlockSpec((1, D_MODEL), lambda i, j: (0, 0)),
            pl.BlockSpec((1, D_MODEL), lambda i, j: (0, 0)),
            pl.BlockSpec((D_MODEL, PROJ_TN), lambda i, j: (0, j)),
        ],
        out_specs=pl.BlockSpec((PROJ_TM, PROJ_TN), lambda i, j: (i, j)),
        out_shape=jax.ShapeDtypeStruct((SEQ, D_IN_PROJ), jnp.float32),
        scratch_shapes=[pltpu.VMEM((PROJ_TM, D_MODEL), jnp.bfloat16)],
        compiler_params=_params("arbitrary", "arbitrary"),
        name="in_proj",
    )(x, scale1, shift1, w_in_bf16)


def _gelu(x):
    return 0.5 * x * (1.0 + lax.erf(x * (1.0 / math.sqrt(2.0))))


def _gmlp_kernel(ua_ref, va_ref, g_ref, b_ref, w_ref, bs_ref, o_ref):
    row = lax.broadcasted_iota(jnp.int32, (CHUNK, CHUNK), 0)
    col = lax.broadcasted_iota(jnp.int32, (CHUNK, CHUNK), 1)
    causal = col <= row
    chunks = [slice(c * CHUNK, (c + 1) * CHUNK) for c in range(GMLP_TM // CHUNK)]
    for g in range(GMLP_GROUPS):
        sl = slice(g * GROUP_DIM, (g + 1) * GROUP_DIM)
        v = [_layer_norm(_gelu(va_ref[rows, sl]), g_ref[g:g + 1, :], b_ref[g:g + 1, :]).astype(jnp.bfloat16)
             for rows in chunks]
        w = jnp.where(causal, w_ref[g], 0.0).astype(jnp.bfloat16)
        s = jnp.dot(w, jnp.concatenate(v, axis=1), preferred_element_type=jnp.float32)
        for c, rows in enumerate(chunks):
            s_c = s[:, c * GROUP_DIM:(c + 1) * GROUP_DIM] + bs_ref[:, g:g + 1]
            o_ref[rows, sl] = (_gelu(ua_ref[rows, sl]) * s_c).astype(o_ref.dtype)


def _gmlp(proj, ln_g, ln_b, w_spatial, b_spatial):
    return pl.pallas_call(
        _gmlp_kernel,
        grid=(SEQ // GMLP_TM,),
        in_specs=[
            pl.BlockSpec((GMLP_TM, D_GMLP), lambda n: (n, 0)),
            pl.BlockSpec((GMLP_TM, D_GMLP), lambda n: (n, 1)),
            pl.BlockSpec((GMLP_GROUPS, GROUP_DIM), lambda n: (0, 0)),
            pl.BlockSpec((GMLP_GROUPS, GROUP_DIM), lambda n: (0, 0)),
            pl.BlockSpec((GMLP_GROUPS, CHUNK, CHUNK), lambda n: (0, 0, 0)),
            pl.BlockSpec((CHUNK, GMLP_GROUPS), lambda n: (0, 0)),
        ],
        out_specs=pl.BlockSpec((GMLP_TM, D_GMLP), lambda n: (n, 0)),
        out_shape=jax.ShapeDtypeStruct((SEQ, D_GMLP), jnp.bfloat16),
        compiler_params=_params("arbitrary"),
        name="gmlp",
    )(proj, proj, ln_g, ln_b, w_spatial, b_spatial.T)


def _attn_kernel(q_ref, k_ref, v_ref, o_ref, out_ref, lse_ref, kv_ref):
    head = pl.program_id(0)
    span = pl.program_id(1)
    log2e = 1.0 / math.log(2.0)
    head_no = (jnp.zeros((BLK, 2 * BLK), jnp.int32) + (head + 1)).astype(jnp.float32)
    slope = jnp.exp2(head_no * (-8.0 / N_HEADS)) * log2e
    scale = HEAD_DIM ** -0.5 * log2e
    qi = lax.broadcasted_iota(jnp.int32, (BLK, 2 * BLK), 0)
    ki = lax.broadcasted_iota(jnp.int32, (BLK, 2 * BLK), 1)
    step = qi + BLK - ki
    contract_last = (((1,), (1,)), ((), ()))
    ones = jnp.ones((2 * BLK, LANES), jnp.bfloat16)

    for b, (window, d) in enumerate(BRANCHES):
        assert window // d == BLK
        valid = (step >= 0) & (step <= BLK)
        bias = jnp.where(valid, -slope * d * step.astype(jnp.float32), NEG)
        bias_first = jnp.where(ki >= BLK, bias, NEG)

        per_class = d * BLK == SPAN
        cur = span % 2
        if per_class:
            def stash(r, carry, d=d):
                rows = pl.ds(pl.multiple_of(r * BLK, BLK), BLK)
                kv_ref[0, cur, rows, :] = k_ref[pl.ds(span * SPAN + r, BLK, stride=d), :].astype(jnp.bfloat16)
                kv_ref[1, cur, rows, :] = v_ref[pl.ds(span * SPAN + r, BLK, stride=d), :].astype(jnp.bfloat16)
                return carry
            lax.fori_loop(0, d, stash, 0, unroll=4)

        def tile(t, carry, d=d, b=b, bias=bias, bias_first=bias_first, per_class=per_class):
            r = t % d
            n = t // d
            q0 = n * (BLK * d) + r
            k0 = span * SPAN + q0
            first = k0 < BLK * d
            q = (q_ref[pl.ds(q0, BLK, stride=d), :] * scale).astype(jnp.bfloat16)
            if per_class:
                rows = pl.ds(pl.multiple_of(r * BLK, BLK), BLK)
                prev = jnp.where(first, cur, 1 - cur)
                k = jnp.concatenate([kv_ref[0, prev, rows, :], kv_ref[0, cur, rows, :]], axis=0)
                v = jnp.concatenate([kv_ref[1, prev, rows, :], kv_ref[1, cur, rows, :]], axis=0)
            else:
                kp = jnp.where(first, k0, k0 - BLK * d)
                k = jnp.concatenate([k_ref[pl.ds(kp, BLK, stride=d), :], k_ref[pl.ds(k0, BLK, stride=d), :]],
                                    axis=0).astype(jnp.bfloat16)
                v = jnp.concatenate([v_ref[pl.ds(kp, BLK, stride=d), :], v_ref[pl.ds(k0, BLK, stride=d), :]],
                                    axis=0).astype(jnp.bfloat16)
            s = lax.dot_general(q, k, contract_last, preferred_element_type=jnp.float32)
            s = s + jnp.where(first, bias_first, bias)
            m = jnp.max(jnp.maximum(s[:, :BLK], s[:, BLK:]), axis=-1, keepdims=True)
            p = jnp.exp2(s - m).astype(jnp.bfloat16)
            v_one = jnp.concatenate([v, ones], axis=1)
            pv = jnp.dot(p, v_one, preferred_element_type=jnp.float32)
            den = pv[:, HEAD_DIM:]
            rows = pl.ds(q0, BLK, stride=d)
            out_ref[b, rows, :] = pv[:, :HEAD_DIM] / den
            lse_ref[b, rows, :] = m + jnp.log2(den)
            return carry

        lax.fori_loop(0, SPAN // BLK, tile, 0, unroll=ATTN_UNROLL)

    lse_all = jnp.maximum(jnp.maximum(lse_ref[0], lse_ref[1]), lse_ref[2])
    num = jnp.zeros((SPAN, HEAD_DIM), jnp.float32)
    den = jnp.zeros((SPAN, LANES), jnp.float32)
    for b in range(len(BRANCHES)):
        w = jnp.exp2(lse_ref[b] - lse_all)
        num = num + w * out_ref[b]
        den = den + w
    o_ref[...] = (num / den).astype(o_ref.dtype)


def _attention(proj):
    q_col = 2 * D_GMLP // HEAD_DIM
    k_col = q_col + N_HEADS
    v_col = k_col + N_HEADS
    nb = len(BRANCHES)
    return pl.pallas_call(
        _attn_kernel,
        grid=(N_HEADS, SEQ // SPAN),
        in_specs=[
            pl.BlockSpec((SPAN, HEAD_DIM), lambda h, s: (s, q_col + h)),
            pl.BlockSpec((SEQ, HEAD_DIM), lambda h, s: (0, k_col + h)),
            pl.BlockSpec((SEQ, HEAD_DIM), lambda h, s: (0, v_col + h)),
        ],
        out_specs=pl.BlockSpec((SPAN, HEAD_DIM), lambda h, s: (s, h)),
        out_shape=jax.ShapeDtypeStruct((SEQ, D_ATTN), jnp.bfloat16),
        scratch_shapes=[
            pltpu.VMEM((nb, SPAN, HEAD_DIM), jnp.float32),
            pltpu.VMEM((nb, SPAN, LANES), jnp.float32),
            pltpu.VMEM((2, 2, SPAN, HEAD_DIM), jnp.bfloat16),
        ],
        compiler_params=_params("arbitrary", "arbitrary"),
        name="attn",
    )(proj, proj, proj)


def _split_bf16(x):
    hi = x.astype(jnp.bfloat16)
    lo = (x - hi.astype(jnp.float32)).astype(jnp.bfloat16)
    return hi, lo


def _out_proj_kernel(ya_ref, yb_ref, wa_ref, wb_ref, x_ref, gate_ref, g_ref, b_ref, sc_ref, sh_ref,
                     wr_ref, br_ref, x1_ref, hp_ref, lg_ref):
    mix = jnp.dot(ya_ref[...], wa_ref[...], preferred_element_type=jnp.float32)
    mix = mix + jnp.dot(yb_ref[...], wb_ref[...], preferred_element_type=jnp.float32)
    x1 = _layer_norm(DEEPNORM_ALPHA * x_ref[...] + gate_ref[...] * mix, g_ref[...], b_ref[...])
    x1_ref[...] = x1
    h = x1 * (1.0 + sc_ref[...]) + sh_ref[...]
    _store_packed(hp_ref, (), 0, _pack_rows(h))
    h_hi, h_lo = _split_bf16(h)
    w_hi, w_lo = _split_bf16(wr_ref[...])
    both = jnp.dot(h_hi, jnp.concatenate([w_hi, w_lo], axis=1), preferred_element_type=jnp.float32)
    lg = both[:, :N_EXPERTS] + both[:, N_EXPERTS:] + jnp.dot(h_lo, w_hi, preferred_element_type=jnp.float32)
    lg_ref[...] = lg + br_ref[...]


def _out_proj(y_a, y_b, w_o_bf16, x, gate1, ln_g, ln_b, scale2, shift2, w_router, b_router):
    row = lambda i: (i, 0)
    fixed = lambda i: (0, 0)
    vec = pl.BlockSpec((1, D_MODEL), fixed)
    return pl.pallas_call(
        _out_proj_kernel,
        grid=(SEQ // OUT_TM,),
        in_specs=[
            pl.BlockSpec((OUT_TM, D_GMLP), row),
            pl.BlockSpec((OUT_TM, D_ATTN), row),
            pl.BlockSpec((D_GMLP, D_MODEL), lambda i: (0, 0)),
            pl.BlockSpec((D_ATTN, D_MODEL), lambda i: (1, 0)),
            pl.BlockSpec((OUT_TM, D_MODEL), row),
            vec, vec, vec, vec, vec,
            pl.BlockSpec((D_MODEL, N_EXPERTS), fixed),
            pl.BlockSpec((1, N_EXPERTS), fixed),
        ],
        out_specs=[
            pl.BlockSpec((OUT_TM, D_MODEL), row),
            pl.BlockSpec((OUT_TM * ROW_SUB, LANES), row),
            pl.BlockSpec((OUT_TM, N_EXPERTS), row),
        ],
        out_shape=[
            jax.ShapeDtypeStruct((SEQ, D_MODEL), jnp.float32),
            jax.ShapeDtypeStruct((SEQ * ROW_SUB, LANES), jnp.uint32),
            jax.ShapeDtypeStruct((SEQ, N_EXPERTS), jnp.float32),
        ],
        compiler_params=_params("arbitrary"),
        name="out_proj",
    )(y_a, y_b, w_o_bf16, w_o_bf16, x, gate1, ln_g, ln_b, scale2, shift2, w_router, b_router)


def _experts_kernel(meta_ref, dstp_ref, x_hbm, wg_ref, wl_ref, bg_ref, bl_ref, w2_ref,
                    b2_ref, y_hbm, xs_ref, os_ref, acc_ref, gsem, ssem):
    i = pl.program_id(0)
    j = pl.program_id(1)
    n_items = pl.num_programs(0)
    last_j = pl.num_programs(1) - 1

    def item_rows(k):
        inside = (k >= 0) & (k < n_items)
        return jnp.where(inside, meta_ref[MOE_ITEMS + jnp.clip(k, 0, n_items - 1)], 0)

    rows = item_rows(i)
    rows_prev = item_rows(i - 1)
    p = i % 2
    q = 1 - p
    group_sub = MOE_G * ROW_SUB

    def load_rows(k, slot):
        first = meta_ref[2 * MOE_ITEMS + jnp.clip(k, 0, n_items - 1)] * ROW_SUB

        def load_group(t, c):
            src = pl.multiple_of(first + t * group_sub, group_sub)
            dst = pl.multiple_of(t * group_sub, group_sub)
            pltpu.make_async_copy(x_hbm.at[pl.ds(src, group_sub), :], xs_ref.at[slot, pl.ds(dst, group_sub), :],
                                  gsem.at[slot]).start()
            return c
        lax.fori_loop(0, item_rows(k) // MOE_G, load_group, 0)

    def scatter(slot, row, queue=1):
        entry = dstp_ref[0, 0, row]
        src = pl.multiple_of(row * ROW_SUB, ROW_SUB)
        dst = pl.multiple_of(entry, ROW_SUB)
        pltpu.make_async_copy(os_ref.at[slot, pl.ds(src, ROW_SUB), :], y_hbm.at[pl.ds(dst, ROW_SUB), :],
                              ssem.at[slot]).start(priority=queue)
        return entry

    def wait_rows(buf_ref, sem, slot, n):
        group = buf_ref.at[slot, pl.ds(0, group_sub), :]

        def wait_group(t, c):
            pltpu.make_async_copy(group, group, sem.at[slot]).wait()
            return c
        lax.fori_loop(0, n // MOE_G, wait_group, 0)

    def each_row(lo, hi, fn):
        lax.fori_loop(lo, jnp.maximum(lo, hi), lambda r, c: (fn(r), c)[1], 0)

    @pl.when((i == 0) & (j == 0))
    def _():
        acc_ref[...] = jnp.zeros_like(acc_ref)
        os_ref[...] = jnp.zeros_like(os_ref)
        for half in range(2):
            fill = pltpu.make_async_copy(
                os_ref.at[0],
                y_hbm.at[pl.ds((SEQ * TOP_K + half * MOE_RMAX) * ROW_SUB, MOE_RMAX * ROW_SUB), :],
                ssem.at[0])
            fill.start()
            fill.wait()
        load_rows(0, 0)

    @pl.when(j == 0)
    def _():
        wait_rows(xs_ref, gsem, p, rows)
        load_rows(i + 1, q)

    def sub_tile(a, m):
        a = pl.multiple_of(a, MOE_G)
        base = (j * rows + a) // MOE_CHUNKS
        lag = 0
        for u in range(m // MOE_CHUNKS):
            lag = scatter(0, base + u + lag, queue=u % 2) >> 31
        x_lo, x_hi = _unpack_words(_load_packed(xs_ref, (p,), a, m))
        x = jnp.concatenate([x_lo.astype(jnp.bfloat16), x_hi.astype(jnp.bfloat16)], axis=1)

        def up(w_ref, b_ref):
            return jnp.dot(x, w_ref[...].astype(jnp.bfloat16), preferred_element_type=jnp.float32) + b_ref[...]

        glu = jnp.minimum(up(wg_ref, bg_ref), SWIGLU_LIMIT)
        lin = jnp.clip(up(wl_ref, bl_ref), -SWIGLU_LIMIT, SWIGLU_LIMIT)
        act = glu * jax.nn.sigmoid(SWIGLU_ALPHA * glu) * (lin + 1.0)
        down = jnp.dot(act.astype(jnp.bfloat16), w2_ref[...].astype(jnp.bfloat16),
                       preferred_element_type=jnp.float32)
        start = jnp.where(j == 0, jnp.broadcast_to(b2_ref[...], (m, D_MODEL)), acc_ref[pl.ds(a, m), :])
        acc_ref[pl.ds(a, m), :] = start + down

    @pl.when(rows > 0)
    def _():
        main = MOE_SUBTILES[0]
        n_main = rows // main
        lax.fori_loop(0, n_main, lambda t, c: (sub_tile(t * main, main), c)[1], 0)
        done = n_main * main
        for m in MOE_SUBTILES[1:]:
            has = ((rows - done) // m) % 2 == 1
            pl.when(has)(lambda done=done, m=m: sub_tile(done, m))
            done = done + jnp.where(has, m, 0)

    @pl.when(j == last_j)
    def _():
        each_row(rows, rows_prev, lambda r: scatter(0, r))
        wait_rows(os_ref, ssem, 0, jnp.maximum(rows, rows_prev))

        def pack(t, c):
            a = pl.multiple_of(t * MOE_G, MOE_G)
            _store_packed(os_ref, (0,), a, _pack_rows(acc_ref[pl.ds(a, MOE_G), :]))
            return c
        lax.fori_loop(0, rows // MOE_G, pack, 0)


def _experts(x_sorted, meta, dst_tab, w1, b1, w2, b2):
    n_j = MOE_CHUNKS

    def col(i, j, m):
        return jnp.where(m[MOE_ITEMS + i] > 0, j, n_j - 1)

    def expert(i, m):
        return m[i]

    tab = lambda f: pl.BlockSpec((1, 1, MOE_RMAX), f, memory_space=pltpu.SMEM)
    return pl.pallas_call(
        _experts_kernel,
        grid_spec=pltpu.PrefetchScalarGridSpec(
            num_scalar_prefetch=1,
            grid=(MOE_ITEMS, n_j),
            in_specs=[
                tab(lambda i, j, m: (i, 0, 0)),
                pl.BlockSpec(memory_space=pl.ANY),
                pl.BlockSpec((None, D_MODEL, MOE_TN), lambda i, j, m: (expert(i, m), 0, col(i, j, m))),
                pl.BlockSpec((None, D_MODEL, MOE_TN), lambda i, j, m: (expert(i, m), 0, n_j + col(i, j, m))),
                pl.BlockSpec((None, 1, MOE_TN), lambda i, j, m: (expert(i, m), 0, col(i, j, m))),
                pl.BlockSpec((None, 1, MOE_TN), lambda i, j, m: (expert(i, m), 0, n_j + col(i, j, m))),
                pl.BlockSpec((None, MOE_TN, D_MODEL), lambda i, j, m: (expert(i, m), col(i, j, m), 0)),
                pl.BlockSpec((None, 1, D_MODEL), lambda i, j, m: (expert(i, m), 0, 0)),
            ],
            out_specs=pl.BlockSpec(memory_space=pl.ANY),
            scratch_shapes=[
                pltpu.VMEM((2, MOE_RMAX * ROW_SUB, LANES), jnp.uint32),
                pltpu.VMEM((1, MOE_RMAX * ROW_SUB, LANES), jnp.uint32),
                pltpu.VMEM((MOE_RMAX, D_MODEL), jnp.float32),
                pltpu.SemaphoreType.DMA((2,)),
                pltpu.SemaphoreType.DMA((1,)),
            ],
        ),
        out_shape=jax.ShapeDtypeStruct((Y_ROWS * ROW_SUB, LANES), jnp.uint32),
        compiler_params=_params("arbitrary", "arbitrary"),
        name="experts",
    )(meta, dst_tab, x_sorted, w1, w1,
      b1.reshape(N_EXPERTS, 1, 2 * D_EXPERT), b1.reshape(N_EXPERTS, 1, 2 * D_EXPERT),
      w2, b2.reshape(N_EXPERTS, 1, D_MODEL))


def _combine_kernel(y0_ref, y1_ref, y2_ref, y3_ref, gates_ref, x1_ref, gate2_ref, g_ref, b_ref, o_ref):
    y_lo = jnp.zeros((COMBINE_TM, D_PACK), jnp.float32)
    y_hi = jnp.zeros((COMBINE_TM, D_PACK), jnp.float32)
    for k, y_ref in enumerate((y0_ref, y1_ref, y2_ref, y3_ref)):
        lo, hi = _unpack_words(_load_packed(y_ref, (), 0, COMBINE_TM))
        gate = gates_ref[:, k:k + 1]
        y_lo = y_lo + gate * lo
        y_hi = y_hi + gate * hi
    y = jnp.concatenate([y_lo, y_hi], axis=1)
    o_ref[...] = _layer_norm(DEEPNORM_ALPHA * x1_ref[...] + gate2_ref[...] * y, g_ref[...], b_ref[...])


def _combine(y_packed, gates, x1, gate2, ln_g, ln_b):
    n_tiles = SEQ // COMBINE_TM
    row = lambda i: (i, 0)
    vec = pl.BlockSpec((1, D_MODEL), lambda i: (0, 0))
    slot = lambda k: pl.BlockSpec((COMBINE_TM * ROW_SUB, LANES), lambda i: (k * n_tiles + i, 0))
    return pl.pallas_call(
        _combine_kernel,
        grid=(n_tiles,),
        in_specs=[
            slot(0), slot(1), slot(2), slot(3),
            pl.BlockSpec((COMBINE_TM, TOP_K), row),
            pl.BlockSpec((COMBINE_TM, D_MODEL), row),
            vec, vec, vec,
        ],
        out_specs=pl.BlockSpec((COMBINE_TM, D_MODEL), row),
        out_shape=jax.ShapeDtypeStruct((SEQ, D_MODEL), jnp.float32),
        compiler_params=_params("arbitrary"),
        name="combine",
    )(y_packed, y_packed, y_packed, y_packed, gates, x1, gate2, ln_g, ln_b)


TAB_ROWS = MOE_RMAX // LANES
WINDOW_ROWS = 2 * ROW_SUB
assert WINDOW_ROWS > TAB_ROWS and TOP_K == 4


def _row_tables_kernel(win_ref, order_ref, dst_ref):
    lane = lax.broadcasted_iota(jnp.int32, (TAB_ROWS, LANES), 1)
    sub = lax.broadcasted_iota(jnp.int32, (TAB_ROWS, LANES), 0)
    r = sub * LANES + lane

    def table(b, carry):
        w0 = win_ref[b]
        n_real = win_ref[MOE_ITEMS + 1 + b]
        off = w0 % LANES
        x = order_ref[pl.ds(w0 // LANES, WINDOW_ROWS), :]
        x = pltpu.roll(x, (LANES - off) % LANES, axis=1)
        flat = jnp.where(lane < LANES - off, x[:TAB_ROWS], x[1:TAB_ROWS + 1])
        tok = flat >> 2
        slot = flat & 3
        spill = SEQ * TOP_K + ((b + 1) % 2) * MOE_RMAX + r
        dst_ref[b] = jnp.where(r < n_real, slot * SEQ + tok, spill) * ROW_SUB
        return carry
    lax.fori_loop(0, MOE_ITEMS + 1, table, 0)


def _row_tables(win, order):
    n_rows = SEQ * TOP_K // LANES
    order2d = jnp.concatenate([order, jnp.zeros((WINDOW_ROWS * LANES,), jnp.int32)]).reshape(-1, LANES)
    return pl.pallas_call(
        _row_tables_kernel,
        grid_spec=pltpu.PrefetchScalarGridSpec(
            num_scalar_prefetch=1,
            grid=(1,),
            in_specs=[pl.BlockSpec((n_rows + WINDOW_ROWS, LANES), lambda b, w: (0, 0))],
            out_specs=pl.BlockSpec((MOE_ITEMS + 1, TAB_ROWS, LANES), lambda b, w: (0, 0, 0)),
        ),
        out_shape=jax.ShapeDtypeStruct((MOE_ITEMS + 1, TAB_ROWS, LANES), jnp.int32),
        compiler_params=_params("arbitrary"),
        name="row_tables",
    )(win, order2d)


def _dispatch_kernel(fill_ref, pos_ref, h_ref, x_hbm, zero_ref, buf_ref, sem, zsem):
    i = pl.program_id(0)
    group_sub = MOE_G * ROW_SUB

    @pl.when(i == 0)
    def _():
        zero_ref[...] = jnp.zeros_like(zero_ref)

        def fill(g):
            dst = pl.multiple_of(g * group_sub, group_sub)
            pltpu.make_async_copy(zero_ref, x_hbm.at[pl.ds(dst, group_sub), :], zsem).start()

        for e in range(N_EXPERTS):
            end = fill_ref[e]
            begin = fill_ref[e - 1] if e else 0
            pl.when(end > begin)(lambda end=end: fill(end - 1))
        lax.fori_loop(fill_ref[N_EXPERTS - 1], X_ROWS // MOE_G, lambda g, c: (fill(g), c)[1], 0)

        def wait_fill(t, c):
            pltpu.make_async_copy(zero_ref, zero_ref, zsem).wait()
            return c
        lax.fori_loop(0, fill_ref[N_EXPERTS], wait_fill, 0)

    slot = i % 2
    buf_ref[slot] = h_ref[...]

    def token(t, c):
        src = pl.multiple_of(t * ROW_SUB, ROW_SUB)
        for k in range(TOP_K):
            dst = pl.multiple_of(pos_ref[0, 0, t * TOP_K + k], ROW_SUB)
            pltpu.make_async_copy(buf_ref.at[slot, pl.ds(src, ROW_SUB), :], x_hbm.at[pl.ds(dst, ROW_SUB), :],
                                  sem.at[slot]).start(priority=k % 2)
        return c
    lax.fori_loop(0, DISPATCH_TM, token, 0, unroll=4)

    def wait_tile(s):
        for k in range(TOP_K):
            pltpu.make_async_copy(buf_ref.at[s], buf_ref.at[s], sem.at[s]).wait()

    pl.when(i > 0)(lambda: wait_tile(1 - slot))
    pl.when(i == pl.num_programs(0) - 1)(lambda: wait_tile(slot))


def _dispatch(h_packed, pos, fill):
    n_tiles = SEQ // DISPATCH_TM
    return pl.pallas_call(
        _dispatch_kernel,
        grid_spec=pltpu.PrefetchScalarGridSpec(
            num_scalar_prefetch=1,
            grid=(n_tiles,),
            in_specs=[
                pl.BlockSpec((1, 1, DISPATCH_TM * TOP_K), lambda i, f: (i, 0, 0), memory_space=pltpu.SMEM),
                pl.BlockSpec((DISPATCH_TM * ROW_SUB, LANES), lambda i, f: (i, 0)),
            ],
            out_specs=pl.BlockSpec(memory_space=pl.ANY),
            scratch_shapes=[
                pltpu.VMEM((MOE_G * ROW_SUB, LANES), jnp.uint32),
                pltpu.VMEM((2, DISPATCH_TM * ROW_SUB, LANES), jnp.uint32),
                pltpu.SemaphoreType.DMA((2,)),
                pltpu.SemaphoreType.DMA(()),
            ],
        ),
        out_shape=jax.ShapeDtypeStruct((X_ROWS * ROW_SUB, LANES), jnp.uint32),
        compiler_params=_params("arbitrary"),
        name="dispatch",
    )(fill, pos.reshape(n_tiles, 1, DISPATCH_TM * TOP_K), h_packed)


def _route(logits):
    top_val, top_idx = lax.top_k(logits, TOP_K)
    gates = jax.nn.softmax(top_val, axis=-1)
    e_flat = top_idx.reshape(-1).astype(jnp.int32)
    experts = jnp.arange(N_EXPERTS, dtype=jnp.int32)
    counts = jnp.sum((e_flat[:, None] == experts[None, :]).astype(jnp.int32), axis=0)
    groups = (counts + MOE_G - 1) // MOE_G
    group_end = jnp.cumsum(groups)
    row0 = (group_end - groups) * MOE_G
    per_item = MOE_RMAX // MOE_G
    n_items_e = (groups + per_item - 1) // per_item
    item_end = jnp.cumsum(n_items_e)
    item_start = item_end - n_items_e
    n_items = item_end[-1]

    item = jnp.arange(MOE_ITEMS, dtype=jnp.int32)
    used = item < n_items
    e_item = jnp.minimum(jnp.searchsorted(item_end, item, side='right'), N_EXPERTS - 1).astype(jnp.int32)
    e_last = e_item[jnp.maximum(n_items - 1, 0)]
    part = item - item_start[e_item]
    rows = jnp.where(used, jnp.clip(groups[e_item] - part * per_item, 0, per_item) * MOE_G, 0)
    first_row = jnp.where(used, row0[e_item] + part * MOE_RMAX, 0)
    meta = jnp.concatenate([jnp.where(used, e_item, e_last), rows, first_row]).astype(jnp.int32)

    order = jnp.argsort(e_flat, stable=True).astype(jnp.int32)
    rank = jnp.argsort(order).astype(jnp.int32)
    start = jnp.cumsum(counts) - counts
    shift = jnp.sum(jnp.where(e_flat[:, None] == experts[None, :], (row0 - start)[None, :], 0), axis=1)
    pos = (rank + shift) * ROW_SUB
    n_fill = jnp.sum((groups > 0).astype(jnp.int32)) + X_ROWS // MOE_G - group_end[-1]
    fill = jnp.concatenate([group_end, n_fill[None]]).astype(jnp.int32)

    window0 = jnp.where(used, start[e_item] + part * MOE_RMAX, 0)
    n_real = jnp.where(used, jnp.clip(counts[e_item] - part * MOE_RMAX, 0, MOE_RMAX), 0)
    zero = jnp.zeros((1,), jnp.int32)
    win = jnp.concatenate([zero, window0, zero, n_real]).astype(jnp.int32)
    dst_tab = _row_tables(win, order).reshape(MOE_ITEMS + 1, 1, MOE_RMAX)
    return gates, meta, pos.astype(jnp.int32), fill, dst_tab


def kernel(x, c, w_ada, b_ada, w_in, sgu_ln_g, sgu_ln_b, w_spatial, b_spatial, w_o, ln1_g, ln1_b,
           w_router, b_router, w_exp1, b_exp1, w_exp2, b_exp2, ln2_g, ln2_b):
    depth = w_ada.shape[0]
    assert x.shape == (1, SEQ, D_MODEL)
    xs = x.reshape(SEQ, D_MODEL)
    for l in range(depth):
        ada = _ada(c, w_ada[l], b_ada[l])
        shift1, scale1, gate1, shift2, scale2, gate2 = jnp.split(ada, 6, axis=-1)

        proj = _in_proj(xs, scale1, shift1, w_in[l].astype(jnp.bfloat16))
        y_a = _gmlp(proj, sgu_ln_g[l], sgu_ln_b[l], w_spatial[l], b_spatial[l])
        y_b = _attention(proj)
        x1, h_packed, logits = _out_proj(y_a, y_b, w_o[l].astype(jnp.bfloat16), xs, gate1,
                                         ln1_g[l].reshape(1, -1), ln1_b[l].reshape(1, -1), scale2, shift2,
                                         w_router[l], b_router[l].reshape(1, -1))

        gates, meta, pos, fill, dst_tab = _route(logits)
        x_sorted = _dispatch(h_packed, pos, fill)
        y_packed = _experts(x_sorted, meta, dst_tab, w_exp1[l], b_exp1[l], w_exp2[l], b_exp2[l])
        xs = _combine(y_packed, gates, x1, gate2, ln2_g[l].reshape(1, -1), ln2_b[l].reshape(1, -1))
    return xs.reshape(x.shape)
```

```python
import math

import jax
import jax.numpy as jnp
from jax import lax
from jax.experimental import pallas as pl
from jax.experimental.pallas import tpu as pltpu

D_MODEL = 2048
SEQ = 8192
D_GMLP = 1024
GMLP_GROUPS = 8
GROUP_DIM = 128
CHUNK = 128
D_ATTN = 1024
HEAD_DIM = 128
N_HEADS = 8
BRANCHES = ((128, 1), (512, 4), (2048, 16))
BLK = 128
SPAN = 16 * BLK
D_IN_PROJ = 2 * D_GMLP + 3 * D_ATTN
N_EXPERTS = 32
TOP_K = 4
D_EXPERT = 2048
SWIGLU_LIMIT = 7.0
SWIGLU_ALPHA = 1.702
LN_EPS = 1e-5
DEEPNORM_ALPHA = 2.0 ** 0.25
NEG = -1e30

LANES = 128
VMEM_LIMIT = 56 * 1024 * 1024

ADA_TN = 1536
PROJ_TM = 1024
PROJ_TN = 1024
OUT_TM = 512
ATTN_UNROLL = 8
ROW_SUB = 8
D_PACK = D_MODEL // 2
MOE_G = 128
MOE_RMAX = 1152
MOE_ITEMS = 64
MOE_TN = 512
MOE_CHUNKS = D_EXPERT // MOE_TN
MOE_SUBTILES = (512, 256, 128)
X_ROWS = SEQ * TOP_K + N_EXPERTS * MOE_G
Y_ROWS = SEQ * TOP_K + 2 * MOE_RMAX
DISPATCH_TM = 256
COMBINE_TM = 256


def _params(*sem):
    return pltpu.CompilerParams(dimension_semantics=sem, vmem_limit_bytes=VMEM_LIMIT)


def _layer_norm(x, g, b):
    mu = jnp.mean(x, axis=-1, keepdims=True)
    xc = x - mu
    var = jnp.mean(xc * xc, axis=-1, keepdims=True)
    return xc * lax.rsqrt(var + LN_EPS) * g + b


def _pack_rows(x):
    r = x.astype(jnp.bfloat16).astype(jnp.float32)
    bits = lax.bitcast_convert_type(r, jnp.uint32)
    return (bits[:, D_PACK:] & jnp.uint32(0xFFFF0000)) | (bits[:, :D_PACK] >> 16)


def _unpack_words(u):
    lo = lax.bitcast_convert_type(u << 16, jnp.float32)
    hi = lax.bitcast_convert_type(u & jnp.uint32(0xFFFF0000), jnp.float32)
    return lo, hi


def _store_packed(ref, lead, row0, packed):
    m = packed.shape[0]
    for c in range(ROW_SUB):
        rows = pl.ds(row0 * ROW_SUB + c, m, stride=ROW_SUB)
        ref[lead + (rows, slice(None))] = packed[:, c * LANES:(c + 1) * LANES]


def _load_packed(ref, lead, row0, m):
    cols = [ref[lead + (pl.ds(row0 * ROW_SUB + c, m, stride=ROW_SUB), slice(None))] for c in range(ROW_SUB)]
    return jnp.concatenate(cols, axis=1)


def _ada_kernel(c_ref, w_ref, b_ref, o_ref):
    c = c_ref[...]
    s = c * jax.nn.sigmoid(c)
    o_ref[...] = jnp.sum(s * w_ref[...], axis=0, keepdims=True) + b_ref[...]


def _ada(c, w_ada, b_ada):
    n = w_ada.shape[1]
    return pl.pallas_call(
        _ada_kernel,
        grid=(n // ADA_TN,),
        in_specs=[
            pl.BlockSpec((D_MODEL, 1), lambda j: (0, 0)),
            pl.BlockSpec((D_MODEL, ADA_TN), lambda j: (0, j)),
            pl.BlockSpec((1, ADA_TN), lambda j: (0, j)),
        ],
        out_specs=pl.BlockSpec((1, ADA_TN), lambda j: (0, j)),
        out_shape=jax.ShapeDtypeStruct((1, n), jnp.float32),
        compiler_params=_params("arbitrary"),
        name="ada",
    )(c.reshape(D_MODEL, 1), w_ada, b_ada.reshape(1, n))


def _gelu(x):
    return 0.5 * x * (1.0 + lax.erf(x * (1.0 / math.sqrt(2.0))))


GROUPS_PER_STEP = 3


def _in_proj_kernel(x_ref, sc_ref, sh_ref, w_ref, g_ref, b_ref, ws_ref, bs_ref, qkv_ref, ya_ref, h_ref, uv_ref):
    i = pl.program_id(0)
    j = pl.program_id(1)

    @pl.when((i == 0) & (j == 0))
    def _():
        uv_ref[...] = jnp.zeros_like(uv_ref)

    @pl.when(j == 0)
    def _():
        h_ref[...] = (x_ref[...] * (1.0 + sc_ref[...]) + sh_ref[...]).astype(jnp.bfloat16)

    row = lax.broadcasted_iota(jnp.int32, (CHUNK, CHUNK), 0)
    col = lax.broadcasted_iota(jnp.int32, (CHUNK, CHUNK), 1)
    causal = col <= row
    chunks = [slice(c * CHUNK, (c + 1) * CHUNK) for c in range(PROJ_TM // CHUNK)]
    for u in range(GROUPS_PER_STEP):
        g = jnp.clip(GROUPS_PER_STEP * (j - 2) + u, 0, GMLP_GROUPS - 1)
        lanes = pl.ds(pl.multiple_of(g * GROUP_DIM, GROUP_DIM), GROUP_DIM)
        ln_g = g_ref[pl.ds(g, 1), :]
        ln_b = b_ref[pl.ds(g, 1), :]
        v = [_layer_norm(_gelu(uv_ref[1, rows, lanes]), ln_g, ln_b).astype(jnp.bfloat16) for rows in chunks]
        w = jnp.where(causal, ws_ref[g], 0.0).astype(jnp.bfloat16)
        s = jnp.dot(w, jnp.concatenate(v, axis=1), preferred_element_type=jnp.float32)
        for c, rows in enumerate(chunks):
            s_c = s[:, c * GROUP_DIM:(c + 1) * GROUP_DIM] + bs_ref[g]
            ya_ref[rows, lanes] = (_gelu(uv_ref[0, rows, lanes]) * s_c).astype(ya_ref.dtype)

    qkv_ref[...] = jnp.dot(h_ref[...], w_ref[...], preferred_element_type=jnp.float32)

    @pl.when(j < 2)
    def _():
        uv_ref[jnp.minimum(j, 1)] = qkv_ref[...]


def _in_proj(x, scale1, shift1, w_in_bf16, ln_g, ln_b, w_spatial, b_spatial):
    n_uv = 2 * D_GMLP // PROJ_TN
    assert GROUPS_PER_STEP * (D_IN_PROJ // PROJ_TN - n_uv) >= GMLP_GROUPS
    fixed2 = lambda i, j: (0, 0)
    bs_lanes = jnp.broadcast_to(b_spatial[:, :, None], (GMLP_GROUPS, CHUNK, LANES))
    return pl.pallas_call(
        _in_proj_kernel,
        grid=(SEQ // PROJ_TM, D_IN_PROJ // PROJ_TN),
        in_specs=[
            pl.BlockSpec((PROJ_TM, D_MODEL), lambda i, j: (i, 0)),
            pl.BlockSpec((1, D_MODEL), fixed2),
            pl.BlockSpec((1, D_MODEL), fixed2),
            pl.BlockSpec((D_MODEL, PROJ_TN), lambda i, j: (0, j)),
            pl.BlockSpec((GMLP_GROUPS, GROUP_DIM), fixed2),
            pl.BlockSpec((GMLP_GROUPS, GROUP_DIM), fixed2),
            pl.BlockSpec((GMLP_GROUPS, CHUNK, CHUNK), lambda i, j: (0, 0, 0)),
            pl.BlockSpec((GMLP_GROUPS, CHUNK, LANES), lambda i, j: (0, 0, 0)),
        ],
        out_specs=[
            pl.BlockSpec((PROJ_TM, PROJ_TN), lambda i, j: (i, jnp.maximum(j - n_uv, 0))),
            pl.BlockSpec((PROJ_TM, D_GMLP), lambda i, j: (i, 0)),
        ],
        out_shape=[
            jax.ShapeDtypeStruct((SEQ, 3 * D_ATTN), jnp.float32),
            jax.ShapeDtypeStruct((SEQ, D_GMLP), jnp.bfloat16),
        ],
        scratch_shapes=[
            pltpu.VMEM((PROJ_TM, D_MODEL), jnp.bfloat16),
            pltpu.VMEM((2, PROJ_TM, D_GMLP), jnp.float32),
        ],
        compiler_params=_params("arbitrary", "arbitrary"),
        name="in_proj",
    )(x, scale1, shift1, w_in_bf16, ln_g, ln_b, w_spatial, bs_lanes)


def _attn_kernel(q_ref, k_ref, v_ref, o_ref, out_ref, lse_ref, kv_ref):
    head = pl.program_id(0)
    span = pl.program_id(1)
    log2e = 1.0 / math.log(2.0)
    head_no = (jnp.zeros((BLK, 2 * BLK), jnp.int32) + (head + 1)).astype(jnp.float32)
    slope = jnp.exp2(head_no * (-8.0 / N_HEADS)) * log2e
    scale = HEAD_DIM ** -0.5 * log2e
    qi = lax.broadcasted_iota(jnp.int32, (BLK, 2 * BLK), 0)
    ki = lax.broadcasted_iota(jnp.int32, (BLK, 2 * BLK), 1)
    step = qi + BLK - ki
    contract_last = (((1,), (1,)), ((), ()))
    ones = jnp.ones((2 * BLK, LANES), jnp.bfloat16)

    for b, (window, d) in enumerate(BRANCHES):
        assert window // d == BLK
        valid = (step >= 0) & (step <= BLK)
        bias = jnp.where(valid, -slope * d * step.astype(jnp.float32), NEG)
        bias_first = jnp.where(ki >= BLK, bias, NEG)

        per_class = d * BLK == SPAN
        cur = span % 2
        if per_class:
            def stash(r, carry, d=d):
                rows = pl.ds(pl.multiple_of(r * BLK, BLK), BLK)
                kv_ref[0, cur, rows, :] = k_ref[pl.ds(span * SPAN + r, BLK, stride=d), :].astype(jnp.bfloat16)
                kv_ref[1, cur, rows, :] = v_ref[pl.ds(span * SPAN + r, BLK, stride=d), :].astype(jnp.bfloat16)
                return carry
            lax.fori_loop(0, d, stash, 0, unroll=4)

        def tile(t, carry, d=d, b=b, bias=bias, bias_first=bias_first, per_class=per_class):
            r = t % d
            n = t // d
            q0 = n * (BLK * d) + r
            k0 = span * SPAN + q0
            first = k0 < BLK * d
            q = (q_ref[pl.ds(q0, BLK, stride=d), :] * scale).astype(jnp.bfloat16)
            if per_class:
                rows = pl.ds(pl.multiple_of(r * BLK, BLK), BLK)
                prev = jnp.where(first, cur, 1 - cur)
                k = jnp.concatenate([kv_ref[0, prev, rows, :], kv_ref[0, cur, rows, :]], axis=0)
                v = jnp.concatenate([kv_ref[1, prev, rows, :], kv_ref[1, cur, rows, :]], axis=0)
            else:
                kp = jnp.where(first, k0, k0 - BLK * d)
                k = jnp.concatenate([k_ref[pl.ds(kp, BLK, stride=d), :], k_ref[pl.ds(k0, BLK, stride=d), :]],
                                    axis=0).astype(jnp.bfloat16)
                v = jnp.concatenate([v_ref[pl.ds(kp, BLK, stride=d), :], v_ref[pl.ds(k0, BLK, stride=d), :]],
                                    axis=0).astype(jnp.bfloat16)
            s = lax.dot_general(q, k, contract_last, preferred_element_type=jnp.float32)
            s = s + jnp.where(first, bias_first, bias)
            m = jnp.max(jnp.maximum(s[:, :BLK], s[:, BLK:]), axis=-1, keepdims=True)
            p = jnp.exp2(s - m).astype(jnp.bfloat16)
            v_one = jnp.concatenate([v, ones], axis=1)
            pv = jnp.dot(p, v_one, preferred_element_type=jnp.float32)
            den = pv[:, HEAD_DIM:]
            rows = pl.ds(q0, BLK, stride=d)
            out_ref[b, rows, :] = pv[:, :HEAD_DIM] / den
            lse_ref[b, rows, :] = m + jnp.log2(den)
            return carry

        lax.fori_loop(0, SPAN // BLK, tile, 0, unroll=ATTN_UNROLL)

    lse_all = jnp.maximum(jnp.maximum(lse_ref[0], lse_ref[1]), lse_ref[2])
    num = jnp.zeros((SPAN, HEAD_DIM), jnp.float32)
    den = jnp.zeros((SPAN, LANES), jnp.float32)
    for b in range(len(BRANCHES)):
        w = jnp.exp2(lse_ref[b] - lse_all)
        num = num + w * out_ref[b]
        den = den + w
    o_ref[...] = (num / den).astype(o_ref.dtype)


def _attention(proj):
    q_col = 0
    k_col = q_col + N_HEADS
    v_col = k_col + N_HEADS
    nb = len(BRANCHES)
    return pl.pallas_call(
        _attn_kernel,
        grid=(N_HEADS, SEQ // SPAN),
        in_specs=[
            pl.BlockSpec((SPAN, HEAD_DIM), lambda h, s: (s, q_col + h)),
            pl.BlockSpec((SEQ, HEAD_DIM), lambda h, s: (0, k_col + h)),
            pl.BlockSpec((SEQ, HEAD_DIM), lambda h, s: (0, v_col + h)),
        ],
        out_specs=pl.BlockSpec((SPAN, HEAD_DIM), lambda h, s: (s, h)),
        out_shape=jax.ShapeDtypeStruct((SEQ, D_ATTN), jnp.bfloat16),
        scratch_shapes=[
            pltpu.VMEM((nb, SPAN, HEAD_DIM), jnp.float32),
            pltpu.VMEM((nb, SPAN, LANES), jnp.float32),
            pltpu.VMEM((2, 2, SPAN, HEAD_DIM), jnp.bfloat16),
        ],
        compiler_params=_params("arbitrary", "arbitrary"),
        name="attn",
    )(proj, proj, proj)


def _split_bf16(x):
    hi = x.astype(jnp.bfloat16)
    lo = (x - hi.astype(jnp.float32)).astype(jnp.bfloat16)
    return hi, lo


def _out_proj_kernel(ya_ref, yb_ref, wa_ref, wb_ref, x_ref, gate_ref, g_ref, b_ref, sc_ref, sh_ref,
                     wr_ref, br_ref, x1_ref, hp_ref, lg_ref):
    mix = jnp.dot(ya_ref[...], wa_ref[...], preferred_element_type=jnp.float32)
    mix = mix + jnp.dot(yb_ref[...], wb_ref[...], preferred_element_type=jnp.float32)
    x1 = _layer_norm(DEEPNORM_ALPHA * x_ref[...] + gate_ref[...] * mix, g_ref[...], b_ref[...])
    x1_ref[...] = x1
    h = x1 * (1.0 + sc_ref[...]) + sh_ref[...]
    _store_packed(hp_ref, (), 0, _pack_rows(h))
    h_hi, h_lo = _split_bf16(h)
    w_hi, w_lo = _split_bf16(wr_ref[...])
    both = jnp.dot(h_hi, jnp.concatenate([w_hi, w_lo], axis=1), preferred_element_type=jnp.float32)
    lg = both[:, :N_EXPERTS] + both[:, N_EXPERTS:] + jnp.dot(h_lo, w_hi, preferred_element_type=jnp.float32)
    lg_ref[...] = lg + br_ref[...]


def _out_proj(y_a, y_b, w_o_bf16, x, gate1, ln_g, ln_b, scale2, shift2, w_router, b_router):
    row = lambda i: (i, 0)
    fixed = lambda i: (0, 0)
    vec = pl.BlockSpec((1, D_MODEL), fixed)
    return pl.pallas_call(
        _out_proj_kernel,
        grid=(SEQ // OUT_TM,),
        in_specs=[
            pl.BlockSpec((OUT_TM, D_GMLP), row),
            pl.BlockSpec((OUT_TM, D_ATTN), row),
            pl.BlockSpec((D_GMLP, D_MODEL), lambda i: (0, 0)),
            pl.BlockSpec((D_ATTN, D_MODEL), lambda i: (1, 0)),
            pl.BlockSpec((OUT_TM, D_MODEL), row),
            vec, vec, vec, vec, vec,
            pl.BlockSpec((D_MODEL, N_EXPERTS), fixed),
            pl.BlockSpec((1, N_EXPERTS), fixed),
        ],
        out_specs=[
            pl.BlockSpec((OUT_TM, D_MODEL), row),
            pl.BlockSpec((OUT_TM * ROW_SUB, LANES), row),
            pl.BlockSpec((OUT_TM, N_EXPERTS), row),
        ],
        out_shape=[
            jax.ShapeDtypeStruct((SEQ, D_MODEL), jnp.float32),
            jax.ShapeDtypeStruct((SEQ * ROW_SUB, LANES), jnp.uint32),
            jax.ShapeDtypeStruct((SEQ, N_EXPERTS), jnp.float32),
        ],
        compiler_params=_params("arbitrary"),
        name="out_proj",
    )(y_a, y_b, w_o_bf16, w_o_bf16, x, gate1, ln_g, ln_b, scale2, shift2, w_router, b_router)


def _experts_kernel(meta_ref, dstp_ref, x_hbm, wg_ref, wl_ref, bg_ref, bl_ref, w2_ref,
                    b2_ref, y_hbm, xs_ref, os_ref, acc_ref, gsem, ssem):
    i = pl.program_id(0)
    j = pl.program_id(1)
    n_items = pl.num_programs(0)
    last_j = pl.num_programs(1) - 1

    def item_rows(k):
        inside = (k >= 0) & (k < n_items)
        return jnp.where(inside, meta_ref[MOE_ITEMS + jnp.clip(k, 0, n_items - 1)], 0)

    rows = item_rows(i)
    rows_prev = item_rows(i - 1)
    p = i % 2
    q = 1 - p
    group_sub = MOE_G * ROW_SUB

    def load_rows(k, slot):
        first = meta_ref[2 * MOE_ITEMS + jnp.clip(k, 0, n_items - 1)] * ROW_SUB

        def load_group(t, c):
            src = pl.multiple_of(first + t * group_sub, group_sub)
            dst = pl.multiple_of(t * group_sub, group_sub)
            pltpu.make_async_copy(x_hbm.at[pl.ds(src, group_sub), :], xs_ref.at[slot, pl.ds(dst, group_sub), :],
                                  gsem.at[slot]).start()
            return c
        lax.fori_loop(0, item_rows(k) // MOE_G, load_group, 0)

    def scatter(slot, row, queue=1):
        entry = dstp_ref[0, 0, row]
        src = pl.multiple_of(row * ROW_SUB, ROW_SUB)
        dst = pl.multiple_of(entry, ROW_SUB)
        pltpu.make_async_copy(os_ref.at[slot, pl.ds(src, ROW_SUB), :], y_hbm.at[pl.ds(dst, ROW_SUB), :],
                              ssem.at[slot]).start(priority=queue)
        return entry

    def wait_rows(buf_ref, sem, slot, n):
        group = buf_ref.at[slot, pl.ds(0, group_sub), :]

        def wait_group(t, c):
            pltpu.make_async_copy(group, group, sem.at[slot]).wait()
            return c
        lax.fori_loop(0, n // MOE_G, wait_group, 0)

    def each_row(lo, hi, fn):
        lax.fori_loop(lo, jnp.maximum(lo, hi), lambda r, c: (fn(r), c)[1], 0)

    @pl.when((i == 0) & (j == 0))
    def _():
        acc_ref[...] = jnp.zeros_like(acc_ref)
        os_ref[...] = jnp.zeros_like(os_ref)
        for half in range(2):
            fill = pltpu.make_async_copy(
                os_ref.at[0],
                y_hbm.at[pl.ds((SEQ * TOP_K + half * MOE_RMAX) * ROW_SUB, MOE_RMAX * ROW_SUB), :],
                ssem.at[0])
            fill.start()
            fill.wait()
        load_rows(0, 0)

    @pl.when(j == 0)
    def _():
        wait_rows(xs_ref, gsem, p, rows)
        load_rows(i + 1, q)

    def sub_tile(a, m):
        a = pl.multiple_of(a, MOE_G)
        base = (j * rows + a) // MOE_CHUNKS
        lag = 0
        for u in range(m // MOE_CHUNKS):
            lag = scatter(0, base + u + lag, queue=u % 2) >> 31
        x_lo, x_hi = _unpack_words(_load_packed(xs_ref, (p,), a, m))
        x = jnp.concatenate([x_lo.astype(jnp.bfloat16), x_hi.astype(jnp.bfloat16)], axis=1)

        def up(w_ref, b_ref):
            return jnp.dot(x, w_ref[...].astype(jnp.bfloat16), preferred_element_type=jnp.float32) + b_ref[...]

        glu = jnp.minimum(up(wg_ref, bg_ref), SWIGLU_LIMIT)
        lin = jnp.clip(up(wl_ref, bl_ref), -SWIGLU_LIMIT, SWIGLU_LIMIT)
        act = glu * jax.nn.sigmoid(SWIGLU_ALPHA * glu) * (lin + 1.0)
        down = jnp.dot(act.astype(jnp.bfloat16), w2_ref[...].astype(jnp.bfloat16),
                       preferred_element_type=jnp.float32)
        start = jnp.where(j == 0, jnp.broadcast_to(b2_ref[...], (m, D_MODEL)), acc_ref[pl.ds(a, m), :])
        acc_ref[pl.ds(a, m), :] = start + down

    @pl.when(rows > 0)
    def _():
        main = MOE_SUBTILES[0]
        n_main = rows // main
        lax.fori_loop(0, n_main, lambda t, c: (sub_tile(t * main, main), c)[1], 0)
        done = n_main * main
        for m in MOE_SUBTILES[1:]:
            has = ((rows - done) // m) % 2 == 1
            pl.when(has)(lambda done=done, m=m: sub_tile(done, m))
            done = done + jnp.where(has, m, 0)

    @pl.when(j == last_j)
    def _():
        each_row(rows, rows_prev, lambda r: scatter(0, r))
        wait_rows(os_ref, ssem, 0, jnp.maximum(rows, rows_prev))

        def pack(t, c):
            a = pl.multiple_of(t * MOE_G, MOE_G)
            _store_packed(os_ref, (0,), a, _pack_rows(acc_ref[pl.ds(a, MOE_G), :]))
            return c
        lax.fori_loop(0, rows // MOE_G, pack, 0)


def _experts(x_sorted, meta, dst_tab, w1, b1, w2, b2):
    n_j = MOE_CHUNKS

    def col(i, j, m):
        return jnp.where(m[MOE_ITEMS + i] > 0, j, n_j - 1)

    def expert(i, m):
        return m[i]

    tab = lambda f: pl.BlockSpec((1, 1, MOE_RMAX), f, memory_space=pltpu.SMEM)
    return pl.pallas_call(
        _experts_kernel,
        grid_spec=pltpu.PrefetchScalarGridSpec(
            num_scalar_prefetch=1,
            grid=(MOE_ITEMS, n_j),
            in_specs=[
                tab(lambda i, j, m: (i, 0, 0)),
                pl.BlockSpec(memory_space=pl.ANY),
                pl.BlockSpec((None, D_MODEL, MOE_TN), lambda i, j, m: (expert(i, m), 0, col(i, j, m))),
                pl.BlockSpec((None, D_MODEL, MOE_TN), lambda i, j, m: (expert(i, m), 0, n_j + col(i, j, m))),
                pl.BlockSpec((None, 1, MOE_TN), lambda i, j, m: (expert(i, m), 0, col(i, j, m))),
                pl.BlockSpec((None, 1, MOE_TN), lambda i, j, m: (expert(i, m), 0, n_j + col(i, j, m))),
                pl.BlockSpec((None, MOE_TN, D_MODEL), lambda i, j, m: (expert(i, m), col(i, j, m), 0)),
                pl.BlockSpec((None, 1, D_MODEL), lambda i, j, m: (expert(i, m), 0, 0)),
            ],
            out_specs=pl.BlockSpec(memory_space=pl.ANY),
            scratch_shapes=[
                pltpu.VMEM((2, MOE_RMAX * ROW_SUB, LANES), jnp.uint32),
                pltpu.VMEM((1, MOE_RMAX * ROW_SUB, LANES), jnp.uint32),
                pltpu.VMEM((MOE_RMAX, D_MODEL), jnp.float32),
                pltpu.SemaphoreType.DMA((2,)),
                pltpu.SemaphoreType.DMA((1,)),
            ],
        ),
        out_shape=jax.ShapeDtypeStruct((Y_ROWS * ROW_SUB, LANES), jnp.uint32),
        compiler_params=_params("arbitrary", "arbitrary"),
        name="experts",
    )(meta, dst_tab, x_sorted, w1, w1,
      b1.reshape(N_EXPERTS, 1, 2 * D_EXPERT), b1.reshape(N_EXPERTS, 1, 2 * D_EXPERT),
      w2, b2.reshape(N_EXPERTS, 1, D_MODEL))


def _combine_kernel(y0_ref, y1_ref, y2_ref, y3_ref, gates_ref, x1_ref, gate2_ref, g_ref, b_ref, o_ref):
    y_lo = jnp.zeros((COMBINE_TM, D_PACK), jnp.float32)
    y_hi = jnp.zeros((COMBINE_TM, D_PACK), jnp.float32)
    for k, y_ref in enumerate((y0_ref, y1_ref, y2_ref, y3_ref)):
        lo, hi = _unpack_words(_load_packed(y_ref, (), 0, COMBINE_TM))
        gate = gates_ref[:, k:k + 1]
        y_lo = y_lo + gate * lo
        y_hi = y_hi + gate * hi
    y = jnp.concatenate([y_lo, y_hi], axis=1)
    o_ref[...] = _layer_norm(DEEPNORM_ALPHA * x1_ref[...] + gate2_ref[...] * y, g_ref[...], b_ref[...])


def _combine(y_packed, gates, x1, gate2, ln_g, ln_b):
    n_tiles = SEQ // COMBINE_TM
    row = lambda i: (i, 0)
    vec = pl.BlockSpec((1, D_MODEL), lambda i: (0, 0))
    slot = lambda k: pl.BlockSpec((COMBINE_TM * ROW_SUB, LANES), lambda i: (k * n_tiles + i, 0))
    return pl.pallas_call(
        _combine_kernel,
        grid=(n_tiles,),
        in_specs=[
            slot(0), slot(1), slot(2), slot(3),
            pl.BlockSpec((COMBINE_TM, TOP_K), row),
            pl.BlockSpec((COMBINE_TM, D_MODEL), row),
            vec, vec, vec,
        ],
        out_specs=pl.BlockSpec((COMBINE_TM, D_MODEL), row),
        out_shape=jax.ShapeDtypeStruct((SEQ, D_MODEL), jnp.float32),
        compiler_params=_params("arbitrary"),
        name="combine",
    )(y_packed, y_packed, y_packed, y_packed, gates, x1, gate2, ln_g, ln_b)


TAB_ROWS = MOE_RMAX // LANES
WINDOW_ROWS = 2 * ROW_SUB
assert WINDOW_ROWS > TAB_ROWS and TOP_K == 4


def _row_tables_kernel(win_ref, order_ref, dst_ref):
    lane = lax.broadcasted_iota(jnp.int32, (TAB_ROWS, LANES), 1)
    sub = lax.broadcasted_iota(jnp.int32, (TAB_ROWS, LANES), 0)
    r = sub * LANES + lane

    def table(b, carry):
        w0 = win_ref[b]
        n_real = win_ref[MOE_ITEMS + 1 + b]
        off = w0 % LANES
        x = order_ref[pl.ds(w0 // LANES, WINDOW_ROWS), :]
        x = pltpu.roll(x, (LANES - off) % LANES, axis=1)
        flat = jnp.where(lane < LANES - off, x[:TAB_ROWS], x[1:TAB_ROWS + 1])
        tok = flat >> 2
        slot = flat & 3
        spill = SEQ * TOP_K + ((b + 1) % 2) * MOE_RMAX + r
        dst_ref[b] = jnp.where(r < n_real, slot * SEQ + tok, spill) * ROW_SUB
        return carry
    lax.fori_loop(0, MOE_ITEMS + 1, table, 0)


def _row_tables(win, order):
    n_rows = SEQ * TOP_K // LANES
    order2d = jnp.concatenate([order, jnp.zeros((WINDOW_ROWS * LANES,), jnp.int32)]).reshape(-1, LANES)
    return pl.pallas_call(
        _row_tables_kernel,
        grid_spec=pltpu.PrefetchScalarGridSpec(
            num_scalar_prefetch=1,
            grid=(1,),
            in_specs=[pl.BlockSpec((n_rows + WINDOW_ROWS, LANES), lambda b, w: (0, 0))],
            out_specs=pl.BlockSpec((MOE_ITEMS + 1, TAB_ROWS, LANES), lambda b, w: (0, 0, 0)),
        ),
        out_shape=jax.ShapeDtypeStruct((MOE_ITEMS + 1, TAB_ROWS, LANES), jnp.int32),
        compiler_params=_params("arbitrary"),
        name="row_tables",
    )(win, order2d)


def _dispatch_kernel(fill_ref, pos_ref, h_ref, x_hbm, zero_ref, buf_ref, sem, zsem):
    i = pl.program_id(0)
    group_sub = MOE_G * ROW_SUB

    @pl.when(i == 0)
    def _():
        zero_ref[...] = jnp.zeros_like(zero_ref)

        def fill(g):
            dst = pl.multiple_of(g * group_sub, group_sub)
            pltpu.make_async_copy(zero_ref, x_hbm.at[pl.ds(dst, group_sub), :], zsem).start()

        for e in range(N_EXPERTS):
            end = fill_ref[e]
            begin = fill_ref[e - 1] if e else 0
            pl.when(end > begin)(lambda end=end: fill(end - 1))
        lax.fori_loop(fill_ref[N_EXPERTS - 1], X_ROWS // MOE_G, lambda g, c: (fill(g), c)[1], 0)

        def wait_fill(t, c):
            pltpu.make_async_copy(zero_ref, zero_ref, zsem).wait()
            return c
        lax.fori_loop(0, fill_ref[N_EXPERTS], wait_fill, 0)

    slot = i % 2
    buf_ref[slot] = h_ref[...]

    def token(t, c):
        src = pl.multiple_of(t * ROW_SUB, ROW_SUB)
        for k in range(TOP_K):
            dst = pl.multiple_of(pos_ref[0, 0, t * TOP_K + k], ROW_SUB)
            pltpu.make_async_copy(buf_ref.at[slot, pl.ds(src, ROW_SUB), :], x_hbm.at[pl.ds(dst, ROW_SUB), :],
                                  sem.at[slot]).start(priority=k % 2)
        return c
    lax.fori_loop(0, DISPATCH_TM, token, 0, unroll=4)

    def wait_tile(s):
        for k in range(TOP_K):
            pltpu.make_async_copy(buf_ref.at[s], buf_ref.at[s], sem.at[s]).wait()

    pl.when(i > 0)(lambda: wait_tile(1 - slot))
    pl.when(i == pl.num_programs(0) - 1)(lambda: wait_tile(slot))


def _dispatch(h_packed, pos, fill):
    n_tiles = SEQ // DISPATCH_TM
    return pl.pallas_call(
        _dispatch_kernel,
        grid_spec=pltpu.PrefetchScalarGridSpec(
            num_scalar_prefetch=1,
            grid=(n_tiles,),
            in_specs=[
                pl.BlockSpec((1, 1, DISPATCH_TM * TOP_K), lambda i, f: (i, 0, 0), memory_space=pltpu.SMEM),
                pl.BlockSpec((DISPATCH_TM * ROW_SUB, LANES), lambda i, f: (i, 0)),
            ],
            out_specs=pl.BlockSpec(memory_space=pl.ANY),
            scratch_shapes=[
                pltpu.VMEM((MOE_G * ROW_SUB, LANES), jnp.uint32),
                pltpu.VMEM((2, DISPATCH_TM * ROW_SUB, LANES), jnp.uint32),
                pltpu.SemaphoreType.DMA((2,)),
                pltpu.SemaphoreType.DMA(()),
            ],
        ),
        out_shape=jax.ShapeDtypeStruct((X_ROWS * ROW_SUB, LANES), jnp.uint32),
        compiler_params=_params("arbitrary"),
        name="dispatch",
    )(fill, pos.reshape(n_tiles, 1, DISPATCH_TM * TOP_K), h_packed)


def _route(logits):
    top_val, top_idx = lax.top_k(logits, TOP_K)
    gates = jax.nn.softmax(top_val, axis=-1)
    e_flat = top_idx.reshape(-1).astype(jnp.int32)
    experts = jnp.arange(N_EXPERTS, dtype=jnp.int32)
    counts = jnp.sum((e_flat[:, None] == experts[None, :]).astype(jnp.int32), axis=0)
    groups = (counts + MOE_G - 1) // MOE_G
    group_end = jnp.cumsum(groups)
    row0 = (group_end - groups) * MOE_G
    per_item = MOE_RMAX // MOE_G
    n_items_e = (groups + per_item - 1) // per_item
    item_end = jnp.cumsum(n_items_e)
    item_start = item_end - n_items_e
    n_items = item_end[-1]

    item = jnp.arange(MOE_ITEMS, dtype=jnp.int32)
    used = item < n_items
    e_item = jnp.minimum(jnp.searchsorted(item_end, item, side='right'), N_EXPERTS - 1).astype(jnp.int32)
    e_last = e_item[jnp.maximum(n_items - 1, 0)]
    part = item - item_start[e_item]
    rows = jnp.where(used, jnp.clip(groups[e_item] - part * per_item, 0, per_item) * MOE_G, 0)
    first_row = jnp.where(used, row0[e_item] + part * MOE_RMAX, 0)
    meta = jnp.concatenate([jnp.where(used, e_item, e_last), rows, first_row]).astype(jnp.int32)

    order = jnp.argsort(e_flat, stable=True).astype(jnp.int32)
    rank = jnp.argsort(order).astype(jnp.int32)
    start = jnp.cumsum(counts) - counts
    shift = jnp.sum(jnp.where(e_flat[:, None] == experts[None, :], (row0 - start)[None, :], 0), axis=1)
    pos = (rank + shift) * ROW_SUB
    n_fill = jnp.sum((groups > 0).astype(jnp.int32)) + X_ROWS // MOE_G - group_end[-1]
    fill = jnp.concatenate([group_end, n_fill[None]]).astype(jnp.int32)

    window0 = jnp.where(used, start[e_item] + part * MOE_RMAX, 0)
    n_real = jnp.where(used, jnp.clip(counts[e_item] - part * MOE_RMAX, 0, MOE_RMAX), 0)
    zero = jnp.zeros((1,), jnp.int32)
    win = jnp.concatenate([zero, window0, zero, n_real]).astype(jnp.int32)
    dst_tab = _row_tables(win, order).reshape(MOE_ITEMS + 1, 1, MOE_RMAX)
    return gates, meta, pos.astype(jnp.int32), fill, dst_tab


def kernel(x, c, w_ada, b_ada, w_in, sgu_ln_g, sgu_ln_b, w_spatial, b_spatial, w_o, ln1_g, ln1_b,
           w_router, b_router, w_exp1, b_exp1, w_exp2, b_exp2, ln2_g, ln2_b):
    depth = w_ada.shape[0]
    assert x.shape == (1, SEQ, D_MODEL)
    xs = x.reshape(SEQ, D_MODEL)
    for l in range(depth):
        ada = _ada(c, w_ada[l], b_ada[l])
        shift1, scale1, gate1, shift2, scale2, gate2 = jnp.split(ada, 6, axis=-1)

        qkv, y_a = _in_proj(xs, scale1, shift1, w_in[l].astype(jnp.bfloat16),
                            sgu_ln_g[l], sgu_ln_b[l], w_spatial[l], b_spatial[l])
        y_b = _attention(qkv)
        x1, h_packed, logits = _out_proj(y_a, y_b, w_o[l].astype(jnp.bfloat16), xs, gate1,
                                         ln1_g[l].reshape(1, -1), ln1_b[l].reshape(1, -1), scale2, shift2,
                                         w_router[l], b_router[l].reshape(1, -1))

        gates, meta, pos, fill, dst_tab = _route(logits)
        x_sorted = _dispatch(h_packed, pos, fill)
        y_packed = _experts(x_sorted, meta, dst_tab, w_exp1[l], b_exp1[l], w_exp2[l], b_exp2[l])
        xs = _combine(y_packed, gates, x1, gate2, ln2_g[l].reshape(1, -1), ln2_b[l].reshape(1, -1))
    return xs.reshape(x.shape)
```

```python
import math

import jax
import jax.numpy as jnp
from jax import lax
from jax.experimental import pallas as pl
from jax.experimental.pallas import tpu as pltpu

D_MODEL = 2048
SEQ = 8192
D_GMLP = 1024
GMLP_GROUPS = 8
GROUP_DIM = 128
CHUNK = 128
D_ATTN = 1024
HEAD_DIM = 128
N_HEADS = 8
BRANCHES = ((128, 1), (512, 4), (2048, 16))
BLK = 128
SPAN = 16 * BLK
D_IN_PROJ = 2 * D_GMLP + 3 * D_ATTN
N_EXPERTS = 32
TOP_K = 4
D_EXPERT = 2048
SWIGLU_LIMIT = 7.0
SWIGLU_ALPHA = 1.702
LN_EPS = 1e-5
DEEPNORM_ALPHA = 2.0 ** 0.25
NEG = -1e30

LANES = 128
VMEM_LIMIT = 56 * 1024 * 1024

ADA_TN = 1536
PROJ_TM = 1024
PROJ_TN = 1024
OUT_TM = 512
ATTN_UNROLL = 8
ROW_SUB = 8
D_PACK = D_MODEL // 2
MOE_G = 128
MOE_RMAX = 1152
MOE_ITEMS = 64
MOE_TN = 512
MOE_CHUNKS = D_EXPERT // MOE_TN
MOE_SUBTILES = (512, 256, 128)
X_ROWS = SEQ * TOP_K + N_EXPERTS * MOE_G
DISPATCH_TM = 256
COLLECT_GROUPS = 2
COMBINE_TM = 256


def _params(*sem):
    return pltpu.CompilerParams(dimension_semantics=sem, vmem_limit_bytes=VMEM_LIMIT)


def _layer_norm(x, g, b):
    mu = jnp.mean(x, axis=-1, keepdims=True)
    xc = x - mu
    var = jnp.mean(xc * xc, axis=-1, keepdims=True)
    return xc * lax.rsqrt(var + LN_EPS) * g + b


def _pack_rows(x):
    r = x.astype(jnp.bfloat16).astype(jnp.float32)
    bits = lax.bitcast_convert_type(r, jnp.uint32)
    return (bits[:, D_PACK:] & jnp.uint32(0xFFFF0000)) | (bits[:, :D_PACK] >> 16)


def _unpack_words(u):
    lo = lax.bitcast_convert_type(u << 16, jnp.float32)
    hi = lax.bitcast_convert_type(u & jnp.uint32(0xFFFF0000), jnp.float32)
    return lo, hi


def _store_packed(ref, lead, row0, packed):
    m = packed.shape[0]
    for c in range(ROW_SUB):
        rows = pl.ds(row0 * ROW_SUB + c, m, stride=ROW_SUB)
        ref[lead + (rows, slice(None))] = packed[:, c * LANES:(c + 1) * LANES]


def _load_packed(ref, lead, row0, m):
    cols = [ref[lead + (pl.ds(row0 * ROW_SUB + c, m, stride=ROW_SUB), slice(None))] for c in range(ROW_SUB)]
    return jnp.concatenate(cols, axis=1)


def _ada_kernel(c_ref, w_ref, b_ref, o_ref):
    c = c_ref[...]
    s = c * jax.nn.sigmoid(c)
    o_ref[...] = jnp.sum(s * w_ref[...], axis=0, keepdims=True) + b_ref[...]


def _ada(c, w_ada, b_ada):
    n = w_ada.shape[1]
    return pl.pallas_call(
        _ada_kernel,
        grid=(n // ADA_TN,),
        in_specs=[
            pl.BlockSpec((D_MODEL, 1), lambda j: (0, 0)),
            pl.BlockSpec((D_MODEL, ADA_TN), lambda j: (0, j)),
            pl.BlockSpec((1, ADA_TN), lambda j: (0, j)),
        ],
        out_specs=pl.BlockSpec((1, ADA_TN), lambda j: (0, j)),
        out_shape=jax.ShapeDtypeStruct((1, n), jnp.float32),
        compiler_params=_params("arbitrary"),
        name="ada",
    )(c.reshape(D_MODEL, 1), w_ada, b_ada.reshape(1, n))


def _gelu(x):
    return 0.5 * x * (1.0 + lax.erf(x * (1.0 / math.sqrt(2.0))))


GROUPS_PER_STEP = 3


def _in_proj_kernel(x_ref, sc_ref, sh_ref, w_ref, g_ref, b_ref, ws_ref, bs_ref, qkv_ref, ya_ref, h_ref, uv_ref):
    i = pl.program_id(0)
    j = pl.program_id(1)

    @pl.when((i == 0) & (j == 0))
    def _():
        uv_ref[...] = jnp.zeros_like(uv_ref)

    @pl.when(j == 0)
    def _():
        h_ref[...] = (x_ref[...] * (1.0 + sc_ref[...]) + sh_ref[...]).astype(jnp.bfloat16)

    row = lax.broadcasted_iota(jnp.int32, (CHUNK, CHUNK), 0)
    col = lax.broadcasted_iota(jnp.int32, (CHUNK, CHUNK), 1)
    causal = col <= row
    chunks = [slice(c * CHUNK, (c + 1) * CHUNK) for c in range(PROJ_TM // CHUNK)]
    for u in range(GROUPS_PER_STEP):
        g = jnp.clip(GROUPS_PER_STEP * (j - 2) + u, 0, GMLP_GROUPS - 1)
        lanes = pl.ds(pl.multiple_of(g * GROUP_DIM, GROUP_DIM), GROUP_DIM)
        ln_g = g_ref[pl.ds(g, 1), :]
        ln_b = b_ref[pl.ds(g, 1), :]
        v = [_layer_norm(_gelu(uv_ref[1, rows, lanes]), ln_g, ln_b).astype(jnp.bfloat16) for rows in chunks]
        w = jnp.where(causal, ws_ref[g], 0.0).astype(jnp.bfloat16)
        s = jnp.dot(w, jnp.concatenate(v, axis=1), preferred_element_type=jnp.float32)
        for c, rows in enumerate(chunks):
            s_c = s[:, c * GROUP_DIM:(c + 1) * GROUP_DIM] + bs_ref[g]
            ya_ref[rows, lanes] = (_gelu(uv_ref[0, rows, lanes]) * s_c).astype(ya_ref.dtype)

    qkv_ref[...] = jnp.dot(h_ref[...], w_ref[...], preferred_element_type=jnp.float32)

    @pl.when(j < 2)
    def _():
        uv_ref[jnp.minimum(j, 1)] = qkv_ref[...]


def _in_proj(x, scale1, shift1, w_in_bf16, ln_g, ln_b, w_spatial, b_spatial):
    n_uv = 2 * D_GMLP // PROJ_TN
    assert GROUPS_PER_STEP * (D_IN_PROJ // PROJ_TN - n_uv) >= GMLP_GROUPS
    fixed2 = lambda i, j: (0, 0)
    bs_lanes = jnp.broadcast_to(b_spatial[:, :, None], (GMLP_GROUPS, CHUNK, LANES))
    return pl.pallas_call(
        _in_proj_kernel,
        grid=(SEQ // PROJ_TM, D_IN_PROJ // PROJ_TN),
        in_specs=[
            pl.BlockSpec((PROJ_TM, D_MODEL), lambda i, j: (i, 0)),
            pl.BlockSpec((1, D_MODEL), fixed2),
            pl.BlockSpec((1, D_MODEL), fixed2),
            pl.BlockSpec((D_MODEL, PROJ_TN), lambda i, j: (0, j)),
            pl.BlockSpec((GMLP_GROUPS, GROUP_DIM), fixed2),
            pl.BlockSpec((GMLP_GROUPS, GROUP_DIM), fixed2),
            pl.BlockSpec((GMLP_GROUPS, CHUNK, CHUNK), lambda i, j: (0, 0, 0)),
            pl.BlockSpec((GMLP_GROUPS, CHUNK, LANES), lambda i, j: (0, 0, 0)),
        ],
        out_specs=[
            pl.BlockSpec((PROJ_TM, PROJ_TN), lambda i, j: (i, jnp.maximum(j - n_uv, 0))),
            pl.BlockSpec((PROJ_TM, D_GMLP), lambda i, j: (i, 0)),
        ],
        out_shape=[
            jax.ShapeDtypeStruct((SEQ, 3 * D_ATTN), jnp.float32),
            jax.ShapeDtypeStruct((SEQ, D_GMLP), jnp.bfloat16),
        ],
        scratch_shapes=[
            pltpu.VMEM((PROJ_TM, D_MODEL), jnp.bfloat16),
            pltpu.VMEM((2, PROJ_TM, D_GMLP), jnp.float32),
        ],
        compiler_params=_params("arbitrary", "arbitrary"),
        name="in_proj",
    )(x, scale1, shift1, w_in_bf16, ln_g, ln_b, w_spatial, bs_lanes)


def _attn_kernel(q_ref, k_ref, v_ref, o_ref, out_ref, lse_ref, kv_ref):
    head = pl.program_id(0)
    span = pl.program_id(1)
    log2e = 1.0 / math.log(2.0)
    head_no = (jnp.zeros((BLK, 2 * BLK), jnp.int32) + (head + 1)).astype(jnp.float32)
    slope = jnp.exp2(head_no * (-8.0 / N_HEADS)) * log2e
    scale = HEAD_DIM ** -0.5 * log2e
    qi = lax.broadcasted_iota(jnp.int32, (BLK, 2 * BLK), 0)
    ki = lax.broadcasted_iota(jnp.int32, (BLK, 2 * BLK), 1)
    step = qi + BLK - ki
    contract_last = (((1,), (1,)), ((), ()))
    ones = jnp.ones((2 * BLK, LANES), jnp.bfloat16)

    for b, (window, d) in enumerate(BRANCHES):
        assert window // d == BLK
        valid = (step >= 0) & (step <= BLK)
        bias = jnp.where(valid, -slope * d * step.astype(jnp.float32), NEG)
        bias_first = jnp.where(ki >= BLK, bias, NEG)

        per_class = d * BLK == SPAN
        cur = span % 2
        if per_class:
            def stash(r, carry, d=d):
                rows = pl.ds(pl.multiple_of(r * BLK, BLK), BLK)
                kv_ref[0, cur, rows, :] = k_ref[pl.ds(span * SPAN + r, BLK, stride=d), :].astype(jnp.bfloat16)
                kv_ref[1, cur, rows, :] = v_ref[pl.ds(span * SPAN + r, BLK, stride=d), :].astype(jnp.bfloat16)
                return carry
            lax.fori_loop(0, d, stash, 0, unroll=4)

        def tile(t, carry, d=d, b=b, bias=bias, bias_first=bias_first, per_class=per_class):
            r = t % d
            n = t // d
            q0 = n * (BLK * d) + r
            k0 = span * SPAN + q0
            first = k0 < BLK * d
            q = (q_ref[pl.ds(q0, BLK, stride=d), :] * scale).astype(jnp.bfloat16)
            if per_class:
                rows = pl.ds(pl.multiple_of(r * BLK, BLK), BLK)
                prev = jnp.where(first, cur, 1 - cur)
                k = jnp.concatenate([kv_ref[0, prev, rows, :], kv_ref[0, cur, rows, :]], axis=0)
                v = jnp.concatenate([kv_ref[1, prev, rows, :], kv_ref[1, cur, rows, :]], axis=0)
            else:
                kp = jnp.where(first, k0, k0 - BLK * d)
                k = jnp.concatenate([k_ref[pl.ds(kp, BLK, stride=d), :], k_ref[pl.ds(k0, BLK, stride=d), :]],
                                    axis=0).astype(jnp.bfloat16)
                v = jnp.concatenate([v_ref[pl.ds(kp, BLK, stride=d), :], v_ref[pl.ds(k0, BLK, stride=d), :]],
                                    axis=0).astype(jnp.bfloat16)
            s = lax.dot_general(q, k, contract_last, preferred_element_type=jnp.float32)
            s = s + jnp.where(first, bias_first, bias)
            m = jnp.max(jnp.maximum(s[:, :BLK], s[:, BLK:]), axis=-1, keepdims=True)
            p = jnp.exp2(s - m).astype(jnp.bfloat16)
            v_one = jnp.concatenate([v, ones], axis=1)
            pv = jnp.dot(p, v_one, preferred_element_type=jnp.float32)
            den = pv[:, HEAD_DIM:]
            rows = pl.ds(q0, BLK, stride=d)
            out_ref[b, rows, :] = pv[:, :HEAD_DIM] / den
            lse_ref[b, rows, :] = m + jnp.log2(den)
            return carry

        lax.fori_loop(0, SPAN // BLK, tile, 0, unroll=ATTN_UNROLL)

    lse_all = jnp.maximum(jnp.maximum(lse_ref[0], lse_ref[1]), lse_ref[2])
    num = jnp.zeros((SPAN, HEAD_DIM), jnp.float32)
    den = jnp.zeros((SPAN, LANES), jnp.float32)
    for b in range(len(BRANCHES)):
        w = jnp.exp2(lse_ref[b] - lse_all)
        num = num + w * out_ref[b]
        den = den + w
    o_ref[...] = (num / den).astype(o_ref.dtype)


def _attention(proj):
    q_col = 0
    k_col = q_col + N_HEADS
    v_col = k_col + N_HEADS
    nb = len(BRANCHES)
    return pl.pallas_call(
        _attn_kernel,
        grid=(N_HEADS, SEQ // SPAN),
        in_specs=[
            pl.BlockSpec((SPAN, HEAD_DIM), lambda h, s: (s, q_col + h)),
            pl.BlockSpec((SEQ, HEAD_DIM), lambda h, s: (0, k_col + h)),
            pl.BlockSpec((SEQ, HEAD_DIM), lambda h, s: (0, v_col + h)),
        ],
        out_specs=pl.BlockSpec((SPAN, HEAD_DIM), lambda h, s: (s, h)),
        out_shape=jax.ShapeDtypeStruct((SEQ, D_ATTN), jnp.bfloat16),
        scratch_shapes=[
            pltpu.VMEM((nb, SPAN, HEAD_DIM), jnp.float32),
            pltpu.VMEM((nb, SPAN, LANES), jnp.float32),
            pltpu.VMEM((2, 2, SPAN, HEAD_DIM), jnp.bfloat16),
        ],
        compiler_params=_params("arbitrary", "arbitrary"),
        name="attn",
    )(proj, proj, proj)


def _split_bf16(x):
    hi = x.astype(jnp.bfloat16)
    lo = (x - hi.astype(jnp.float32)).astype(jnp.bfloat16)
    return hi, lo


def _out_proj_kernel(ya_ref, yb_ref, wa_ref, wb_ref, x_ref, gate_ref, g_ref, b_ref, sc_ref, sh_ref,
                     wr_ref, br_ref, x1_ref, hp_ref, lg_ref):
    mix = jnp.dot(ya_ref[...], wa_ref[...], preferred_element_type=jnp.float32)
    mix = mix + jnp.dot(yb_ref[...], wb_ref[...], preferred_element_type=jnp.float32)
    x1 = _layer_norm(DEEPNORM_ALPHA * x_ref[...] + gate_ref[...] * mix, g_ref[...], b_ref[...])
    x1_ref[...] = x1
    h = x1 * (1.0 + sc_ref[...]) + sh_ref[...]
    _store_packed(hp_ref, (), 0, _pack_rows(h))
    h_hi, h_lo = _split_bf16(h)
    w_hi, w_lo = _split_bf16(wr_ref[...])
    both = jnp.dot(h_hi, jnp.concatenate([w_hi, w_lo], axis=1), preferred_element_type=jnp.float32)
    lg = both[:, :N_EXPERTS] + both[:, N_EXPERTS:] + jnp.dot(h_lo, w_hi, preferred_element_type=jnp.float32)
    lg_ref[...] = lg + br_ref[...]


def _out_proj(y_a, y_b, w_o_bf16, x, gate1, ln_g, ln_b, scale2, shift2, w_router, b_router):
    row = lambda i: (i, 0)
    fixed = lambda i: (0, 0)
    vec = pl.BlockSpec((1, D_MODEL), fixed)
    return pl.pallas_call(
        _out_proj_kernel,
        grid=(SEQ // OUT_TM,),
        in_specs=[
            pl.BlockSpec((OUT_TM, D_GMLP), row),
            pl.BlockSpec((OUT_TM, D_ATTN), row),
            pl.BlockSpec((D_GMLP, D_MODEL), lambda i: (0, 0)),
            pl.BlockSpec((D_ATTN, D_MODEL), lambda i: (1, 0)),
            pl.BlockSpec((OUT_TM, D_MODEL), row),
            vec, vec, vec, vec, vec,
            pl.BlockSpec((D_MODEL, N_EXPERTS), fixed),
            pl.BlockSpec((1, N_EXPERTS), fixed),
        ],
        out_specs=[
            pl.BlockSpec((OUT_TM, D_MODEL), row),
            pl.BlockSpec((OUT_TM * ROW_SUB, LANES), row),
            pl.BlockSpec((OUT_TM, N_EXPERTS), row),
        ],
        out_shape=[
            jax.ShapeDtypeStruct((SEQ, D_MODEL), jnp.float32),
            jax.ShapeDtypeStruct((SEQ * ROW_SUB, LANES), jnp.uint32),
            jax.ShapeDtypeStruct((SEQ, N_EXPERTS), jnp.float32),
        ],
        compiler_params=_params("arbitrary"),
        name="out_proj",
    )(y_a, y_b, w_o_bf16, w_o_bf16, x, gate1, ln_g, ln_b, scale2, shift2, w_router, b_router)


def _experts_kernel(meta_ref, x_hbm, wg_ref, wl_ref, bg_ref, bl_ref, w2_ref, b2_ref, y_hbm,
                    xs_ref, os_ref, acc_ref, gsem, ssem):
    i = pl.program_id(0)
    j = pl.program_id(1)
    n_items = pl.num_programs(0)
    last_j = pl.num_programs(1) - 1

    def item_rows(k):
        inside = (k >= 0) & (k < n_items)
        return jnp.where(inside, meta_ref[MOE_ITEMS + jnp.clip(k, 0, n_items - 1)], 0)

    def item_first(k):
        return meta_ref[2 * MOE_ITEMS + jnp.clip(k, 0, n_items - 1)] * ROW_SUB

    rows = item_rows(i)
    rows_prev = item_rows(i - 1)
    p = i % 2
    q = 1 - p
    group_sub = MOE_G * ROW_SUB

    def each_group(n_rows, fn):
        lax.fori_loop(0, n_rows // MOE_G, lambda t, c: (fn(pl.multiple_of(t * group_sub, group_sub)), c)[1], 0)

    def load_rows(k, slot):
        first = item_first(k)
        each_group(item_rows(k), lambda at: pltpu.make_async_copy(
            x_hbm.at[pl.ds(pl.multiple_of(first + at, group_sub), group_sub), :],
            xs_ref.at[slot, pl.ds(at, group_sub), :], gsem.at[slot]).start())

    def wait_rows(buf_ref, sem, n_rows):
        group = buf_ref.at[pl.ds(0, group_sub), :]
        each_group(n_rows, lambda at: pltpu.make_async_copy(group, group, sem).wait())

    @pl.when((i == 0) & (j == 0))
    def _():
        acc_ref[...] = jnp.zeros_like(acc_ref)
        os_ref[...] = jnp.zeros_like(os_ref)
        zeros = os_ref.at[pl.ds(0, group_sub), :]
        first_unused = meta_ref[3 * MOE_ITEMS]
        n_unused = X_ROWS // MOE_G - first_unused

        def fill(t, c):
            dst = pl.multiple_of((first_unused + t) * group_sub, group_sub)
            pltpu.make_async_copy(zeros, y_hbm.at[pl.ds(dst, group_sub), :], ssem).start()
            return c
        lax.fori_loop(0, n_unused, fill, 0)
        wait_rows(os_ref, ssem, n_unused * MOE_G)
        load_rows(0, 0)

    @pl.when(j == 0)
    def _():
        wait_rows(xs_ref.at[p], gsem.at[p], rows)
        load_rows(i + 1, q)

    def sub_tile(a, m):
        a = pl.multiple_of(a, MOE_G)
        x_lo, x_hi = _unpack_words(_load_packed(xs_ref, (p,), a, m))
        x = jnp.concatenate([x_lo.astype(jnp.bfloat16), x_hi.astype(jnp.bfloat16)], axis=1)

        def up(w_ref, b_ref):
            return jnp.dot(x, w_ref[...].astype(jnp.bfloat16), preferred_element_type=jnp.float32) + b_ref[...]

        glu = jnp.minimum(up(wg_ref, bg_ref), SWIGLU_LIMIT)
        lin = jnp.clip(up(wl_ref, bl_ref), -SWIGLU_LIMIT, SWIGLU_LIMIT)
        act = glu * jax.nn.sigmoid(SWIGLU_ALPHA * glu) * (lin + 1.0)
        down = jnp.dot(act.astype(jnp.bfloat16), w2_ref[...].astype(jnp.bfloat16),
                       preferred_element_type=jnp.float32)
        start = jnp.where(j == 0, jnp.broadcast_to(b2_ref[...], (m, D_MODEL)), acc_ref[pl.ds(a, m), :])
        acc_ref[pl.ds(a, m), :] = start + down

    @pl.when(rows > 0)
    def _():
        main = MOE_SUBTILES[0]
        n_main = rows // main
        lax.fori_loop(0, n_main, lambda t, c: (sub_tile(t * main, main), c)[1], 0)
        done = n_main * main
        for m in MOE_SUBTILES[1:]:
            has = ((rows - done) // m) % 2 == 1
            pl.when(has)(lambda done=done, m=m: sub_tile(done, m))
            done = done + jnp.where(has, m, 0)

    @pl.when(j == last_j)
    def _():
        wait_rows(os_ref, ssem, rows_prev)

        def pack(t, c):
            a = pl.multiple_of(t * MOE_G, MOE_G)
            _store_packed(os_ref, (), a, _pack_rows(acc_ref[pl.ds(a, MOE_G), :]))
            return c
        lax.fori_loop(0, rows // MOE_G, pack, 0)
        first = item_first(i)
        each_group(rows, lambda at: pltpu.make_async_copy(
            os_ref.at[pl.ds(at, group_sub), :],
            y_hbm.at[pl.ds(pl.multiple_of(first + at, group_sub), group_sub), :], ssem).start())


def _experts(x_sorted, meta, w1, b1, w2, b2):
    n_j = MOE_CHUNKS

    def col(i, j, m):
        return jnp.where(m[MOE_ITEMS + i] > 0, j, n_j - 1)

    def expert(i, m):
        return m[i]

    return pl.pallas_call(
        _experts_kernel,
        grid_spec=pltpu.PrefetchScalarGridSpec(
            num_scalar_prefetch=1,
            grid=(MOE_ITEMS, n_j),
            in_specs=[
                pl.BlockSpec(memory_space=pl.ANY),
                pl.BlockSpec((None, D_MODEL, MOE_TN), lambda i, j, m: (expert(i, m), 0, col(i, j, m))),
                pl.BlockSpec((None, D_MODEL, MOE_TN), lambda i, j, m: (expert(i, m), 0, n_j + col(i, j, m))),
                pl.BlockSpec((None, 1, MOE_TN), lambda i, j, m: (expert(i, m), 0, col(i, j, m))),
                pl.BlockSpec((None, 1, MOE_TN), lambda i, j, m: (expert(i, m), 0, n_j + col(i, j, m))),
                pl.BlockSpec((None, MOE_TN, D_MODEL), lambda i, j, m: (expert(i, m), col(i, j, m), 0)),
                pl.BlockSpec((None, 1, D_MODEL), lambda i, j, m: (expert(i, m), 0, 0)),
            ],
            out_specs=pl.BlockSpec(memory_space=pl.ANY),
            scratch_shapes=[
                pltpu.VMEM((2, MOE_RMAX * ROW_SUB, LANES), jnp.uint32),
                pltpu.VMEM((MOE_RMAX * ROW_SUB, LANES), jnp.uint32),
                pltpu.VMEM((MOE_RMAX, D_MODEL), jnp.float32),
                pltpu.SemaphoreType.DMA((2,)),
                pltpu.SemaphoreType.DMA(()),
            ],
        ),
        out_shape=jax.ShapeDtypeStruct((X_ROWS * ROW_SUB, LANES), jnp.uint32),
        compiler_params=_params("arbitrary", "arbitrary"),
        name="experts",
    )(meta, x_sorted, w1, w1,
      b1.reshape(N_EXPERTS, 1, 2 * D_EXPERT), b1.reshape(N_EXPERTS, 1, 2 * D_EXPERT),
      w2, b2.reshape(N_EXPERTS, 1, D_MODEL))


def _combine_kernel(y0_ref, y1_ref, y2_ref, y3_ref, gates_ref, x1_ref, gate2_ref, g_ref, b_ref, o_ref):
    y_lo = jnp.zeros((COMBINE_TM, D_PACK), jnp.float32)
    y_hi = jnp.zeros((COMBINE_TM, D_PACK), jnp.float32)
    for k, y_ref in enumerate((y0_ref, y1_ref, y2_ref, y3_ref)):
        lo, hi = _unpack_words(_load_packed(y_ref, (), 0, COMBINE_TM))
        gate = gates_ref[:, k:k + 1]
        y_lo = y_lo + gate * lo
        y_hi = y_hi + gate * hi
    y = jnp.concatenate([y_lo, y_hi], axis=1)
    o_ref[...] = _layer_norm(DEEPNORM_ALPHA * x1_ref[...] + gate2_ref[...] * y, g_ref[...], b_ref[...])


def _combine(y_packed, gates, x1, gate2, ln_g, ln_b):
    n_tiles = SEQ // COMBINE_TM
    row = lambda i: (i, 0)
    vec = pl.BlockSpec((1, D_MODEL), lambda i: (0, 0))
    slot = lambda k: pl.BlockSpec((COMBINE_TM * ROW_SUB, LANES), lambda i: (k * n_tiles + i, 0))
    return pl.pallas_call(
        _combine_kernel,
        grid=(n_tiles,),
        in_specs=[
            slot(0), slot(1), slot(2), slot(3),
            pl.BlockSpec((COMBINE_TM, TOP_K), row),
            pl.BlockSpec((COMBINE_TM, D_MODEL), row),
            vec, vec, vec,
        ],
        out_specs=pl.BlockSpec((COMBINE_TM, D_MODEL), row),
        out_shape=jax.ShapeDtypeStruct((SEQ, D_MODEL), jnp.float32),
        compiler_params=_params("arbitrary"),
        name="combine",
    )(y_packed, y_packed, y_packed, y_packed, gates, x1, gate2, ln_g, ln_b)


N_GROUPS = X_ROWS // MOE_G
WINDOW_ROWS = ROW_SUB
assert MOE_G == LANES and TOP_K == 4


def _row_tables_kernel(win_ref, order_ref, dst_ref):
    lane = lax.broadcasted_iota(jnp.int32, (1, LANES), 1)

    def table(g, carry):
        w0 = win_ref[g]
        n_real = win_ref[N_GROUPS + g]
        off = w0 % LANES
        x = order_ref[pl.ds(w0 // LANES, WINDOW_ROWS), :]
        x = pltpu.roll(x, (LANES - off) % LANES, axis=1)
        flat = jnp.where(lane < LANES - off, x[0:1], x[1:2])
        tok = flat >> 2
        slot = flat & 3
        dst_ref[pl.ds(g, 1), :] = jnp.where(lane < n_real, slot * SEQ + tok, 0) * ROW_SUB
        return carry
    lax.fori_loop(0, N_GROUPS, table, 0)


def _row_tables(win, order):
    n_rows = SEQ * TOP_K // LANES
    order2d = jnp.concatenate([order, jnp.zeros((WINDOW_ROWS * LANES,), jnp.int32)]).reshape(-1, LANES)
    return pl.pallas_call(
        _row_tables_kernel,
        grid_spec=pltpu.PrefetchScalarGridSpec(
            num_scalar_prefetch=1,
            grid=(1,),
            in_specs=[pl.BlockSpec((n_rows + WINDOW_ROWS, LANES), lambda b, w: (0, 0))],
            out_specs=pl.BlockSpec((N_GROUPS, LANES), lambda b, w: (0, 0)),
        ),
        out_shape=jax.ShapeDtypeStruct((N_GROUPS, LANES), jnp.int32),
        compiler_params=_params("arbitrary"),
        name="row_tables",
    )(win, order2d)


def _collect_kernel(real_ref, dst_ref, y_ref, out_hbm, buf_ref, sem):
    i = pl.program_id(0)
    slot = i % 2
    buf_ref[slot] = y_ref[...]

    def row_copy(s, r):
        at = pl.multiple_of(r * ROW_SUB, ROW_SUB)
        dst = pl.multiple_of(dst_ref[0, 0, r], ROW_SUB)
        return pltpu.make_async_copy(buf_ref.at[s, pl.ds(at, ROW_SUB), :], out_hbm.at[pl.ds(dst, ROW_SUB), :],
                                     sem.at[s])

    group_sub = MOE_G * ROW_SUB

    def rows_loop(n, fn, **kw):
        lax.fori_loop(0, n, lambda r, c: (fn(r), c)[1], 0, **kw)

    def start_tile(step, s):
        for h in range(COLLECT_GROUPS):
            n_real = real_ref[step * COLLECT_GROUPS + h]
            start = lambda r, h=h: row_copy(s, h * MOE_G + r).start(priority=h % 2)

            @pl.when(n_real == MOE_G)
            def _():
                rows_loop(MOE_G, start, unroll=8)

            @pl.when(n_real < MOE_G)
            def _():
                rows_loop(n_real, start)

    def wait_tile(step, s):
        whole = buf_ref.at[s, pl.ds(0, group_sub), :]
        for h in range(COLLECT_GROUPS):
            n_real = real_ref[step * COLLECT_GROUPS + h]

            @pl.when(n_real == MOE_G)
            def _():
                pltpu.make_async_copy(whole, whole, sem.at[s]).wait()

            @pl.when(n_real < MOE_G)
            def _():
                rows_loop(n_real, lambda r: row_copy(s, 0).wait())

    start_tile(i, slot)

    @pl.when(i > 0)
    def _():
        wait_tile(i - 1, 1 - slot)

    @pl.when(i == pl.num_programs(0) - 1)
    def _():
        wait_tile(i, slot)


def _collect(y_sorted, dst_tab, n_real):
    n_tiles = N_GROUPS // COLLECT_GROUPS
    tile_rows = COLLECT_GROUPS * MOE_G
    return pl.pallas_call(
        _collect_kernel,
        grid_spec=pltpu.PrefetchScalarGridSpec(
            num_scalar_prefetch=1,
            grid=(n_tiles,),
            in_specs=[
                pl.BlockSpec((1, 1, tile_rows), lambda i, n: (i, 0, 0), memory_space=pltpu.SMEM),
                pl.BlockSpec((tile_rows * ROW_SUB, LANES), lambda i, n: (i, 0)),
            ],
            out_specs=pl.BlockSpec(memory_space=pl.ANY),
            scratch_shapes=[
                pltpu.VMEM((2, tile_rows * ROW_SUB, LANES), jnp.uint32),
                pltpu.SemaphoreType.DMA((2,)),
            ],
        ),
        out_shape=jax.ShapeDtypeStruct((SEQ * TOP_K * ROW_SUB, LANES), jnp.uint32),
        compiler_params=_params("arbitrary"),
        name="collect",
    )(n_real, dst_tab.reshape(n_tiles, 1, tile_rows), y_sorted)


def _dispatch_kernel(fill_ref, pos_ref, h_ref, x_hbm, zero_ref, buf_ref, sem, zsem):
    i = pl.program_id(0)
    group_sub = MOE_G * ROW_SUB

    @pl.when(i == 0)
    def _():
        zero_ref[...] = jnp.zeros_like(zero_ref)

        def fill(g):
            dst = pl.multiple_of(g * group_sub, group_sub)
            pltpu.make_async_copy(zero_ref, x_hbm.at[pl.ds(dst, group_sub), :], zsem).start()

        for e in range(N_EXPERTS):
            end = fill_ref[e]
            begin = fill_ref[e - 1] if e else 0
            pl.when(end > begin)(lambda end=end: fill(end - 1))
        lax.fori_loop(fill_ref[N_EXPERTS - 1], X_ROWS // MOE_G, lambda g, c: (fill(g), c)[1], 0)

        def wait_fill(t, c):
            pltpu.make_async_copy(zero_ref, zero_ref, zsem).wait()
            return c
        lax.fori_loop(0, fill_ref[N_EXPERTS], wait_fill, 0)

    slot = i % 2
    buf_ref[slot] = h_ref[...]

    def token(t, c):
        src = pl.multiple_of(t * ROW_SUB, ROW_SUB)
        for k in range(TOP_K):
            dst = pl.multiple_of(pos_ref[0, 0, t * TOP_K + k], ROW_SUB)
            pltpu.make_async_copy(buf_ref.at[slot, pl.ds(src, ROW_SUB), :], x_hbm.at[pl.ds(dst, ROW_SUB), :],
                                  sem.at[slot]).start(priority=k % 2)
        return c
    lax.fori_loop(0, DISPATCH_TM, token, 0, unroll=4)

    def wait_tile(s):
        for k in range(TOP_K):
            pltpu.make_async_copy(buf_ref.at[s], buf_ref.at[s], sem.at[s]).wait()

    pl.when(i > 0)(lambda: wait_tile(1 - slot))
    pl.when(i == pl.num_programs(0) - 1)(lambda: wait_tile(slot))


def _dispatch(h_packed, pos, fill):
    n_tiles = SEQ // DISPATCH_TM
    return pl.pallas_call(
        _dispatch_kernel,
        grid_spec=pltpu.PrefetchScalarGridSpec(
            num_scalar_prefetch=1,
            grid=(n_tiles,),
            in_specs=[
                pl.BlockSpec((1, 1, DISPATCH_TM * TOP_K), lambda i, f: (i, 0, 0), memory_space=pltpu.SMEM),
                pl.BlockSpec((DISPATCH_TM * ROW_SUB, LANES), lambda i, f: (i, 0)),
            ],
            out_specs=pl.BlockSpec(memory_space=pl.ANY),
            scratch_shapes=[
                pltpu.VMEM((MOE_G * ROW_SUB, LANES), jnp.uint32),
                pltpu.VMEM((2, DISPATCH_TM * ROW_SUB, LANES), jnp.uint32),
                pltpu.SemaphoreType.DMA((2,)),
                pltpu.SemaphoreType.DMA(()),
            ],
        ),
        out_shape=jax.ShapeDtypeStruct((X_ROWS * ROW_SUB, LANES), jnp.uint32),
        compiler_params=_params("arbitrary"),
        name="dispatch",
    )(fill, pos.reshape(n_tiles, 1, DISPATCH_TM * TOP_K), h_packed)


def _route(logits):
    top_val, top_idx = lax.top_k(logits, TOP_K)
    gates = jax.nn.softmax(top_val, axis=-1)
    e_flat = top_idx.reshape(-1).astype(jnp.int32)
    experts = jnp.arange(N_EXPERTS, dtype=jnp.int32)
    counts = jnp.sum((e_flat[:, None] == experts[None, :]).astype(jnp.int32), axis=0)
    groups = (counts + MOE_G - 1) // MOE_G
    group_end = jnp.cumsum(groups)
    row0 = (group_end - groups) * MOE_G
    per_item = MOE_RMAX // MOE_G
    n_items_e = (groups + per_item - 1) // per_item
    item_end = jnp.cumsum(n_items_e)
    item_start = item_end - n_items_e
    n_items = item_end[-1]

    item = jnp.arange(MOE_ITEMS, dtype=jnp.int32)
    used = item < n_items
    e_item = jnp.minimum(jnp.searchsorted(item_end, item, side='right'), N_EXPERTS - 1).astype(jnp.int32)
    e_last = e_item[jnp.maximum(n_items - 1, 0)]
    part = item - item_start[e_item]
    rows = jnp.where(used, jnp.clip(groups[e_item] - part * per_item, 0, per_item) * MOE_G, 0)
    first_row = jnp.where(used, row0[e_item] + part * MOE_RMAX, 0)
    meta = jnp.concatenate([jnp.where(used, e_item, e_last), rows, first_row, group_end[-1:]]).astype(jnp.int32)

    order = jnp.argsort(e_flat, stable=True).astype(jnp.int32)
    rank = jnp.argsort(order).astype(jnp.int32)
    start = jnp.cumsum(counts) - counts
    shift = jnp.sum(jnp.where(e_flat[:, None] == experts[None, :], (row0 - start)[None, :], 0), axis=1)
    pos = (rank + shift) * ROW_SUB
    n_fill = jnp.sum((groups > 0).astype(jnp.int32)) + X_ROWS // MOE_G - group_end[-1]
    fill = jnp.concatenate([group_end, n_fill[None]]).astype(jnp.int32)

    group = jnp.arange(N_GROUPS, dtype=jnp.int32)
    e_group = jnp.minimum(jnp.searchsorted(group_end, group, side='right'), N_EXPERTS - 1).astype(jnp.int32)
    in_use = group < group_end[-1]
    nth = group - (group_end - groups)[e_group]
    window0 = jnp.where(in_use, start[e_group] + nth * MOE_G, 0)
    n_real = jnp.where(in_use, jnp.clip(counts[e_group] - nth * MOE_G, 0, MOE_G), 0).astype(jnp.int32)
    dst_tab = _row_tables(jnp.concatenate([window0, n_real]).astype(jnp.int32), order)
    return gates, meta, pos.astype(jnp.int32), fill, dst_tab, n_real


def kernel(x, c, w_ada, b_ada, w_in, sgu_ln_g, sgu_ln_b, w_spatial, b_spatial, w_o, ln1_g, ln1_b,
           w_router, b_router, w_exp1, b_exp1, w_exp2, b_exp2, ln2_g, ln2_b):
    depth = w_ada.shape[0]
    assert x.shape == (1, SEQ, D_MODEL)
    xs = x.reshape(SEQ, D_MODEL)
    for l in range(depth):
        ada = _ada(c, w_ada[l], b_ada[l])
        shift1, scale1, gate1, shift2, scale2, gate2 = jnp.split(ada, 6, axis=-1)

        qkv, y_a = _in_proj(xs, scale1, shift1, w_in[l].astype(jnp.bfloat16),
                            sgu_ln_g[l], sgu_ln_b[l], w_spatial[l], b_spatial[l])
        y_b = _attention(qkv)
        x1, h_packed, logits = _out_proj(y_a, y_b, w_o[l].astype(jnp.bfloat16), xs, gate1,
                                         ln1_g[l].reshape(1, -1), ln1_b[l].reshape(1, -1), scale2, shift2,
                                         w_router[l], b_router[l].reshape(1, -1))

        gates, meta, pos, fill, dst_tab, n_real = _route(logits)
        x_sorted = _dispatch(h_packed, pos, fill)
        y_sorted = _experts(x_sorted, meta, w_exp1[l], b_exp1[l], w_exp2[l], b_exp2[l])
        y_packed = _collect(y_sorted, dst_tab, n_real)
        xs = _combine(y_packed, gates, x1, gate2, ln2_g[l].reshape(1, -1), ln2_b[l].reshape(1, -1))
    return xs.reshape(x.shape)
```

```python
import math

import jax
import jax.numpy as jnp
from jax import lax
from jax.experimental import pallas as pl
from jax.experimental.pallas import tpu as pltpu

D_MODEL = 2048
SEQ = 8192
D_GMLP = 1024
GMLP_GROUPS = 8
GROUP_DIM = 128
CHUNK = 128
D_ATTN = 1024
HEAD_DIM = 128
N_HEADS = 8
BRANCHES = ((128, 1), (512, 4), (2048, 16))
BLK = 128
SPAN = 16 * BLK
D_IN_PROJ = 2 * D_GMLP + 3 * D_ATTN
N_EXPERTS = 32
TOP_K = 4
D_EXPERT = 2048
SWIGLU_LIMIT = 7.0
SWIGLU_ALPHA = 1.702
LN_EPS = 1e-5
DEEPNORM_ALPHA = 2.0 ** 0.25
NEG = -1e30

LANES = 128
VMEM_LIMIT = 56 * 1024 * 1024

ADA_TN = 1536
PROJ_TM = 1024
PROJ_TN = 1024
OUT_TM = 512
ATTN_UNROLL = 8
ROW_SUB = 8
D_PACK = D_MODEL // 2
MOE_G = 128
MOE_RMAX = 1152
MOE_ITEMS = 64
MOE_TN = 512
MOE_CHUNKS = D_EXPERT // MOE_TN
MOE_SUBTILES = (512, 256, 128)
X_ROWS = SEQ * TOP_K + N_EXPERTS * MOE_G
DISPATCH_TM = 256
COLLECT_GROUPS = 8
COMBINE_TM = 256


def _params(*sem):
    return pltpu.CompilerParams(dimension_semantics=sem, vmem_limit_bytes=VMEM_LIMIT)


def _layer_norm(x, g, b):
    mu = jnp.mean(x, axis=-1, keepdims=True)
    xc = x - mu
    var = jnp.mean(xc * xc, axis=-1, keepdims=True)
    return xc * lax.rsqrt(var + LN_EPS) * g + b


def _pack_rows(x):
    r = x.astype(jnp.bfloat16).astype(jnp.float32)
    bits = lax.bitcast_convert_type(r, jnp.uint32)
    return (bits[:, D_PACK:] & jnp.uint32(0xFFFF0000)) | (bits[:, :D_PACK] >> 16)


def _unpack_words(u):
    lo = lax.bitcast_convert_type(u << 16, jnp.float32)
    hi = lax.bitcast_convert_type(u & jnp.uint32(0xFFFF0000), jnp.float32)
    return lo, hi


def _store_packed(ref, lead, row0, packed):
    m = packed.shape[0]
    for c in range(ROW_SUB):
        rows = pl.ds(row0 * ROW_SUB + c, m, stride=ROW_SUB)
        ref[lead + (rows, slice(None))] = packed[:, c * LANES:(c + 1) * LANES]


def _load_packed(ref, lead, row0, m):
    cols = [ref[lead + (pl.ds(row0 * ROW_SUB + c, m, stride=ROW_SUB), slice(None))] for c in range(ROW_SUB)]
    return jnp.concatenate(cols, axis=1)


def _ada_kernel(c_ref, w_ref, b_ref, o_ref):
    c = c_ref[...]
    s = c * jax.nn.sigmoid(c)
    o_ref[...] = jnp.sum(s * w_ref[...], axis=0, keepdims=True) + b_ref[...]


def _ada(c, w_ada, b_ada):
    n = w_ada.shape[1]
    return pl.pallas_call(
        _ada_kernel,
        grid=(n // ADA_TN,),
        in_specs=[
            pl.BlockSpec((D_MODEL, 1), lambda j: (0, 0)),
            pl.BlockSpec((D_MODEL, ADA_TN), lambda j: (0, j)),
            pl.BlockSpec((1, ADA_TN), lambda j: (0, j)),
        ],
        out_specs=pl.BlockSpec((1, ADA_TN), lambda j: (0, j)),
        out_shape=jax.ShapeDtypeStruct((1, n), jnp.float32),
        compiler_params=_params("arbitrary"),
        name="ada",
    )(c.reshape(D_MODEL, 1), w_ada, b_ada.reshape(1, n))


def _gelu(x):
    return 0.5 * x * (1.0 + lax.erf(x * (1.0 / math.sqrt(2.0))))


GROUPS_PER_STEP = 3


def _in_proj_kernel(x_ref, sc_ref, sh_ref, w_ref, g_ref, b_ref, ws_ref, bs_ref, qkv_ref, ya_ref, h_ref, uv_ref):
    i = pl.program_id(0)
    j = pl.program_id(1)

    @pl.when((i == 0) & (j == 0))
    def _():
        uv_ref[...] = jnp.zeros_like(uv_ref)

    @pl.when(j == 0)
    def _():
        h_ref[...] = (x_ref[...] * (1.0 + sc_ref[...]) + sh_ref[...]).astype(jnp.bfloat16)

    row = lax.broadcasted_iota(jnp.int32, (CHUNK, CHUNK), 0)
    col = lax.broadcasted_iota(jnp.int32, (CHUNK, CHUNK), 1)
    causal = col <= row
    chunks = [slice(c * CHUNK, (c + 1) * CHUNK) for c in range(PROJ_TM // CHUNK)]
    for u in range(GROUPS_PER_STEP):
        g = jnp.clip(GROUPS_PER_STEP * (j - 2) + u, 0, GMLP_GROUPS - 1)
        lanes = pl.ds(pl.multiple_of(g * GROUP_DIM, GROUP_DIM), GROUP_DIM)
        ln_g = g_ref[pl.ds(g, 1), :]
        ln_b = b_ref[pl.ds(g, 1), :]
        v = [_layer_norm(_gelu(uv_ref[1, rows, lanes]), ln_g, ln_b).astype(jnp.bfloat16) for rows in chunks]
        w = jnp.where(causal, ws_ref[g], 0.0).astype(jnp.bfloat16)
        s = jnp.dot(w, jnp.concatenate(v, axis=1), preferred_element_type=jnp.float32)
        for c, rows in enumerate(chunks):
            s_c = s[:, c * GROUP_DIM:(c + 1) * GROUP_DIM] + bs_ref[g]
            ya_ref[rows, lanes] = (_gelu(uv_ref[0, rows, lanes]) * s_c).astype(ya_ref.dtype)

    qkv_ref[...] = jnp.dot(h_ref[...], w_ref[...], preferred_element_type=jnp.float32)

    @pl.when(j < 2)
    def _():
        uv_ref[jnp.minimum(j, 1)] = qkv_ref[...]


def _in_proj(x, scale1, shift1, w_in_bf16, ln_g, ln_b, w_spatial, b_spatial):
    n_uv = 2 * D_GMLP // PROJ_TN
    assert GROUPS_PER_STEP * (D_IN_PROJ // PROJ_TN - n_uv) >= GMLP_GROUPS
    fixed2 = lambda i, j: (0, 0)
    bs_lanes = jnp.broadcast_to(b_spatial[:, :, None], (GMLP_GROUPS, CHUNK, LANES))
    return pl.pallas_call(
        _in_proj_kernel,
        grid=(SEQ // PROJ_TM, D_IN_PROJ // PROJ_TN),
        in_specs=[
            pl.BlockSpec((PROJ_TM, D_MODEL), lambda i, j: (i, 0)),
            pl.BlockSpec((1, D_MODEL), fixed2),
            pl.BlockSpec((1, D_MODEL), fixed2),
            pl.BlockSpec((D_MODEL, PROJ_TN), lambda i, j: (0, j)),
            pl.BlockSpec((GMLP_GROUPS, GROUP_DIM), fixed2),
            pl.BlockSpec((GMLP_GROUPS, GROUP_DIM), fixed2),
            pl.BlockSpec((GMLP_GROUPS, CHUNK, CHUNK), lambda i, j: (0, 0, 0)),
            pl.BlockSpec((GMLP_GROUPS, CHUNK, LANES), lambda i, j: (0, 0, 0)),
        ],
        out_specs=[
            pl.BlockSpec((PROJ_TM, PROJ_TN), lambda i, j: (i, jnp.maximum(j - n_uv, 0))),
            pl.BlockSpec((PROJ_TM, D_GMLP), lambda i, j: (i, 0)),
        ],
        out_shape=[
            jax.ShapeDtypeStruct((SEQ, 3 * D_ATTN), jnp.float32),
            jax.ShapeDtypeStruct((SEQ, D_GMLP), jnp.bfloat16),
        ],
        scratch_shapes=[
            pltpu.VMEM((PROJ_TM, D_MODEL), jnp.bfloat16),
            pltpu.VMEM((2, PROJ_TM, D_GMLP), jnp.float32),
        ],
        compiler_params=_params("arbitrary", "arbitrary"),
        name="in_proj",
    )(x, scale1, shift1, w_in_bf16, ln_g, ln_b, w_spatial, bs_lanes)


def _attn_kernel(q_ref, k_ref, v_ref, o_ref, out_ref, lse_ref, kv_ref):
    head = pl.program_id(0)
    span = pl.program_id(1)
    log2e = 1.0 / math.log(2.0)
    head_no = (jnp.zeros((BLK, 2 * BLK), jnp.int32) + (head + 1)).astype(jnp.float32)
    slope = jnp.exp2(head_no * (-8.0 / N_HEADS)) * log2e
    scale = HEAD_DIM ** -0.5 * log2e
    qi = lax.broadcasted_iota(jnp.int32, (BLK, 2 * BLK), 0)
    ki = lax.broadcasted_iota(jnp.int32, (BLK, 2 * BLK), 1)
    step = qi + BLK - ki
    contract_last = (((1,), (1,)), ((), ()))
    ones = jnp.ones((2 * BLK, LANES), jnp.bfloat16)

    for b, (window, d) in enumerate(BRANCHES):
        assert window // d == BLK
        valid = (step >= 0) & (step <= BLK)
        bias = jnp.where(valid, -slope * d * step.astype(jnp.float32), NEG)
        bias_first = jnp.where(ki >= BLK, bias, NEG)

        per_class = d * BLK == SPAN
        cur = span % 2
        if per_class:
            def stash(r, carry, d=d):
                rows = pl.ds(pl.multiple_of(r * BLK, BLK), BLK)
                kv_ref[0, cur, rows, :] = k_ref[pl.ds(span * SPAN + r, BLK, stride=d), :].astype(jnp.bfloat16)
                kv_ref[1, cur, rows, :] = v_ref[pl.ds(span * SPAN + r, BLK, stride=d), :].astype(jnp.bfloat16)
                return carry
            lax.fori_loop(0, d, stash, 0, unroll=4)

        def tile(t, carry, d=d, b=b, bias=bias, bias_first=bias_first, per_class=per_class):
            r = t % d
            n = t // d
            q0 = n * (BLK * d) + r
            k0 = span * SPAN + q0
            first = k0 < BLK * d
            q = (q_ref[pl.ds(q0, BLK, stride=d), :] * scale).astype(jnp.bfloat16)
            if per_class:
                rows = pl.ds(pl.multiple_of(r * BLK, BLK), BLK)
                prev = jnp.where(first, cur, 1 - cur)
                k = jnp.concatenate([kv_ref[0, prev, rows, :], kv_ref[0, cur, rows, :]], axis=0)
                v = jnp.concatenate([kv_ref[1, prev, rows, :], kv_ref[1, cur, rows, :]], axis=0)
            else:
                kp = jnp.where(first, k0, k0 - BLK * d)
                k = jnp.concatenate([k_ref[pl.ds(kp, BLK, stride=d), :], k_ref[pl.ds(k0, BLK, stride=d), :]],
                                    axis=0).astype(jnp.bfloat16)
                v = jnp.concatenate([v_ref[pl.ds(kp, BLK, stride=d), :], v_ref[pl.ds(k0, BLK, stride=d), :]],
                                    axis=0).astype(jnp.bfloat16)
            s = lax.dot_general(q, k, contract_last, preferred_element_type=jnp.float32)
            s = s + jnp.where(first, bias_first, bias)
            m = jnp.max(jnp.maximum(s[:, :BLK], s[:, BLK:]), axis=-1, keepdims=True)
            p = jnp.exp2(s - m).astype(jnp.bfloat16)
            v_one = jnp.concatenate([v, ones], axis=1)
            pv = jnp.dot(p, v_one, preferred_element_type=jnp.float32)
            den = pv[:, HEAD_DIM:]
            rows = pl.ds(q0, BLK, stride=d)
            out_ref[b, rows, :] = pv[:, :HEAD_DIM] / den
            lse_ref[b, rows, :] = m + jnp.log2(den)
            return carry

        lax.fori_loop(0, SPAN // BLK, tile, 0, unroll=ATTN_UNROLL)

    lse_all = jnp.maximum(jnp.maximum(lse_ref[0], lse_ref[1]), lse_ref[2])
    num = jnp.zeros((SPAN, HEAD_DIM), jnp.float32)
    den = jnp.zeros((SPAN, LANES), jnp.float32)
    for b in range(len(BRANCHES)):
        w = jnp.exp2(lse_ref[b] - lse_all)
        num = num + w * out_ref[b]
        den = den + w
    o_ref[...] = (num / den).astype(o_ref.dtype)


def _attention(proj):
    q_col = 0
    k_col = q_col + N_HEADS
    v_col = k_col + N_HEADS
    nb = len(BRANCHES)
    return pl.pallas_call(
        _attn_kernel,
        grid=(N_HEADS, SEQ // SPAN),
        in_specs=[
            pl.BlockSpec((SPAN, HEAD_DIM), lambda h, s: (s, q_col + h)),
            pl.BlockSpec((SEQ, HEAD_DIM), lambda h, s: (0, k_col + h)),
            pl.BlockSpec((SEQ, HEAD_DIM), lambda h, s: (0, v_col + h)),
        ],
        out_specs=pl.BlockSpec((SPAN, HEAD_DIM), lambda h, s: (s, h)),
        out_shape=jax.ShapeDtypeStruct((SEQ, D_ATTN), jnp.bfloat16),
        scratch_shapes=[
            pltpu.VMEM((nb, SPAN, HEAD_DIM), jnp.float32),
            pltpu.VMEM((nb, SPAN, LANES), jnp.float32),
            pltpu.VMEM((2, 2, SPAN, HEAD_DIM), jnp.bfloat16),
        ],
        compiler_params=_params("arbitrary", "arbitrary"),
        name="attn",
    )(proj, proj, proj)


def _split_bf16(x):
    hi = x.astype(jnp.bfloat16)
    lo = (x - hi.astype(jnp.float32)).astype(jnp.bfloat16)
    return hi, lo


def _out_proj_kernel(ya_ref, yb_ref, wa_ref, wb_ref, x_ref, gate_ref, g_ref, b_ref, sc_ref, sh_ref,
                     wr_ref, br_ref, x1_ref, hp_ref, lg_ref):
    mix = jnp.dot(ya_ref[...], wa_ref[...], preferred_element_type=jnp.float32)
    mix = mix + jnp.dot(yb_ref[...], wb_ref[...], preferred_element_type=jnp.float32)
    x1 = _layer_norm(DEEPNORM_ALPHA * x_ref[...] + gate_ref[...] * mix, g_ref[...], b_ref[...])
    x1_ref[...] = x1
    h = x1 * (1.0 + sc_ref[...]) + sh_ref[...]
    _store_packed(hp_ref, (), 0, _pack_rows(h))
    h_hi, h_lo = _split_bf16(h)
    w_hi, w_lo = _split_bf16(wr_ref[...])
    both = jnp.dot(h_hi, jnp.concatenate([w_hi, w_lo], axis=1), preferred_element_type=jnp.float32)
    lg = both[:, :N_EXPERTS] + both[:, N_EXPERTS:] + jnp.dot(h_lo, w_hi, preferred_element_type=jnp.float32)
    lg_ref[...] = lg + br_ref[...]


def _out_proj(y_a, y_b, w_o_bf16, x, gate1, ln_g, ln_b, scale2, shift2, w_router, b_router):
    row = lambda i: (i, 0)
    fixed = lambda i: (0, 0)
    vec = pl.BlockSpec((1, D_MODEL), fixed)
    return pl.pallas_call(
        _out_proj_kernel,
        grid=(SEQ // OUT_TM,),
        in_specs=[
            pl.BlockSpec((OUT_TM, D_GMLP), row),
            pl.BlockSpec((OUT_TM, D_ATTN), row),
            pl.BlockSpec((D_GMLP, D_MODEL), lambda i: (0, 0)),
            pl.BlockSpec((D_ATTN, D_MODEL), lambda i: (1, 0)),
            pl.BlockSpec((OUT_TM, D_MODEL), row),
            vec, vec, vec, vec, vec,
            pl.BlockSpec((D_MODEL, N_EXPERTS), fixed),
            pl.BlockSpec((1, N_EXPERTS), fixed),
        ],
        out_specs=[
            pl.BlockSpec((OUT_TM, D_MODEL), row),
            pl.BlockSpec((OUT_TM * ROW_SUB, LANES), row),
            pl.BlockSpec((OUT_TM, N_EXPERTS), row),
        ],
        out_shape=[
            jax.ShapeDtypeStruct((SEQ, D_MODEL), jnp.float32),
            jax.ShapeDtypeStruct((SEQ * ROW_SUB, LANES), jnp.uint32),
            jax.ShapeDtypeStruct((SEQ, N_EXPERTS), jnp.float32),
        ],
        compiler_params=_params("arbitrary"),
        name="out_proj",
    )(y_a, y_b, w_o_bf16, w_o_bf16, x, gate1, ln_g, ln_b, scale2, shift2, w_router, b_router)


def _experts_kernel(meta_ref, x_hbm, wg_ref, wl_ref, bg_ref, bl_ref, w2_ref, b2_ref, y_hbm,
                    xs_ref, os_ref, acc_ref, gsem, ssem):
    i = pl.program_id(0)
    j = pl.program_id(1)
    n_items = pl.num_programs(0)
    last_j = pl.num_programs(1) - 1

    def item_rows(k):
        inside = (k >= 0) & (k < n_items)
        return jnp.where(inside, meta_ref[MOE_ITEMS + jnp.clip(k, 0, n_items - 1)], 0)

    def item_first(k):
        return meta_ref[2 * MOE_ITEMS + jnp.clip(k, 0, n_items - 1)] * ROW_SUB

    rows = item_rows(i)
    rows_prev = item_rows(i - 1)
    p = i % 2
    q = 1 - p
    group_sub = MOE_G * ROW_SUB

    def each_group(n_rows, fn):
        lax.fori_loop(0, n_rows // MOE_G, lambda t, c: (fn(pl.multiple_of(t * group_sub, group_sub)), c)[1], 0)

    def load_rows(k, slot):
        first = item_first(k)
        each_group(item_rows(k), lambda at: pltpu.make_async_copy(
            x_hbm.at[pl.ds(pl.multiple_of(first + at, group_sub), group_sub), :],
            xs_ref.at[slot, pl.ds(at, group_sub), :], gsem.at[slot]).start())

    def wait_rows(buf_ref, sem, n_rows):
        group = buf_ref.at[pl.ds(0, group_sub), :]
        each_group(n_rows, lambda at: pltpu.make_async_copy(group, group, sem).wait())

    @pl.when((i == 0) & (j == 0))
    def _():
        acc_ref[...] = jnp.zeros_like(acc_ref)
        os_ref[...] = jnp.zeros_like(os_ref)
        zeros = os_ref.at[pl.ds(0, group_sub), :]
        first_unused = meta_ref[3 * MOE_ITEMS]
        n_unused = X_ROWS // MOE_G - first_unused

        def fill(t, c):
            dst = pl.multiple_of((first_unused + t) * group_sub, group_sub)
            pltpu.make_async_copy(zeros, y_hbm.at[pl.ds(dst, group_sub), :], ssem).start()
            return c
        lax.fori_loop(0, n_unused, fill, 0)
        wait_rows(os_ref, ssem, n_unused * MOE_G)
        load_rows(0, 0)

    @pl.when(j == 0)
    def _():
        wait_rows(xs_ref.at[p], gsem.at[p], rows)
        load_rows(i + 1, q)

    def sub_tile(a, m):
        a = pl.multiple_of(a, MOE_G)
        x_lo, x_hi = _unpack_words(_load_packed(xs_ref, (p,), a, m))
        x = jnp.concatenate([x_lo.astype(jnp.bfloat16), x_hi.astype(jnp.bfloat16)], axis=1)

        def up(w_ref, b_ref):
            return jnp.dot(x, w_ref[...].astype(jnp.bfloat16), preferred_element_type=jnp.float32) + b_ref[...]

        glu = jnp.minimum(up(wg_ref, bg_ref), SWIGLU_LIMIT)
        lin = jnp.clip(up(wl_ref, bl_ref), -SWIGLU_LIMIT, SWIGLU_LIMIT)
        act = glu * jax.nn.sigmoid(SWIGLU_ALPHA * glu) * (lin + 1.0)
        down = jnp.dot(act.astype(jnp.bfloat16), w2_ref[...].astype(jnp.bfloat16),
                       preferred_element_type=jnp.float32)
        start = jnp.where(j == 0, jnp.broadcast_to(b2_ref[...], (m, D_MODEL)), acc_ref[pl.ds(a, m), :])
        acc_ref[pl.ds(a, m), :] = start + down

    @pl.when(rows > 0)
    def _():
        main = MOE_SUBTILES[0]
        n_main = rows // main
        lax.fori_loop(0, n_main, lambda t, c: (sub_tile(t * main, main), c)[1], 0)
        done = n_main * main
        for m in MOE_SUBTILES[1:]:
            has = ((rows - done) // m) % 2 == 1
            pl.when(has)(lambda done=done, m=m: sub_tile(done, m))
            done = done + jnp.where(has, m, 0)

    @pl.when(j == last_j)
    def _():
        wait_rows(os_ref, ssem, rows_prev)

        def pack(t, c):
            a = pl.multiple_of(t * MOE_G, MOE_G)
            _store_packed(os_ref, (), a, _pack_rows(acc_ref[pl.ds(a, MOE_G), :]))
            return c
        lax.fori_loop(0, rows // MOE_G, pack, 0)
        first = item_first(i)
        each_group(rows, lambda at: pltpu.make_async_copy(
            os_ref.at[pl.ds(at, group_sub), :],
            y_hbm.at[pl.ds(pl.multiple_of(first + at, group_sub), group_sub), :], ssem).start())


def _experts(x_sorted, meta, w1, b1, w2, b2):
    n_j = MOE_CHUNKS

    def col(i, j, m):
        return jnp.where(m[MOE_ITEMS + i] > 0, j, n_j - 1)

    def expert(i, m):
        return m[i]

    return pl.pallas_call(
        _experts_kernel,
        grid_spec=pltpu.PrefetchScalarGridSpec(
            num_scalar_prefetch=1,
            grid=(MOE_ITEMS, n_j),
            in_specs=[
                pl.BlockSpec(memory_space=pl.ANY),
                pl.BlockSpec((None, D_MODEL, MOE_TN), lambda i, j, m: (expert(i, m), 0, col(i, j, m))),
                pl.BlockSpec((None, D_MODEL, MOE_TN), lambda i, j, m: (expert(i, m), 0, n_j + col(i, j, m))),
                pl.BlockSpec((None, 1, MOE_TN), lambda i, j, m: (expert(i, m), 0, col(i, j, m))),
                pl.BlockSpec((None, 1, MOE_TN), lambda i, j, m: (expert(i, m), 0, n_j + col(i, j, m))),
                pl.BlockSpec((None, MOE_TN, D_MODEL), lambda i, j, m: (expert(i, m), col(i, j, m), 0)),
                pl.BlockSpec((None, 1, D_MODEL), lambda i, j, m: (expert(i, m), 0, 0)),
            ],
            out_specs=pl.BlockSpec(memory_space=pl.ANY),
            scratch_shapes=[
                pltpu.VMEM((2, MOE_RMAX * ROW_SUB, LANES), jnp.uint32),
                pltpu.VMEM((MOE_RMAX * ROW_SUB, LANES), jnp.uint32),
                pltpu.VMEM((MOE_RMAX, D_MODEL), jnp.float32),
                pltpu.SemaphoreType.DMA((2,)),
                pltpu.SemaphoreType.DMA(()),
            ],
        ),
        out_shape=jax.ShapeDtypeStruct((X_ROWS * ROW_SUB, LANES), jnp.uint32),
        compiler_params=_params("arbitrary", "arbitrary"),
        name="experts",
    )(meta, x_sorted, w1, w1,
      b1.reshape(N_EXPERTS, 1, 2 * D_EXPERT), b1.reshape(N_EXPERTS, 1, 2 * D_EXPERT),
      w2, b2.reshape(N_EXPERTS, 1, D_MODEL))


def _combine_kernel(y0_ref, y1_ref, y2_ref, y3_ref, gates_ref, x1_ref, gate2_ref, g_ref, b_ref, o_ref):
    y_lo = jnp.zeros((COMBINE_TM, D_PACK), jnp.float32)
    y_hi = jnp.zeros((COMBINE_TM, D_PACK), jnp.float32)
    for k, y_ref in enumerate((y0_ref, y1_ref, y2_ref, y3_ref)):
        lo, hi = _unpack_words(_load_packed(y_ref, (), 0, COMBINE_TM))
        gate = gates_ref[:, k:k + 1]
        y_lo = y_lo + gate * lo
        y_hi = y_hi + gate * hi
    y = jnp.concatenate([y_lo, y_hi], axis=1)
    o_ref[...] = _layer_norm(DEEPNORM_ALPHA * x1_ref[...] + gate2_ref[...] * y, g_ref[...], b_ref[...])


def _combine(y_packed, gates, x1, gate2, ln_g, ln_b):
    n_tiles = SEQ // COMBINE_TM
    row = lambda i: (i, 0)
    vec = pl.BlockSpec((1, D_MODEL), lambda i: (0, 0))
    slot = lambda k: pl.BlockSpec((COMBINE_TM * ROW_SUB, LANES), lambda i: (k * n_tiles + i, 0))
    return pl.pallas_call(
        _combine_kernel,
        grid=(n_tiles,),
        in_specs=[
            slot(0), slot(1), slot(2), slot(3),
            pl.BlockSpec((COMBINE_TM, TOP_K), row),
            pl.BlockSpec((COMBINE_TM, D_MODEL), row),
            vec, vec, vec,
        ],
        out_specs=pl.BlockSpec((COMBINE_TM, D_MODEL), row),
        out_shape=jax.ShapeDtypeStruct((SEQ, D_MODEL), jnp.float32),
        compiler_params=_params("arbitrary"),
        name="combine",
    )(y_packed, y_packed, y_packed, y_packed, gates, x1, gate2, ln_g, ln_b)


N_GROUPS = X_ROWS // MOE_G
WINDOW_ROWS = ROW_SUB
assert MOE_G == LANES and TOP_K == 4


def _row_tables_kernel(win_ref, order_ref, dst_ref):
    lane = lax.broadcasted_iota(jnp.int32, (1, LANES), 1)

    def table(g, carry):
        w0 = win_ref[g]
        n_real = win_ref[N_GROUPS + g]
        off = w0 % LANES
        x = order_ref[pl.ds(w0 // LANES, WINDOW_ROWS), :]
        x = pltpu.roll(x, (LANES - off) % LANES, axis=1)
        flat = jnp.where(lane < LANES - off, x[0:1], x[1:2])
        tok = flat >> 2
        slot = flat & 3
        dst_ref[pl.ds(g, 1), :] = jnp.where(lane < n_real, slot * SEQ + tok, 0) * ROW_SUB
        return carry
    lax.fori_loop(0, N_GROUPS, table, 0)


def _row_tables(win, order):
    n_rows = SEQ * TOP_K // LANES
    order2d = jnp.concatenate([order, jnp.zeros((WINDOW_ROWS * LANES,), jnp.int32)]).reshape(-1, LANES)
    return pl.pallas_call(
        _row_tables_kernel,
        grid_spec=pltpu.PrefetchScalarGridSpec(
            num_scalar_prefetch=1,
            grid=(1,),
            in_specs=[pl.BlockSpec((n_rows + WINDOW_ROWS, LANES), lambda b, w: (0, 0))],
            out_specs=pl.BlockSpec((N_GROUPS, LANES), lambda b, w: (0, 0)),
        ),
        out_shape=jax.ShapeDtypeStruct((N_GROUPS, LANES), jnp.int32),
        compiler_params=_params("arbitrary"),
        name="row_tables",
    )(win, order2d)


def _collect_kernel(real_ref, dst_ref, y_ref, out_hbm, buf_ref, sem):
    i = pl.program_id(0)
    slot = i % 2
    buf_ref[slot] = y_ref[...]

    def row_copy(s, r):
        at = pl.multiple_of(r * ROW_SUB, ROW_SUB)
        dst = pl.multiple_of(dst_ref[0, 0, r], ROW_SUB)
        return pltpu.make_async_copy(buf_ref.at[s, pl.ds(at, ROW_SUB), :], out_hbm.at[pl.ds(dst, ROW_SUB), :],
                                     sem.at[s])

    group_sub = MOE_G * ROW_SUB

    def rows_loop(n, fn, **kw):
        lax.fori_loop(0, n, lambda r, c: (fn(r), c)[1], 0, **kw)

    def start_tile(step, s):
        for h in range(COLLECT_GROUPS):
            n_real = real_ref[step * COLLECT_GROUPS + h]
            row0 = h * MOE_G

            def start_pair(r2, row0=row0):
                row_copy(s, row0 + 2 * r2).start(priority=0)
                row_copy(s, row0 + 2 * r2 + 1).start(priority=1)

            @pl.when(n_real == MOE_G)
            def _():
                rows_loop(MOE_G // 2, start_pair, unroll=8)

            @pl.when(n_real < MOE_G)
            def _():
                rows_loop(n_real, lambda r, row0=row0: row_copy(s, row0 + r).start())

    def wait_tile(step, s):
        whole = buf_ref.at[s, pl.ds(0, group_sub), :]
        for h in range(COLLECT_GROUPS):
            n_real = real_ref[step * COLLECT_GROUPS + h]

            @pl.when(n_real == MOE_G)
            def _():
                pltpu.make_async_copy(whole, whole, sem.at[s]).wait()

            @pl.when(n_real < MOE_G)
            def _():
                rows_loop(n_real, lambda r: row_copy(s, 0).wait())

    start_tile(i, slot)

    @pl.when(i > 0)
    def _():
        wait_tile(i - 1, 1 - slot)

    @pl.when(i == pl.num_programs(0) - 1)
    def _():
        wait_tile(i, slot)


def _collect(y_sorted, dst_tab, n_real):
    n_tiles = N_GROUPS // COLLECT_GROUPS
    tile_rows = COLLECT_GROUPS * MOE_G
    return pl.pallas_call(
        _collect_kernel,
        grid_spec=pltpu.PrefetchScalarGridSpec(
            num_scalar_prefetch=1,
            grid=(n_tiles,),
            in_specs=[
                pl.BlockSpec((1, 1, tile_rows), lambda i, n: (i, 0, 0), memory_space=pltpu.SMEM),
                pl.BlockSpec((tile_rows * ROW_SUB, LANES), lambda i, n: (i, 0)),
            ],
            out_specs=pl.BlockSpec(memory_space=pl.ANY),
            scratch_shapes=[
                pltpu.VMEM((2, tile_rows * ROW_SUB, LANES), jnp.uint32),
                pltpu.SemaphoreType.DMA((2,)),
            ],
        ),
        out_shape=jax.ShapeDtypeStruct((SEQ * TOP_K * ROW_SUB, LANES), jnp.uint32),
        compiler_params=_params("arbitrary"),
        name="collect",
    )(n_real, dst_tab.reshape(n_tiles, 1, tile_rows), y_sorted)


def _dispatch_kernel(fill_ref, pos_ref, h_ref, x_hbm, zero_ref, buf_ref, sem, zsem):
    i = pl.program_id(0)
    group_sub = MOE_G * ROW_SUB

    @pl.when(i == 0)
    def _():
        zero_ref[...] = jnp.zeros_like(zero_ref)

        def fill(g):
            dst = pl.multiple_of(g * group_sub, group_sub)
            pltpu.make_async_copy(zero_ref, x_hbm.at[pl.ds(dst, group_sub), :], zsem).start()

        for e in range(N_EXPERTS):
            end = fill_ref[e]
            begin = fill_ref[e - 1] if e else 0
            pl.when(end > begin)(lambda end=end: fill(end - 1))
        lax.fori_loop(fill_ref[N_EXPERTS - 1], X_ROWS // MOE_G, lambda g, c: (fill(g), c)[1], 0)

        def wait_fill(t, c):
            pltpu.make_async_copy(zero_ref, zero_ref, zsem).wait()
            return c
        lax.fori_loop(0, fill_ref[N_EXPERTS], wait_fill, 0)

    slot = i % 2
    buf_ref[slot] = h_ref[...]

    def token(t, c):
        src = pl.multiple_of(t * ROW_SUB, ROW_SUB)
        for k in range(TOP_K):
            dst = pl.multiple_of(pos_ref[0, 0, t * TOP_K + k], ROW_SUB)
            pltpu.make_async_copy(buf_ref.at[slot, pl.ds(src, ROW_SUB), :], x_hbm.at[pl.ds(dst, ROW_SUB), :],
                                  sem.at[slot]).start(priority=k % 2)
        return c
    lax.fori_loop(0, DISPATCH_TM, token, 0, unroll=4)

    def wait_tile(s):
        for k in range(TOP_K):
            pltpu.make_async_copy(buf_ref.at[s], buf_ref.at[s], sem.at[s]).wait()

    pl.when(i > 0)(lambda: wait_tile(1 - slot))
    pl.when(i == pl.num_programs(0) - 1)(lambda: wait_tile(slot))


def _dispatch(h_packed, pos, fill):
    n_tiles = SEQ // DISPATCH_TM
    return pl.pallas_call(
        _dispatch_kernel,
        grid_spec=pltpu.PrefetchScalarGridSpec(
            num_scalar_prefetch=1,
            grid=(n_tiles,),
            in_specs=[
                pl.BlockSpec((1, 1, DISPATCH_TM * TOP_K), lambda i, f: (i, 0, 0), memory_space=pltpu.SMEM),
                pl.BlockSpec((DISPATCH_TM * ROW_SUB, LANES), lambda i, f: (i, 0)),
            ],
            out_specs=pl.BlockSpec(memory_space=pl.ANY),
            scratch_shapes=[
                pltpu.VMEM((MOE_G * ROW_SUB, LANES), jnp.uint32),
                pltpu.VMEM((2, DISPATCH_TM * ROW_SUB, LANES), jnp.uint32),
                pltpu.SemaphoreType.DMA((2,)),
                pltpu.SemaphoreType.DMA(()),
            ],
        ),
        out_shape=jax.ShapeDtypeStruct((X_ROWS * ROW_SUB, LANES), jnp.uint32),
        compiler_params=_params("arbitrary"),
        name="dispatch",
    )(fill, pos.reshape(n_tiles, 1, DISPATCH_TM * TOP_K), h_packed)


def _route(logits):
    top_val, top_idx = lax.top_k(logits, TOP_K)
    gates = jax.nn.softmax(top_val, axis=-1)
    e_flat = top_idx.reshape(-1).astype(jnp.int32)
    experts = jnp.arange(N_EXPERTS, dtype=jnp.int32)
    counts = jnp.sum((e_flat[:, None] == experts[None, :]).astype(jnp.int32), axis=0)
    groups = (counts + MOE_G - 1) // MOE_G
    group_end = jnp.cumsum(groups)
    row0 = (group_end - groups) * MOE_G
    per_item = MOE_RMAX // MOE_G
    n_items_e = (groups + per_item - 1) // per_item
    item_end = jnp.cumsum(n_items_e)
    item_start = item_end - n_items_e
    n_items = item_end[-1]

    item = jnp.arange(MOE_ITEMS, dtype=jnp.int32)
    used = item < n_items
    e_item = jnp.minimum(jnp.searchsorted(item_end, item, side='right'), N_EXPERTS - 1).astype(jnp.int32)
    e_last = e_item[jnp.maximum(n_items - 1, 0)]
    part = item - item_start[e_item]
    rows = jnp.where(used, jnp.clip(groups[e_item] - part * per_item, 0, per_item) * MOE_G, 0)
    first_row = jnp.where(used, row0[e_item] + part * MOE_RMAX, 0)
    meta = jnp.concatenate([jnp.where(used, e_item, e_last), rows, first_row, group_end[-1:]]).astype(jnp.int32)

    order = jnp.argsort(e_flat, stable=True).astype(jnp.int32)
    rank = jnp.argsort(order).astype(jnp.int32)
    start = jnp.cumsum(counts) - counts
    shift = jnp.sum(jnp.where(e_flat[:, None] == experts[None, :], (row0 - start)[None, :], 0), axis=1)
    pos = (rank + shift) * ROW_SUB
    n_fill = jnp.sum((groups > 0).astype(jnp.int32)) + X_ROWS // MOE_G - group_end[-1]
    fill = jnp.concatenate([group_end, n_fill[None]]).astype(jnp.int32)

    group = jnp.arange(N_GROUPS, dtype=jnp.int32)
    e_group = jnp.minimum(jnp.sum((group[:, None] >= group_end[None, :]).astype(jnp.int32), axis=1), N_EXPERTS - 1)
    in_use = group < group_end[-1]
    nth = group - (group_end - groups)[e_group]
    window0 = jnp.where(in_use, start[e_group] + nth * MOE_G, 0)
    n_real = jnp.where(in_use, jnp.clip(counts[e_group] - nth * MOE_G, 0, MOE_G), 0).astype(jnp.int32)
    dst_tab = _row_tables(jnp.concatenate([window0, n_real]).astype(jnp.int32), order)
    return gates, meta, pos.astype(jnp.int32), fill, dst_tab, n_real


def kernel(x, c, w_ada, b_ada, w_in, sgu_ln_g, sgu_ln_b, w_spatial, b_spatial, w_o, ln1_g, ln1_b,
           w_router, b_router, w_exp1, b_exp1, w_exp2, b_exp2, ln2_g, ln2_b):
    depth = w_ada.shape[0]
    assert x.shape == (1, SEQ, D_MODEL)
    xs = x.reshape(SEQ, D_MODEL)
    for l in range(depth):
        ada = _ada(c, w_ada[l], b_ada[l])
        shift1, scale1, gate1, shift2, scale2, gate2 = jnp.split(ada, 6, axis=-1)

        qkv, y_a = _in_proj(xs, scale1, shift1, w_in[l].astype(jnp.bfloat16),
                            sgu_ln_g[l], sgu_ln_b[l], w_spatial[l], b_spatial[l])
        y_b = _attention(qkv)
        x1, h_packed, logits = _out_proj(y_a, y_b, w_o[l].astype(jnp.bfloat16), xs, gate1,
                                         ln1_g[l].reshape(1, -1), ln1_b[l].reshape(1, -1), scale2, shift2,
                                         w_router[l], b_router[l].reshape(1, -1))

        gates, meta, pos, fill, dst_tab, n_real = _route(logits)
        x_sorted = _dispatch(h_packed, pos, fill)
        y_sorted = _experts(x_sorted, meta, w_exp1[l], b_exp1[l], w_exp2[l], b_exp2[l])
        y_packed = _collect(y_sorted, dst_tab, n_real)
        xs = _combine(y_packed, gates, x1, gate2, ln2_g[l].reshape(1, -1), ln2_b[l].reshape(1, -1))
    return xs.reshape(x.shape)
```

```python
import math

import jax
import jax.numpy as jnp
from jax import lax
from jax.experimental import pallas as pl
from jax.experimental.pallas import tpu as pltpu

D_MODEL = 2048
SEQ = 8192
D_GMLP = 1024
GMLP_GROUPS = 8
GROUP_DIM = 128
CHUNK = 128
D_ATTN = 1024
HEAD_DIM = 128
N_HEADS = 8
BRANCHES = ((128, 1), (512, 4), (2048, 16))
BLK = 128
SPAN = 16 * BLK
D_IN_PROJ = 2 * D_GMLP + 3 * D_ATTN
N_EXPERTS = 32
TOP_K = 4
D_EXPERT = 2048
SWIGLU_LIMIT = 7.0
SWIGLU_ALPHA = 1.702
LN_EPS = 1e-5
DEEPNORM_ALPHA = 2.0 ** 0.25
NEG = -1e30

LANES = 128
VMEM_LIMIT = 56 * 1024 * 1024

ADA_TN = 1536
PROJ_TM = 1024
PROJ_TN = 1024
OUT_TM = 512
ATTN_UNROLL = 8
ROW_SUB = 8
D_PACK = D_MODEL // 2
MOE_G = 128
MOE_RMAX = 1152
MOE_ITEMS = 64
MOE_TN = 512
MOE_CHUNKS = D_EXPERT // MOE_TN
MOE_SUBTILES = (512, 256, 128)
X_ROWS = SEQ * TOP_K + N_EXPERTS * MOE_G
DISPATCH_TM = 256
COLLECT_GROUPS = 8
COMBINE_TM = 256


def _params(*sem):
    return pltpu.CompilerParams(dimension_semantics=sem, vmem_limit_bytes=VMEM_LIMIT)


def _layer_norm(x, g, b):
    mu = jnp.mean(x, axis=-1, keepdims=True)
    xc = x - mu
    var = jnp.mean(xc * xc, axis=-1, keepdims=True)
    return xc * lax.rsqrt(var + LN_EPS) * g + b


def _pack_rows(x):
    r = x.astype(jnp.bfloat16).astype(jnp.float32)
    bits = lax.bitcast_convert_type(r, jnp.uint32)
    return (bits[:, D_PACK:] & jnp.uint32(0xFFFF0000)) | (bits[:, :D_PACK] >> 16)


def _unpack_words(u):
    lo = lax.bitcast_convert_type(u << 16, jnp.float32)
    hi = lax.bitcast_convert_type(u & jnp.uint32(0xFFFF0000), jnp.float32)
    return lo, hi


def _store_packed(ref, lead, row0, packed):
    m = packed.shape[0]
    for c in range(ROW_SUB):
        rows = pl.ds(row0 * ROW_SUB + c, m, stride=ROW_SUB)
        ref[lead + (rows, slice(None))] = packed[:, c * LANES:(c + 1) * LANES]


def _load_packed(ref, lead, row0, m):
    cols = [ref[lead + (pl.ds(row0 * ROW_SUB + c, m, stride=ROW_SUB), slice(None))] for c in range(ROW_SUB)]
    return jnp.concatenate(cols, axis=1)


def _ada_kernel(c_ref, w_ref, b_ref, o_ref):
    c = c_ref[...]
    s = c * jax.nn.sigmoid(c)
    o_ref[...] = jnp.sum(s * w_ref[...], axis=0, keepdims=True) + b_ref[...]


def _ada(c, w_ada, b_ada):
    n = w_ada.shape[1]
    return pl.pallas_call(
        _ada_kernel,
        grid=(n // ADA_TN,),
        in_specs=[
            pl.BlockSpec((D_MODEL, 1), lambda j: (0, 0)),
            pl.BlockSpec((D_MODEL, ADA_TN), lambda j: (0, j)),
            pl.BlockSpec((1, ADA_TN), lambda j: (0, j)),
        ],
        out_specs=pl.BlockSpec((1, ADA_TN), lambda j: (0, j)),
        out_shape=jax.ShapeDtypeStruct((1, n), jnp.float32),
        compiler_params=_params("arbitrary"),
        name="ada",
    )(c.reshape(D_MODEL, 1), w_ada, b_ada.reshape(1, n))


def _gelu(x):
    return 0.5 * x * (1.0 + lax.erf(x * (1.0 / math.sqrt(2.0))))


GROUPS_PER_STEP = 3


def _in_proj_kernel(x_ref, sc_ref, sh_ref, w_ref, g_ref, b_ref, ws_ref, bs_ref, qkv_ref, ya_ref, h_ref, uv_ref):
    i = pl.program_id(0)
    j = pl.program_id(1)

    @pl.when((i == 0) & (j == 0))
    def _():
        uv_ref[...] = jnp.zeros_like(uv_ref)

    @pl.when(j == 0)
    def _():
        h_ref[...] = (x_ref[...] * (1.0 + sc_ref[...]) + sh_ref[...]).astype(jnp.bfloat16)

    row = lax.broadcasted_iota(jnp.int32, (CHUNK, CHUNK), 0)
    col = lax.broadcasted_iota(jnp.int32, (CHUNK, CHUNK), 1)
    causal = col <= row
    chunks = [slice(c * CHUNK, (c + 1) * CHUNK) for c in range(PROJ_TM // CHUNK)]
    for u in range(GROUPS_PER_STEP):
        g = jnp.clip(GROUPS_PER_STEP * (j - 2) + u, 0, GMLP_GROUPS - 1)
        lanes = pl.ds(pl.multiple_of(g * GROUP_DIM, GROUP_DIM), GROUP_DIM)
        ln_g = g_ref[pl.ds(g, 1), :]
        ln_b = b_ref[pl.ds(g, 1), :]
        v = [_layer_norm(_gelu(uv_ref[1, rows, lanes]), ln_g, ln_b).astype(jnp.bfloat16) for rows in chunks]
        w = jnp.where(causal, ws_ref[g], 0.0).astype(jnp.bfloat16)
        s = jnp.dot(w, jnp.concatenate(v, axis=1), preferred_element_type=jnp.float32)
        for c, rows in enumerate(chunks):
            s_c = s[:, c * GROUP_DIM:(c + 1) * GROUP_DIM] + bs_ref[g]
            ya_ref[rows, lanes] = (_gelu(uv_ref[0, rows, lanes]) * s_c).astype(ya_ref.dtype)

    qkv_ref[...] = jnp.dot(h_ref[...], w_ref[...], preferred_element_type=jnp.float32)

    @pl.when(j < 2)
    def _():
        uv_ref[jnp.minimum(j, 1)] = qkv_ref[...]


def _in_proj(x, scale1, shift1, w_in_bf16, ln_g, ln_b, w_spatial, b_spatial):
    n_uv = 2 * D_GMLP // PROJ_TN
    assert GROUPS_PER_STEP * (D_IN_PROJ // PROJ_TN - n_uv) >= GMLP_GROUPS
    fixed2 = lambda i, j: (0, 0)
    bs_lanes = jnp.broadcast_to(b_spatial[:, :, None], (GMLP_GROUPS, CHUNK, LANES))
    return pl.pallas_call(
        _in_proj_kernel,
        grid=(SEQ // PROJ_TM, D_IN_PROJ // PROJ_TN),
        in_specs=[
            pl.BlockSpec((PROJ_TM, D_MODEL), lambda i, j: (i, 0)),
            pl.BlockSpec((1, D_MODEL), fixed2),
            pl.BlockSpec((1, D_MODEL), fixed2),
            pl.BlockSpec((D_MODEL, PROJ_TN), lambda i, j: (0, j)),
            pl.BlockSpec((GMLP_GROUPS, GROUP_DIM), fixed2),
            pl.BlockSpec((GMLP_GROUPS, GROUP_DIM), fixed2),
            pl.BlockSpec((GMLP_GROUPS, CHUNK, CHUNK), lambda i, j: (0, 0, 0)),
            pl.BlockSpec((GMLP_GROUPS, CHUNK, LANES), lambda i, j: (0, 0, 0)),
        ],
        out_specs=[
            pl.BlockSpec((PROJ_TM, PROJ_TN), lambda i, j: (i, jnp.maximum(j - n_uv, 0))),
            pl.BlockSpec((PROJ_TM, D_GMLP), lambda i, j: (i, 0)),
        ],
        out_shape=[
            jax.ShapeDtypeStruct((SEQ, 3 * D_ATTN), jnp.float32),
            jax.ShapeDtypeStruct((SEQ, D_GMLP), jnp.bfloat16),
        ],
        scratch_shapes=[
            pltpu.VMEM((PROJ_TM, D_MODEL), jnp.bfloat16),
            pltpu.VMEM((2, PROJ_TM, D_GMLP), jnp.float32),
        ],
        compiler_params=_params("arbitrary", "arbitrary"),
        name="in_proj",
    )(x, scale1, shift1, w_in_bf16, ln_g, ln_b, w_spatial, bs_lanes)


def _attn_kernel(q_ref, k_ref, v_ref, o_ref, out_ref, lse_ref, kv_ref):
    head = pl.program_id(0)
    span = pl.program_id(1)
    log2e = 1.0 / math.log(2.0)
    head_no = (jnp.zeros((BLK, 2 * BLK), jnp.int32) + (head + 1)).astype(jnp.float32)
    slope = jnp.exp2(head_no * (-8.0 / N_HEADS)) * log2e
    scale = HEAD_DIM ** -0.5 * log2e
    qi = lax.broadcasted_iota(jnp.int32, (BLK, 2 * BLK), 0)
    ki = lax.broadcasted_iota(jnp.int32, (BLK, 2 * BLK), 1)
    step = qi + BLK - ki
    contract_last = (((1,), (1,)), ((), ()))
    ones = jnp.ones((2 * BLK, LANES), jnp.bfloat16)

    for b, (window, d) in enumerate(BRANCHES):
        assert window // d == BLK
        valid = (step >= 0) & (step <= BLK)
        bias = jnp.where(valid, -slope * d * step.astype(jnp.float32), NEG)
        bias_first = jnp.where(ki >= BLK, bias, NEG)

        per_class = d * BLK == SPAN
        cur = span % 2
        if per_class:
            def stash(r, carry, d=d):
                rows = pl.ds(pl.multiple_of(r * BLK, BLK), BLK)
                kv_ref[0, cur, rows, :] = k_ref[pl.ds(span * SPAN + r, BLK, stride=d), :].astype(jnp.bfloat16)
                kv_ref[1, cur, rows, :] = v_ref[pl.ds(span * SPAN + r, BLK, stride=d), :].astype(jnp.bfloat16)
                return carry
            lax.fori_loop(0, d, stash, 0, unroll=4)

        def tile(t, carry, d=d, b=b, bias=bias, bias_first=bias_first, per_class=per_class):
            r = t % d
            n = t // d
            q0 = n * (BLK * d) + r
            k0 = span * SPAN + q0
            first = k0 < BLK * d
            q = (q_ref[pl.ds(q0, BLK, stride=d), :] * scale).astype(jnp.bfloat16)
            if per_class:
                rows = pl.ds(pl.multiple_of(r * BLK, BLK), BLK)
                prev = jnp.where(first, cur, 1 - cur)
                k = jnp.concatenate([kv_ref[0, prev, rows, :], kv_ref[0, cur, rows, :]], axis=0)
                v = jnp.concatenate([kv_ref[1, prev, rows, :], kv_ref[1, cur, rows, :]], axis=0)
            else:
                kp = jnp.where(first, k0, k0 - BLK * d)
                k = jnp.concatenate([k_ref[pl.ds(kp, BLK, stride=d), :], k_ref[pl.ds(k0, BLK, stride=d), :]],
                                    axis=0).astype(jnp.bfloat16)
                v = jnp.concatenate([v_ref[pl.ds(kp, BLK, stride=d), :], v_ref[pl.ds(k0, BLK, stride=d), :]],
                                    axis=0).astype(jnp.bfloat16)
            s = lax.dot_general(q, k, contract_last, preferred_element_type=jnp.float32)
            s = s + jnp.where(first, bias_first, bias)
            m = jnp.max(jnp.maximum(s[:, :BLK], s[:, BLK:]), axis=-1, keepdims=True)
            p = jnp.exp2(s - m).astype(jnp.bfloat16)
            v_one = jnp.concatenate([v, ones], axis=1)
            pv = jnp.dot(p, v_one, preferred_element_type=jnp.float32)
            den = pv[:, HEAD_DIM:]
            rows = pl.ds(q0, BLK, stride=d)
            out_ref[b, rows, :] = pv[:, :HEAD_DIM] / den
            lse_ref[b, rows, :] = m + jnp.log2(den)
            return carry

        lax.fori_loop(0, SPAN // BLK, tile, 0, unroll=ATTN_UNROLL)

    lse_all = jnp.maximum(jnp.maximum(lse_ref[0], lse_ref[1]), lse_ref[2])
    num = jnp.zeros((SPAN, HEAD_DIM), jnp.float32)
    den = jnp.zeros((SPAN, LANES), jnp.float32)
    for b in range(len(BRANCHES)):
        w = jnp.exp2(lse_ref[b] - lse_all)
        num = num + w * out_ref[b]
        den = den + w
    o_ref[...] = (num / den).astype(o_ref.dtype)


def _attention(proj):
    q_col = 0
    k_col = q_col + N_HEADS
    v_col = k_col + N_HEADS
    nb = len(BRANCHES)
    return pl.pallas_call(
        _attn_kernel,
        grid=(N_HEADS, SEQ // SPAN),
        in_specs=[
            pl.BlockSpec((SPAN, HEAD_DIM), lambda h, s: (s, q_col + h)),
            pl.BlockSpec((SEQ, HEAD_DIM), lambda h, s: (0, k_col + h)),
            pl.BlockSpec((SEQ, HEAD_DIM), lambda h, s: (0, v_col + h)),
        ],
        out_specs=pl.BlockSpec((SPAN, HEAD_DIM), lambda h, s: (s, h)),
        out_shape=jax.ShapeDtypeStruct((SEQ, D_ATTN), jnp.bfloat16),
        scratch_shapes=[
            pltpu.VMEM((nb, SPAN, HEAD_DIM), jnp.float32),
            pltpu.VMEM((nb, SPAN, LANES), jnp.float32),
            pltpu.VMEM((2, 2, SPAN, HEAD_DIM), jnp.bfloat16),
        ],
        compiler_params=_params("arbitrary", "arbitrary"),
        name="attn",
    )(proj, proj, proj)


def _split_bf16(x):
    hi = x.astype(jnp.bfloat16)
    lo = (x - hi.astype(jnp.float32)).astype(jnp.bfloat16)
    return hi, lo


def _out_proj_kernel(ya_ref, yb_ref, wa_ref, wb_ref, x_ref, gate_ref, g_ref, b_ref, sc_ref, sh_ref,
                     wr_ref, br_ref, x1_ref, hp_ref, lg_ref):
    mix = jnp.dot(ya_ref[...], wa_ref[...], preferred_element_type=jnp.float32)
    mix = mix + jnp.dot(yb_ref[...], wb_ref[...], preferred_element_type=jnp.float32)
    x1 = _layer_norm(DEEPNORM_ALPHA * x_ref[...] + gate_ref[...] * mix, g_ref[...], b_ref[...])
    x1_ref[...] = x1
    h = x1 * (1.0 + sc_ref[...]) + sh_ref[...]
    _store_packed(hp_ref, (), 0, _pack_rows(h))
    h_hi, h_lo = _split_bf16(h)
    w_hi, w_lo = _split_bf16(wr_ref[...])
    both = jnp.dot(h_hi, jnp.concatenate([w_hi, w_lo], axis=1), preferred_element_type=jnp.float32)
    lg = both[:, :N_EXPERTS] + both[:, N_EXPERTS:] + jnp.dot(h_lo, w_hi, preferred_element_type=jnp.float32)
    lg_ref[...] = lg + br_ref[...]


def _out_proj(y_a, y_b, w_o_bf16, x, gate1, ln_g, ln_b, scale2, shift2, w_router, b_router):
    row = lambda i: (i, 0)
    fixed = lambda i: (0, 0)
    vec = pl.BlockSpec((1, D_MODEL), fixed)
    return pl.pallas_call(
        _out_proj_kernel,
        grid=(SEQ // OUT_TM,),
        in_specs=[
            pl.BlockSpec((OUT_TM, D_GMLP), row),
            pl.BlockSpec((OUT_TM, D_ATTN), row),
            pl.BlockSpec((D_GMLP, D_MODEL), lambda i: (0, 0)),
            pl.BlockSpec((D_ATTN, D_MODEL), lambda i: (1, 0)),
            pl.BlockSpec((OUT_TM, D_MODEL), row),
            vec, vec, vec, vec, vec,
            pl.BlockSpec((D_MODEL, N_EXPERTS), fixed),
            pl.BlockSpec((1, N_EXPERTS), fixed),
        ],
        out_specs=[
            pl.BlockSpec((OUT_TM, D_MODEL), row),
            pl.BlockSpec((OUT_TM * ROW_SUB, LANES), row),
            pl.BlockSpec((OUT_TM, N_EXPERTS), row),
        ],
        out_shape=[
            jax.ShapeDtypeStruct((SEQ, D_MODEL), jnp.float32),
            jax.ShapeDtypeStruct((SEQ * ROW_SUB, LANES), jnp.uint32),
            jax.ShapeDtypeStruct((SEQ, N_EXPERTS), jnp.float32),
        ],
        compiler_params=_params("arbitrary"),
        name="out_proj",
    )(y_a, y_b, w_o_bf16, w_o_bf16, x, gate1, ln_g, ln_b, scale2, shift2, w_router, b_router)


def _experts_kernel(meta_ref, x_hbm, wg_ref, wl_ref, bg_ref, bl_ref, w2_ref, b2_ref, y_hbm,
                    xs_ref, os_ref, acc_ref, gsem, ssem):
    i = pl.program_id(0)
    j = pl.program_id(1)
    n_items = pl.num_programs(0)
    last_j = pl.num_programs(1) - 1

    def item_rows(k):
        inside = (k >= 0) & (k < n_items)
        return jnp.where(inside, meta_ref[MOE_ITEMS + jnp.clip(k, 0, n_items - 1)], 0)

    def item_first(k):
        return meta_ref[2 * MOE_ITEMS + jnp.clip(k, 0, n_items - 1)] * ROW_SUB

    rows = item_rows(i)
    rows_prev = item_rows(i - 1)
    p = i % 2
    q = 1 - p
    group_sub = MOE_G * ROW_SUB

    def each_group(n_rows, fn):
        lax.fori_loop(0, n_rows // MOE_G, lambda t, c: (fn(pl.multiple_of(t * group_sub, group_sub)), c)[1], 0)

    def load_rows(k, slot):
        first = item_first(k)
        each_group(item_rows(k), lambda at: pltpu.make_async_copy(
            x_hbm.at[pl.ds(pl.multiple_of(first + at, group_sub), group_sub), :],
            xs_ref.at[slot, pl.ds(at, group_sub), :], gsem.at[slot]).start())

    def wait_rows(buf_ref, sem, n_rows):
        group = buf_ref.at[pl.ds(0, group_sub), :]
        each_group(n_rows, lambda at: pltpu.make_async_copy(group, group, sem).wait())

    @pl.when((i == 0) & (j == 0))
    def _():
        acc_ref[...] = jnp.zeros_like(acc_ref)
        os_ref[...] = jnp.zeros_like(os_ref)
        zeros = os_ref.at[pl.ds(0, group_sub), :]
        first_unused = meta_ref[3 * MOE_ITEMS]
        n_unused = X_ROWS // MOE_G - first_unused

        def fill(t, c):
            dst = pl.multiple_of((first_unused + t) * group_sub, group_sub)
            pltpu.make_async_copy(zeros, y_hbm.at[pl.ds(dst, group_sub), :], ssem).start()
            return c
        lax.fori_loop(0, n_unused, fill, 0)
        wait_rows(os_ref, ssem, n_unused * MOE_G)
        load_rows(0, 0)

    @pl.when(j == 0)
    def _():
        wait_rows(xs_ref.at[p], gsem.at[p], rows)
        load_rows(i + 1, q)

    def sub_tile(a, m):
        a = pl.multiple_of(a, MOE_G)
        x_lo, x_hi = _unpack_words(_load_packed(xs_ref, (p,), a, m))
        x = jnp.concatenate([x_lo.astype(jnp.bfloat16), x_hi.astype(jnp.bfloat16)], axis=1)

        def up(w_ref, b_ref):
            return jnp.dot(x, w_ref[...].astype(jnp.bfloat16), preferred_element_type=jnp.float32) + b_ref[...]

        glu = jnp.minimum(up(wg_ref, bg_ref), SWIGLU_LIMIT)
        lin = jnp.clip(up(wl_ref, bl_ref), -SWIGLU_LIMIT, SWIGLU_LIMIT)
        act = glu * jax.nn.sigmoid(SWIGLU_ALPHA * glu) * (lin + 1.0)
        down = jnp.dot(act.astype(jnp.bfloat16), w2_ref[...].astype(jnp.bfloat16),
                       preferred_element_type=jnp.float32)
        start = jnp.where(j == 0, jnp.broadcast_to(b2_ref[...], (m, D_MODEL)), acc_ref[pl.ds(a, m), :])
        acc_ref[pl.ds(a, m), :] = start + down

    @pl.when(rows > 0)
    def _():
        main = MOE_SUBTILES[0]
        n_main = rows // main
        lax.fori_loop(0, n_main, lambda t, c: (sub_tile(t * main, main), c)[1], 0)
        done = n_main * main
        for m in MOE_SUBTILES[1:]:
            has = ((rows - done) // m) % 2 == 1
            pl.when(has)(lambda done=done, m=m: sub_tile(done, m))
            done = done + jnp.where(has, m, 0)

    @pl.when(j == last_j)
    def _():
        wait_rows(os_ref, ssem, rows_prev)

        def pack(t, c):
            a = pl.multiple_of(t * MOE_G, MOE_G)
            _store_packed(os_ref, (), a, _pack_rows(acc_ref[pl.ds(a, MOE_G), :]))
            return c
        lax.fori_loop(0, rows // MOE_G, pack, 0)
        first = item_first(i)
        each_group(rows, lambda at: pltpu.make_async_copy(
            os_ref.at[pl.ds(at, group_sub), :],
            y_hbm.at[pl.ds(pl.multiple_of(first + at, group_sub), group_sub), :], ssem).start())


def _experts(x_sorted, meta, w1, b1, w2, b2):
    n_j = MOE_CHUNKS

    def col(i, j, m):
        return jnp.where(m[MOE_ITEMS + i] > 0, j, n_j - 1)

    def expert(i, m):
        return m[i]

    return pl.pallas_call(
        _experts_kernel,
        grid_spec=pltpu.PrefetchScalarGridSpec(
            num_scalar_prefetch=1,
            grid=(MOE_ITEMS, n_j),
            in_specs=[
                pl.BlockSpec(memory_space=pl.ANY),
                pl.BlockSpec((None, D_MODEL, MOE_TN), lambda i, j, m: (expert(i, m), 0, col(i, j, m))),
                pl.BlockSpec((None, D_MODEL, MOE_TN), lambda i, j, m: (expert(i, m), 0, n_j + col(i, j, m))),
                pl.BlockSpec((None, 1, MOE_TN), lambda i, j, m: (expert(i, m), 0, col(i, j, m))),
                pl.BlockSpec((None, 1, MOE_TN), lambda i, j, m: (expert(i, m), 0, n_j + col(i, j, m))),
                pl.BlockSpec((None, MOE_TN, D_MODEL), lambda i, j, m: (expert(i, m), col(i, j, m), 0)),
                pl.BlockSpec((None, 1, D_MODEL), lambda i, j, m: (expert(i, m), 0, 0)),
            ],
            out_specs=pl.BlockSpec(memory_space=pl.ANY),
            scratch_shapes=[
                pltpu.VMEM((2, MOE_RMAX * ROW_SUB, LANES), jnp.uint32),
                pltpu.VMEM((MOE_RMAX * ROW_SUB, LANES), jnp.uint32),
                pltpu.VMEM((MOE_RMAX, D_MODEL), jnp.float32),
                pltpu.SemaphoreType.DMA((2,)),
                pltpu.SemaphoreType.DMA(()),
            ],
        ),
        out_shape=jax.ShapeDtypeStruct((X_ROWS * ROW_SUB, LANES), jnp.uint32),
        compiler_params=_params("arbitrary", "arbitrary"),
        name="experts",
    )(meta, x_sorted, w1, w1,
      b1.reshape(N_EXPERTS, 1, 2 * D_EXPERT), b1.reshape(N_EXPERTS, 1, 2 * D_EXPERT),
      w2, b2.reshape(N_EXPERTS, 1, D_MODEL))


def _combine_kernel(y0_ref, y1_ref, y2_ref, y3_ref, gates_ref, x1_ref, gate2_ref, g_ref, b_ref, o_ref):
    y_lo = jnp.zeros((COMBINE_TM, D_PACK), jnp.float32)
    y_hi = jnp.zeros((COMBINE_TM, D_PACK), jnp.float32)
    for k, y_ref in enumerate((y0_ref, y1_ref, y2_ref, y3_ref)):
        lo, hi = _unpack_words(_load_packed(y_ref, (), 0, COMBINE_TM))
        gate = gates_ref[:, k:k + 1]
        y_lo = y_lo + gate * lo
        y_hi = y_hi + gate * hi
    y = jnp.concatenate([y_lo, y_hi], axis=1)
    o_ref[...] = _layer_norm(DEEPNORM_ALPHA * x1_ref[...] + gate2_ref[...] * y, g_ref[...], b_ref[...])


def _combine(y_packed, gates, x1, gate2, ln_g, ln_b):
    n_tiles = SEQ // COMBINE_TM
    row = lambda i: (i, 0)
    vec = pl.BlockSpec((1, D_MODEL), lambda i: (0, 0))
    slot = lambda k: pl.BlockSpec((COMBINE_TM * ROW_SUB, LANES), lambda i: (k * n_tiles + i, 0))
    return pl.pallas_call(
        _combine_kernel,
        grid=(n_tiles,),
        in_specs=[
            slot(0), slot(1), slot(2), slot(3),
            pl.BlockSpec((COMBINE_TM, TOP_K), row),
            pl.BlockSpec((COMBINE_TM, D_MODEL), row),
            vec, vec, vec,
        ],
        out_specs=pl.BlockSpec((COMBINE_TM, D_MODEL), row),
        out_shape=jax.ShapeDtypeStruct((SEQ, D_MODEL), jnp.float32),
        compiler_params=_params("arbitrary"),
        name="combine",
    )(y_packed, y_packed, y_packed, y_packed, gates, x1, gate2, ln_g, ln_b)


N_GROUPS = X_ROWS // MOE_G
WINDOW_ROWS = ROW_SUB
assert MOE_G == LANES and TOP_K == 4


def _row_tables_kernel(win_ref, order_ref, dst_ref):
    lane = lax.broadcasted_iota(jnp.int32, (1, LANES), 1)

    def table(g, carry):
        w0 = win_ref[g]
        n_real = win_ref[N_GROUPS + g]
        off = w0 % LANES
        x = order_ref[pl.ds(w0 // LANES, WINDOW_ROWS), :]
        x = pltpu.roll(x, (LANES - off) % LANES, axis=1)
        flat = jnp.where(lane < LANES - off, x[0:1], x[1:2])
        tok = flat >> 2
        slot = flat & 3
        dst_ref[pl.ds(g, 1), :] = jnp.where(lane < n_real, slot * SEQ + tok, 0) * ROW_SUB
        return carry
    lax.fori_loop(0, N_GROUPS, table, 0, unroll=8)


def _row_tables(win, order):
    n_rows = SEQ * TOP_K // LANES
    order2d = jnp.concatenate([order, jnp.zeros((WINDOW_ROWS * LANES,), jnp.int32)]).reshape(-1, LANES)
    return pl.pallas_call(
        _row_tables_kernel,
        grid_spec=pltpu.PrefetchScalarGridSpec(
            num_scalar_prefetch=1,
            grid=(1,),
            in_specs=[pl.BlockSpec((n_rows + WINDOW_ROWS, LANES), lambda b, w: (0, 0))],
            out_specs=pl.BlockSpec((N_GROUPS, LANES), lambda b, w: (0, 0)),
        ),
        out_shape=jax.ShapeDtypeStruct((N_GROUPS, LANES), jnp.int32),
        compiler_params=_params("arbitrary"),
        name="row_tables",
    )(win, order2d)


def _collect_kernel(real_ref, dst_ref, y_ref, out_hbm, buf_ref, sem):
    i = pl.program_id(0)
    slot = i % 2
    buf_ref[slot] = y_ref[...]

    def row_copy(s, r):
        at = pl.multiple_of(r * ROW_SUB, ROW_SUB)
        dst = pl.multiple_of(dst_ref[0, 0, r], ROW_SUB)
        return pltpu.make_async_copy(buf_ref.at[s, pl.ds(at, ROW_SUB), :], out_hbm.at[pl.ds(dst, ROW_SUB), :],
                                     sem.at[s])

    group_sub = MOE_G * ROW_SUB

    def rows_loop(n, fn, **kw):
        lax.fori_loop(0, n, lambda r, c: (fn(r), c)[1], 0, **kw)

    def start_tile(step, s):
        for h in range(COLLECT_GROUPS):
            n_real = real_ref[step * COLLECT_GROUPS + h]
            row0 = h * MOE_G

            def start_pair(r2, row0=row0):
                row_copy(s, row0 + 2 * r2).start(priority=0)
                row_copy(s, row0 + 2 * r2 + 1).start(priority=1)

            @pl.when(n_real == MOE_G)
            def _():
                rows_loop(MOE_G // 2, start_pair, unroll=8)

            @pl.when(n_real < MOE_G)
            def _():
                rows_loop(n_real, lambda r, row0=row0: row_copy(s, row0 + r).start())

    def wait_tile(step, s):
        whole = buf_ref.at[s, pl.ds(0, group_sub), :]
        for h in range(COLLECT_GROUPS):
            n_real = real_ref[step * COLLECT_GROUPS + h]

            @pl.when(n_real == MOE_G)
            def _():
                pltpu.make_async_copy(whole, whole, sem.at[s]).wait()

            @pl.when(n_real < MOE_G)
            def _():
                rows_loop(n_real, lambda r: row_copy(s, 0).wait())

    start_tile(i, slot)

    @pl.when(i > 0)
    def _():
        wait_tile(i - 1, 1 - slot)

    @pl.when(i == pl.num_programs(0) - 1)
    def _():
        wait_tile(i, slot)


def _collect(y_sorted, dst_tab, n_real):
    n_tiles = N_GROUPS // COLLECT_GROUPS
    tile_rows = COLLECT_GROUPS * MOE_G
    return pl.pallas_call(
        _collect_kernel,
        grid_spec=pltpu.PrefetchScalarGridSpec(
            num_scalar_prefetch=1,
            grid=(n_tiles,),
            in_specs=[
                pl.BlockSpec((1, 1, tile_rows), lambda i, n: (i, 0, 0), memory_space=pltpu.SMEM),
                pl.BlockSpec((tile_rows * ROW_SUB, LANES), lambda i, n: (i, 0)),
            ],
            out_specs=pl.BlockSpec(memory_space=pl.ANY),
            scratch_shapes=[
                pltpu.VMEM((2, tile_rows * ROW_SUB, LANES), jnp.uint32),
                pltpu.SemaphoreType.DMA((2,)),
            ],
        ),
        out_shape=jax.ShapeDtypeStruct((SEQ * TOP_K * ROW_SUB, LANES), jnp.uint32),
        compiler_params=_params("arbitrary"),
        name="collect",
    )(n_real, dst_tab.reshape(n_tiles, 1, tile_rows), y_sorted)


def _dispatch_kernel(fill_ref, pos_ref, h_ref, x_hbm, zero_ref, buf_ref, sem, zsem):
    i = pl.program_id(0)
    group_sub = MOE_G * ROW_SUB

    @pl.when(i == 0)
    def _():
        zero_ref[...] = jnp.zeros_like(zero_ref)

        def fill(g):
            dst = pl.multiple_of(g * group_sub, group_sub)
            pltpu.make_async_copy(zero_ref, x_hbm.at[pl.ds(dst, group_sub), :], zsem).start()

        for e in range(N_EXPERTS):
            end = fill_ref[e]
            begin = fill_ref[e - 1] if e else 0
            pl.when(end > begin)(lambda end=end: fill(end - 1))
        lax.fori_loop(fill_ref[N_EXPERTS - 1], X_ROWS // MOE_G, lambda g, c: (fill(g), c)[1], 0)

        def wait_fill(t, c):
            pltpu.make_async_copy(zero_ref, zero_ref, zsem).wait()
            return c
        lax.fori_loop(0, fill_ref[N_EXPERTS], wait_fill, 0)

    slot = i % 2
    buf_ref[slot] = h_ref[...]

    def token(t, c):
        src = pl.multiple_of(t * ROW_SUB, ROW_SUB)
        for k in range(TOP_K):
            dst = pl.multiple_of(pos_ref[0, 0, t * TOP_K + k], ROW_SUB)
            pltpu.make_async_copy(buf_ref.at[slot, pl.ds(src, ROW_SUB), :], x_hbm.at[pl.ds(dst, ROW_SUB), :],
                                  sem.at[slot]).start(priority=k % 2)
        return c
    lax.fori_loop(0, DISPATCH_TM, token, 0, unroll=4)

    def wait_tile(s):
        for k in range(TOP_K):
            pltpu.make_async_copy(buf_ref.at[s], buf_ref.at[s], sem.at[s]).wait()

    pl.when(i > 0)(lambda: wait_tile(1 - slot))
    pl.when(i == pl.num_programs(0) - 1)(lambda: wait_tile(slot))


def _dispatch(h_packed, pos, fill):
    n_tiles = SEQ // DISPATCH_TM
    return pl.pallas_call(
        _dispatch_kernel,
        grid_spec=pltpu.PrefetchScalarGridSpec(
            num_scalar_prefetch=1,
            grid=(n_tiles,),
            in_specs=[
                pl.BlockSpec((1, 1, DISPATCH_TM * TOP_K), lambda i, f: (i, 0, 0), memory_space=pltpu.SMEM),
                pl.BlockSpec((DISPATCH_TM * ROW_SUB, LANES), lambda i, f: (i, 0)),
            ],
            out_specs=pl.BlockSpec(memory_space=pl.ANY),
            scratch_shapes=[
                pltpu.VMEM((MOE_G * ROW_SUB, LANES), jnp.uint32),
                pltpu.VMEM((2, DISPATCH_TM * ROW_SUB, LANES), jnp.uint32),
                pltpu.SemaphoreType.DMA((2,)),
                pltpu.SemaphoreType.DMA(()),
            ],
        ),
        out_shape=jax.ShapeDtypeStruct((X_ROWS * ROW_SUB, LANES), jnp.uint32),
        compiler_params=_params("arbitrary"),
        name="dispatch",
    )(fill, pos.reshape(n_tiles, 1, DISPATCH_TM * TOP_K), h_packed)


def _route(logits):
    top_val, top_idx = lax.top_k(logits, TOP_K)
    gates = jax.nn.softmax(top_val, axis=-1)
    e_flat = top_idx.reshape(-1).astype(jnp.int32)
    experts = jnp.arange(N_EXPERTS, dtype=jnp.int32)
    counts = jnp.sum((e_flat[:, None] == experts[None, :]).astype(jnp.int32), axis=0)
    groups = (counts + MOE_G - 1) // MOE_G
    group_end = jnp.cumsum(groups)
    row0 = (group_end - groups) * MOE_G
    per_item = MOE_RMAX // MOE_G
    n_items_e = (groups + per_item - 1) // per_item
    item_end = jnp.cumsum(n_items_e)
    item_start = item_end - n_items_e
    n_items = item_end[-1]

    item = jnp.arange(MOE_ITEMS, dtype=jnp.int32)
    used = item < n_items
    e_item = jnp.minimum(jnp.searchsorted(item_end, item, side='right'), N_EXPERTS - 1).astype(jnp.int32)
    e_last = e_item[jnp.maximum(n_items - 1, 0)]
    part = item - item_start[e_item]
    rows = jnp.where(used, jnp.clip(groups[e_item] - part * per_item, 0, per_item) * MOE_G, 0)
    first_row = jnp.where(used, row0[e_item] + part * MOE_RMAX, 0)
    meta = jnp.concatenate([jnp.where(used, e_item, e_last), rows, first_row, group_end[-1:]]).astype(jnp.int32)

    order = jnp.argsort(e_flat, stable=True).astype(jnp.int32)
    rank = jnp.argsort(order).astype(jnp.int32)
    start = jnp.cumsum(counts) - counts
    shift = jnp.sum(jnp.where(e_flat[:, None] == experts[None, :], (row0 - start)[None, :], 0), axis=1)
    pos = (rank + shift) * ROW_SUB
    n_fill = jnp.sum((groups > 0).astype(jnp.int32)) + X_ROWS // MOE_G - group_end[-1]
    fill = jnp.concatenate([group_end, n_fill[None]]).astype(jnp.int32)

    group = jnp.arange(N_GROUPS, dtype=jnp.int32)
    mine = (group[:, None] >= (group_end - groups)[None, :]) & (group[:, None] < group_end[None, :])
    pick = lambda per_expert: jnp.sum(jnp.where(mine, per_expert[None, :], 0), axis=1)
    in_use = group < group_end[-1]
    nth = group - pick(group_end - groups)
    window0 = jnp.where(in_use, pick(start) + nth * MOE_G, 0)
    n_real = jnp.where(in_use, jnp.clip(pick(counts) - nth * MOE_G, 0, MOE_G), 0).astype(jnp.int32)
    dst_tab = _row_tables(jnp.concatenate([window0, n_real]).astype(jnp.int32), order)
    return gates, meta, pos.astype(jnp.int32), fill, dst_tab, n_real


def kernel(x, c, w_ada, b_ada, w_in, sgu_ln_g, sgu_ln_b, w_spatial, b_spatial, w_o, ln1_g, ln1_b,
           w_router, b_router, w_exp1, b_exp1, w_exp2, b_exp2, ln2_g, ln2_b):
    depth = w_ada.shape[0]
    assert x.shape == (1, SEQ, D_MODEL)
    xs = x.reshape(SEQ, D_MODEL)
    for l in range(depth):
        ada = _ada(c, w_ada[l], b_ada[l])
        shift1, scale1, gate1, shift2, scale2, gate2 = jnp.split(ada, 6, axis=-1)

        qkv, y_a = _in_proj(xs, scale1, shift1, w_in[l].astype(jnp.bfloat16),
                            sgu_ln_g[l], sgu_ln_b[l], w_spatial[l], b_spatial[l])
        y_b = _attention(qkv)
        x1, h_packed, logits = _out_proj(y_a, y_b, w_o[l].astype(jnp.bfloat16), xs, gate1,
                                         ln1_g[l].reshape(1, -1), ln1_b[l].reshape(1, -1), scale2, shift2,
                                         w_router[l], b_router[l].reshape(1, -1))

        gates, meta, pos, fill, dst_tab, n_real = _route(logits)
        x_sorted = _dispatch(h_packed, pos, fill)
        y_sorted = _experts(x_sorted, meta, w_exp1[l], b_exp1[l], w_exp2[l], b_exp2[l])
        y_packed = _collect(y_sorted, dst_tab, n_real)
        xs = _combine(y_packed, gates, x1, gate2, ln2_g[l].reshape(1, -1), ln2_b[l].reshape(1, -1))
    return xs.reshape(x.shape)
```

```python
import math

import jax
import jax.numpy as jnp
from jax import lax
from jax.experimental import pallas as pl
from jax.experimental.pallas import tpu as pltpu

D_MODEL = 2048
SEQ = 8192
D_GMLP = 1024
GMLP_GROUPS = 8
GROUP_DIM = 128
CHUNK = 128
D_ATTN = 1024
HEAD_DIM = 128
N_HEADS = 8
BRANCHES = ((128, 1), (512, 4), (2048, 16))
BLK = 128
SPAN = 16 * BLK
D_IN_PROJ = 2 * D_GMLP + 3 * D_ATTN
N_EXPERTS = 32
TOP_K = 4
D_EXPERT = 2048
SWIGLU_LIMIT = 7.0
SWIGLU_ALPHA = 1.702
LN_EPS = 1e-5
DEEPNORM_ALPHA = 2.0 ** 0.25
NEG = -1e30

LANES = 128
VMEM_LIMIT = 56 * 1024 * 1024

ADA_TN = 1536
PROJ_TM = 1024
PROJ_TN = 1024
OUT_TM = 512
ATTN_UNROLL = 16
ROW_SUB = 8
D_PACK = D_MODEL // 2
MOE_G = 128
MOE_RMAX = 1152
MOE_ITEMS = 64
MOE_TN = 512
MOE_CHUNKS = D_EXPERT // MOE_TN
MOE_SUBTILES = (512, 256, 128)
X_ROWS = SEQ * TOP_K + N_EXPERTS * MOE_G
DISPATCH_TM = 256
COLLECT_GROUPS = 8
COMBINE_TM = 256


def _params(*sem):
    return pltpu.CompilerParams(dimension_semantics=sem, vmem_limit_bytes=VMEM_LIMIT)


def _layer_norm(x, g, b):
    mu = jnp.mean(x, axis=-1, keepdims=True)
    xc = x - mu
    var = jnp.mean(xc * xc, axis=-1, keepdims=True)
    return xc * lax.rsqrt(var + LN_EPS) * g + b


def _pack_rows(x):
    r = x.astype(jnp.bfloat16).astype(jnp.float32)
    bits = lax.bitcast_convert_type(r, jnp.uint32)
    return (bits[:, D_PACK:] & jnp.uint32(0xFFFF0000)) | (bits[:, :D_PACK] >> 16)


def _unpack_words(u):
    lo = lax.bitcast_convert_type(u << 16, jnp.float32)
    hi = lax.bitcast_convert_type(u & jnp.uint32(0xFFFF0000), jnp.float32)
    return lo, hi


def _store_packed(ref, lead, row0, packed):
    m = packed.shape[0]
    for c in range(ROW_SUB):
        rows = pl.ds(row0 * ROW_SUB + c, m, stride=ROW_SUB)
        ref[lead + (rows, slice(None))] = packed[:, c * LANES:(c + 1) * LANES]


def _load_packed(ref, lead, row0, m):
    cols = [ref[lead + (pl.ds(row0 * ROW_SUB + c, m, stride=ROW_SUB), slice(None))] for c in range(ROW_SUB)]
    return jnp.concatenate(cols, axis=1)


def _ada_kernel(c_ref, w_ref, b_ref, o_ref):
    c = c_ref[...]
    s = c * jax.nn.sigmoid(c)
    o_ref[...] = jnp.sum(s * w_ref[...], axis=0, keepdims=True) + b_ref[...]


def _ada(c, w_ada, b_ada):
    n = w_ada.shape[1]
    return pl.pallas_call(
        _ada_kernel,
        grid=(n // ADA_TN,),
        in_specs=[
            pl.BlockSpec((D_MODEL, 1), lambda j: (0, 0)),
            pl.BlockSpec((D_MODEL, ADA_TN), lambda j: (0, j)),
            pl.BlockSpec((1, ADA_TN), lambda j: (0, j)),
        ],
        out_specs=pl.BlockSpec((1, ADA_TN), lambda j: (0, j)),
        out_shape=jax.ShapeDtypeStruct((1, n), jnp.float32),
        compiler_params=_params("arbitrary"),
        name="ada",
    )(c.reshape(D_MODEL, 1), w_ada, b_ada.reshape(1, n))


def _gelu(x):
    return 0.5 * x * (1.0 + lax.erf(x * (1.0 / math.sqrt(2.0))))


GROUPS_PER_STEP = 3


def _in_proj_kernel(x_ref, sc_ref, sh_ref, w_ref, g_ref, b_ref, ws_ref, bs_ref, qkv_ref, ya_ref, h_ref, uv_ref):
    i = pl.program_id(0)
    j = pl.program_id(1)

    @pl.when((i == 0) & (j == 0))
    def _():
        uv_ref[...] = jnp.zeros_like(uv_ref)

    @pl.when(j == 0)
    def _():
        h_ref[...] = (x_ref[...] * (1.0 + sc_ref[...]) + sh_ref[...]).astype(jnp.bfloat16)

    row = lax.broadcasted_iota(jnp.int32, (CHUNK, CHUNK), 0)
    col = lax.broadcasted_iota(jnp.int32, (CHUNK, CHUNK), 1)
    causal = col <= row
    chunks = [slice(c * CHUNK, (c + 1) * CHUNK) for c in range(PROJ_TM // CHUNK)]
    for u in range(GROUPS_PER_STEP):
        g = jnp.clip(GROUPS_PER_STEP * (j - 2) + u, 0, GMLP_GROUPS - 1)
        lanes = pl.ds(pl.multiple_of(g * GROUP_DIM, GROUP_DIM), GROUP_DIM)
        ln_g = g_ref[pl.ds(g, 1), :]
        ln_b = b_ref[pl.ds(g, 1), :]
        v = [_layer_norm(_gelu(uv_ref[1, rows, lanes]), ln_g, ln_b).astype(jnp.bfloat16) for rows in chunks]
        w = jnp.where(causal, ws_ref[g], 0.0).astype(jnp.bfloat16)
        s = jnp.dot(w, jnp.concatenate(v, axis=1), preferred_element_type=jnp.float32)
        for c, rows in enumerate(chunks):
            s_c = s[:, c * GROUP_DIM:(c + 1) * GROUP_DIM] + bs_ref[g]
            ya_ref[rows, lanes] = (_gelu(uv_ref[0, rows, lanes]) * s_c).astype(ya_ref.dtype)

    qkv_ref[...] = jnp.dot(h_ref[...], w_ref[...], preferred_element_type=jnp.float32)

    @pl.when(j < 2)
    def _():
        uv_ref[jnp.minimum(j, 1)] = qkv_ref[...]


def _in_proj(x, scale1, shift1, w_in_bf16, ln_g, ln_b, w_spatial, b_spatial):
    n_uv = 2 * D_GMLP // PROJ_TN
    assert GROUPS_PER_STEP * (D_IN_PROJ // PROJ_TN - n_uv) >= GMLP_GROUPS
    fixed2 = lambda i, j: (0, 0)
    bs_lanes = jnp.broadcast_to(b_spatial[:, :, None], (GMLP_GROUPS, CHUNK, LANES))
    return pl.pallas_call(
        _in_proj_kernel,
        grid=(SEQ // PROJ_TM, D_IN_PROJ // PROJ_TN),
        in_specs=[
            pl.BlockSpec((PROJ_TM, D_MODEL), lambda i, j: (i, 0)),
            pl.BlockSpec((1, D_MODEL), fixed2),
            pl.BlockSpec((1, D_MODEL), fixed2),
            pl.BlockSpec((D_MODEL, PROJ_TN), lambda i, j: (0, j)),
            pl.BlockSpec((GMLP_GROUPS, GROUP_DIM), fixed2),
            pl.BlockSpec((GMLP_GROUPS, GROUP_DIM), fixed2),
            pl.BlockSpec((GMLP_GROUPS, CHUNK, CHUNK), lambda i, j: (0, 0, 0)),
            pl.BlockSpec((GMLP_GROUPS, CHUNK, LANES), lambda i, j: (0, 0, 0)),
        ],
        out_specs=[
            pl.BlockSpec((PROJ_TM, PROJ_TN), lambda i, j: (i, jnp.maximum(j - n_uv, 0))),
            pl.BlockSpec((PROJ_TM, D_GMLP), lambda i, j: (i, 0)),
        ],
        out_shape=[
            jax.ShapeDtypeStruct((SEQ, 3 * D_ATTN), jnp.float32),
            jax.ShapeDtypeStruct((SEQ, D_GMLP), jnp.bfloat16),
        ],
        scratch_shapes=[
            pltpu.VMEM((PROJ_TM, D_MODEL), jnp.bfloat16),
            pltpu.VMEM((2, PROJ_TM, D_GMLP), jnp.float32),
        ],
        compiler_params=_params("arbitrary", "arbitrary"),
        name="in_proj",
    )(x, scale1, shift1, w_in_bf16, ln_g, ln_b, w_spatial, bs_lanes)


def _attn_kernel(q_ref, k_ref, v_ref, o_ref, out_ref, lse_ref, kv_ref):
    head = pl.program_id(0)
    span = pl.program_id(1)
    log2e = 1.0 / math.log(2.0)
    head_no = (jnp.zeros((BLK, 2 * BLK), jnp.int32) + (head + 1)).astype(jnp.float32)
    slope = jnp.exp2(head_no * (-8.0 / N_HEADS)) * log2e
    scale = HEAD_DIM ** -0.5 * log2e
    qi = lax.broadcasted_iota(jnp.int32, (BLK, 2 * BLK), 0)
    ki = lax.broadcasted_iota(jnp.int32, (BLK, 2 * BLK), 1)
    step = qi + BLK - ki
    contract_last = (((1,), (1,)), ((), ()))
    ones = jnp.ones((2 * BLK, LANES), jnp.bfloat16)

    for b, (window, d) in enumerate(BRANCHES):
        assert window // d == BLK
        valid = (step >= 0) & (step <= BLK)
        bias = jnp.where(valid, -slope * d * step.astype(jnp.float32), NEG)
        bias_first = jnp.where(ki >= BLK, bias, NEG)

        per_class = d * BLK == SPAN
        cur = span % 2
        if per_class:
            def stash(r, carry, d=d):
                rows = pl.ds(pl.multiple_of(r * BLK, BLK), BLK)
                kv_ref[0, cur, rows, :] = k_ref[pl.ds(span * SPAN + r, BLK, stride=d), :].astype(jnp.bfloat16)
                kv_ref[1, cur, rows, :] = v_ref[pl.ds(span * SPAN + r, BLK, stride=d), :].astype(jnp.bfloat16)
                return carry
            lax.fori_loop(0, d, stash, 0, unroll=4)

        def tile(t, carry, d=d, b=b, bias=bias, bias_first=bias_first, per_class=per_class):
            r = t % d
            n = t // d
            q0 = n * (BLK * d) + r
            k0 = span * SPAN + q0
            first = k0 < BLK * d
            q = (q_ref[pl.ds(q0, BLK, stride=d), :] * scale).astype(jnp.bfloat16)
            if per_class:
                rows = pl.ds(pl.multiple_of(r * BLK, BLK), BLK)
                prev = jnp.where(first, cur, 1 - cur)
                k = jnp.concatenate([kv_ref[0, prev, rows, :], kv_ref[0, cur, rows, :]], axis=0)
                v = jnp.concatenate([kv_ref[1, prev, rows, :], kv_ref[1, cur, rows, :]], axis=0)
            else:
                kp = jnp.where(first, k0, k0 - BLK * d)
                k = jnp.concatenate([k_ref[pl.ds(kp, BLK, stride=d), :], k_ref[pl.ds(k0, BLK, stride=d), :]],
                                    axis=0).astype(jnp.bfloat16)
                v = jnp.concatenate([v_ref[pl.ds(kp, BLK, stride=d), :], v_ref[pl.ds(k0, BLK, stride=d), :]],
                                    axis=0).astype(jnp.bfloat16)
            s = lax.dot_general(q, k, contract_last, preferred_element_type=jnp.float32)
            s = s + jnp.where(first, bias_first, bias)
            m = jnp.max(jnp.maximum(s[:, :BLK], s[:, BLK:]), axis=-1, keepdims=True)
            p = jnp.exp2(s - m).astype(jnp.bfloat16)
            v_one = jnp.concatenate([v, ones], axis=1)
            pv = jnp.dot(p, v_one, preferred_element_type=jnp.float32)
            den = pv[:, HEAD_DIM:]
            rows = pl.ds(q0, BLK, stride=d)
            out_ref[b, rows, :] = pv[:, :HEAD_DIM] / den
            lse_ref[b, rows, :] = m + jnp.log2(den)
            return carry

        lax.fori_loop(0, SPAN // BLK, tile, 0, unroll=ATTN_UNROLL)

    lse_all = jnp.maximum(jnp.maximum(lse_ref[0], lse_ref[1]), lse_ref[2])
    num = jnp.zeros((SPAN, HEAD_DIM), jnp.float32)
    den = jnp.zeros((SPAN, LANES), jnp.float32)
    for b in range(len(BRANCHES)):
        w = jnp.exp2(lse_ref[b] - lse_all)
        num = num + w * out_ref[b]
        den = den + w
    o_ref[...] = (num / den).astype(o_ref.dtype)


def _attention(proj):
    q_col = 0
    k_col = q_col + N_HEADS
    v_col = k_col + N_HEADS
    nb = len(BRANCHES)
    return pl.pallas_call(
        _attn_kernel,
        grid=(N_HEADS, SEQ // SPAN),
        in_specs=[
            pl.BlockSpec((SPAN, HEAD_DIM), lambda h, s: (s, q_col + h)),
            pl.BlockSpec((SEQ, HEAD_DIM), lambda h, s: (0, k_col + h)),
            pl.BlockSpec((SEQ, HEAD_DIM), lambda h, s: (0, v_col + h)),
        ],
        out_specs=pl.BlockSpec((SPAN, HEAD_DIM), lambda h, s: (s, h)),
        out_shape=jax.ShapeDtypeStruct((SEQ, D_ATTN), jnp.bfloat16),
        scratch_shapes=[
            pltpu.VMEM((nb, SPAN, HEAD_DIM), jnp.float32),
            pltpu.VMEM((nb, SPAN, LANES), jnp.float32),
            pltpu.VMEM((2, 2, SPAN, HEAD_DIM), jnp.bfloat16),
        ],
        compiler_params=_params("arbitrary", "arbitrary"),
        name="attn",
    )(proj, proj, proj)


def _split_bf16(x):
    hi = x.astype(jnp.bfloat16)
    lo = (x - hi.astype(jnp.float32)).astype(jnp.bfloat16)
    return hi, lo


def _out_proj_kernel(ya_ref, yb_ref, wa_ref, wb_ref, x_ref, gate_ref, g_ref, b_ref, sc_ref, sh_ref,
                     wr_ref, br_ref, x1_ref, hp_ref, lg_ref):
    mix = jnp.dot(ya_ref[...], wa_ref[...], preferred_element_type=jnp.float32)
    mix = mix + jnp.dot(yb_ref[...], wb_ref[...], preferred_element_type=jnp.float32)
    x1 = _layer_norm(DEEPNORM_ALPHA * x_ref[...] + gate_ref[...] * mix, g_ref[...], b_ref[...])
    x1_ref[...] = x1
    h = x1 * (1.0 + sc_ref[...]) + sh_ref[...]
    _store_packed(hp_ref, (), 0, _pack_rows(h))
    h_hi, h_lo = _split_bf16(h)
    w_hi, w_lo = _split_bf16(wr_ref[...])
    both = jnp.dot(h_hi, jnp.concatenate([w_hi, w_lo], axis=1), preferred_element_type=jnp.float32)
    lg = both[:, :N_EXPERTS] + both[:, N_EXPERTS:] + jnp.dot(h_lo, w_hi, preferred_element_type=jnp.float32)
    lg_ref[...] = lg + br_ref[...]


def _out_proj(y_a, y_b, w_o_bf16, x, gate1, ln_g, ln_b, scale2, shift2, w_router, b_router):
    row = lambda i: (i, 0)
    fixed = lambda i: (0, 0)
    vec = pl.BlockSpec((1, D_MODEL), fixed)
    return pl.pallas_call(
        _out_proj_kernel,
        grid=(SEQ // OUT_TM,),
        in_specs=[
            pl.BlockSpec((OUT_TM, D_GMLP), row),
            pl.BlockSpec((OUT_TM, D_ATTN), row),
            pl.BlockSpec((D_GMLP, D_MODEL), lambda i: (0, 0)),
            pl.BlockSpec((D_ATTN, D_MODEL), lambda i: (1, 0)),
            pl.BlockSpec((OUT_TM, D_MODEL), row),
            vec, vec, vec, vec, vec,
            pl.BlockSpec((D_MODEL, N_EXPERTS), fixed),
            pl.BlockSpec((1, N_EXPERTS), fixed),
        ],
        out_specs=[
            pl.BlockSpec((OUT_TM, D_MODEL), row),
            pl.BlockSpec((OUT_TM * ROW_SUB, LANES), row),
            pl.BlockSpec((OUT_TM, N_EXPERTS), row),
        ],
        out_shape=[
            jax.ShapeDtypeStruct((SEQ, D_MODEL), jnp.float32),
            jax.ShapeDtypeStruct((SEQ * ROW_SUB, LANES), jnp.uint32),
            jax.ShapeDtypeStruct((SEQ, N_EXPERTS), jnp.float32),
        ],
        compiler_params=_params("arbitrary"),
        name="out_proj",
    )(y_a, y_b, w_o_bf16, w_o_bf16, x, gate1, ln_g, ln_b, scale2, shift2, w_router, b_router)


def _experts_kernel(meta_ref, x_hbm, wg_ref, wl_ref, bg_ref, bl_ref, w2_ref, b2_ref, y_hbm,
                    xs_ref, os_ref, acc_ref, gsem, ssem):
    i = pl.program_id(0)
    j = pl.program_id(1)
    n_items = pl.num_programs(0)
    last_j = pl.num_programs(1) - 1

    def item_rows(k):
        inside = (k >= 0) & (k < n_items)
        return jnp.where(inside, meta_ref[MOE_ITEMS + jnp.clip(k, 0, n_items - 1)], 0)

    def item_first(k):
        return meta_ref[2 * MOE_ITEMS + jnp.clip(k, 0, n_items - 1)] * ROW_SUB

    rows = item_rows(i)
    rows_prev = item_rows(i - 1)
    p = i % 2
    q = 1 - p
    group_sub = MOE_G * ROW_SUB

    def each_group(n_rows, fn):
        lax.fori_loop(0, n_rows // MOE_G, lambda t, c: (fn(pl.multiple_of(t * group_sub, group_sub)), c)[1], 0)

    def load_rows(k, slot):
        first = item_first(k)
        each_group(item_rows(k), lambda at: pltpu.make_async_copy(
            x_hbm.at[pl.ds(pl.multiple_of(first + at, group_sub), group_sub), :],
            xs_ref.at[slot, pl.ds(at, group_sub), :], gsem.at[slot]).start())

    def wait_rows(buf_ref, sem, n_rows):
        group = buf_ref.at[pl.ds(0, group_sub), :]
        each_group(n_rows, lambda at: pltpu.make_async_copy(group, group, sem).wait())

    @pl.when((i == 0) & (j == 0))
    def _():
        acc_ref[...] = jnp.zeros_like(acc_ref)
        os_ref[...] = jnp.zeros_like(os_ref)
        zeros = os_ref.at[pl.ds(0, group_sub), :]
        first_unused = meta_ref[3 * MOE_ITEMS]
        n_unused = X_ROWS // MOE_G - first_unused

        def fill(t, c):
            dst = pl.multiple_of((first_unused + t) * group_sub, group_sub)
            pltpu.make_async_copy(zeros, y_hbm.at[pl.ds(dst, group_sub), :], ssem).start()
            return c
        lax.fori_loop(0, n_unused, fill, 0)
        wait_rows(os_ref, ssem, n_unused * MOE_G)
        load_rows(0, 0)

    @pl.when(j == 0)
    def _():
        wait_rows(xs_ref.at[p], gsem.at[p], rows)
        load_rows(i + 1, q)

    def sub_tile(a, m):
        a = pl.multiple_of(a, MOE_G)
        x_lo, x_hi = _unpack_words(_load_packed(xs_ref, (p,), a, m))
        x = jnp.concatenate([x_lo.astype(jnp.bfloat16), x_hi.astype(jnp.bfloat16)], axis=1)

        def up(w_ref, b_ref):
            return jnp.dot(x, w_ref[...].astype(jnp.bfloat16), preferred_element_type=jnp.float32) + b_ref[...]

        glu = jnp.minimum(up(wg_ref, bg_ref), SWIGLU_LIMIT)
        lin = jnp.clip(up(wl_ref, bl_ref), -SWIGLU_LIMIT, SWIGLU_LIMIT)
        act = glu * jax.nn.sigmoid(SWIGLU_ALPHA * glu) * (lin + 1.0)
        down = jnp.dot(act.astype(jnp.bfloat16), w2_ref[...].astype(jnp.bfloat16),
                       preferred_element_type=jnp.float32)
        start = jnp.where(j == 0, jnp.broadcast_to(b2_ref[...], (m, D_MODEL)), acc_ref[pl.ds(a, m), :])
        acc_ref[pl.ds(a, m), :] = start + down

    @pl.when(rows > 0)
    def _():
        main = MOE_SUBTILES[0]
        n_main = rows // main
        lax.fori_loop(0, n_main, lambda t, c: (sub_tile(t * main, main), c)[1], 0)
        done = n_main * main
        for m in MOE_SUBTILES[1:]:
            has = ((rows - done) // m) % 2 == 1
            pl.when(has)(lambda done=done, m=m: sub_tile(done, m))
            done = done + jnp.where(has, m, 0)

    @pl.when(j == last_j)
    def _():
        wait_rows(os_ref, ssem, rows_prev)

        def pack(t, c):
            a = pl.multiple_of(t * MOE_G, MOE_G)
            _store_packed(os_ref, (), a, _pack_rows(acc_ref[pl.ds(a, MOE_G), :]))
            return c
        lax.fori_loop(0, rows // MOE_G, pack, 0)
        first = item_first(i)
        each_group(rows, lambda at: pltpu.make_async_copy(
            os_ref.at[pl.ds(at, group_sub), :],
            y_hbm.at[pl.ds(pl.multiple_of(first + at, group_sub), group_sub), :], ssem).start())


def _experts(x_sorted, meta, w1, b1, w2, b2):
    n_j = MOE_CHUNKS

    def col(i, j, m):
        return jnp.where(m[MOE_ITEMS + i] > 0, j, n_j - 1)

    def expert(i, m):
        return m[i]

    return pl.pallas_call(
        _experts_kernel,
        grid_spec=pltpu.PrefetchScalarGridSpec(
            num_scalar_prefetch=1,
            grid=(MOE_ITEMS, n_j),
            in_specs=[
                pl.BlockSpec(memory_space=pl.ANY),
                pl.BlockSpec((None, D_MODEL, MOE_TN), lambda i, j, m: (expert(i, m), 0, col(i, j, m))),
                pl.BlockSpec((None, D_MODEL, MOE_TN), lambda i, j, m: (expert(i, m), 0, n_j + col(i, j, m))),
                pl.BlockSpec((None, 1, MOE_TN), lambda i, j, m: (expert(i, m), 0, col(i, j, m))),
                pl.BlockSpec((None, 1, MOE_TN), lambda i, j, m: (expert(i, m), 0, n_j + col(i, j, m))),
                pl.BlockSpec((None, MOE_TN, D_MODEL), lambda i, j, m: (expert(i, m), col(i, j, m), 0)),
                pl.BlockSpec((None, 1, D_MODEL), lambda i, j, m: (expert(i, m), 0, 0)),
            ],
            out_specs=pl.BlockSpec(memory_space=pl.ANY),
            scratch_shapes=[
                pltpu.VMEM((2, MOE_RMAX * ROW_SUB, LANES), jnp.uint32),
                pltpu.VMEM((MOE_RMAX * ROW_SUB, LANES), jnp.uint32),
                pltpu.VMEM((MOE_RMAX, D_MODEL), jnp.float32),
                pltpu.SemaphoreType.DMA((2,)),
                pltpu.SemaphoreType.DMA(()),
            ],
        ),
        out_shape=jax.ShapeDtypeStruct((X_ROWS * ROW_SUB, LANES), jnp.uint32),
        compiler_params=_params("arbitrary", "arbitrary"),
        name="experts",
    )(meta, x_sorted, w1, w1,
      b1.reshape(N_EXPERTS, 1, 2 * D_EXPERT), b1.reshape(N_EXPERTS, 1, 2 * D_EXPERT),
      w2, b2.reshape(N_EXPERTS, 1, D_MODEL))


def _combine_kernel(y0_ref, y1_ref, y2_ref, y3_ref, gates_ref, x1_ref, gate2_ref, g_ref, b_ref, o_ref):
    y_lo = jnp.zeros((COMBINE_TM, D_PACK), jnp.float32)
    y_hi = jnp.zeros((COMBINE_TM, D_PACK), jnp.float32)
    for k, y_ref in enumerate((y0_ref, y1_ref, y2_ref, y3_ref)):
        lo, hi = _unpack_words(_load_packed(y_ref, (), 0, COMBINE_TM))
        gate = gates_ref[:, k:k + 1]
        y_lo = y_lo + gate * lo
        y_hi = y_hi + gate * hi
    y = jnp.concatenate([y_lo, y_hi], axis=1)
    o_ref[...] = _layer_norm(DEEPNORM_ALPHA * x1_ref[...] + gate2_ref[...] * y, g_ref[...], b_ref[...])


def _combine(y_packed, gates, x1, gate2, ln_g, ln_b):
    n_tiles = SEQ // COMBINE_TM
    row = lambda i: (i, 0)
    vec = pl.BlockSpec((1, D_MODEL), lambda i: (0, 0))
    slot = lambda k: pl.BlockSpec((COMBINE_TM * ROW_SUB, LANES), lambda i: (k * n_tiles + i, 0))
    return pl.pallas_call(
        _combine_kernel,
        grid=(n_tiles,),
        in_specs=[
            slot(0), slot(1), slot(2), slot(3),
            pl.BlockSpec((COMBINE_TM, TOP_K), row),
            pl.BlockSpec((COMBINE_TM, D_MODEL), row),
            vec, vec, vec,
        ],
        out_specs=pl.BlockSpec((COMBINE_TM, D_MODEL), row),
        out_shape=jax.ShapeDtypeStruct((SEQ, D_MODEL), jnp.float32),
        compiler_params=_params("arbitrary"),
        name="combine",
    )(y_packed, y_packed, y_packed, y_packed, gates, x1, gate2, ln_g, ln_b)


N_GROUPS = X_ROWS // MOE_G
WINDOW_ROWS = ROW_SUB
assert MOE_G == LANES and TOP_K == 4


def _row_tables_kernel(win_ref, order_ref, dst_ref):
    lane = lax.broadcasted_iota(jnp.int32, (1, LANES), 1)

    def table(g, carry):
        w0 = win_ref[g]
        n_real = win_ref[N_GROUPS + g]
        off = w0 % LANES
        x = order_ref[pl.ds(w0 // LANES, WINDOW_ROWS), :]
        x = pltpu.roll(x, (LANES - off) % LANES, axis=1)
        flat = jnp.where(lane < LANES - off, x[0:1], x[1:2])
        tok = flat >> 2
        slot = flat & 3
        dst_ref[pl.ds(g, 1), :] = jnp.where(lane < n_real, slot * SEQ + tok, 0) * ROW_SUB
        return carry
    lax.fori_loop(0, N_GROUPS, table, 0, unroll=8)


def _row_tables(win, order):
    n_rows = SEQ * TOP_K // LANES
    order2d = jnp.concatenate([order, jnp.zeros((WINDOW_ROWS * LANES,), jnp.int32)]).reshape(-1, LANES)
    return pl.pallas_call(
        _row_tables_kernel,
        grid_spec=pltpu.PrefetchScalarGridSpec(
            num_scalar_prefetch=1,
            grid=(1,),
            in_specs=[pl.BlockSpec((n_rows + WINDOW_ROWS, LANES), lambda b, w: (0, 0))],
            out_specs=pl.BlockSpec((N_GROUPS, LANES), lambda b, w: (0, 0)),
        ),
        out_shape=jax.ShapeDtypeStruct((N_GROUPS, LANES), jnp.int32),
        compiler_params=_params("arbitrary"),
        name="row_tables",
    )(win, order2d)


def _collect_kernel(real_ref, dst_ref, y_ref, out_hbm, buf_ref, sem):
    i = pl.program_id(0)
    slot = i % 2
    buf_ref[slot] = y_ref[...]

    def row_copy(s, r):
        at = r * ROW_SUB if isinstance(r, int) else pl.multiple_of(r * ROW_SUB, ROW_SUB)
        dst = pl.multiple_of(dst_ref[0, 0, r], ROW_SUB)
        return pltpu.make_async_copy(buf_ref.at[s, pl.ds(at, ROW_SUB), :], out_hbm.at[pl.ds(dst, ROW_SUB), :],
                                     sem.at[s])

    group_sub = MOE_G * ROW_SUB

    def rows_loop(n, fn, **kw):
        lax.fori_loop(0, n, lambda r, c: (fn(r), c)[1], 0, **kw)

    def start_tile(step, s):
        for h in range(COLLECT_GROUPS):
            n_real = real_ref[step * COLLECT_GROUPS + h]
            row0 = h * MOE_G

            @pl.when(n_real == MOE_G)
            def _():
                for r in range(MOE_G):
                    row_copy(s, row0 + r).start(priority=r % 2)

            @pl.when(n_real < MOE_G)
            def _():
                rows_loop(n_real, lambda r, row0=row0: row_copy(s, row0 + r).start())

    def wait_tile(step, s):
        whole = buf_ref.at[s, pl.ds(0, group_sub), :]
        for h in range(COLLECT_GROUPS):
            n_real = real_ref[step * COLLECT_GROUPS + h]

            @pl.when(n_real == MOE_G)
            def _():
                pltpu.make_async_copy(whole, whole, sem.at[s]).wait()

            @pl.when(n_real < MOE_G)
            def _():
                rows_loop(n_real, lambda r: row_copy(s, 0).wait())

    start_tile(i, slot)

    @pl.when(i > 0)
    def _():
        wait_tile(i - 1, 1 - slot)

    @pl.when(i == pl.num_programs(0) - 1)
    def _():
        wait_tile(i, slot)


def _collect(y_sorted, dst_tab, n_real):
    n_tiles = N_GROUPS // COLLECT_GROUPS
    tile_rows = COLLECT_GROUPS * MOE_G
    return pl.pallas_call(
        _collect_kernel,
        grid_spec=pltpu.PrefetchScalarGridSpec(
            num_scalar_prefetch=1,
            grid=(n_tiles,),
            in_specs=[
                pl.BlockSpec((1, 1, tile_rows), lambda i, n: (i, 0, 0), memory_space=pltpu.SMEM),
                pl.BlockSpec((tile_rows * ROW_SUB, LANES), lambda i, n: (i, 0)),
            ],
            out_specs=pl.BlockSpec(memory_space=pl.ANY),
            scratch_shapes=[
                pltpu.VMEM((2, tile_rows * ROW_SUB, LANES), jnp.uint32),
                pltpu.SemaphoreType.DMA((2,)),
            ],
        ),
        out_shape=jax.ShapeDtypeStruct((SEQ * TOP_K * ROW_SUB, LANES), jnp.uint32),
        compiler_params=_params("arbitrary"),
        name="collect",
    )(n_real, dst_tab.reshape(n_tiles, 1, tile_rows), y_sorted)


def _dispatch_kernel(fill_ref, pos_ref, h_ref, x_hbm, zero_ref, buf_ref, sem, zsem):
    i = pl.program_id(0)
    group_sub = MOE_G * ROW_SUB

    @pl.when(i == 0)
    def _():
        zero_ref[...] = jnp.zeros_like(zero_ref)

        def fill(g):
            dst = pl.multiple_of(g * group_sub, group_sub)
            pltpu.make_async_copy(zero_ref, x_hbm.at[pl.ds(dst, group_sub), :], zsem).start()

        for e in range(N_EXPERTS):
            end = fill_ref[e]
            begin = fill_ref[e - 1] if e else 0
            pl.when(end > begin)(lambda end=end: fill(end - 1))
        lax.fori_loop(fill_ref[N_EXPERTS - 1], X_ROWS // MOE_G, lambda g, c: (fill(g), c)[1], 0)

        def wait_fill(t, c):
            pltpu.make_async_copy(zero_ref, zero_ref, zsem).wait()
            return c
        lax.fori_loop(0, fill_ref[N_EXPERTS], wait_fill, 0)

    slot = i % 2
    buf_ref[slot] = h_ref[...]

    def token(t, c):
        src = pl.multiple_of(t * ROW_SUB, ROW_SUB)
        for k in range(TOP_K):
            dst = pl.multiple_of(pos_ref[0, 0, t * TOP_K + k], ROW_SUB)
            pltpu.make_async_copy(buf_ref.at[slot, pl.ds(src, ROW_SUB), :], x_hbm.at[pl.ds(dst, ROW_SUB), :],
                                  sem.at[slot]).start(priority=k % 2)
        return c
    lax.fori_loop(0, DISPATCH_TM, token, 0, unroll=16)

    def wait_tile(s):
        for k in range(TOP_K):
            pltpu.make_async_copy(buf_ref.at[s], buf_ref.at[s], sem.at[s]).wait()

    pl.when(i > 0)(lambda: wait_tile(1 - slot))
    pl.when(i == pl.num_programs(0) - 1)(lambda: wait_tile(slot))


def _dispatch(h_packed, pos, fill):
    n_tiles = SEQ // DISPATCH_TM
    return pl.pallas_call(
        _dispatch_kernel,
        grid_spec=pltpu.PrefetchScalarGridSpec(
            num_scalar_prefetch=1,
            grid=(n_tiles,),
            in_specs=[
                pl.BlockSpec((1, 1, DISPATCH_TM * TOP_K), lambda i, f: (i, 0, 0), memory_space=pltpu.SMEM),
                pl.BlockSpec((DISPATCH_TM * ROW_SUB, LANES), lambda i, f: (i, 0)),
            ],
            out_specs=pl.BlockSpec(memory_space=pl.ANY),
            scratch_shapes=[
                pltpu.VMEM((MOE_G * ROW_SUB, LANES), jnp.uint32),
                pltpu.VMEM((2, DISPATCH_TM * ROW_SUB, LANES), jnp.uint32),
                pltpu.SemaphoreType.DMA((2,)),
                pltpu.SemaphoreType.DMA(()),
            ],
        ),
        out_shape=jax.ShapeDtypeStruct((X_ROWS * ROW_SUB, LANES), jnp.uint32),
        compiler_params=_params("arbitrary"),
        name="dispatch",
    )(fill, pos.reshape(n_tiles, 1, DISPATCH_TM * TOP_K), h_packed)


def _route(logits):
    top_val, top_idx = lax.top_k(logits, TOP_K)
    gates = jax.nn.softmax(top_val, axis=-1)
    e_flat = top_idx.reshape(-1).astype(jnp.int32)
    experts = jnp.arange(N_EXPERTS, dtype=jnp.int32)
    counts = jnp.sum((e_flat[:, None] == experts[None, :]).astype(jnp.int32), axis=0)
    groups = (counts + MOE_G - 1) // MOE_G
    group_end = jnp.cumsum(groups)
    row0 = (group_end - groups) * MOE_G
    per_item = MOE_RMAX // MOE_G
    n_items_e = (groups + per_item - 1) // per_item
    item_end = jnp.cumsum(n_items_e)
    item_start = item_end - n_items_e
    n_items = item_end[-1]

    item = jnp.arange(MOE_ITEMS, dtype=jnp.int32)
    used = item < n_items
    e_item = jnp.minimum(jnp.searchsorted(item_end, item, side='right'), N_EXPERTS - 1).astype(jnp.int32)
    e_last = e_item[jnp.maximum(n_items - 1, 0)]
    part = item - item_start[e_item]
    rows = jnp.where(used, jnp.clip(groups[e_item] - part * per_item, 0, per_item) * MOE_G, 0)
    first_row = jnp.where(used, row0[e_item] + part * MOE_RMAX, 0)
    meta = jnp.concatenate([jnp.where(used, e_item, e_last), rows, first_row, group_end[-1:]]).astype(jnp.int32)

    order = jnp.argsort(e_flat, stable=True).astype(jnp.int32)
    rank = jnp.argsort(order).astype(jnp.int32)
    start = jnp.cumsum(counts) - counts
    shift = jnp.sum(jnp.where(e_flat[:, None] == experts[None, :], (row0 - start)[None, :], 0), axis=1)
    pos = (rank + shift) * ROW_SUB
    n_fill = jnp.sum((groups > 0).astype(jnp.int32)) + X_ROWS // MOE_G - group_end[-1]
    fill = jnp.concatenate([group_end, n_fill[None]]).astype(jnp.int32)

    group = jnp.arange(N_GROUPS, dtype=jnp.int32)
    mine = (group[:, None] >= (group_end - groups)[None, :]) & (group[:, None] < group_end[None, :])
    pick = lambda per_expert: jnp.sum(jnp.where(mine, per_expert[None, :], 0), axis=1)
    in_use = group < group_end[-1]
    nth = group - pick(group_end - groups)
    window0 = jnp.where(in_use, pick(start) + nth * MOE_G, 0)
    n_real = jnp.where(in_use, jnp.clip(pick(counts) - nth * MOE_G, 0, MOE_G), 0).astype(jnp.int32)
    dst_tab = _row_tables(jnp.concatenate([window0, n_real]).astype(jnp.int32), order)
    return gates, meta, pos.astype(jnp.int32), fill, dst_tab, n_real


def kernel(x, c, w_ada, b_ada, w_in, sgu_ln_g, sgu_ln_b, w_spatial, b_spatial, w_o, ln1_g, ln1_b,
           w_router, b_router, w_exp1, b_exp1, w_exp2, b_exp2, ln2_g, ln2_b):
    depth = w_ada.shape[0]
    assert x.shape == (1, SEQ, D_MODEL)
    xs = x.reshape(SEQ, D_MODEL)
    for l in range(depth):
        ada = _ada(c, w_ada[l], b_ada[l])
        shift1, scale1, gate1, shift2, scale2, gate2 = jnp.split(ada, 6, axis=-1)

        qkv, y_a = _in_proj(xs, scale1, shift1, w_in[l].astype(jnp.bfloat16),
                            sgu_ln_g[l], sgu_ln_b[l], w_spatial[l], b_spatial[l])
        y_b = _attention(qkv)
        x1, h_packed, logits = _out_proj(y_a, y_b, w_o[l].astype(jnp.bfloat16), xs, gate1,
                                         ln1_g[l].reshape(1, -1), ln1_b[l].reshape(1, -1), scale2, shift2,
                                         w_router[l], b_router[l].reshape(1, -1))

        gates, meta, pos, fill, dst_tab, n_real = _route(logits)
        x_sorted = _dispatch(h_packed, pos, fill)
        y_sorted = _experts(x_sorted, meta, w_exp1[l], b_exp1[l], w_exp2[l], b_exp2[l])
        y_packed = _collect(y_sorted, dst_tab, n_real)
        xs = _combine(y_packed, gates, x1, gate2, ln2_g[l].reshape(1, -1), ln2_b[l].reshape(1, -1))
    return xs.reshape(x.shape)
```

```python
import math

import jax
import jax.numpy as jnp
from jax import lax
from jax.experimental import pallas as pl
from jax.experimental.pallas import tpu as pltpu

D_MODEL = 2048
SEQ = 8192
D_GMLP = 1024
GMLP_GROUPS = 8
GROUP_DIM = 128
CHUNK = 128
D_ATTN = 1024
HEAD_DIM = 128
N_HEADS = 8
BRANCHES = ((128, 1), (512, 4), (2048, 16))
BLK = 128
SPAN = 16 * BLK
D_IN_PROJ = 2 * D_GMLP + 3 * D_ATTN
N_EXPERTS = 32
TOP_K = 4
D_EXPERT = 2048
SWIGLU_LIMIT = 7.0
SWIGLU_ALPHA = 1.702
LN_EPS = 1e-5
DEEPNORM_ALPHA = 2.0 ** 0.25
NEG = -1e30

LANES = 128
VMEM_LIMIT = 56 * 1024 * 1024

ADA_TN = 1536
PROJ_TM = 1024
PROJ_TN = 1024
OUT_TM = 512
ATTN_UNROLL = 16
ROW_SUB = 8
D_PACK = D_MODEL // 2
MOE_G = 128
MOE_RMAX = 1152
MOE_ITEMS = 61
MOE_TN = 512
MOE_CHUNKS = D_EXPERT // MOE_TN
MOE_SUBTILES = (512, 256, 128)
X_ROWS = SEQ * TOP_K + N_EXPERTS * MOE_G
DISPATCH_TM = 512
COLLECT_GROUPS = 8
COMBINE_TM = 512
MAX_WORK_ITEMS = (X_ROWS // MOE_G + N_EXPERTS * (MOE_RMAX // MOE_G - 1)) // (MOE_RMAX // MOE_G)
assert MAX_WORK_ITEMS < MOE_ITEMS


def _params(*sem):
    return pltpu.CompilerParams(dimension_semantics=sem, vmem_limit_bytes=VMEM_LIMIT)


def _layer_norm(x, g, b):
    mu = jnp.mean(x, axis=-1, keepdims=True)
    xc = x - mu
    var = jnp.mean(xc * xc, axis=-1, keepdims=True)
    return xc * lax.rsqrt(var + LN_EPS) * g + b


def _pack_rows(x):
    r = x.astype(jnp.bfloat16).astype(jnp.float32)
    bits = lax.bitcast_convert_type(r, jnp.uint32)
    return (bits[:, D_PACK:] & jnp.uint32(0xFFFF0000)) | (bits[:, :D_PACK] >> 16)


def _unpack_words(u):
    lo = lax.bitcast_convert_type(u << 16, jnp.float32)
    hi = lax.bitcast_convert_type(u & jnp.uint32(0xFFFF0000), jnp.float32)
    return lo, hi


def _store_packed(ref, lead, row0, packed):
    m = packed.shape[0]
    for c in range(ROW_SUB):
        rows = pl.ds(row0 * ROW_SUB + c, m, stride=ROW_SUB)
        ref[lead + (rows, slice(None))] = packed[:, c * LANES:(c + 1) * LANES]


def _load_packed(ref, lead, row0, m):
    cols = [ref[lead + (pl.ds(row0 * ROW_SUB + c, m, stride=ROW_SUB), slice(None))] for c in range(ROW_SUB)]
    return jnp.concatenate(cols, axis=1)


def _ada_kernel(c_ref, w_ref, b_ref, o_ref):
    c = c_ref[...]
    s = c * jax.nn.sigmoid(c)
    o_ref[...] = jnp.sum(s * w_ref[...], axis=0, keepdims=True) + b_ref[...]


def _ada(c, w_ada, b_ada):
    n = w_ada.shape[1]
    return pl.pallas_call(
        _ada_kernel,
        grid=(n // ADA_TN,),
        in_specs=[
            pl.BlockSpec((D_MODEL, 1), lambda j: (0, 0)),
            pl.BlockSpec((D_MODEL, ADA_TN), lambda j: (0, j)),
            pl.BlockSpec((1, ADA_TN), lambda j: (0, j)),
        ],
        out_specs=pl.BlockSpec((1, ADA_TN), lambda j: (0, j)),
        out_shape=jax.ShapeDtypeStruct((1, n), jnp.float32),
        compiler_params=_params("arbitrary"),
        name="ada",
    )(c.reshape(D_MODEL, 1), w_ada, b_ada.reshape(1, n))


def _gelu(x):
    return 0.5 * x * (1.0 + lax.erf(x * (1.0 / math.sqrt(2.0))))


GROUPS_PER_STEP = 3


def _in_proj_kernel(x_ref, sc_ref, sh_ref, w_ref, g_ref, b_ref, ws_ref, bs_ref, qkv_ref, ya_ref, h_ref, uv_ref):
    i = pl.program_id(0)
    j = pl.program_id(1)

    @pl.when((i == 0) & (j == 0))
    def _():
        uv_ref[...] = jnp.zeros_like(uv_ref)

    @pl.when(j == 0)
    def _():
        h_ref[...] = (x_ref[...] * (1.0 + sc_ref[...]) + sh_ref[...]).astype(jnp.bfloat16)

    row = lax.broadcasted_iota(jnp.int32, (CHUNK, CHUNK), 0)
    col = lax.broadcasted_iota(jnp.int32, (CHUNK, CHUNK), 1)
    causal = col <= row
    chunks = [slice(c * CHUNK, (c + 1) * CHUNK) for c in range(PROJ_TM // CHUNK)]
    for u in range(GROUPS_PER_STEP):
        g = jnp.clip(GROUPS_PER_STEP * (j - 2) + u, 0, GMLP_GROUPS - 1)
        lanes = pl.ds(pl.multiple_of(g * GROUP_DIM, GROUP_DIM), GROUP_DIM)
        ln_g = g_ref[pl.ds(g, 1), :]
        ln_b = b_ref[pl.ds(g, 1), :]
        v = [_layer_norm(_gelu(uv_ref[1, rows, lanes]), ln_g, ln_b).astype(jnp.bfloat16) for rows in chunks]
        w = jnp.where(causal, ws_ref[g], 0.0).astype(jnp.bfloat16)
        s = jnp.dot(w, jnp.concatenate(v, axis=1), preferred_element_type=jnp.float32)
        for c, rows in enumerate(chunks):
            s_c = s[:, c * GROUP_DIM:(c + 1) * GROUP_DIM] + bs_ref[g]
            ya_ref[rows, lanes] = (_gelu(uv_ref[0, rows, lanes]) * s_c).astype(ya_ref.dtype)

    qkv_ref[...] = jnp.dot(h_ref[...], w_ref[...], preferred_element_type=jnp.float32)

    @pl.when(j < 2)
    def _():
        uv_ref[jnp.minimum(j, 1)] = qkv_ref[...]


def _in_proj(x, scale1, shift1, w_in_bf16, ln_g, ln_b, w_spatial, b_spatial):
    n_uv = 2 * D_GMLP // PROJ_TN
    assert GROUPS_PER_STEP * (D_IN_PROJ // PROJ_TN - n_uv) >= GMLP_GROUPS
    fixed2 = lambda i, j: (0, 0)
    bs_lanes = jnp.broadcast_to(b_spatial[:, :, None], (GMLP_GROUPS, CHUNK, LANES))
    return pl.pallas_call(
        _in_proj_kernel,
        grid=(SEQ // PROJ_TM, D_IN_PROJ // PROJ_TN),
        in_specs=[
            pl.BlockSpec((PROJ_TM, D_MODEL), lambda i, j: (i, 0)),
            pl.BlockSpec((1, D_MODEL), fixed2),
            pl.BlockSpec((1, D_MODEL), fixed2),
            pl.BlockSpec((D_MODEL, PROJ_TN), lambda i, j: (0, j)),
            pl.BlockSpec((GMLP_GROUPS, GROUP_DIM), fixed2),
            pl.BlockSpec((GMLP_GROUPS, GROUP_DIM), fixed2),
            pl.BlockSpec((GMLP_GROUPS, CHUNK, CHUNK), lambda i, j: (0, 0, 0)),
            pl.BlockSpec((GMLP_GROUPS, CHUNK, LANES), lambda i, j: (0, 0, 0)),
        ],
        out_specs=[
            pl.BlockSpec((PROJ_TM, PROJ_TN), lambda i, j: (i, jnp.maximum(j - n_uv, 0))),
            pl.BlockSpec((PROJ_TM, D_GMLP), lambda i, j: (i, 0)),
        ],
        out_shape=[
            jax.ShapeDtypeStruct((SEQ, 3 * D_ATTN), jnp.float32),
            jax.ShapeDtypeStruct((SEQ, D_GMLP), jnp.bfloat16),
        ],
        scratch_shapes=[
            pltpu.VMEM((PROJ_TM, D_MODEL), jnp.bfloat16),
            pltpu.VMEM((2, PROJ_TM, D_GMLP), jnp.float32),
        ],
        compiler_params=_params("arbitrary", "arbitrary"),
        name="in_proj",
    )(x, scale1, shift1, w_in_bf16, ln_g, ln_b, w_spatial, bs_lanes)


def _attn_kernel(q_ref, k_ref, v_ref, o_ref, out_ref, lse_ref, kv_ref):
    head = pl.program_id(0)
    span = pl.program_id(1)
    log2e = 1.0 / math.log(2.0)
    head_no = (jnp.zeros((BLK, 2 * BLK), jnp.int32) + (head + 1)).astype(jnp.float32)
    slope = jnp.exp2(head_no * (-8.0 / N_HEADS)) * log2e
    scale = HEAD_DIM ** -0.5 * log2e
    qi = lax.broadcasted_iota(jnp.int32, (BLK, 2 * BLK), 0)
    ki = lax.broadcasted_iota(jnp.int32, (BLK, 2 * BLK), 1)
    step = qi + BLK - ki
    contract_last = (((1,), (1,)), ((), ()))
    ones = jnp.ones((2 * BLK, LANES), jnp.bfloat16)

    for b, (window, d) in enumerate(BRANCHES):
        assert window // d == BLK
        valid = (step >= 0) & (step <= BLK)
        bias = jnp.where(valid, -slope * d * step.astype(jnp.float32), NEG)
        bias_first = jnp.where(ki >= BLK, bias, NEG)

        per_class = d * BLK == SPAN
        cur = span % 2
        if per_class:
            def stash(r, carry, d=d):
                rows = pl.ds(pl.multiple_of(r * BLK, BLK), BLK)
                kv_ref[0, cur, rows, :] = k_ref[pl.ds(span * SPAN + r, BLK, stride=d), :].astype(jnp.bfloat16)
                kv_ref[1, cur, rows, :] = v_ref[pl.ds(span * SPAN + r, BLK, stride=d), :].astype(jnp.bfloat16)
                return carry
            lax.fori_loop(0, d, stash, 0, unroll=4)

        def tile(t, carry, d=d, b=b, bias=bias, bias_first=bias_first, per_class=per_class):
            r = t % d
            n = t // d
            q0 = n * (BLK * d) + r
            k0 = span * SPAN + q0
            first = k0 < BLK * d
            q = (q_ref[pl.ds(q0, BLK, stride=d), :] * scale).astype(jnp.bfloat16)
            if per_class:
                rows = pl.ds(pl.multiple_of(r * BLK, BLK), BLK)
                prev = jnp.where(first, cur, 1 - cur)
                k = jnp.concatenate([kv_ref[0, prev, rows, :], kv_ref[0, cur, rows, :]], axis=0)
                v = jnp.concatenate([kv_ref[1, prev, rows, :], kv_ref[1, cur, rows, :]], axis=0)
            else:
                kp = jnp.where(first, k0, k0 - BLK * d)
                k = jnp.concatenate([k_ref[pl.ds(kp, BLK, stride=d), :], k_ref[pl.ds(k0, BLK, stride=d), :]],
                                    axis=0).astype(jnp.bfloat16)
                v = jnp.concatenate([v_ref[pl.ds(kp, BLK, stride=d), :], v_ref[pl.ds(k0, BLK, stride=d), :]],
                                    axis=0).astype(jnp.bfloat16)
            s = lax.dot_general(q, k, contract_last, preferred_element_type=jnp.float32)
            s = s + jnp.where(first, bias_first, bias)
            m = jnp.max(jnp.maximum(s[:, :BLK], s[:, BLK:]), axis=-1, keepdims=True)
            p = jnp.exp2(s - m).astype(jnp.bfloat16)
            v_one = jnp.concatenate([v, ones], axis=1)
            pv = jnp.dot(p, v_one, preferred_element_type=jnp.float32)
            den = pv[:, HEAD_DIM:]
            rows = pl.ds(q0, BLK, stride=d)
            out_ref[b, rows, :] = pv[:, :HEAD_DIM] / den
            lse_ref[b, rows, :] = m + jnp.log2(den)
            return carry

        lax.fori_loop(0, SPAN // BLK, tile, 0, unroll=ATTN_UNROLL)

    lse_all = jnp.maximum(jnp.maximum(lse_ref[0], lse_ref[1]), lse_ref[2])
    num = jnp.zeros((SPAN, HEAD_DIM), jnp.float32)
    den = jnp.zeros((SPAN, LANES), jnp.float32)
    for b in range(len(BRANCHES)):
        w = jnp.exp2(lse_ref[b] - lse_all)
        num = num + w * out_ref[b]
        den = den + w
    o_ref[...] = (num / den).astype(o_ref.dtype)


def _attention(proj):
    q_col = 0
    k_col = q_col + N_HEADS
    v_col = k_col + N_HEADS
    nb = len(BRANCHES)
    return pl.pallas_call(
        _attn_kernel,
        grid=(N_HEADS, SEQ // SPAN),
        in_specs=[
            pl.BlockSpec((SPAN, HEAD_DIM), lambda h, s: (s, q_col + h)),
            pl.BlockSpec((SEQ, HEAD_DIM), lambda h, s: (0, k_col + h)),
            pl.BlockSpec((SEQ, HEAD_DIM), lambda h, s: (0, v_col + h)),
        ],
        out_specs=pl.BlockSpec((SPAN, HEAD_DIM), lambda h, s: (s, h)),
        out_shape=jax.ShapeDtypeStruct((SEQ, D_ATTN), jnp.bfloat16),
        scratch_shapes=[
            pltpu.VMEM((nb, SPAN, HEAD_DIM), jnp.float32),
            pltpu.VMEM((nb, SPAN, LANES), jnp.float32),
            pltpu.VMEM((2, 2, SPAN, HEAD_DIM), jnp.bfloat16),
        ],
        compiler_params=_params("arbitrary", "arbitrary"),
        name="attn",
    )(proj, proj, proj)


def _split_bf16(x):
    hi = x.astype(jnp.bfloat16)
    lo = (x - hi.astype(jnp.float32)).astype(jnp.bfloat16)
    return hi, lo


def _out_proj_kernel(ya_ref, yb_ref, wa_ref, wb_ref, x_ref, gate_ref, g_ref, b_ref, sc_ref, sh_ref,
                     wr_ref, br_ref, x1_ref, hp_ref, lg_ref):
    mix = jnp.dot(ya_ref[...], wa_ref[...], preferred_element_type=jnp.float32)
    mix = mix + jnp.dot(yb_ref[...], wb_ref[...], preferred_element_type=jnp.float32)
    x1 = _layer_norm(DEEPNORM_ALPHA * x_ref[...] + gate_ref[...] * mix, g_ref[...], b_ref[...])
    x1_ref[...] = x1
    h = x1 * (1.0 + sc_ref[...]) + sh_ref[...]
    _store_packed(hp_ref, (), 0, _pack_rows(h))
    h_hi, h_lo = _split_bf16(h)
    w_hi, w_lo = _split_bf16(wr_ref[...])
    both = jnp.dot(h_hi, jnp.concatenate([w_hi, w_lo], axis=1), preferred_element_type=jnp.float32)
    lg = both[:, :N_EXPERTS] + both[:, N_EXPERTS:] + jnp.dot(h_lo, w_hi, preferred_element_type=jnp.float32)
    lg_ref[...] = lg + br_ref[...]


def _out_proj(y_a, y_b, w_o_bf16, x, gate1, ln_g, ln_b, scale2, shift2, w_router, b_router):
    row = lambda i: (i, 0)
    fixed = lambda i: (0, 0)
    vec = pl.BlockSpec((1, D_MODEL), fixed)
    return pl.pallas_call(
        _out_proj_kernel,
        grid=(SEQ // OUT_TM,),
        in_specs=[
            pl.BlockSpec((OUT_TM, D_GMLP), row),
            pl.BlockSpec((OUT_TM, D_ATTN), row),
            pl.BlockSpec((D_GMLP, D_MODEL), lambda i: (0, 0)),
            pl.BlockSpec((D_ATTN, D_MODEL), lambda i: (1, 0)),
            pl.BlockSpec((OUT_TM, D_MODEL), row),
            vec, vec, vec, vec, vec,
            pl.BlockSpec((D_MODEL, N_EXPERTS), fixed),
            pl.BlockSpec((1, N_EXPERTS), fixed),
        ],
        out_specs=[
            pl.BlockSpec((OUT_TM, D_MODEL), row),
            pl.BlockSpec((OUT_TM * ROW_SUB, LANES), row),
            pl.BlockSpec((OUT_TM, N_EXPERTS), row),
        ],
        out_shape=[
            jax.ShapeDtypeStruct((SEQ, D_MODEL), jnp.float32),
            jax.ShapeDtypeStruct((SEQ * ROW_SUB, LANES), jnp.uint32),
            jax.ShapeDtypeStruct((SEQ, N_EXPERTS), jnp.float32),
        ],
        compiler_params=_params("arbitrary"),
        name="out_proj",
    )(y_a, y_b, w_o_bf16, w_o_bf16, x, gate1, ln_g, ln_b, scale2, shift2, w_router, b_router)


def _experts_kernel(meta_ref, x_hbm, wg_ref, wl_ref, bg_ref, bl_ref, w2_ref, b2_ref, y_hbm,
                    xs_ref, os_ref, acc_ref, gsem, ssem):
    i = pl.program_id(0)
    j = pl.program_id(1)
    n_items = pl.num_programs(0)
    last_j = pl.num_programs(1) - 1

    def item_rows(k):
        inside = (k >= 0) & (k < n_items)
        return jnp.where(inside, meta_ref[MOE_ITEMS + jnp.clip(k, 0, n_items - 1)], 0)

    def item_first(k):
        return meta_ref[2 * MOE_ITEMS + jnp.clip(k, 0, n_items - 1)] * ROW_SUB

    rows = item_rows(i)
    rows_prev = item_rows(i - 1)
    p = i % 2
    q = 1 - p
    group_sub = MOE_G * ROW_SUB

    def each_group(n_rows, fn):
        lax.fori_loop(0, n_rows // MOE_G, lambda t, c: (fn(pl.multiple_of(t * group_sub, group_sub)), c)[1], 0)

    def load_rows(k, slot):
        first = item_first(k)
        each_group(item_rows(k), lambda at: pltpu.make_async_copy(
            x_hbm.at[pl.ds(pl.multiple_of(first + at, group_sub), group_sub), :],
            xs_ref.at[slot, pl.ds(at, group_sub), :], gsem.at[slot]).start())

    def wait_rows(buf_ref, sem, n_rows):
        group = buf_ref.at[pl.ds(0, group_sub), :]
        each_group(n_rows, lambda at: pltpu.make_async_copy(group, group, sem).wait())

    @pl.when((i == 0) & (j == 0))
    def _():
        acc_ref[...] = jnp.zeros_like(acc_ref)
        os_ref[...] = jnp.zeros_like(os_ref)
        zeros = os_ref.at[pl.ds(0, group_sub), :]
        first_unused = meta_ref[3 * MOE_ITEMS]
        n_unused = X_ROWS // MOE_G - first_unused

        def fill(t, c):
            dst = pl.multiple_of((first_unused + t) * group_sub, group_sub)
            pltpu.make_async_copy(zeros, y_hbm.at[pl.ds(dst, group_sub), :], ssem).start()
            return c
        lax.fori_loop(0, n_unused, fill, 0)
        wait_rows(os_ref, ssem, n_unused * MOE_G)
        load_rows(0, 0)

    @pl.when(j == 0)
    def _():
        wait_rows(xs_ref.at[p], gsem.at[p], rows)
        load_rows(i + 1, q)

    def sub_tile(a, m):
        a = pl.multiple_of(a, MOE_G)
        x_lo, x_hi = _unpack_words(_load_packed(xs_ref, (p,), a, m))
        x = jnp.concatenate([x_lo.astype(jnp.bfloat16), x_hi.astype(jnp.bfloat16)], axis=1)

        def up(w_ref, b_ref):
            return jnp.dot(x, w_ref[...].astype(jnp.bfloat16), preferred_element_type=jnp.float32) + b_ref[...]

        glu = jnp.minimum(up(wg_ref, bg_ref), SWIGLU_LIMIT)
        lin = jnp.clip(up(wl_ref, bl_ref), -SWIGLU_LIMIT, SWIGLU_LIMIT)
        act = glu * jax.nn.sigmoid(SWIGLU_ALPHA * glu) * (lin + 1.0)
        down = jnp.dot(act.astype(jnp.bfloat16), w2_ref[...].astype(jnp.bfloat16),
                       preferred_element_type=jnp.float32)
        start = jnp.where(j == 0, jnp.broadcast_to(b2_ref[...], (m, D_MODEL)), acc_ref[pl.ds(a, m), :])
        acc_ref[pl.ds(a, m), :] = start + down

    @pl.when(rows > 0)
    def _():
        main = MOE_SUBTILES[0]
        n_main = rows // main
        lax.fori_loop(0, n_main, lambda t, c: (sub_tile(t * main, main), c)[1], 0)
        done = n_main * main
        for m in MOE_SUBTILES[1:]:
            has = ((rows - done) // m) % 2 == 1
            pl.when(has)(lambda done=done, m=m: sub_tile(done, m))
            done = done + jnp.where(has, m, 0)

    @pl.when(j == last_j)
    def _():
        wait_rows(os_ref, ssem, rows_prev)

        def pack(t, c):
            a = pl.multiple_of(t * MOE_G, MOE_G)
            _store_packed(os_ref, (), a, _pack_rows(acc_ref[pl.ds(a, MOE_G), :]))
            return c
        lax.fori_loop(0, rows // MOE_G, pack, 0)
        first = item_first(i)
        each_group(rows, lambda at: pltpu.make_async_copy(
            os_ref.at[pl.ds(at, group_sub), :],
            y_hbm.at[pl.ds(pl.multiple_of(first + at, group_sub), group_sub), :], ssem).start())


def _experts(x_sorted, meta, w1, b1, w2, b2):
    n_j = MOE_CHUNKS

    def col(i, j, m):
        return jnp.where(m[MOE_ITEMS + i] > 0, j, n_j - 1)

    def expert(i, m):
        return m[i]

    return pl.pallas_call(
        _experts_kernel,
        grid_spec=pltpu.PrefetchScalarGridSpec(
            num_scalar_prefetch=1,
            grid=(MOE_ITEMS, n_j),
            in_specs=[
                pl.BlockSpec(memory_space=pl.ANY),
                pl.BlockSpec((None, D_MODEL, MOE_TN), lambda i, j, m: (expert(i, m), 0, col(i, j, m))),
                pl.BlockSpec((None, D_MODEL, MOE_TN), lambda i, j, m: (expert(i, m), 0, n_j + col(i, j, m))),
                pl.BlockSpec((None, 1, MOE_TN), lambda i, j, m: (expert(i, m), 0, col(i, j, m))),
                pl.BlockSpec((None, 1, MOE_TN), lambda i, j, m: (expert(i, m), 0, n_j + col(i, j, m))),
                pl.BlockSpec((None, MOE_TN, D_MODEL), lambda i, j, m: (expert(i, m), col(i, j, m), 0)),
                pl.BlockSpec((None, 1, D_MODEL), lambda i, j, m: (expert(i, m), 0, 0)),
            ],
            out_specs=pl.BlockSpec(memory_space=pl.ANY),
            scratch_shapes=[
                pltpu.VMEM((2, MOE_RMAX * ROW_SUB, LANES), jnp.uint32),
                pltpu.VMEM((MOE_RMAX * ROW_SUB, LANES), jnp.uint32),
                pltpu.VMEM((MOE_RMAX, D_MODEL), jnp.float32),
                pltpu.SemaphoreType.DMA((2,)),
                pltpu.SemaphoreType.DMA(()),
            ],
        ),
        out_shape=jax.ShapeDtypeStruct((X_ROWS * ROW_SUB, LANES), jnp.uint32),
        compiler_params=_params("arbitrary", "arbitrary"),
        name="experts",
    )(meta, x_sorted, w1, w1,
      b1.reshape(N_EXPERTS, 1, 2 * D_EXPERT), b1.reshape(N_EXPERTS, 1, 2 * D_EXPERT),
      w2, b2.reshape(N_EXPERTS, 1, D_MODEL))


def _combine_kernel(y0_ref, y1_ref, y2_ref, y3_ref, gates_ref, x1_ref, gate2_ref, g_ref, b_ref, o_ref):
    y_lo = jnp.zeros((COMBINE_TM, D_PACK), jnp.float32)
    y_hi = jnp.zeros((COMBINE_TM, D_PACK), jnp.float32)
    for k, y_ref in enumerate((y0_ref, y1_ref, y2_ref, y3_ref)):
        lo, hi = _unpack_words(_load_packed(y_ref, (), 0, COMBINE_TM))
        gate = gates_ref[:, k:k + 1]
        y_lo = y_lo + gate * lo
        y_hi = y_hi + gate * hi
    y = jnp.concatenate([y_lo, y_hi], axis=1)
    o_ref[...] = _layer_norm(DEEPNORM_ALPHA * x1_ref[...] + gate2_ref[...] * y, g_ref[...], b_ref[...])


def _combine(y_packed, gates, x1, gate2, ln_g, ln_b):
    n_tiles = SEQ // COMBINE_TM
    row = lambda i: (i, 0)
    vec = pl.BlockSpec((1, D_MODEL), lambda i: (0, 0))
    slot = lambda k: pl.BlockSpec((COMBINE_TM * ROW_SUB, LANES), lambda i: (k * n_tiles + i, 0))
    return pl.pallas_call(
        _combine_kernel,
        grid=(n_tiles,),
        in_specs=[
            slot(0), slot(1), slot(2), slot(3),
            pl.BlockSpec((COMBINE_TM, TOP_K), row),
            pl.BlockSpec((COMBINE_TM, D_MODEL), row),
            vec, vec, vec,
        ],
        out_specs=pl.BlockSpec((COMBINE_TM, D_MODEL), row),
        out_shape=jax.ShapeDtypeStruct((SEQ, D_MODEL), jnp.float32),
        compiler_params=_params("arbitrary"),
        name="combine",
    )(y_packed, y_packed, y_packed, y_packed, gates, x1, gate2, ln_g, ln_b)


N_GROUPS = X_ROWS // MOE_G
WINDOW_ROWS = ROW_SUB
assert MOE_G == LANES and TOP_K == 4


def _row_tables_kernel(win_ref, order_ref, dst_ref):
    lane = lax.broadcasted_iota(jnp.int32, (1, LANES), 1)

    def table(g, carry):
        w0 = win_ref[g]
        n_real = win_ref[N_GROUPS + g]
        off = w0 % LANES
        x = order_ref[pl.ds(w0 // LANES, WINDOW_ROWS), :]
        x = pltpu.roll(x, (LANES - off) % LANES, axis=1)
        flat = jnp.where(lane < LANES - off, x[0:1], x[1:2])
        tok = flat >> 2
        slot = flat & 3
        dst_ref[pl.ds(g, 1), :] = jnp.where(lane < n_real, slot * SEQ + tok, 0) * ROW_SUB
        return carry
    lax.fori_loop(0, N_GROUPS, table, 0, unroll=8)


def _row_tables(win, order):
    n_rows = SEQ * TOP_K // LANES
    order2d = jnp.concatenate([order, jnp.zeros((WINDOW_ROWS * LANES,), jnp.int32)]).reshape(-1, LANES)
    return pl.pallas_call(
        _row_tables_kernel,
        grid_spec=pltpu.PrefetchScalarGridSpec(
            num_scalar_prefetch=1,
            grid=(1,),
            in_specs=[pl.BlockSpec((n_rows + WINDOW_ROWS, LANES), lambda b, w: (0, 0))],
            out_specs=pl.BlockSpec((N_GROUPS, LANES), lambda b, w: (0, 0)),
        ),
        out_shape=jax.ShapeDtypeStruct((N_GROUPS, LANES), jnp.int32),
        compiler_params=_params("arbitrary"),
        name="row_tables",
    )(win, order2d)


def _collect_kernel(real_ref, dst_ref, y_ref, out_hbm, buf_ref, sem):
    i = pl.program_id(0)
    slot = i % 2
    buf_ref[slot] = y_ref[...]

    def row_copy(s, r):
        at = r * ROW_SUB if isinstance(r, int) else pl.multiple_of(r * ROW_SUB, ROW_SUB)
        dst = pl.multiple_of(dst_ref[0, 0, r], ROW_SUB)
        return pltpu.make_async_copy(buf_ref.at[s, pl.ds(at, ROW_SUB), :], out_hbm.at[pl.ds(dst, ROW_SUB), :],
                                     sem.at[s])

    group_sub = MOE_G * ROW_SUB

    def rows_loop(n, fn, **kw):
        lax.fori_loop(0, n, lambda r, c: (fn(r), c)[1], 0, **kw)

    def start_tile(step, s):
        for h in range(COLLECT_GROUPS):
            n_real = real_ref[step * COLLECT_GROUPS + h]
            row0 = h * MOE_G

            @pl.when(n_real == MOE_G)
            def _():
                for r in range(MOE_G):
                    row_copy(s, row0 + r).start(priority=r % 2)

            @pl.when(n_real < MOE_G)
            def _():
                rows_loop(n_real, lambda r, row0=row0: row_copy(s, row0 + r).start())

    def wait_tile(step, s):
        whole = buf_ref.at[s, pl.ds(0, group_sub), :]
        for h in range(COLLECT_GROUPS):
            n_real = real_ref[step * COLLECT_GROUPS + h]

            @pl.when(n_real == MOE_G)
            def _():
                pltpu.make_async_copy(whole, whole, sem.at[s]).wait()

            @pl.when(n_real < MOE_G)
            def _():
                rows_loop(n_real, lambda r: row_copy(s, 0).wait())

    start_tile(i, slot)

    @pl.when(i > 0)
    def _():
        wait_tile(i - 1, 1 - slot)

    @pl.when(i == pl.num_programs(0) - 1)
    def _():
        wait_tile(i, slot)


def _collect(y_sorted, dst_tab, n_real):
    n_tiles = N_GROUPS // COLLECT_GROUPS
    tile_rows = COLLECT_GROUPS * MOE_G
    return pl.pallas_call(
        _collect_kernel,
        grid_spec=pltpu.PrefetchScalarGridSpec(
            num_scalar_prefetch=1,
            grid=(n_tiles,),
            in_specs=[
                pl.BlockSpec((1, 1, tile_rows), lambda i, n: (i, 0, 0), memory_space=pltpu.SMEM),
                pl.BlockSpec((tile_rows * ROW_SUB, LANES), lambda i, n: (i, 0)),
            ],
            out_specs=pl.BlockSpec(memory_space=pl.ANY),
            scratch_shapes=[
                pltpu.VMEM((2, tile_rows * ROW_SUB, LANES), jnp.uint32),
                pltpu.SemaphoreType.DMA((2,)),
            ],
        ),
        out_shape=jax.ShapeDtypeStruct((SEQ * TOP_K * ROW_SUB, LANES), jnp.uint32),
        compiler_params=_params("arbitrary"),
        name="collect",
    )(n_real, dst_tab.reshape(n_tiles, 1, tile_rows), y_sorted)


def _dispatch_kernel(fill_ref, pos_ref, h_ref, x_hbm, zero_ref, buf_ref, sem, zsem):
    i = pl.program_id(0)
    group_sub = MOE_G * ROW_SUB

    @pl.when(i == 0)
    def _():
        zero_ref[...] = jnp.zeros_like(zero_ref)

        def fill(g):
            dst = pl.multiple_of(g * group_sub, group_sub)
            pltpu.make_async_copy(zero_ref, x_hbm.at[pl.ds(dst, group_sub), :], zsem).start()

        for e in range(N_EXPERTS):
            end = fill_ref[e]
            begin = fill_ref[e - 1] if e else 0
            pl.when(end > begin)(lambda end=end: fill(end - 1))
        lax.fori_loop(fill_ref[N_EXPERTS - 1], X_ROWS // MOE_G, lambda g, c: (fill(g), c)[1], 0)

        def wait_fill(t, c):
            pltpu.make_async_copy(zero_ref, zero_ref, zsem).wait()
            return c
        lax.fori_loop(0, fill_ref[N_EXPERTS], wait_fill, 0)

    slot = i % 2
    buf_ref[slot] = h_ref[...]

    def token(t, c):
        src = pl.multiple_of(t * ROW_SUB, ROW_SUB)
        for k in range(TOP_K):
            dst = pl.multiple_of(pos_ref[0, 0, t * TOP_K + k], ROW_SUB)
            pltpu.make_async_copy(buf_ref.at[slot, pl.ds(src, ROW_SUB), :], x_hbm.at[pl.ds(dst, ROW_SUB), :],
                                  sem.at[slot]).start(priority=k % 2)
        return c
    lax.fori_loop(0, DISPATCH_TM, token, 0, unroll=16)

    def wait_tile(s):
        for k in range(TOP_K):
            pltpu.make_async_copy(buf_ref.at[s], buf_ref.at[s], sem.at[s]).wait()

    pl.when(i > 0)(lambda: wait_tile(1 - slot))
    pl.when(i == pl.num_programs(0) - 1)(lambda: wait_tile(slot))


def _dispatch(h_packed, pos, fill):
    n_tiles = SEQ // DISPATCH_TM
    return pl.pallas_call(
        _dispatch_kernel,
        grid_spec=pltpu.PrefetchScalarGridSpec(
            num_scalar_prefetch=1,
            grid=(n_tiles,),
            in_specs=[
                pl.BlockSpec((1, 1, DISPATCH_TM * TOP_K), lambda i, f: (i, 0, 0), memory_space=pltpu.SMEM),
                pl.BlockSpec((DISPATCH_TM * ROW_SUB, LANES), lambda i, f: (i, 0)),
            ],
            out_specs=pl.BlockSpec(memory_space=pl.ANY),
            scratch_shapes=[
                pltpu.VMEM((MOE_G * ROW_SUB, LANES), jnp.uint32),
                pltpu.VMEM((2, DISPATCH_TM * ROW_SUB, LANES), jnp.uint32),
                pltpu.SemaphoreType.DMA((2,)),
                pltpu.SemaphoreType.DMA(()),
            ],
        ),
        out_shape=jax.ShapeDtypeStruct((X_ROWS * ROW_SUB, LANES), jnp.uint32),
        compiler_params=_params("arbitrary"),
        name="dispatch",
    )(fill, pos.reshape(n_tiles, 1, DISPATCH_TM * TOP_K), h_packed)


def _route(logits):
    top_val, top_idx = lax.top_k(logits, TOP_K)
    gates = jax.nn.softmax(top_val, axis=-1)
    e_flat = top_idx.reshape(-1).astype(jnp.int32)
    experts = jnp.arange(N_EXPERTS, dtype=jnp.int32)
    counts = jnp.sum((e_flat[:, None] == experts[None, :]).astype(jnp.int32), axis=0)
    groups = (counts + MOE_G - 1) // MOE_G
    group_end = jnp.cumsum(groups)
    row0 = (group_end - groups) * MOE_G
    per_item = MOE_RMAX // MOE_G
    n_items_e = (groups + per_item - 1) // per_item
    item_end = jnp.cumsum(n_items_e)
    item_start = item_end - n_items_e
    n_items = item_end[-1]

    item = jnp.arange(MOE_ITEMS, dtype=jnp.int32)
    used = item < n_items
    e_item = jnp.minimum(jnp.searchsorted(item_end, item, side='right'), N_EXPERTS - 1).astype(jnp.int32)
    e_last = e_item[jnp.maximum(n_items - 1, 0)]
    part = item - item_start[e_item]
    rows = jnp.where(used, jnp.clip(groups[e_item] - part * per_item, 0, per_item) * MOE_G, 0)
    first_row = jnp.where(used, row0[e_item] + part * MOE_RMAX, 0)
    meta = jnp.concatenate([jnp.where(used, e_item, e_last), rows, first_row, group_end[-1:]]).astype(jnp.int32)

    order = jnp.argsort(e_flat, stable=True).astype(jnp.int32)
    rank = jnp.argsort(order).astype(jnp.int32)
    start = jnp.cumsum(counts) - counts
    shift = jnp.sum(jnp.where(e_flat[:, None] == experts[None, :], (row0 - start)[None, :], 0), axis=1)
    pos = (rank + shift) * ROW_SUB
    n_fill = jnp.sum((groups > 0).astype(jnp.int32)) + X_ROWS // MOE_G - group_end[-1]
    fill = jnp.concatenate([group_end, n_fill[None]]).astype(jnp.int32)

    group = jnp.arange(N_GROUPS, dtype=jnp.int32)
    mine = (group[:, None] >= (group_end - groups)[None, :]) & (group[:, None] < group_end[None, :])
    pick = lambda per_expert: jnp.sum(jnp.where(mine, per_expert[None, :], 0), axis=1)
    in_use = group < group_end[-1]
    nth = group - pick(group_end - groups)
    window0 = jnp.where(in_use, pick(start) + nth * MOE_G, 0)
    n_real = jnp.where(in_use, jnp.clip(pick(counts) - nth * MOE_G, 0, MOE_G), 0).astype(jnp.int32)
    dst_tab = _row_tables(jnp.concatenate([window0, n_real]).astype(jnp.int32), order)
    return gates, meta, pos.astype(jnp.int32), fill, dst_tab, n_real


def kernel(x, c, w_ada, b_ada, w_in, sgu_ln_g, sgu_ln_b, w_spatial, b_spatial, w_o, ln1_g, ln1_b,
           w_router, b_router, w_exp1, b_exp1, w_exp2, b_exp2, ln2_g, ln2_b):
    depth = w_ada.shape[0]
    assert x.shape == (1, SEQ, D_MODEL)
    xs = x.reshape(SEQ, D_MODEL)
    for l in range(depth):
        ada = _ada(c, w_ada[l], b_ada[l])
        shift1, scale1, gate1, shift2, scale2, gate2 = jnp.split(ada, 6, axis=-1)

        qkv, y_a = _in_proj(xs, scale1, shift1, w_in[l].astype(jnp.bfloat16),
                            sgu_ln_g[l], sgu_ln_b[l], w_spatial[l], b_spatial[l])
        y_b = _attention(qkv)
        x1, h_packed, logits = _out_proj(y_a, y_b, w_o[l].astype(jnp.bfloat16), xs, gate1,
                                         ln1_g[l].reshape(1, -1), ln1_b[l].reshape(1, -1), scale2, shift2,
                                         w_router[l], b_router[l].reshape(1, -1))

        gates, meta, pos, fill, dst_tab, n_real = _route(logits)
        x_sorted = _dispatch(h_packed, pos, fill)
        y_sorted = _experts(x_sorted, meta, w_exp1[l], b_exp1[l], w_exp2[l], b_exp2[l])
        y_packed = _collect(y_sorted, dst_tab, n_real)
        xs = _combine(y_packed, gates, x1, gate2, ln2_g[l].reshape(1, -1), ln2_b[l].reshape(1, -1))
    return xs.reshape(x.shape)
```

```python
import math

import jax
import jax.numpy as jnp
from jax import lax
from jax.experimental import pallas as pl
from jax.experimental.pallas import tpu as pltpu

D_MODEL = 2048
SEQ = 8192
D_GMLP = 1024
GMLP_GROUPS = 8
GROUP_DIM = 128
CHUNK = 128
D_ATTN = 1024
HEAD_DIM = 128
N_HEADS = 8
BRANCHES = ((128, 1), (512, 4), (2048, 16))
BLK = 128
SPAN = 16 * BLK
D_IN_PROJ = 2 * D_GMLP + 3 * D_ATTN
N_EXPERTS = 32
TOP_K = 4
D_EXPERT = 2048
SWIGLU_LIMIT = 7.0
SWIGLU_ALPHA = 1.702
LN_EPS = 1e-5
DEEPNORM_ALPHA = 2.0 ** 0.25
NEG = -1e30

LANES = 128
VMEM_LIMIT = 56 * 1024 * 1024

ADA_TN = 1536
PROJ_TM = 1024
PROJ_TN = 1024
OUT_TM = 512
ATTN_UNROLL = 16
ROW_SUB = 8
D_PACK = D_MODEL // 2
MOE_G = 128
MOE_RMAX = 1152
MOE_ITEMS = 61
MOE_TN = 512
MOE_CHUNKS = D_EXPERT // MOE_TN
MOE_SUBTILES = (512, 256, 128)
X_ROWS = SEQ * TOP_K + N_EXPERTS * MOE_G
DISPATCH_TM = 512
COLLECT_GROUPS = 8
COMBINE_TM = 512
MAX_WORK_ITEMS = (X_ROWS // MOE_G + N_EXPERTS * (MOE_RMAX // MOE_G - 1)) // (MOE_RMAX // MOE_G)
assert MAX_WORK_ITEMS < MOE_ITEMS


def _params(*sem):
    return pltpu.CompilerParams(dimension_semantics=sem, vmem_limit_bytes=VMEM_LIMIT)


def _layer_norm(x, g, b):
    mu = jnp.mean(x, axis=-1, keepdims=True)
    xc = x - mu
    var = jnp.mean(xc * xc, axis=-1, keepdims=True)
    return xc * lax.rsqrt(var + LN_EPS) * g + b


def _pack_rows(x):
    r = x.astype(jnp.bfloat16).astype(jnp.float32)
    bits = lax.bitcast_convert_type(r, jnp.uint32)
    return (bits[:, D_PACK:] & jnp.uint32(0xFFFF0000)) | (bits[:, :D_PACK] >> 16)


def _unpack_words(u):
    lo = lax.bitcast_convert_type(u << 16, jnp.float32)
    hi = lax.bitcast_convert_type(u & jnp.uint32(0xFFFF0000), jnp.float32)
    return lo, hi


def _store_packed(ref, lead, row0, packed):
    m = packed.shape[0]
    for c in range(ROW_SUB):
        rows = pl.ds(row0 * ROW_SUB + c, m, stride=ROW_SUB)
        ref[lead + (rows, slice(None))] = packed[:, c * LANES:(c + 1) * LANES]


def _load_packed(ref, lead, row0, m):
    cols = [ref[lead + (pl.ds(row0 * ROW_SUB + c, m, stride=ROW_SUB), slice(None))] for c in range(ROW_SUB)]
    return jnp.concatenate(cols, axis=1)


def _ada_kernel(c_ref, w_ref, b_ref, o_ref):
    c = c_ref[...]
    s = c * jax.nn.sigmoid(c)
    o_ref[...] = jnp.sum(s * w_ref[...], axis=0, keepdims=True) + b_ref[...]


def _ada(c, w_ada, b_ada):
    n = w_ada.shape[1]
    return pl.pallas_call(
        _ada_kernel,
        grid=(n // ADA_TN,),
        in_specs=[
            pl.BlockSpec((D_MODEL, 1), lambda j: (0, 0)),
            pl.BlockSpec((D_MODEL, ADA_TN), lambda j: (0, j)),
            pl.BlockSpec((1, ADA_TN), lambda j: (0, j)),
        ],
        out_specs=pl.BlockSpec((1, ADA_TN), lambda j: (0, j)),
        out_shape=jax.ShapeDtypeStruct((1, n), jnp.float32),
        compiler_params=_params("arbitrary"),
        name="ada",
    )(c.reshape(D_MODEL, 1), w_ada, b_ada.reshape(1, n))


def _gelu(x):
    return 0.5 * x * (1.0 + lax.erf(x * (1.0 / math.sqrt(2.0))))


GROUPS_PER_STEP = 3


def _in_proj_kernel(x_ref, sc_ref, sh_ref, w_ref, g_ref, b_ref, ws_ref, bs_ref, qkv_ref, ya_ref, h_ref, uv_ref):
    i = pl.program_id(0)
    j = pl.program_id(1)

    @pl.when((i == 0) & (j == 0))
    def _():
        uv_ref[...] = jnp.zeros_like(uv_ref)

    @pl.when(j == 0)
    def _():
        h_ref[...] = (x_ref[...] * (1.0 + sc_ref[...]) + sh_ref[...]).astype(jnp.bfloat16)

    row = lax.broadcasted_iota(jnp.int32, (CHUNK, CHUNK), 0)
    col = lax.broadcasted_iota(jnp.int32, (CHUNK, CHUNK), 1)
    causal = col <= row
    chunks = [slice(c * CHUNK, (c + 1) * CHUNK) for c in range(PROJ_TM // CHUNK)]
    for u in range(GROUPS_PER_STEP):
        g = jnp.clip(GROUPS_PER_STEP * (j - 2) + u, 0, GMLP_GROUPS - 1)
        lanes = pl.ds(pl.multiple_of(g * GROUP_DIM, GROUP_DIM), GROUP_DIM)
        ln_g = g_ref[pl.ds(g, 1), :]
        ln_b = b_ref[pl.ds(g, 1), :]
        v = [_layer_norm(_gelu(uv_ref[1, rows, lanes]), ln_g, ln_b).astype(jnp.bfloat16) for rows in chunks]
        w = jnp.where(causal, ws_ref[g], 0.0).astype(jnp.bfloat16)
        s = jnp.dot(w, jnp.concatenate(v, axis=1), preferred_element_type=jnp.float32)
        for c, rows in enumerate(chunks):
            s_c = s[:, c * GROUP_DIM:(c + 1) * GROUP_DIM] + bs_ref[g]
            ya_ref[rows, lanes] = (_gelu(uv_ref[0, rows, lanes]) * s_c).astype(ya_ref.dtype)

    qkv_ref[...] = jnp.dot(h_ref[...], w_ref[...], preferred_element_type=jnp.float32)

    @pl.when(j < 2)
    def _():
        uv_ref[jnp.minimum(j, 1)] = qkv_ref[...]


def _in_proj(x, scale1, shift1, w_in_bf16, ln_g, ln_b, w_spatial, b_spatial):
    n_uv = 2 * D_GMLP // PROJ_TN
    assert GROUPS_PER_STEP * (D_IN_PROJ // PROJ_TN - n_uv) >= GMLP_GROUPS
    fixed2 = lambda i, j: (0, 0)
    bs_lanes = jnp.broadcast_to(b_spatial[:, :, None], (GMLP_GROUPS, CHUNK, LANES))
    return pl.pallas_call(
        _in_proj_kernel,
        grid=(SEQ // PROJ_TM, D_IN_PROJ // PROJ_TN),
        in_specs=[
            pl.BlockSpec((PROJ_TM, D_MODEL), lambda i, j: (i, 0)),
            pl.BlockSpec((1, D_MODEL), fixed2),
            pl.BlockSpec((1, D_MODEL), fixed2),
            pl.BlockSpec((D_MODEL, PROJ_TN), lambda i, j: (0, j)),
            pl.BlockSpec((GMLP_GROUPS, GROUP_DIM), fixed2),
            pl.BlockSpec((GMLP_GROUPS, GROUP_DIM), fixed2),
            pl.BlockSpec((GMLP_GROUPS, CHUNK, CHUNK), lambda i, j: (0, 0, 0)),
            pl.BlockSpec((GMLP_GROUPS, CHUNK, LANES), lambda i, j: (0, 0, 0)),
        ],
        out_specs=[
            pl.BlockSpec((PROJ_TM, PROJ_TN), lambda i, j: (i, jnp.maximum(j - n_uv, 0))),
            pl.BlockSpec((PROJ_TM, D_GMLP), lambda i, j: (i, 0)),
        ],
        out_shape=[
            jax.ShapeDtypeStruct((SEQ, 3 * D_ATTN), jnp.float32),
            jax.ShapeDtypeStruct((SEQ, D_GMLP), jnp.bfloat16),
        ],
        scratch_shapes=[
            pltpu.VMEM((PROJ_TM, D_MODEL), jnp.bfloat16),
            pltpu.VMEM((2, PROJ_TM, D_GMLP), jnp.float32),
        ],
        compiler_params=_params("arbitrary", "arbitrary"),
        name="in_proj",
    )(x, scale1, shift1, w_in_bf16, ln_g, ln_b, w_spatial, bs_lanes)


def _attn_kernel(q_ref, k_ref, v_ref, o_ref, out_ref, lse_ref, kv_ref):
    head = pl.program_id(0)
    span = pl.program_id(1)
    log2e = 1.0 / math.log(2.0)
    head_no = (jnp.zeros((BLK, 2 * BLK), jnp.int32) + (head + 1)).astype(jnp.float32)
    slope = jnp.exp2(head_no * (-8.0 / N_HEADS)) * log2e
    scale = HEAD_DIM ** -0.5 * log2e
    qi = lax.broadcasted_iota(jnp.int32, (BLK, 2 * BLK), 0)
    ki = lax.broadcasted_iota(jnp.int32, (BLK, 2 * BLK), 1)
    step = qi + BLK - ki
    contract_last = (((1,), (1,)), ((), ()))
    ones = jnp.ones((2 * BLK, LANES), jnp.bfloat16)

    for b, (window, d) in enumerate(BRANCHES):
        assert window // d == BLK
        valid = (step >= 0) & (step <= BLK)
        bias = jnp.where(valid, -slope * d * step.astype(jnp.float32), NEG)
        bias_first = jnp.where(ki >= BLK, bias, NEG)

        per_class = d * BLK == SPAN
        cur = span % 2
        if per_class:
            def stash(r, carry, d=d):
                rows = pl.ds(pl.multiple_of(r * BLK, BLK), BLK)
                kv_ref[0, cur, rows, :] = k_ref[pl.ds(span * SPAN + r, BLK, stride=d), :].astype(jnp.bfloat16)
                kv_ref[1, cur, rows, :] = v_ref[pl.ds(span * SPAN + r, BLK, stride=d), :].astype(jnp.bfloat16)
                return carry
            lax.fori_loop(0, d, stash, 0, unroll=4)

        def tile(t, carry, d=d, b=b, bias=bias, bias_first=bias_first, per_class=per_class):
            r = t % d
            n = t // d
            q0 = n * (BLK * d) + r
            k0 = span * SPAN + q0
            first = k0 < BLK * d
            q = (q_ref[pl.ds(q0, BLK, stride=d), :] * scale).astype(jnp.bfloat16)
            if per_class:
                rows = pl.ds(pl.multiple_of(r * BLK, BLK), BLK)
                prev = jnp.where(first, cur, 1 - cur)
                k = jnp.concatenate([kv_ref[0, prev, rows, :], kv_ref[0, cur, rows, :]], axis=0)
                v = jnp.concatenate([kv_ref[1, prev, rows, :], kv_ref[1, cur, rows, :]], axis=0)
            else:
                kp = jnp.where(first, k0, k0 - BLK * d)
                k = jnp.concatenate([k_ref[pl.ds(kp, BLK, stride=d), :], k_ref[pl.ds(k0, BLK, stride=d), :]],
                                    axis=0).astype(jnp.bfloat16)
                v = jnp.concatenate([v_ref[pl.ds(kp, BLK, stride=d), :], v_ref[pl.ds(k0, BLK, stride=d), :]],
                                    axis=0).astype(jnp.bfloat16)
            s = lax.dot_general(q, k, contract_last, preferred_element_type=jnp.float32)
            s = s + jnp.where(first, bias_first, bias)
            m = jnp.max(jnp.maximum(s[:, :BLK], s[:, BLK:]), axis=-1, keepdims=True)
            p = jnp.exp2(s - m).astype(jnp.bfloat16)
            v_one = jnp.concatenate([v, ones], axis=1)
            pv = jnp.dot(p, v_one, preferred_element_type=jnp.float32)
            den = pv[:, HEAD_DIM:]
            rows = pl.ds(q0, BLK, stride=d)
            out_ref[b, rows, :] = pv[:, :HEAD_DIM] / den
            lse_ref[b, rows, :] = m + jnp.log2(den)
            return carry

        lax.fori_loop(0, SPAN // BLK, tile, 0, unroll=ATTN_UNROLL)

    lse_all = jnp.maximum(jnp.maximum(lse_ref[0], lse_ref[1]), lse_ref[2])
    num = jnp.zeros((SPAN, HEAD_DIM), jnp.float32)
    den = jnp.zeros((SPAN, LANES), jnp.float32)
    for b in range(len(BRANCHES)):
        w = jnp.exp2(lse_ref[b] - lse_all)
        num = num + w * out_ref[b]
        den = den + w
    o_ref[...] = (num / den).astype(o_ref.dtype)


def _attention(proj):
    q_col = 0
    k_col = q_col + N_HEADS
    v_col = k_col + N_HEADS
    nb = len(BRANCHES)
    return pl.pallas_call(
        _attn_kernel,
        grid=(N_HEADS, SEQ // SPAN),
        in_specs=[
            pl.BlockSpec((SPAN, HEAD_DIM), lambda h, s: (s, q_col + h)),
            pl.BlockSpec((SEQ, HEAD_DIM), lambda h, s: (0, k_col + h)),
            pl.BlockSpec((SEQ, HEAD_DIM), lambda h, s: (0, v_col + h)),
        ],
        out_specs=pl.BlockSpec((SPAN, HEAD_DIM), lambda h, s: (s, h)),
        out_shape=jax.ShapeDtypeStruct((SEQ, D_ATTN), jnp.bfloat16),
        scratch_shapes=[
            pltpu.VMEM((nb, SPAN, HEAD_DIM), jnp.float32),
            pltpu.VMEM((nb, SPAN, LANES), jnp.float32),
            pltpu.VMEM((2, 2, SPAN, HEAD_DIM), jnp.bfloat16),
        ],
        compiler_params=_params("arbitrary", "arbitrary"),
        name="attn",
    )(proj, proj, proj)


def _split_bf16(x):
    hi = x.astype(jnp.bfloat16)
    lo = (x - hi.astype(jnp.float32)).astype(jnp.bfloat16)
    return hi, lo


def _out_proj_kernel(ya_ref, yb_ref, wa_ref, wb_ref, x_ref, gate_ref, g_ref, b_ref, sc_ref, sh_ref,
                     wr_ref, br_ref, x1_ref, hp_ref, lg_ref):
    mix = jnp.dot(ya_ref[...], wa_ref[...], preferred_element_type=jnp.float32)
    mix = mix + jnp.dot(yb_ref[...], wb_ref[...], preferred_element_type=jnp.float32)
    x1 = _layer_norm(DEEPNORM_ALPHA * x_ref[...] + gate_ref[...] * mix, g_ref[...], b_ref[...])
    x1_ref[...] = x1
    h = x1 * (1.0 + sc_ref[...]) + sh_ref[...]
    _store_packed(hp_ref, (), 0, _pack_rows(h))
    h_hi, h_lo = _split_bf16(h)
    w_hi, w_lo = _split_bf16(wr_ref[...])
    both = jnp.dot(h_hi, jnp.concatenate([w_hi, w_lo], axis=1), preferred_element_type=jnp.float32)
    lg = both[:, :N_EXPERTS] + both[:, N_EXPERTS:] + jnp.dot(h_lo, w_hi, preferred_element_type=jnp.float32)
    lg_ref[...] = lg + br_ref[...]


def _out_proj(y_a, y_b, w_o_bf16, x, gate1, ln_g, ln_b, scale2, shift2, w_router, b_router):
    row = lambda i: (i, 0)
    fixed = lambda i: (0, 0)
    vec = pl.BlockSpec((1, D_MODEL), fixed)
    return pl.pallas_call(
        _out_proj_kernel,
        grid=(SEQ // OUT_TM,),
        in_specs=[
            pl.BlockSpec((OUT_TM, D_GMLP), row),
            pl.BlockSpec((OUT_TM, D_ATTN), row),
            pl.BlockSpec((D_GMLP, D_MODEL), lambda i: (0, 0)),
            pl.BlockSpec((D_ATTN, D_MODEL), lambda i: (1, 0)),
            pl.BlockSpec((OUT_TM, D_MODEL), row),
            vec, vec, vec, vec, vec,
            pl.BlockSpec((D_MODEL, N_EXPERTS), fixed),
            pl.BlockSpec((1, N_EXPERTS), fixed),
        ],
        out_specs=[
            pl.BlockSpec((OUT_TM, D_MODEL), row),
            pl.BlockSpec((OUT_TM * ROW_SUB, LANES), row),
            pl.BlockSpec((OUT_TM, N_EXPERTS), row),
        ],
        out_shape=[
            jax.ShapeDtypeStruct((SEQ, D_MODEL), jnp.float32),
            jax.ShapeDtypeStruct((SEQ * ROW_SUB, LANES), jnp.uint32),
            jax.ShapeDtypeStruct((SEQ, N_EXPERTS), jnp.float32),
        ],
        compiler_params=_params("arbitrary"),
        name="out_proj",
    )(y_a, y_b, w_o_bf16, w_o_bf16, x, gate1, ln_g, ln_b, scale2, shift2, w_router, b_router)


def _experts_kernel(meta_ref, x_hbm, w1_hbm, w2_hbm, b1_ref, b2_ref, y_hbm,
                    xs_ref, os_ref, acc_ref, wg_buf, wl_buf, w2_buf, gsem, ssem, wsem):
    i = pl.program_id(0)
    n_items = pl.num_programs(0)

    def item_rows(k):
        inside = (k >= 0) & (k < n_items)
        return jnp.where(inside, meta_ref[MOE_ITEMS + jnp.clip(k, 0, n_items - 1)], 0)

    def item_first(k):
        return meta_ref[2 * MOE_ITEMS + jnp.clip(k, 0, n_items - 1)] * ROW_SUB

    rows = item_rows(i)
    rows_prev = item_rows(i - 1)
    rows_next = item_rows(i + 1)
    p = i % 2
    q = 1 - p
    group_sub = MOE_G * ROW_SUB

    def each_group(n_rows, fn):
        lax.fori_loop(0, n_rows // MOE_G, lambda t, c: (fn(pl.multiple_of(t * group_sub, group_sub)), c)[1], 0)

    def load_rows(k, slot):
        first = item_first(k)
        each_group(item_rows(k), lambda at: pltpu.make_async_copy(
            x_hbm.at[pl.ds(pl.multiple_of(first + at, group_sub), group_sub), :],
            xs_ref.at[slot, pl.ds(at, group_sub), :], gsem.at[slot]).start())

    def wait_rows(buf_ref, sem, n_rows):
        group = buf_ref.at[pl.ds(0, group_sub), :]
        each_group(n_rows, lambda at: pltpu.make_async_copy(group, group, sem).wait())

    def weight_copies(k, j, slot):
        e = meta_ref[jnp.clip(k, 0, n_items - 1)]
        c0 = pl.multiple_of(j * MOE_TN, MOE_TN)
        return (pltpu.make_async_copy(w1_hbm.at[e, :, pl.ds(c0, MOE_TN)], wg_buf.at[slot], wsem.at[slot]),
                pltpu.make_async_copy(w1_hbm.at[e, :, pl.ds(D_EXPERT + c0, MOE_TN)], wl_buf.at[slot],
                                      wsem.at[slot]),
                pltpu.make_async_copy(w2_hbm.at[e, pl.ds(c0, MOE_TN), :], w2_buf.at[slot], wsem.at[slot]))

    @pl.when(i == 0)
    def _():
        acc_ref[...] = jnp.zeros_like(acc_ref)
        os_ref[...] = jnp.zeros_like(os_ref)
        zeros = os_ref.at[pl.ds(0, group_sub), :]
        first_unused = meta_ref[3 * MOE_ITEMS]
        n_unused = X_ROWS // MOE_G - first_unused

        def fill(t, c):
            dst = pl.multiple_of((first_unused + t) * group_sub, group_sub)
            pltpu.make_async_copy(zeros, y_hbm.at[pl.ds(dst, group_sub), :], ssem).start()
            return c
        lax.fori_loop(0, n_unused, fill, 0)
        wait_rows(os_ref, ssem, n_unused * MOE_G)
        load_rows(0, 0)

        @pl.when(rows > 0)
        def _():
            for cp in weight_copies(0, 0, 0):
                cp.start()

    wait_rows(xs_ref.at[p], gsem.at[p], rows)
    load_rows(i + 1, q)

    def sub_tile(a, m, j, slot):
        a = pl.multiple_of(a, MOE_G)
        x_lo, x_hi = _unpack_words(_load_packed(xs_ref, (p,), a, m))
        x = jnp.concatenate([x_lo.astype(jnp.bfloat16), x_hi.astype(jnp.bfloat16)], axis=1)

        def up(w_buf, first_col):
            bias = b1_ref[:, pl.ds(pl.multiple_of(first_col + j * MOE_TN, MOE_TN), MOE_TN)]
            return jnp.dot(x, w_buf[slot].astype(jnp.bfloat16), preferred_element_type=jnp.float32) + bias

        glu = jnp.minimum(up(wg_buf, 0), SWIGLU_LIMIT)
        lin = jnp.clip(up(wl_buf, D_EXPERT), -SWIGLU_LIMIT, SWIGLU_LIMIT)
        act = glu * jax.nn.sigmoid(SWIGLU_ALPHA * glu) * (lin + 1.0)
        down = jnp.dot(act.astype(jnp.bfloat16), w2_buf[slot].astype(jnp.bfloat16),
                       preferred_element_type=jnp.float32)
        start = jnp.where(j == 0, jnp.broadcast_to(b2_ref[...], (m, D_MODEL)), acc_ref[pl.ds(a, m), :])
        acc_ref[pl.ds(a, m), :] = start + down

    def chunk(j, carry):
        slot = j % 2
        for cp in weight_copies(i, j, slot):
            cp.wait()
        last = j == MOE_CHUNKS - 1

        @pl.when(jnp.logical_not(last) | (rows_next > 0))
        def _():
            for cp in weight_copies(jnp.where(last, i + 1, i), jnp.where(last, 0, j + 1), 1 - slot):
                cp.start()

        main = MOE_SUBTILES[0]
        n_main = rows // main
        lax.fori_loop(0, n_main, lambda t, c: (sub_tile(t * main, main, j, slot), c)[1], 0)
        done = n_main * main
        for m in MOE_SUBTILES[1:]:
            has = ((rows - done) // m) % 2 == 1
            pl.when(has)(lambda done=done, m=m: sub_tile(done, m, j, slot))
            done = done + jnp.where(has, m, 0)
        return carry

    @pl.when(rows > 0)
    def _():
        lax.fori_loop(0, MOE_CHUNKS, chunk, 0)

    wait_rows(os_ref, ssem, rows_prev)

    def pack(t, c):
        a = pl.multiple_of(t * MOE_G, MOE_G)
        _store_packed(os_ref, (), a, _pack_rows(acc_ref[pl.ds(a, MOE_G), :]))
        return c
    lax.fori_loop(0, rows // MOE_G, pack, 0)
    first = item_first(i)
    each_group(rows, lambda at: pltpu.make_async_copy(
        os_ref.at[pl.ds(at, group_sub), :],
        y_hbm.at[pl.ds(pl.multiple_of(first + at, group_sub), group_sub), :], ssem).start())


def _experts(x_sorted, meta, w1, b1, w2, b2):
    return pl.pallas_call(
        _experts_kernel,
        grid_spec=pltpu.PrefetchScalarGridSpec(
            num_scalar_prefetch=1,
            grid=(MOE_ITEMS,),
            in_specs=[
                pl.BlockSpec(memory_space=pl.ANY),
                pl.BlockSpec(memory_space=pl.ANY),
                pl.BlockSpec(memory_space=pl.ANY),
                pl.BlockSpec((None, 1, 2 * D_EXPERT), lambda i, m: (m[i], 0, 0)),
                pl.BlockSpec((None, 1, D_MODEL), lambda i, m: (m[i], 0, 0)),
            ],
            out_specs=pl.BlockSpec(memory_space=pl.ANY),
            scratch_shapes=[
                pltpu.VMEM((2, MOE_RMAX * ROW_SUB, LANES), jnp.uint32),
                pltpu.VMEM((MOE_RMAX * ROW_SUB, LANES), jnp.uint32),
                pltpu.VMEM((MOE_RMAX, D_MODEL), jnp.float32),
                pltpu.VMEM((2, D_MODEL, MOE_TN), jnp.float32),
                pltpu.VMEM((2, D_MODEL, MOE_TN), jnp.float32),
                pltpu.VMEM((2, MOE_TN, D_MODEL), jnp.float32),
                pltpu.SemaphoreType.DMA((2,)),
                pltpu.SemaphoreType.DMA(()),
                pltpu.SemaphoreType.DMA((2,)),
            ],
        ),
        out_shape=jax.ShapeDtypeStruct((X_ROWS * ROW_SUB, LANES), jnp.uint32),
        compiler_params=_params("arbitrary"),
        name="experts",
    )(meta, x_sorted, w1, w2, b1.reshape(N_EXPERTS, 1, 2 * D_EXPERT), b2.reshape(N_EXPERTS, 1, D_MODEL))


def _combine_kernel(y0_ref, y1_ref, y2_ref, y3_ref, gates_ref, x1_ref, gate2_ref, g_ref, b_ref, o_ref):
    y_lo = jnp.zeros((COMBINE_TM, D_PACK), jnp.float32)
    y_hi = jnp.zeros((COMBINE_TM, D_PACK), jnp.float32)
    for k, y_ref in enumerate((y0_ref, y1_ref, y2_ref, y3_ref)):
        lo, hi = _unpack_words(_load_packed(y_ref, (), 0, COMBINE_TM))
        gate = gates_ref[:, k:k + 1]
        y_lo = y_lo + gate * lo
        y_hi = y_hi + gate * hi
    y = jnp.concatenate([y_lo, y_hi], axis=1)
    o_ref[...] = _layer_norm(DEEPNORM_ALPHA * x1_ref[...] + gate2_ref[...] * y, g_ref[...], b_ref[...])


def _combine(y_packed, gates, x1, gate2, ln_g, ln_b):
    n_tiles = SEQ // COMBINE_TM
    row = lambda i: (i, 0)
    vec = pl.BlockSpec((1, D_MODEL), lambda i: (0, 0))
    slot = lambda k: pl.BlockSpec((COMBINE_TM * ROW_SUB, LANES), lambda i: (k * n_tiles + i, 0))
    return pl.pallas_call(
        _combine_kernel,
        grid=(n_tiles,),
        in_specs=[
            slot(0), slot(1), slot(2), slot(3),
            pl.BlockSpec((COMBINE_TM, TOP_K), row),
            pl.BlockSpec((COMBINE_TM, D_MODEL), row),
            vec, vec, vec,
        ],
        out_specs=pl.BlockSpec((COMBINE_TM, D_MODEL), row),
        out_shape=jax.ShapeDtypeStruct((SEQ, D_MODEL), jnp.float32),
        compiler_params=_params("arbitrary"),
        name="combine",
    )(y_packed, y_packed, y_packed, y_packed, gates, x1, gate2, ln_g, ln_b)


N_GROUPS = X_ROWS // MOE_G
WINDOW_ROWS = ROW_SUB
assert MOE_G == LANES and TOP_K == 4


def _row_tables_kernel(win_ref, order_ref, dst_ref):
    lane = lax.broadcasted_iota(jnp.int32, (1, LANES), 1)

    def table(g, carry):
        w0 = win_ref[g]
        n_real = win_ref[N_GROUPS + g]
        off = w0 % LANES
        x = order_ref[pl.ds(w0 // LANES, WINDOW_ROWS), :]
        x = pltpu.roll(x, (LANES - off) % LANES, axis=1)
        flat = jnp.where(lane < LANES - off, x[0:1], x[1:2])
        tok = flat >> 2
        slot = flat & 3
        dst_ref[pl.ds(g, 1), :] = jnp.where(lane < n_real, slot * SEQ + tok, 0) * ROW_SUB
        return carry
    lax.fori_loop(0, N_GROUPS, table, 0, unroll=8)


def _row_tables(win, order):
    n_rows = SEQ * TOP_K // LANES
    order2d = jnp.concatenate([order, jnp.zeros((WINDOW_ROWS * LANES,), jnp.int32)]).reshape(-1, LANES)
    return pl.pallas_call(
        _row_tables_kernel,
        grid_spec=pltpu.PrefetchScalarGridSpec(
            num_scalar_prefetch=1,
            grid=(1,),
            in_specs=[pl.BlockSpec((n_rows + WINDOW_ROWS, LANES), lambda b, w: (0, 0))],
            out_specs=pl.BlockSpec((N_GROUPS, LANES), lambda b, w: (0, 0)),
        ),
        out_shape=jax.ShapeDtypeStruct((N_GROUPS, LANES), jnp.int32),
        compiler_params=_params("arbitrary"),
        name="row_tables",
    )(win, order2d)


def _collect_kernel(real_ref, dst_ref, y_ref, out_hbm, buf_ref, sem):
    i = pl.program_id(0)
    slot = i % 2
    buf_ref[slot] = y_ref[...]

    def row_copy(s, r):
        at = r * ROW_SUB if isinstance(r, int) else pl.multiple_of(r * ROW_SUB, ROW_SUB)
        dst = pl.multiple_of(dst_ref[0, 0, r], ROW_SUB)
        return pltpu.make_async_copy(buf_ref.at[s, pl.ds(at, ROW_SUB), :], out_hbm.at[pl.ds(dst, ROW_SUB), :],
                                     sem.at[s])

    group_sub = MOE_G * ROW_SUB

    def rows_loop(n, fn, **kw):
        lax.fori_loop(0, n, lambda r, c: (fn(r), c)[1], 0, **kw)

    def start_tile(step, s):
        for h in range(COLLECT_GROUPS):
            n_real = real_ref[step * COLLECT_GROUPS + h]
            row0 = h * MOE_G

            @pl.when(n_real == MOE_G)
            def _():
                for r in range(MOE_G):
                    row_copy(s, row0 + r).start(priority=r % 2)

            @pl.when(n_real < MOE_G)
            def _():
                rows_loop(n_real, lambda r, row0=row0: row_copy(s, row0 + r).start())

    def wait_tile(step, s):
        whole = buf_ref.at[s, pl.ds(0, group_sub), :]
        for h in range(COLLECT_GROUPS):
            n_real = real_ref[step * COLLECT_GROUPS + h]

            @pl.when(n_real == MOE_G)
            def _():
                pltpu.make_async_copy(whole, whole, sem.at[s]).wait()

            @pl.when(n_real < MOE_G)
            def _():
                rows_loop(n_real, lambda r: row_copy(s, 0).wait())

    start_tile(i, slot)

    @pl.when(i > 0)
    def _():
        wait_tile(i - 1, 1 - slot)

    @pl.when(i == pl.num_programs(0) - 1)
    def _():
        wait_tile(i, slot)


def _collect(y_sorted, dst_tab, n_real):
    n_tiles = N_GROUPS // COLLECT_GROUPS
    tile_rows = COLLECT_GROUPS * MOE_G
    return pl.pallas_call(
        _collect_kernel,
        grid_spec=pltpu.PrefetchScalarGridSpec(
            num_scalar_prefetch=1,
            grid=(n_tiles,),
            in_specs=[
                pl.BlockSpec((1, 1, tile_rows), lambda i, n: (i, 0, 0), memory_space=pltpu.SMEM),
                pl.BlockSpec((tile_rows * ROW_SUB, LANES), lambda i, n: (i, 0)),
            ],
            out_specs=pl.BlockSpec(memory_space=pl.ANY),
            scratch_shapes=[
                pltpu.VMEM((2, tile_rows * ROW_SUB, LANES), jnp.uint32),
                pltpu.SemaphoreType.DMA((2,)),
            ],
        ),
        out_shape=jax.ShapeDtypeStruct((SEQ * TOP_K * ROW_SUB, LANES), jnp.uint32),
        compiler_params=_params("arbitrary"),
        name="collect",
    )(n_real, dst_tab.reshape(n_tiles, 1, tile_rows), y_sorted)


def _dispatch_kernel(fill_ref, pos_ref, h_ref, x_hbm, zero_ref, buf_ref, sem, zsem):
    i = pl.program_id(0)
    group_sub = MOE_G * ROW_SUB

    @pl.when(i == 0)
    def _():
        zero_ref[...] = jnp.zeros_like(zero_ref)

        def fill(g):
            dst = pl.multiple_of(g * group_sub, group_sub)
            pltpu.make_async_copy(zero_ref, x_hbm.at[pl.ds(dst, group_sub), :], zsem).start()

        for e in range(N_EXPERTS):
            end = fill_ref[e]
            begin = fill_ref[e - 1] if e else 0
            pl.when(end > begin)(lambda end=end: fill(end - 1))
        lax.fori_loop(fill_ref[N_EXPERTS - 1], X_ROWS // MOE_G, lambda g, c: (fill(g), c)[1], 0)

        def wait_fill(t, c):
            pltpu.make_async_copy(zero_ref, zero_ref, zsem).wait()
            return c
        lax.fori_loop(0, fill_ref[N_EXPERTS], wait_fill, 0)

    slot = i % 2
    buf_ref[slot] = h_ref[...]

    def token(t, c):
        src = pl.multiple_of(t * ROW_SUB, ROW_SUB)
        for k in range(TOP_K):
            dst = pl.multiple_of(pos_ref[0, 0, t * TOP_K + k], ROW_SUB)
            pltpu.make_async_copy(buf_ref.at[slot, pl.ds(src, ROW_SUB), :], x_hbm.at[pl.ds(dst, ROW_SUB), :],
                                  sem.at[slot]).start(priority=k % 2)
        return c
    lax.fori_loop(0, DISPATCH_TM, token, 0, unroll=16)

    def wait_tile(s):
        for k in range(TOP_K):
            pltpu.make_async_copy(buf_ref.at[s], buf_ref.at[s], sem.at[s]).wait()

    pl.when(i > 0)(lambda: wait_tile(1 - slot))
    pl.when(i == pl.num_programs(0) - 1)(lambda: wait_tile(slot))


def _dispatch(h_packed, pos, fill):
    n_tiles = SEQ // DISPATCH_TM
    return pl.pallas_call(
        _dispatch_kernel,
        grid_spec=pltpu.PrefetchScalarGridSpec(
            num_scalar_prefetch=1,
            grid=(n_tiles,),
            in_specs=[
                pl.BlockSpec((1, 1, DISPATCH_TM * TOP_K), lambda i, f: (i, 0, 0), memory_space=pltpu.SMEM),
                pl.BlockSpec((DISPATCH_TM * ROW_SUB, LANES), lambda i, f: (i, 0)),
            ],
            out_specs=pl.BlockSpec(memory_space=pl.ANY),
            scratch_shapes=[
                pltpu.VMEM((MOE_G * ROW_SUB, LANES), jnp.uint32),
                pltpu.VMEM((2, DISPATCH_TM * ROW_SUB, LANES), jnp.uint32),
                pltpu.SemaphoreType.DMA((2,)),
                pltpu.SemaphoreType.DMA(()),
            ],
        ),
        out_shape=jax.ShapeDtypeStruct((X_ROWS * ROW_SUB, LANES), jnp.uint32),
        compiler_params=_params("arbitrary"),
        name="dispatch",
    )(fill, pos.reshape(n_tiles, 1, DISPATCH_TM * TOP_K), h_packed)


def _route(logits):
    top_val, top_idx = lax.top_k(logits, TOP_K)
    gates = jax.nn.softmax(top_val, axis=-1)
    e_flat = top_idx.reshape(-1).astype(jnp.int32)
    experts = jnp.arange(N_EXPERTS, dtype=jnp.int32)
    counts = jnp.sum((e_flat[:, None] == experts[None, :]).astype(jnp.int32), axis=0)
    groups = (counts + MOE_G - 1) // MOE_G
    group_end = jnp.cumsum(groups)
    row0 = (group_end - groups) * MOE_G
    per_item = MOE_RMAX // MOE_G
    n_items_e = (groups + per_item - 1) // per_item
    item_end = jnp.cumsum(n_items_e)
    item_start = item_end - n_items_e
    n_items = item_end[-1]

    item = jnp.arange(MOE_ITEMS, dtype=jnp.int32)
    used = item < n_items
    e_item = jnp.minimum(jnp.searchsorted(item_end, item, side='right'), N_EXPERTS - 1).astype(jnp.int32)
    e_last = e_item[jnp.maximum(n_items - 1, 0)]
    part = item - item_start[e_item]
    rows = jnp.where(used, jnp.clip(groups[e_item] - part * per_item, 0, per_item) * MOE_G, 0)
    first_row = jnp.where(used, row0[e_item] + part * MOE_RMAX, 0)
    meta = jnp.concatenate([jnp.where(used, e_item, e_last), rows, first_row, group_end[-1:]]).astype(jnp.int32)

    order = jnp.argsort(e_flat, stable=True).astype(jnp.int32)
    rank = jnp.argsort(order).astype(jnp.int32)
    start = jnp.cumsum(counts) - counts
    shift = jnp.sum(jnp.where(e_flat[:, None] == experts[None, :], (row0 - start)[None, :], 0), axis=1)
    pos = (rank + shift) * ROW_SUB
    n_fill = jnp.sum((groups > 0).astype(jnp.int32)) + X_ROWS // MOE_G - group_end[-1]
    fill = jnp.concatenate([group_end, n_fill[None]]).astype(jnp.int32)

    group = jnp.arange(N_GROUPS, dtype=jnp.int32)
    mine = (group[:, None] >= (group_end - groups)[None, :]) & (group[:, None] < group_end[None, :])
    pick = lambda per_expert: jnp.sum(jnp.where(mine, per_expert[None, :], 0), axis=1)
    in_use = group < group_end[-1]
    nth = group - pick(group_end - groups)
    window0 = jnp.where(in_use, pick(start) + nth * MOE_G, 0)
    n_real = jnp.where(in_use, jnp.clip(pick(counts) - nth * MOE_G, 0, MOE_G), 0).astype(jnp.int32)
    dst_tab = _row_tables(jnp.concatenate([window0, n_real]).astype(jnp.int32), order)
    return gates, meta, pos.astype(jnp.int32), fill, dst_tab, n_real


def kernel(x, c, w_ada, b_ada, w_in, sgu_ln_g, sgu_ln_b, w_spatial, b_spatial, w_o, ln1_g, ln1_b,
           w_router, b_router, w_exp1, b_exp1, w_exp2, b_exp2, ln2_g, ln2_b):
    depth = w_ada.shape[0]
    assert x.shape == (1, SEQ, D_MODEL)
    xs = x.reshape(SEQ, D_MODEL)
    for l in range(depth):
        ada = _ada(c, w_ada[l], b_ada[l])
        shift1, scale1, gate1, shift2, scale2, gate2 = jnp.split(ada, 6, axis=-1)

        qkv, y_a = _in_proj(xs, scale1, shift1, w_in[l].astype(jnp.bfloat16),
                            sgu_ln_g[l], sgu_ln_b[l], w_spatial[l], b_spatial[l])
        y_b = _attention(qkv)
        x1, h_packed, logits = _out_proj(y_a, y_b, w_o[l].astype(jnp.bfloat16), xs, gate1,
                                         ln1_g[l].reshape(1, -1), ln1_b[l].reshape(1, -1), scale2, shift2,
                                         w_router[l], b_router[l].reshape(1, -1))

        gates, meta, pos, fill, dst_tab, n_real = _route(logits)
        x_sorted = _dispatch(h_packed, pos, fill)
        y_sorted = _experts(x_sorted, meta, w_exp1[l], b_exp1[l], w_exp2[l], b_exp2[l])
        y_packed = _collect(y_sorted, dst_tab, n_real)
        xs = _combine(y_packed, gates, x1, gate2, ln2_g[l].reshape(1, -1), ln2_b[l].reshape(1, -1))
    return xs.reshape(x.shape)
```

```python
import math

import jax
import jax.numpy as jnp
from jax import lax
from jax.experimental import pallas as pl
from jax.experimental.pallas import tpu as pltpu

D_MODEL = 2048
SEQ = 8192
D_GMLP = 1024
GMLP_GROUPS = 8
GROUP_DIM = 128
CHUNK = 128
D_ATTN = 1024
HEAD_DIM = 128
N_HEADS = 8
BRANCHES = ((128, 1), (512, 4), (2048, 16))
BLK = 128
SPAN = 16 * BLK
D_IN_PROJ = 2 * D_GMLP + 3 * D_ATTN
N_EXPERTS = 32
TOP_K = 4
D_EXPERT = 2048
SWIGLU_LIMIT = 7.0
SWIGLU_ALPHA = 1.702
LN_EPS = 1e-5
DEEPNORM_ALPHA = 2.0 ** 0.25
NEG = -1e30

LANES = 128
VMEM_LIMIT = 56 * 1024 * 1024

ADA_TN = 1536
PROJ_TM = 1024
PROJ_TN = 1024
OUT_TM = 512
ATTN_UNROLL = 16
ROW_SUB = 8
D_PACK = D_MODEL // 2
MOE_G = 128
MOE_RMAX = 1152
MOE_ITEMS = 61
MOE_TN = 512
MOE_CHUNKS = D_EXPERT // MOE_TN
MOE_SUBTILES = (512, 256, 128)
X_ROWS = SEQ * TOP_K + N_EXPERTS * MOE_G
DISPATCH_TM = 512
COLLECT_GROUPS = 8
COMBINE_TM = 512
MAX_WORK_ITEMS = (X_ROWS // MOE_G + N_EXPERTS * (MOE_RMAX // MOE_G - 1)) // (MOE_RMAX // MOE_G)
assert MAX_WORK_ITEMS < MOE_ITEMS


def _params(*sem):
    return pltpu.CompilerParams(dimension_semantics=sem, vmem_limit_bytes=VMEM_LIMIT)


def _layer_norm(x, g, b):
    mu = jnp.mean(x, axis=-1, keepdims=True)
    xc = x - mu
    var = jnp.mean(xc * xc, axis=-1, keepdims=True)
    return xc * lax.rsqrt(var + LN_EPS) * g + b


def _pack_rows(x):
    r = x.astype(jnp.bfloat16).astype(jnp.float32)
    bits = lax.bitcast_convert_type(r, jnp.uint32)
    return (bits[:, D_PACK:] & jnp.uint32(0xFFFF0000)) | (bits[:, :D_PACK] >> 16)


def _unpack_words(u):
    lo = lax.bitcast_convert_type(u << 16, jnp.float32)
    hi = lax.bitcast_convert_type(u & jnp.uint32(0xFFFF0000), jnp.float32)
    return lo, hi


def _store_packed(ref, lead, row0, packed):
    m = packed.shape[0]
    for c in range(ROW_SUB):
        rows = pl.ds(row0 * ROW_SUB + c, m, stride=ROW_SUB)
        ref[lead + (rows, slice(None))] = packed[:, c * LANES:(c + 1) * LANES]


def _load_packed(ref, lead, row0, m):
    cols = [ref[lead + (pl.ds(row0 * ROW_SUB + c, m, stride=ROW_SUB), slice(None))] for c in range(ROW_SUB)]
    return jnp.concatenate(cols, axis=1)


def _ada_kernel(c_ref, w_ref, b_ref, o_ref):
    c = c_ref[...]
    s = c * jax.nn.sigmoid(c)
    o_ref[...] = jnp.sum(s * w_ref[...], axis=0, keepdims=True) + b_ref[...]


def _ada(c, w_ada, b_ada):
    n = w_ada.shape[1]
    return pl.pallas_call(
        _ada_kernel,
        grid=(n // ADA_TN,),
        in_specs=[
            pl.BlockSpec((D_MODEL, 1), lambda j: (0, 0)),
            pl.BlockSpec((D_MODEL, ADA_TN), lambda j: (0, j)),
            pl.BlockSpec((1, ADA_TN), lambda j: (0, j)),
        ],
        out_specs=pl.BlockSpec((1, ADA_TN), lambda j: (0, j)),
        out_shape=jax.ShapeDtypeStruct((1, n), jnp.float32),
        compiler_params=_params("arbitrary"),
        name="ada",
    )(c.reshape(D_MODEL, 1), w_ada, b_ada.reshape(1, n))


def _gelu(x):
    return 0.5 * x * (1.0 + lax.erf(x * (1.0 / math.sqrt(2.0))))


GROUPS_PER_STEP = 3


def _in_proj_kernel(x_ref, sc_ref, sh_ref, w_ref, g_ref, b_ref, ws_ref, bs_ref, qkv_ref, ya_ref, h_ref, uv_ref):
    i = pl.program_id(0)
    j = pl.program_id(1)

    @pl.when((i == 0) & (j == 0))
    def _():
        uv_ref[...] = jnp.zeros_like(uv_ref)

    @pl.when(j == 0)
    def _():
        h_ref[...] = (x_ref[...] * (1.0 + sc_ref[...]) + sh_ref[...]).astype(jnp.bfloat16)

    row = lax.broadcasted_iota(jnp.int32, (CHUNK, CHUNK), 0)
    col = lax.broadcasted_iota(jnp.int32, (CHUNK, CHUNK), 1)
    causal = col <= row
    chunks = [slice(c * CHUNK, (c + 1) * CHUNK) for c in range(PROJ_TM // CHUNK)]
    for u in range(GROUPS_PER_STEP):
        g = jnp.clip(GROUPS_PER_STEP * (j - 2) + u, 0, GMLP_GROUPS - 1)
        lanes = pl.ds(pl.multiple_of(g * GROUP_DIM, GROUP_DIM), GROUP_DIM)
        ln_g = g_ref[pl.ds(g, 1), :]
        ln_b = b_ref[pl.ds(g, 1), :]
        v = [_layer_norm(_gelu(uv_ref[1, rows, lanes]), ln_g, ln_b).astype(jnp.bfloat16) for rows in chunks]
        w = jnp.where(causal, ws_ref[g], 0.0).astype(jnp.bfloat16)
        s = jnp.dot(w, jnp.concatenate(v, axis=1), preferred_element_type=jnp.float32)
        for c, rows in enumerate(chunks):
            s_c = s[:, c * GROUP_DIM:(c + 1) * GROUP_DIM] + bs_ref[g]
            ya_ref[rows, lanes] = (_gelu(uv_ref[0, rows, lanes]) * s_c).astype(ya_ref.dtype)

    qkv_ref[...] = jnp.dot(h_ref[...], w_ref[...], preferred_element_type=jnp.float32)

    @pl.when(j < 2)
    def _():
        uv_ref[jnp.minimum(j, 1)] = qkv_ref[...]


def _in_proj(x, scale1, shift1, w_in_bf16, ln_g, ln_b, w_spatial, b_spatial):
    n_uv = 2 * D_GMLP // PROJ_TN
    assert GROUPS_PER_STEP * (D_IN_PROJ // PROJ_TN - n_uv) >= GMLP_GROUPS
    fixed2 = lambda i, j: (0, 0)
    bs_lanes = jnp.broadcast_to(b_spatial[:, :, None], (GMLP_GROUPS, CHUNK, LANES))
    return pl.pallas_call(
        _in_proj_kernel,
        grid=(SEQ // PROJ_TM, D_IN_PROJ // PROJ_TN),
        in_specs=[
            pl.BlockSpec((PROJ_TM, D_MODEL), lambda i, j: (i, 0)),
            pl.BlockSpec((1, D_MODEL), fixed2),
            pl.BlockSpec((1, D_MODEL), fixed2),
            pl.BlockSpec((D_MODEL, PROJ_TN), lambda i, j: (0, j)),
            pl.BlockSpec((GMLP_GROUPS, GROUP_DIM), fixed2),
            pl.BlockSpec((GMLP_GROUPS, GROUP_DIM), fixed2),
            pl.BlockSpec((GMLP_GROUPS, CHUNK, CHUNK), lambda i, j: (0, 0, 0)),
            pl.BlockSpec((GMLP_GROUPS, CHUNK, LANES), lambda i, j: (0, 0, 0)),
        ],
        out_specs=[
            pl.BlockSpec((PROJ_TM, PROJ_TN), lambda i, j: (i, jnp.maximum(j - n_uv, 0))),
            pl.BlockSpec((PROJ_TM, D_GMLP), lambda i, j: (i, 0)),
        ],
        out_shape=[
            jax.ShapeDtypeStruct((SEQ, 3 * D_ATTN), jnp.float32),
            jax.ShapeDtypeStruct((SEQ, D_GMLP), jnp.bfloat16),
        ],
        scratch_shapes=[
            pltpu.VMEM((PROJ_TM, D_MODEL), jnp.bfloat16),
            pltpu.VMEM((2, PROJ_TM, D_GMLP), jnp.float32),
        ],
        compiler_params=_params("arbitrary", "arbitrary"),
        name="in_proj",
    )(x, scale1, shift1, w_in_bf16, ln_g, ln_b, w_spatial, bs_lanes)


def _attn_kernel(q_ref, k_ref, v_ref, o_ref, out_ref, lse_ref, kv_ref):
    head = pl.program_id(0)
    span = pl.program_id(1)
    log2e = 1.0 / math.log(2.0)
    head_no = (jnp.zeros((BLK, 2 * BLK), jnp.int32) + (head + 1)).astype(jnp.float32)
    slope = jnp.exp2(head_no * (-8.0 / N_HEADS)) * log2e
    scale = HEAD_DIM ** -0.5 * log2e
    qi = lax.broadcasted_iota(jnp.int32, (BLK, 2 * BLK), 0)
    ki = lax.broadcasted_iota(jnp.int32, (BLK, 2 * BLK), 1)
    step = qi + BLK - ki
    contract_last = (((1,), (1,)), ((), ()))
    ones = jnp.ones((2 * BLK, LANES), jnp.bfloat16)

    for b, (window, d) in enumerate(BRANCHES):
        assert window // d == BLK
        valid = (step >= 0) & (step <= BLK)
        bias = jnp.where(valid, -slope * d * step.astype(jnp.float32), NEG)
        bias_first = jnp.where(ki >= BLK, bias, NEG)

        per_class = d * BLK == SPAN
        cur = span % 2
        if per_class:
            def stash(r, carry, d=d):
                rows = pl.ds(pl.multiple_of(r * BLK, BLK), BLK)
                kv_ref[0, cur, rows, :] = k_ref[pl.ds(span * SPAN + r, BLK, stride=d), :].astype(jnp.bfloat16)
                kv_ref[1, cur, rows, :] = v_ref[pl.ds(span * SPAN + r, BLK, stride=d), :].astype(jnp.bfloat16)
                return carry
            lax.fori_loop(0, d, stash, 0, unroll=4)

        def tile(t, carry, d=d, b=b, bias=bias, bias_first=bias_first, per_class=per_class):
            r = t % d
            n = t // d
            q0 = n * (BLK * d) + r
            k0 = span * SPAN + q0
            first = k0 < BLK * d
            q = (q_ref[pl.ds(q0, BLK, stride=d), :] * scale).astype(jnp.bfloat16)
            if per_class:
                rows = pl.ds(pl.multiple_of(r * BLK, BLK), BLK)
                prev = jnp.where(first, cur, 1 - cur)
                k = jnp.concatenate([kv_ref[0, prev, rows, :], kv_ref[0, cur, rows, :]], axis=0)
                v = jnp.concatenate([kv_ref[1, prev, rows, :], kv_ref[1, cur, rows, :]], axis=0)
            else:
                kp = jnp.where(first, k0, k0 - BLK * d)
                k = jnp.concatenate([k_ref[pl.ds(kp, BLK, stride=d), :], k_ref[pl.ds(k0, BLK, stride=d), :]],
                                    axis=0).astype(jnp.bfloat16)
                v = jnp.concatenate([v_ref[pl.ds(kp, BLK, stride=d), :], v_ref[pl.ds(k0, BLK, stride=d), :]],
                                    axis=0).astype(jnp.bfloat16)
            s = lax.dot_general(q, k, contract_last, preferred_element_type=jnp.float32)
            s = s + jnp.where(first, bias_first, bias)
            m = jnp.max(jnp.maximum(s[:, :BLK], s[:, BLK:]), axis=-1, keepdims=True)
            p = jnp.exp2(s - m).astype(jnp.bfloat16)
            v_one = jnp.concatenate([v, ones], axis=1)
            pv = jnp.dot(p, v_one, preferred_element_type=jnp.float32)
            den = pv[:, HEAD_DIM:]
            rows = pl.ds(q0, BLK, stride=d)
            out_ref[b, rows, :] = pv[:, :HEAD_DIM] / den
            lse_ref[b, rows, :] = m + jnp.log2(den)
            return carry

        lax.fori_loop(0, SPAN // BLK, tile, 0, unroll=ATTN_UNROLL)

    lse_all = jnp.maximum(jnp.maximum(lse_ref[0], lse_ref[1]), lse_ref[2])
    num = jnp.zeros((SPAN, HEAD_DIM), jnp.float32)
    den = jnp.zeros((SPAN, LANES), jnp.float32)
    for b in range(len(BRANCHES)):
        w = jnp.exp2(lse_ref[b] - lse_all)
        num = num + w * out_ref[b]
        den = den + w
    o_ref[...] = (num / den).astype(o_ref.dtype)


def _attention(proj):
    q_col = 0
    k_col = q_col + N_HEADS
    v_col = k_col + N_HEADS
    nb = len(BRANCHES)
    return pl.pallas_call(
        _attn_kernel,
        grid=(N_HEADS, SEQ // SPAN),
        in_specs=[
            pl.BlockSpec((SPAN, HEAD_DIM), lambda h, s: (s, q_col + h)),
            pl.BlockSpec((SEQ, HEAD_DIM), lambda h, s: (0, k_col + h)),
            pl.BlockSpec((SEQ, HEAD_DIM), lambda h, s: (0, v_col + h)),
        ],
        out_specs=pl.BlockSpec((SPAN, HEAD_DIM), lambda h, s: (s, h)),
        out_shape=jax.ShapeDtypeStruct((SEQ, D_ATTN), jnp.bfloat16),
        scratch_shapes=[
            pltpu.VMEM((nb, SPAN, HEAD_DIM), jnp.float32),
            pltpu.VMEM((nb, SPAN, LANES), jnp.float32),
            pltpu.VMEM((2, 2, SPAN, HEAD_DIM), jnp.bfloat16),
        ],
        compiler_params=_params("arbitrary", "arbitrary"),
        name="attn",
    )(proj, proj, proj)


def _split_bf16(x):
    hi = x.astype(jnp.bfloat16)
    lo = (x - hi.astype(jnp.float32)).astype(jnp.bfloat16)
    return hi, lo


def _out_proj_kernel(ya_ref, yb_ref, wa_ref, wb_ref, x_ref, gate_ref, g_ref, b_ref, sc_ref, sh_ref,
                     wr_ref, br_ref, x1_ref, hp_ref, lg_ref):
    mix = jnp.dot(ya_ref[...], wa_ref[...], preferred_element_type=jnp.float32)
    mix = mix + jnp.dot(yb_ref[...], wb_ref[...], preferred_element_type=jnp.float32)
    x1 = _layer_norm(DEEPNORM_ALPHA * x_ref[...] + gate_ref[...] * mix, g_ref[...], b_ref[...])
    x1_ref[...] = x1
    h = x1 * (1.0 + sc_ref[...]) + sh_ref[...]
    _store_packed(hp_ref, (), 0, _pack_rows(h))
    h_hi, h_lo = _split_bf16(h)
    w_hi, w_lo = _split_bf16(wr_ref[...])
    both = jnp.dot(h_hi, jnp.concatenate([w_hi, w_lo], axis=1), preferred_element_type=jnp.float32)
    lg = both[:, :N_EXPERTS] + both[:, N_EXPERTS:] + jnp.dot(h_lo, w_hi, preferred_element_type=jnp.float32)
    lg_ref[...] = lg + br_ref[...]


def _out_proj(y_a, y_b, w_o_bf16, x, gate1, ln_g, ln_b, scale2, shift2, w_router, b_router):
    row = lambda i: (i, 0)
    fixed = lambda i: (0, 0)
    vec = pl.BlockSpec((1, D_MODEL), fixed)
    return pl.pallas_call(
        _out_proj_kernel,
        grid=(SEQ // OUT_TM,),
        in_specs=[
            pl.BlockSpec((OUT_TM, D_GMLP), row),
            pl.BlockSpec((OUT_TM, D_ATTN), row),
            pl.BlockSpec((D_GMLP, D_MODEL), lambda i: (0, 0)),
            pl.BlockSpec((D_ATTN, D_MODEL), lambda i: (1, 0)),
            pl.BlockSpec((OUT_TM, D_MODEL), row),
            vec, vec, vec, vec, vec,
            pl.BlockSpec((D_MODEL, N_EXPERTS), fixed),
            pl.BlockSpec((1, N_EXPERTS), fixed),
        ],
        out_specs=[
            pl.BlockSpec((OUT_TM, D_MODEL), row),
            pl.BlockSpec((OUT_TM * ROW_SUB, LANES), row),
            pl.BlockSpec((OUT_TM, N_EXPERTS), row),
        ],
        out_shape=[
            jax.ShapeDtypeStruct((SEQ, D_MODEL), jnp.float32),
            jax.ShapeDtypeStruct((SEQ * ROW_SUB, LANES), jnp.uint32),
            jax.ShapeDtypeStruct((SEQ, N_EXPERTS), jnp.float32),
        ],
        compiler_params=_params("arbitrary"),
        name="out_proj",
    )(y_a, y_b, w_o_bf16, w_o_bf16, x, gate1, ln_g, ln_b, scale2, shift2, w_router, b_router)


def _experts_kernel(meta_ref, x_hbm, w1_hbm, w2_hbm, b1_ref, b2_ref, y_hbm,
                    xs_ref, os_ref, acc_ref, wg_buf, wl_buf, w2_buf, gsem, ssem, wsem):
    i = pl.program_id(0)
    n_items = pl.num_programs(0)

    def item_rows(k):
        inside = (k >= 0) & (k < n_items)
        return jnp.where(inside, meta_ref[MOE_ITEMS + jnp.clip(k, 0, n_items - 1)], 0)

    def item_first(k):
        return meta_ref[2 * MOE_ITEMS + jnp.clip(k, 0, n_items - 1)] * ROW_SUB

    rows = item_rows(i)
    rows_prev = item_rows(i - 1)
    rows_next = item_rows(i + 1)
    p = i % 2
    q = 1 - p
    group_sub = MOE_G * ROW_SUB

    def each_group(n_rows, fn):
        lax.fori_loop(0, n_rows // MOE_G, lambda t, c: (fn(pl.multiple_of(t * group_sub, group_sub)), c)[1], 0)

    def load_rows(k, slot):
        first = item_first(k)
        each_group(item_rows(k), lambda at: pltpu.make_async_copy(
            x_hbm.at[pl.ds(pl.multiple_of(first + at, group_sub), group_sub), :],
            xs_ref.at[slot, pl.ds(at, group_sub), :], gsem.at[slot]).start())

    def wait_rows(buf_ref, sem, n_rows):
        group = buf_ref.at[pl.ds(0, group_sub), :]
        each_group(n_rows, lambda at: pltpu.make_async_copy(group, group, sem).wait())

    def weight_copies(k, j, slot):
        e = meta_ref[jnp.clip(k, 0, n_items - 1)]
        c0 = pl.multiple_of(j * MOE_TN, MOE_TN)
        return (pltpu.make_async_copy(w1_hbm.at[e, :, pl.ds(c0, MOE_TN)], wg_buf.at[slot], wsem.at[slot]),
                pltpu.make_async_copy(w1_hbm.at[e, :, pl.ds(D_EXPERT + c0, MOE_TN)], wl_buf.at[slot],
                                      wsem.at[slot]),
                pltpu.make_async_copy(w2_hbm.at[e, pl.ds(c0, MOE_TN), :], w2_buf.at[slot], wsem.at[slot]))

    @pl.when(i == 0)
    def _():
        acc_ref[...] = jnp.zeros_like(acc_ref)
        os_ref[...] = jnp.zeros_like(os_ref)
        zeros = os_ref.at[pl.ds(0, group_sub), :]
        first_unused = meta_ref[3 * MOE_ITEMS]
        n_unused = X_ROWS // MOE_G - first_unused

        def fill(t, c):
            dst = pl.multiple_of((first_unused + t) * group_sub, group_sub)
            pltpu.make_async_copy(zeros, y_hbm.at[pl.ds(dst, group_sub), :], ssem).start()
            return c
        lax.fori_loop(0, n_unused, fill, 0)
        wait_rows(os_ref, ssem, n_unused * MOE_G)
        load_rows(0, 0)

        @pl.when(rows > 0)
        def _():
            for cp in weight_copies(0, 0, 0):
                cp.start(priority=1)

    wait_rows(xs_ref.at[p], gsem.at[p], rows)
    load_rows(i + 1, q)

    def sub_tile(a, m, j, slot):
        a = pl.multiple_of(a, MOE_G)
        x_lo, x_hi = _unpack_words(_load_packed(xs_ref, (p,), a, m))
        x = jnp.concatenate([x_lo.astype(jnp.bfloat16), x_hi.astype(jnp.bfloat16)], axis=1)

        def up(w_buf, first_col):
            bias = b1_ref[:, pl.ds(pl.multiple_of(first_col + j * MOE_TN, MOE_TN), MOE_TN)]
            return jnp.dot(x, w_buf[slot].astype(jnp.bfloat16), preferred_element_type=jnp.float32) + bias

        glu = jnp.minimum(up(wg_buf, 0), SWIGLU_LIMIT)
        lin = jnp.clip(up(wl_buf, D_EXPERT), -SWIGLU_LIMIT, SWIGLU_LIMIT)
        act = glu * jax.nn.sigmoid(SWIGLU_ALPHA * glu) * (lin + 1.0)
        down = jnp.dot(act.astype(jnp.bfloat16), w2_buf[slot].astype(jnp.bfloat16),
                       preferred_element_type=jnp.float32)
        start = jnp.where(j == 0, jnp.broadcast_to(b2_ref[...], (m, D_MODEL)), acc_ref[pl.ds(a, m), :])
        acc_ref[pl.ds(a, m), :] = start + down

    def chunk(j, carry):
        slot = j % 2
        last = j == MOE_CHUNKS - 1

        @pl.when(jnp.logical_not(last) | (rows_next > 0))
        def _():
            for cp in weight_copies(jnp.where(last, i + 1, i), jnp.where(last, 0, j + 1), 1 - slot):
                cp.start(priority=1)

        for cp in weight_copies(i, j, slot):
            cp.wait()

        main = MOE_SUBTILES[0]
        n_main = rows // main
        lax.fori_loop(0, n_main, lambda t, c: (sub_tile(t * main, main, j, slot), c)[1], 0)
        done = n_main * main
        for m in MOE_SUBTILES[1:]:
            has = ((rows - done) // m) % 2 == 1
            pl.when(has)(lambda done=done, m=m: sub_tile(done, m, j, slot))
            done = done + jnp.where(has, m, 0)
        return carry

    @pl.when(rows > 0)
    def _():
        lax.fori_loop(0, MOE_CHUNKS, chunk, 0)

    wait_rows(os_ref, ssem, rows_prev)

    def pack(t, c):
        a = pl.multiple_of(t * MOE_G, MOE_G)
        _store_packed(os_ref, (), a, _pack_rows(acc_ref[pl.ds(a, MOE_G), :]))
        return c
    lax.fori_loop(0, rows // MOE_G, pack, 0)
    first = item_first(i)
    each_group(rows, lambda at: pltpu.make_async_copy(
        os_ref.at[pl.ds(at, group_sub), :],
        y_hbm.at[pl.ds(pl.multiple_of(first + at, group_sub), group_sub), :], ssem).start())


def _experts(x_sorted, meta, w1, b1, w2, b2):
    return pl.pallas_call(
        _experts_kernel,
        grid_spec=pltpu.PrefetchScalarGridSpec(
            num_scalar_prefetch=1,
            grid=(MOE_ITEMS,),
            in_specs=[
                pl.BlockSpec(memory_space=pl.ANY),
                pl.BlockSpec(memory_space=pl.ANY),
                pl.BlockSpec(memory_space=pl.ANY),
                pl.BlockSpec((None, 1, 2 * D_EXPERT), lambda i, m: (m[i], 0, 0)),
                pl.BlockSpec((None, 1, D_MODEL), lambda i, m: (m[i], 0, 0)),
            ],
            out_specs=pl.BlockSpec(memory_space=pl.ANY),
            scratch_shapes=[
                pltpu.VMEM((2, MOE_RMAX * ROW_SUB, LANES), jnp.uint32),
                pltpu.VMEM((MOE_RMAX * ROW_SUB, LANES), jnp.uint32),
                pltpu.VMEM((MOE_RMAX, D_MODEL), jnp.float32),
                pltpu.VMEM((2, D_MODEL, MOE_TN), jnp.float32),
                pltpu.VMEM((2, D_MODEL, MOE_TN), jnp.float32),
                pltpu.VMEM((2, MOE_TN, D_MODEL), jnp.float32),
                pltpu.SemaphoreType.DMA((2,)),
                pltpu.SemaphoreType.DMA(()),
                pltpu.SemaphoreType.DMA((2,)),
            ],
        ),
        out_shape=jax.ShapeDtypeStruct((X_ROWS * ROW_SUB, LANES), jnp.uint32),
        compiler_params=_params("arbitrary"),
        name="experts",
    )(meta, x_sorted, w1, w2, b1.reshape(N_EXPERTS, 1, 2 * D_EXPERT), b2.reshape(N_EXPERTS, 1, D_MODEL))


def _combine_kernel(y0_ref, y1_ref, y2_ref, y3_ref, gates_ref, x1_ref, gate2_ref, g_ref, b_ref, o_ref):
    y_lo = jnp.zeros((COMBINE_TM, D_PACK), jnp.float32)
    y_hi = jnp.zeros((COMBINE_TM, D_PACK), jnp.float32)
    for k, y_ref in enumerate((y0_ref, y1_ref, y2_ref, y3_ref)):
        lo, hi = _unpack_words(_load_packed(y_ref, (), 0, COMBINE_TM))
        gate = gates_ref[:, k:k + 1]
        y_lo = y_lo + gate * lo
        y_hi = y_hi + gate * hi
    y = jnp.concatenate([y_lo, y_hi], axis=1)
    o_ref[...] = _layer_norm(DEEPNORM_ALPHA * x1_ref[...] + gate2_ref[...] * y, g_ref[...], b_ref[...])


def _combine(y_packed, gates, x1, gate2, ln_g, ln_b):
    n_tiles = SEQ // COMBINE_TM
    row = lambda i: (i, 0)
    vec = pl.BlockSpec((1, D_MODEL), lambda i: (0, 0))
    slot = lambda k: pl.BlockSpec((COMBINE_TM * ROW_SUB, LANES), lambda i: (k * n_tiles + i, 0))
    return pl.pallas_call(
        _combine_kernel,
        grid=(n_tiles,),
        in_specs=[
            slot(0), slot(1), slot(2), slot(3),
            pl.BlockSpec((COMBINE_TM, TOP_K), row),
            pl.BlockSpec((COMBINE_TM, D_MODEL), row),
            vec, vec, vec,
        ],
        out_specs=pl.BlockSpec((COMBINE_TM, D_MODEL), row),
        out_shape=jax.ShapeDtypeStruct((SEQ, D_MODEL), jnp.float32),
        compiler_params=_params("arbitrary"),
        name="combine",
    )(y_packed, y_packed, y_packed, y_packed, gates, x1, gate2, ln_g, ln_b)


N_GROUPS = X_ROWS // MOE_G
WINDOW_ROWS = ROW_SUB
assert MOE_G == LANES and TOP_K == 4


def _row_tables_kernel(win_ref, order_ref, dst_ref):
    lane = lax.broadcasted_iota(jnp.int32, (1, LANES), 1)

    def table(g, carry):
        w0 = win_ref[g]
        n_real = win_ref[N_GROUPS + g]
        off = w0 % LANES
        x = order_ref[pl.ds(w0 // LANES, WINDOW_ROWS), :]
        x = pltpu.roll(x, (LANES - off) % LANES, axis=1)
        flat = jnp.where(lane < LANES - off, x[0:1], x[1:2])
        tok = flat >> 2
        slot = flat & 3
        dst_ref[pl.ds(g, 1), :] = jnp.where(lane < n_real, slot * SEQ + tok, 0) * ROW_SUB
        return carry
    lax.fori_loop(0, N_GROUPS, table, 0, unroll=8)


def _row_tables(win, order):
    n_rows = SEQ * TOP_K // LANES
    order2d = jnp.concatenate([order, jnp.zeros((WINDOW_ROWS * LANES,), jnp.int32)]).reshape(-1, LANES)
    return pl.pallas_call(
        _row_tables_kernel,
        grid_spec=pltpu.PrefetchScalarGridSpec(
            num_scalar_prefetch=1,
            grid=(1,),
            in_specs=[pl.BlockSpec((n_rows + WINDOW_ROWS, LANES), lambda b, w: (0, 0))],
            out_specs=pl.BlockSpec((N_GROUPS, LANES), lambda b, w: (0, 0)),
        ),
        out_shape=jax.ShapeDtypeStruct((N_GROUPS, LANES), jnp.int32),
        compiler_params=_params("arbitrary"),
        name="row_tables",
    )(win, order2d)


def _collect_kernel(real_ref, dst_ref, y_ref, out_hbm, buf_ref, sem):
    i = pl.program_id(0)
    slot = i % 2
    buf_ref[slot] = y_ref[...]

    def row_copy(s, r):
        at = r * ROW_SUB if isinstance(r, int) else pl.multiple_of(r * ROW_SUB, ROW_SUB)
        dst = pl.multiple_of(dst_ref[0, 0, r], ROW_SUB)
        return pltpu.make_async_copy(buf_ref.at[s, pl.ds(at, ROW_SUB), :], out_hbm.at[pl.ds(dst, ROW_SUB), :],
                                     sem.at[s])

    group_sub = MOE_G * ROW_SUB

    def rows_loop(n, fn, **kw):
        lax.fori_loop(0, n, lambda r, c: (fn(r), c)[1], 0, **kw)

    def start_tile(step, s):
        for h in range(COLLECT_GROUPS):
            n_real = real_ref[step * COLLECT_GROUPS + h]
            row0 = h * MOE_G

            @pl.when(n_real == MOE_G)
            def _():
                for r in range(MOE_G):
                    row_copy(s, row0 + r).start(priority=r % 2)

            @pl.when(n_real < MOE_G)
            def _():
                rows_loop(n_real, lambda r, row0=row0: row_copy(s, row0 + r).start())

    def wait_tile(step, s):
        whole = buf_ref.at[s, pl.ds(0, group_sub), :]
        for h in range(COLLECT_GROUPS):
            n_real = real_ref[step * COLLECT_GROUPS + h]

            @pl.when(n_real == MOE_G)
            def _():
                pltpu.make_async_copy(whole, whole, sem.at[s]).wait()

            @pl.when(n_real < MOE_G)
            def _():
                rows_loop(n_real, lambda r: row_copy(s, 0).wait())

    start_tile(i, slot)

    @pl.when(i > 0)
    def _():
        wait_tile(i - 1, 1 - slot)

    @pl.when(i == pl.num_programs(0) - 1)
    def _():
        wait_tile(i, slot)


def _collect(y_sorted, dst_tab, n_real):
    n_tiles = N_GROUPS // COLLECT_GROUPS
    tile_rows = COLLECT_GROUPS * MOE_G
    return pl.pallas_call(
        _collect_kernel,
        grid_spec=pltpu.PrefetchScalarGridSpec(
            num_scalar_prefetch=1,
            grid=(n_tiles,),
            in_specs=[
                pl.BlockSpec((1, 1, tile_rows), lambda i, n: (i, 0, 0), memory_space=pltpu.SMEM),
                pl.BlockSpec((tile_rows * ROW_SUB, LANES), lambda i, n: (i, 0)),
            ],
            out_specs=pl.BlockSpec(memory_space=pl.ANY),
            scratch_shapes=[
                pltpu.VMEM((2, tile_rows * ROW_SUB, LANES), jnp.uint32),
                pltpu.SemaphoreType.DMA((2,)),
            ],
        ),
        out_shape=jax.ShapeDtypeStruct((SEQ * TOP_K * ROW_SUB, LANES), jnp.uint32),
        compiler_params=_params("arbitrary"),
        name="collect",
    )(n_real, dst_tab.reshape(n_tiles, 1, tile_rows), y_sorted)


def _dispatch_kernel(fill_ref, pos_ref, h_ref, x_hbm, zero_ref, buf_ref, sem, zsem):
    i = pl.program_id(0)
    group_sub = MOE_G * ROW_SUB

    @pl.when(i == 0)
    def _():
        zero_ref[...] = jnp.zeros_like(zero_ref)

        def fill(g):
            dst = pl.multiple_of(g * group_sub, group_sub)
            pltpu.make_async_copy(zero_ref, x_hbm.at[pl.ds(dst, group_sub), :], zsem).start()

        for e in range(N_EXPERTS):
            end = fill_ref[e]
            begin = fill_ref[e - 1] if e else 0
            pl.when(end > begin)(lambda end=end: fill(end - 1))
        lax.fori_loop(fill_ref[N_EXPERTS - 1], X_ROWS // MOE_G, lambda g, c: (fill(g), c)[1], 0)

        def wait_fill(t, c):
            pltpu.make_async_copy(zero_ref, zero_ref, zsem).wait()
            return c
        lax.fori_loop(0, fill_ref[N_EXPERTS], wait_fill, 0)

    slot = i % 2
    buf_ref[slot] = h_ref[...]

    def token(t, c):
        src = pl.multiple_of(t * ROW_SUB, ROW_SUB)
        for k in range(TOP_K):
            dst = pl.multiple_of(pos_ref[0, 0, t * TOP_K + k], ROW_SUB)
            pltpu.make_async_copy(buf_ref.at[slot, pl.ds(src, ROW_SUB), :], x_hbm.at[pl.ds(dst, ROW_SUB), :],
                                  sem.at[slot]).start(priority=k % 2)
        return c
    lax.fori_loop(0, DISPATCH_TM, token, 0, unroll=16)

    def wait_tile(s):
        for k in range(TOP_K):
            pltpu.make_async_copy(buf_ref.at[s], buf_ref.at[s], sem.at[s]).wait()

    pl.when(i > 0)(lambda: wait_tile(1 - slot))
    pl.when(i == pl.num_programs(0) - 1)(lambda: wait_tile(slot))


def _dispatch(h_packed, pos, fill):
    n_tiles = SEQ // DISPATCH_TM
    return pl.pallas_call(
        _dispatch_kernel,
        grid_spec=pltpu.PrefetchScalarGridSpec(
            num_scalar_prefetch=1,
            grid=(n_tiles,),
            in_specs=[
                pl.BlockSpec((1, 1, DISPATCH_TM * TOP_K), lambda i, f: (i, 0, 0), memory_space=pltpu.SMEM),
                pl.BlockSpec((DISPATCH_TM * ROW_SUB, LANES), lambda i, f: (i, 0)),
            ],
            out_specs=pl.BlockSpec(memory_space=pl.ANY),
            scratch_shapes=[
                pltpu.VMEM((MOE_G * ROW_SUB, LANES), jnp.uint32),
                pltpu.VMEM((2, DISPATCH_TM * ROW_SUB, LANES), jnp.uint32),
                pltpu.SemaphoreType.DMA((2,)),
                pltpu.SemaphoreType.DMA(()),
            ],
        ),
        out_shape=jax.ShapeDtypeStruct((X_ROWS * ROW_SUB, LANES), jnp.uint32),
        compiler_params=_params("arbitrary"),
        name="dispatch",
    )(fill, pos.reshape(n_tiles, 1, DISPATCH_TM * TOP_K), h_packed)


def _route(logits):
    top_val, top_idx = lax.top_k(logits, TOP_K)
    gates = jax.nn.softmax(top_val, axis=-1)
    e_flat = top_idx.reshape(-1).astype(jnp.int32)
    experts = jnp.arange(N_EXPERTS, dtype=jnp.int32)
    counts = jnp.sum((e_flat[:, None] == experts[None, :]).astype(jnp.int32), axis=0)
    groups = (counts + MOE_G - 1) // MOE_G
    group_end = jnp.cumsum(groups)
    row0 = (group_end - groups) * MOE_G
    per_item = MOE_RMAX // MOE_G
    n_items_e = (groups + per_item - 1) // per_item
    item_end = jnp.cumsum(n_items_e)
    item_start = item_end - n_items_e
    n_items = item_end[-1]

    item = jnp.arange(MOE_ITEMS, dtype=jnp.int32)
    used = item < n_items
    e_item = jnp.minimum(jnp.searchsorted(item_end, item, side='right'), N_EXPERTS - 1).astype(jnp.int32)
    e_last = e_item[jnp.maximum(n_items - 1, 0)]
    part = item - item_start[e_item]
    rows = jnp.where(used, jnp.clip(groups[e_item] - part * per_item, 0, per_item) * MOE_G, 0)
    first_row = jnp.where(used, row0[e_item] + part * MOE_RMAX, 0)
    meta = jnp.concatenate([jnp.where(used, e_item, e_last), rows, first_row, group_end[-1:]]).astype(jnp.int32)

    order = jnp.argsort(e_flat, stable=True).astype(jnp.int32)
    rank = jnp.argsort(order).astype(jnp.int32)
    start = jnp.cumsum(counts) - counts
    shift = jnp.sum(jnp.where(e_flat[:, None] == experts[None, :], (row0 - start)[None, :], 0), axis=1)
    pos = (rank + shift) * ROW_SUB
    n_fill = jnp.sum((groups > 0).astype(jnp.int32)) + X_ROWS // MOE_G - group_end[-1]
    fill = jnp.concatenate([group_end, n_fill[None]]).astype(jnp.int32)

    group = jnp.arange(N_GROUPS, dtype=jnp.int32)
    mine = (group[:, None] >= (group_end - groups)[None, :]) & (group[:, None] < group_end[None, :])
    pick = lambda per_expert: jnp.sum(jnp.where(mine, per_expert[None, :], 0), axis=1)
    in_use = group < group_end[-1]
    nth = group - pick(group_end - groups)
    window0 = jnp.where(in_use, pick(start) + nth * MOE_G, 0)
    n_real = jnp.where(in_use, jnp.clip(pick(counts) - nth * MOE_G, 0, MOE_G), 0).astype(jnp.int32)
    dst_tab = _row_tables(jnp.concatenate([window0, n_real]).astype(jnp.int32), order)
    return gates, meta, pos.astype(jnp.int32), fill, dst_tab, n_real


def kernel(x, c, w_ada, b_ada, w_in, sgu_ln_g, sgu_ln_b, w_spatial, b_spatial, w_o, ln1_g, ln1_b,
           w_router, b_router, w_exp1, b_exp1, w_exp2, b_exp2, ln2_g, ln2_b):
    depth = w_ada.shape[0]
    assert x.shape == (1, SEQ, D_MODEL)
    xs = x.reshape(SEQ, D_MODEL)
    for l in range(depth):
        ada = _ada(c, w_ada[l], b_ada[l])
        shift1, scale1, gate1, shift2, scale2, gate2 = jnp.split(ada, 6, axis=-1)

        qkv, y_a = _in_proj(xs, scale1, shift1, w_in[l].astype(jnp.bfloat16),
                            sgu_ln_g[l], sgu_ln_b[l], w_spatial[l], b_spatial[l])
        y_b = _attention(qkv)
        x1, h_packed, logits = _out_proj(y_a, y_b, w_o[l].astype(jnp.bfloat16), xs, gate1,
                                         ln1_g[l].reshape(1, -1), ln1_b[l].reshape(1, -1), scale2, shift2,
                                         w_router[l], b_router[l].reshape(1, -1))

        gates, meta, pos, fill, dst_tab, n_real = _route(logits)
        x_sorted = _dispatch(h_packed, pos, fill)
        y_sorted = _experts(x_sorted, meta, w_exp1[l], b_exp1[l], w_exp2[l], b_exp2[l])
        y_packed = _collect(y_sorted, dst_tab, n_real)
        xs = _combine(y_packed, gates, x1, gate2, ln2_g[l].reshape(1, -1), ln2_b[l].reshape(1, -1))
    return xs.reshape(x.shape)
```
